```python
import math
import jax, jax.numpy as jnp
from jax import lax
import numpy as np

D_MODEL = 4096
BATCH = 2
SEQ = 4096
DEPTH = 1

MIX_WIDTH = D_MODEL
ML_HEADS = 8
ML_DV = MIX_WIDTH // 2 // ML_HEADS
ML_DQK = ML_DV // 2
ML_WIDTH = ML_HEADS * ML_DV
ML_QK_WIDTH = ML_HEADS * ML_DQK
ML_CHUNK = 64
ML_CONV = 4
NSA_HEADS = 16
NSA_HD = (MIX_WIDTH - ML_WIDTH) // NSA_HEADS
NSA_GROUPS = 4
NSA_HPG = NSA_HEADS // NSA_GROUPS
NSA_WIDTH = NSA_HEADS * NSA_HD
NSA_KV_WIDTH = NSA_GROUPS * NSA_HD
CMP_STRIDE = 16
CMP_LEN = 2 * CMP_STRIDE
CMP_HIDDEN = 2 * NSA_HD
SEL_LEN = 64
SEL_TOPK = 16
SEL_QBLOCK = 64
WIN = 512
WIN_QBLOCK = 128
N_BRANCH = 3
REL_BUCKETS = 32
REL_MAX_DIST = 128
PLE_DIM = 256
EPS = 1e-6
NEG_INF = -1e30
FORCE_SCORE = 1e4
IN_SPLITS = (ML_QK_WIDTH, ML_QK_WIDTH, ML_WIDTH, ML_WIDTH, ML_WIDTH, ML_HEADS, ML_HEADS,
             NSA_WIDTH, NSA_KV_WIDTH, NSA_KV_WIDTH, NSA_KV_WIDTH, NSA_KV_WIDTH,
             NSA_KV_WIDTH, NSA_KV_WIDTH, NSA_HEADS * N_BRANCH, NSA_WIDTH)
IN_WIDTH = sum(IN_SPLITS)

kernel_name = 'hymba_mlstm_nsa_block'


def _split(u, widths):
    offs = np.cumsum(widths)[:-1].tolist()
    return jnp.split(u, offs, axis=-1)


def _rmsnorm(x, w):
    xf = x.astype(jnp.float32)
    y = xf * lax.rsqrt(jnp.mean(xf * xf, axis=-1, keepdims=True) + EPS)
    return (y * w.astype(jnp.float32)).astype(x.dtype)


def _causal_conv(x, w):
    K, S = w.shape[0], x.shape[1]
    xp = jnp.pad(x, ((0, 0), (K - 1, 0), (0, 0)))
    return sum(xp[:, j:j + S] * w[j] for j in range(K))


def _rel_bucket(dist):
    n = jnp.maximum(dist, 0)
    max_exact = REL_BUCKETS // 2
    nf = jnp.maximum(n, 1).astype(jnp.float32)
    large = max_exact + (jnp.log(nf / max_exact) / math.log(REL_MAX_DIST / max_exact)
                         * (REL_BUCKETS - max_exact)).astype(jnp.int32)
    large = jnp.minimum(large, REL_BUCKETS - 1)
    return jnp.where(n < max_exact, n, large)


def _masked_softmax(scores, mask):
    s = jnp.where(mask, scores, NEG_INF)
    return jnp.where(mask, jax.nn.softmax(s, axis=-1), 0.0)


def _mlstm(q, k, v, log_i, log_f):
    B, S, NH, _ = q.shape
    nc = S // ML_CHUNK

    def to_chunks(a):
        a = a.astype(jnp.float32).reshape((B, nc, ML_CHUNK) + a.shape[2:])
        return jnp.swapaxes(jnp.moveaxis(a, 1, 0), 2, 3)

    tril = jnp.tril(jnp.ones((ML_CHUNK, ML_CHUNK), dtype=bool))

    def step(carry, xs):
        C, n, m = carry
        qc, kc, vc, li, lf = xs
        b = jnp.cumsum(lf, axis=-1)
        D = jnp.where(tril, b[..., :, None] - b[..., None, :] + li[..., None, :], -jnp.inf)
        a = b + m[..., None]
        m_t = jnp.maximum(a, jnp.max(D, axis=-1))
        Dw = jnp.exp(D - m_t[..., None])
        inter = jnp.exp(a - m_t)
        sc = jnp.einsum('bhtd,bhsd->bhts', qc, kc) * Dw
        num = inter[..., None] * jnp.einsum('bhtd,bhdv->bhtv', qc, C) + jnp.einsum('bhts,bhsv->bhtv', sc, vc)
        den = inter * jnp.einsum('bhtd,bhd->bht', qc, n) + jnp.sum(sc, axis=-1)
        h = num / jnp.maximum(jnp.abs(den), jnp.exp(-m_t))[..., None]
        bL = b[..., -1]
        w = bL[..., None] - b + li
        m_new = jnp.maximum(bL + m, jnp.max(w, axis=-1))
        wk = jnp.exp(w - m_new[..., None])
        decay = jnp.exp(bL + m - m_new)
        C_new = decay[..., None, None] * C + jnp.einsum('bhs,bhsd,bhsv->bhdv', wk, kc, vc)
        n_new = decay[..., None] * n + jnp.einsum('bhs,bhsd->bhd', wk, kc)
        return (C_new, n_new, m_new), h

    init = (jnp.zeros((B, NH, ML_DQK, ML_DV), jnp.float32),
            jnp.zeros((B, NH, ML_DQK), jnp.float32),
            jnp.zeros((B, NH), jnp.float32))
    _, h = lax.scan(step, init, (to_chunks(q), to_chunks(k), to_chunks(v),
                                 to_chunks(log_i), to_chunks(log_f)))
    h = jnp.moveaxis(jnp.swapaxes(h, 2, 3), 0, 1)
    return h.reshape(B, S, NH, ML_DV)


def _compress(x, pe, w1, w2):
    B, S, G, hd = x.shape
    seg = x.reshape(B, S // CMP_STRIDE, CMP_STRIDE, G, hd)
    blk = jnp.concatenate([seg[:, :-1], seg[:, 1:]], axis=2) + pe[:, None, :]
    blk = jnp.moveaxis(blk, 2, 3).reshape(B, -1, G, CMP_LEN * hd)
    return jax.nn.silu(blk @ w1) @ w2


def _nsa(q, kc, vc, ks, vs, kwin, vwin, g, q_norm_w, k_norm_w,
         pe_k, pe_v, k_w1, k_w2, v_w1, v_w2, rel_bias):
    B, S = q.shape[:2]
    G, Hg, hd = NSA_GROUPS, NSA_HPG, NSA_HD
    scale = hd ** -0.5
    t = jnp.arange(S)
    qg = _rmsnorm(q.reshape(B, S, G, Hg, hd), q_norm_w)

    kcmp = _rmsnorm(_compress(kc.reshape(B, S, G, hd), pe_k, k_w1, k_w2), k_norm_w[0])
    vcmp = _compress(vc.reshape(B, S, G, hd), pe_v, v_w1, v_w2)
    n_cmp = S // CMP_STRIDE - 1
    cmp_start = jnp.arange(n_cmp) * CMP_STRIDE
    dist_c = t[:, None] - (cmp_start + CMP_LEN - 1)[None, :]
    bias_c = rel_bias[_rel_bucket(dist_c)].reshape(S, n_cmp, G, Hg).transpose(2, 3, 0, 1)
    sc = jnp.einsum('bsghd,bcgd->bghsc', qg, kcmp).astype(jnp.float32) * scale + bias_c.astype(jnp.float32)
    p_c = _masked_softmax(sc, dist_c >= 0)
    o_c = jnp.einsum('bghsc,bcgd->bsghd', p_c.astype(vcmp.dtype), vcmp)

    n_sel = S // SEL_LEN
    sel_start = jnp.arange(n_sel) * SEL_LEN
    overlap = ((cmp_start[:, None] < sel_start[None, :] + SEL_LEN)
               & (cmp_start[:, None] + CMP_LEN > sel_start[None, :])).astype(jnp.float32)
    imp = jnp.einsum('bghsc,cn->bgsn', p_c, overlap)
    cur = (t // SEL_LEN)[:, None]
    blk_i = jnp.arange(n_sel)[None, :]
    forced = (blk_i == 0) | (blk_i == cur) | (blk_i == cur - 1)
    imp = jnp.where(forced, FORCE_SCORE, imp)
    imp = jnp.where(blk_i > cur, NEG_INF, imp)
    k_top = min(SEL_TOPK, n_sel)
    _, idx = lax.top_k(imp, k_top)

    ks_b = _rmsnorm(ks.reshape(B, n_sel, SEL_LEN, G, hd), k_norm_w[1]).transpose(0, 3, 1, 2, 4)
    vs_b = vs.reshape(B, n_sel, SEL_LEN, G, hd).transpose(0, 3, 1, 2, 4)
    nqb = S // SEL_QBLOCK
    q_blocks = jnp.moveaxis(qg.reshape(B, nqb, SEL_QBLOCK, G, Hg, hd), 1, 0)
    idx_blocks = idx.reshape(B, G, nqb, SEL_QBLOCK, k_top).transpose(2, 0, 1, 3, 4)
    t_blocks = t.reshape(nqb, SEL_QBLOCK)
    tbl = rel_bias.reshape(REL_BUCKETS, G, Hg).transpose(1, 0, 2)
    bi = jnp.arange(B)[:, None, None, None]
    gi = jnp.arange(G)[None, :, None, None]
    n_keys = k_top * SEL_LEN

    def sel_block(args):
        qb, ib, tb = args
        kg = ks_b[bi, gi, ib].reshape(B, G, SEL_QBLOCK, n_keys, hd)
        vg = vs_b[bi, gi, ib].reshape(B, G, SEL_QBLOCK, n_keys, hd)
        pos = (ib[..., None] * SEL_LEN + jnp.arange(SEL_LEN)).reshape(B, G, SEL_QBLOCK, n_keys)
        dist = tb[:, None] - pos
        bias = jnp.moveaxis(tbl[gi, _rel_bucket(dist)], -1, 2)
        s = jnp.einsum('bqghd,bgqkd->bghqk', qb, kg).astype(jnp.float32) * scale + bias.astype(jnp.float32)
        pr = _masked_softmax(s, (dist >= 0)[:, :, None])
        return jnp.einsum('bghqk,bgqkd->bqghd', pr.astype(vg.dtype), vg)

    o_s = lax.map(sel_block, (q_blocks, idx_blocks, t_blocks))
    o_s = jnp.moveaxis(o_s, 0, 1).reshape(B, S, G, Hg, hd)

    nwb = S // WIN_QBLOCK
    n_back = WIN // WIN_QBLOCK

    def band(a):
        ap = jnp.pad(a, ((0, 0), (WIN, 0), (0, 0), (0, 0))).reshape(B, nwb + n_back, WIN_QBLOCK, G, hd)
        return jnp.concatenate([ap[:, j:j + nwb] for j in range(n_back + 1)], axis=2)

    kwb = band(_rmsnorm(kwin.reshape(B, S, G, hd), k_norm_w[2]))
    vwb = band(vwin.reshape(B, S, G, hd))
    qw = qg.reshape(B, nwb, WIN_QBLOCK, G, Hg, hd)
    kw_len = (n_back + 1) * WIN_QBLOCK
    qi = jnp.arange(WIN_QBLOCK)
    ki = jnp.arange(kw_len)
    dist_w = qi[:, None] + WIN - ki[None, :]
    k_abs = jnp.arange(nwb)[:, None, None] * WIN_QBLOCK - WIN + ki[None, None, :]
    mask_w = (dist_w >= 0) & (dist_w < WIN) & (k_abs >= 0)
    bias_w = rel_bias[_rel_bucket(dist_w)].reshape(WIN_QBLOCK, kw_len, G, Hg).transpose(2, 3, 0, 1)
    s = jnp.einsum('bnqghd,bnkgd->bnghqk', qw, kwb).astype(jnp.float32) * scale + bias_w.astype(jnp.float32)
    pr = _masked_softmax(s, mask_w[None, :, None, None])
    o_w = jnp.einsum('bnghqk,bnkgd->bnqghd', pr.astype(vwb.dtype), vwb).reshape(B, S, G, Hg, hd)

    gs = jax.nn.sigmoid(g.reshape(B, S, G, Hg, N_BRANCH))
    o = gs[..., 0:1] * o_c + gs[..., 1:2] * o_s + gs[..., 2:3] * o_w
    return o.reshape(B, S, NSA_WIDTH)


def setup_inputs(seed: int = 0) -> dict:
    key = jax.random.key(seed)
    ks = jax.random.split(key, 20)
    L = DEPTH

    def nrm(k, shape, s):
        return s * jax.random.normal(k, shape, jnp.float32)

    return {
        'x': nrm(ks[0], (BATCH, SEQ, D_MODEL), 1.0),
        'p': nrm(ks[1], (DEPTH, BATCH, SEQ, PLE_DIM), 1.0),
        'norm_w': 1.0 + nrm(ks[2], (L, D_MODEL), 0.05),
        'w_in': nrm(ks[3], (L, D_MODEL, IN_WIDTH), D_MODEL ** -0.5),
        'ml_conv_w': nrm(ks[4], (L, ML_CONV, 2 * ML_QK_WIDTH), ML_CONV ** -0.5),
        'ml_i_bias': nrm(ks[5], (L, ML_HEADS), 0.1),
        'ml_f_bias': 3.0 + nrm(ks[6], (L, ML_HEADS), 0.5),
        'ml_head_norm_w': 1.0 + nrm(ks[7], (L, ML_HEADS, ML_DV), 0.05),
        'nsa_q_norm_w': 1.0 + nrm(ks[8], (L, NSA_HD), 0.05),
        'nsa_k_norm_w': 1.0 + nrm(ks[9], (L, N_BRANCH, NSA_HD), 0.05),
        'cmp_pe_k': nrm(ks[10], (L, CMP_LEN, NSA_HD), 0.1),
        'cmp_pe_v': nrm(ks[11], (L, CMP_LEN, NSA_HD), 0.1),
        'cmp_k_w1': nrm(ks[12], (L, CMP_LEN * NSA_HD, CMP_HIDDEN), (CMP_LEN * NSA_HD) ** -0.5),
        'cmp_k_w2': nrm(ks[13], (L, CMP_HIDDEN, NSA_HD), CMP_HIDDEN ** -0.5),
        'cmp_v_w1': nrm(ks[14], (L, CMP_LEN * NSA_HD, CMP_HIDDEN), (CMP_LEN * NSA_HD) ** -0.5),
        'cmp_v_w2': nrm(ks[15], (L, CMP_HIDDEN, NSA_HD), CMP_HIDDEN ** -0.5),
        'rel_bias': nrm(ks[16], (REL_BUCKETS, NSA_HEADS), 0.5),
        'w_out': nrm(ks[17], (L, MIX_WIDTH, D_MODEL), MIX_WIDTH ** -0.5),
        'ple_proj': nrm(ks[18], (L, PLE_DIM, D_MODEL), PLE_DIM ** -0.5),
        'ple_gate': nrm(ks[19], (L, D_MODEL, D_MODEL), D_MODEL ** -0.5),
    }


def reference(x, p, norm_w, w_in, ml_conv_w, ml_i_bias, ml_f_bias, ml_head_norm_w,
              nsa_q_norm_w, nsa_k_norm_w, cmp_pe_k, cmp_pe_v, cmp_k_w1, cmp_k_w2,
              cmp_v_w1, cmp_v_w2, rel_bias, w_out, ple_proj, ple_gate):
    B, S, _ = x.shape
    for layer in range(DEPTH):
        h = _rmsnorm(x, norm_w[layer])
        u = h @ w_in[layer]
        (ml_q, ml_k, ml_v, ml_o, ml_z, ml_i, ml_f,
         ns_q, ns_kc, ns_vc, ns_ks, ns_vs, ns_kw, ns_vw, ns_g, ns_z) = _split(u, IN_SPLITS)

        qk = jax.nn.silu(_causal_conv(jnp.concatenate([ml_q, ml_k], axis=-1), ml_conv_w[layer]))
        mq, mk = jnp.split(qk, 2, axis=-1)
        mq = mq.reshape(B, S, ML_HEADS, ML_DQK)
        mk = mk.reshape(B, S, ML_HEADS, ML_DQK) * (ML_DQK ** -0.5)
        mv = ml_v.reshape(B, S, ML_HEADS, ML_DV)
        log_i = (ml_i + ml_i_bias[layer]).astype(jnp.float32)
        log_f = jax.nn.log_sigmoid((ml_f + ml_f_bias[layer]).astype(jnp.float32))
        hm = _rmsnorm(_mlstm(mq, mk, mv, log_i, log_f).astype(x.dtype), ml_head_norm_w[layer])
        y_ml = hm.reshape(B, S, ML_WIDTH) * jax.nn.sigmoid(ml_o) * jax.nn.silu(ml_z)

        y_ns = _nsa(ns_q, ns_kc, ns_vc, ns_ks, ns_vs, ns_kw, ns_vw, ns_g,
                    nsa_q_norm_w[layer], nsa_k_norm_w[layer], cmp_pe_k[layer], cmp_pe_v[layer],
                    cmp_k_w1[layer], cmp_k_w2[layer], cmp_v_w1[layer], cmp_v_w2[layer],
                    rel_bias) * jax.nn.silu(ns_z)

        x = x + jnp.concatenate([y_ml, y_ns], axis=-1) @ w_out[layer]
        x = x + jax.nn.sigmoid(x @ ple_gate[layer]) * (p[layer] @ ple_proj[layer])
    return x
```

```python
import functools
import math

import numpy as np
import jax
import jax.numpy as jnp
from jax import lax
from jax.experimental import pallas as pl
from jax.experimental.pallas import tpu as pltpu

F32 = jnp.float32
BF16 = jnp.bfloat16
HIGHEST = lax.Precision.HIGHEST

ML_HEADS = 8
ML_DQK = 128
ML_DV = 256
ML_CHUNK = 64
ML_CONV = 4
NSA_HEADS = 16
NSA_HD = 128
NSA_GROUPS = 4
NSA_HPG = 4
N_BRANCH = 3
CMP_STRIDE = 16
CMP_LEN = 32
SEL_LEN = 64
SEL_TOPK = 16
WIN = 512
REL_BUCKETS = 32
REL_MAX_DIST = 128
EPS = 1e-6
NEG = -1e30
FORCE_SCORE = 1e4

LANES = 128
TQ = 128
VMEM_LIMIT = 48 * 1024 * 1024


def _bucket_thresholds():
    n = np.arange(0, 4 * REL_MAX_DIST, dtype=np.int64)
    max_exact = REL_BUCKETS // 2
    nf = np.maximum(n, 1).astype(np.float32)
    large = max_exact + (np.log(nf / np.float32(max_exact)) / np.float32(math.log(REL_MAX_DIST / max_exact))
                         * np.float32(REL_BUCKETS - max_exact)).astype(np.int32)
    large = np.minimum(large, REL_BUCKETS - 1)
    bucket = np.where(n < max_exact, n, large)
    assert np.all(np.diff(bucket) >= 0)
    thr = [int(np.argmax(bucket >= b)) for b in range(REL_BUCKETS)]
    assert thr[REL_BUCKETS - 1] <= REL_MAX_DIST
    return thr


BUCKET_THR = _bucket_thresholds()


def _dot(a, b, precision=None):
    return jnp.dot(a, b, preferred_element_type=F32, precision=precision)


def _dot_nt(a, b, precision=None):
    return lax.dot_general(a, b, (((1,), (1,)), ((), ())), preferred_element_type=F32, precision=precision)


def _sigmoid(x):
    return 1.0 / (1.0 + jnp.exp(-x))


def _silu(x):
    return x * _sigmoid(x)


def _iota(shape, dim):
    return lax.broadcasted_iota(jnp.int32, shape, dim)


def _rel_bias_pattern(dist, tbl_ref, head):
    val = jnp.full(dist.shape, tbl_ref[0, head], F32)
    for b in range(1, REL_BUCKETS):
        val = jnp.where(dist >= BUCKET_THR[b], tbl_ref[b, head], val)
    return val - tbl_ref[REL_BUCKETS - 1, head]


def _inproj_kernel(x_ref, nw_ref, w_ref, ws_ref, u_ref, g_ref, h_ref, *, row_chunk):
    j = pl.program_id(1)

    @pl.when(j == 0)
    def _():
        def body(c, carry):
            rows = pl.ds(pl.multiple_of(c * row_chunk, row_chunk), row_chunk)
            x = x_ref[rows, :]
            ms = jnp.mean(x * x, axis=-1, keepdims=True)
            h_ref[rows, :] = (x * lax.rsqrt(ms + EPS) * nw_ref[...]).astype(BF16)
            return carry
        lax.fori_loop(0, x_ref.shape[0] // row_chunk, body, 0)
        g_ref[...] = _dot(h_ref[...], ws_ref[...])

    u_ref[...] = _dot(h_ref[...], w_ref[...])


def _inproj(x2, norm_w, w_main, w_small, tm=512, tn=512):
    T, D = x2.shape
    N = w_main.shape[1]
    NS = w_small.shape[1]
    return pl.pallas_call(
        functools.partial(_inproj_kernel, row_chunk=64),
        grid=(T // tm, N // tn),
        in_specs=[
            pl.BlockSpec((tm, D), lambda i, j: (i, 0)),
            pl.BlockSpec((1, D), lambda i, j: (0, 0)),
            pl.BlockSpec((D, tn), lambda i, j: (0, j)),
            pl.BlockSpec((D, NS), lambda i, j: (0, 0)),
        ],
        out_specs=[
            pl.BlockSpec((tm, tn), lambda i, j: (i, j)),
            pl.BlockSpec((tm, NS), lambda i, j: (i, 0)),
        ],
        out_shape=[jax.ShapeDtypeStruct((T, N), F32), jax.ShapeDtypeStruct((T, NS), F32)],
        scratch_shapes=[pltpu.VMEM((tm, D), BF16)],
        compiler_params=pltpu.CompilerParams(dimension_semantics=("arbitrary", "arbitrary"),
                                             vmem_limit_bytes=VMEM_LIMIT),
        name="inproj",
    )(x2, norm_w, w_main, w_small)


def _mlstm_kernel(q_ref, k_ref, v_ref, o_ref, z_ref, g_ref, cw_ref, ib_ref, fb_ref, hw_ref, y_ref,
                  xbuf, qa, ka, c_ref, m_ref, *, lb):
    L = ML_CHUNK
    qkw = ML_HEADS * ML_DQK

    @pl.when(pl.program_id(1) == 0)
    def _():
        xbuf[0:8, :] = jnp.zeros((8, 2 * qkw), F32)
        c_ref[...] = jnp.zeros(c_ref.shape, F32)
        m_ref[...] = jnp.zeros(m_ref.shape, F32)

    xbuf[8:8 + lb, 0:qkw] = q_ref[...]
    xbuf[8:8 + lb, qkw:2 * qkw] = k_ref[...]
    acc = cw_ref[ML_CONV - 1:ML_CONV, :] * xbuf[8:8 + lb, :]
    for j in range(1, ML_CONV):
        acc = acc + cw_ref[ML_CONV - 1 - j:ML_CONV - j, :] * xbuf[8 - j:8 - j + lb, :]
    act = _silu(acc)
    qa[...] = act[:, 0:qkw].astype(BF16)
    ka[...] = act[:, qkw:2 * qkw] * (ML_DQK ** -0.5)
    xbuf[0:8, :] = xbuf[lb:lb + 8, :]

    row = _iota((L, L), 0)
    col = _iota((L, L), 1)
    tril = col <= row
    rowg = _iota((L, LANES), 0)
    eye_h = (_iota((8, LANES), 0) == _iota((8, LANES), 1)).astype(F32)
    eye_k = (_iota((ML_DQK, ML_DQK), 0) == _iota((ML_DQK, ML_DQK), 1)).astype(BF16)
    ones_col = (_iota((L, LANES), 1) == 0).astype(BF16)

    def chunk(c, carry):
        rows = pl.ds(pl.multiple_of(c * L, L), L)
        G = g_ref[rows, :]
        li = G[:, 0:LANES] + ib_ref[...]
        fp = G[:, LANES:2 * LANES] + fb_ref[...]
        lf = jnp.minimum(fp, 0.0) - jnp.log(1.0 + jnp.exp(-jnp.abs(fp)))
        b = lf
        for sh in (1, 2, 4, 8, 16, 32):
            b = b + jnp.where(rowg >= sh, pltpu.roll(b, sh, 0), 0.0)
        g_rows = _dot_nt(eye_h, li - b, precision=HIGHEST)

        for h in range(ML_HEADS):
            li_c = li[:, h:h + 1]
            b_c = b[:, h:h + 1]
            m = m_ref[h:h + 1, 0:1]
            D = jnp.where(tril, b_c + g_rows[h:h + 1, :], NEG)
            a_c = b_c + m
            m_t = jnp.maximum(a_c, jnp.max(D, axis=-1, keepdims=True))
            Dw = jnp.exp(D - m_t)
            inter = jnp.exp(a_c - m_t)
            q = qa[rows, h * ML_DQK:(h + 1) * ML_DQK]
            kf = ka[rows, h * ML_DQK:(h + 1) * ML_DQK]
            sc = (_dot_nt(q, kf.astype(BF16)) * Dw).astype(BF16)
            vaug = jnp.concatenate([v_ref[rows, h * ML_DV:(h + 1) * ML_DV].astype(BF16), ones_col], axis=1)
            cst = c_ref[h]
            num_aug = inter * _dot(q, cst.astype(BF16)) + _dot(sc, vaug)
            num = num_aug[:, 0:ML_DV]
            den = num_aug[:, ML_DV:ML_DV + 1]
            hh = num / jnp.maximum(jnp.abs(den), jnp.exp(-m_t))

            bL = b_c[L - 1:L, :]
            w_c = bL - b_c + li_c
            m_new = jnp.maximum(bL + m, jnp.max(w_c, axis=0, keepdims=True))
            wk = jnp.exp(w_c - m_new)
            decay = jnp.exp(bL + m - m_new)
            kw_t = _dot_nt(eye_k, (wk * kf).astype(BF16)).astype(BF16)
            c_ref[h] = decay * cst + _dot(kw_t, vaug)
            m_ref[h:h + 1, :] = jnp.broadcast_to(m_new, (1, LANES))

            ms = jnp.mean(hh * hh, axis=-1, keepdims=True)
            hn = hh * lax.rsqrt(ms + EPS) * hw_ref[h:h + 1, :]
            og = _sigmoid(o_ref[rows, h * ML_DV:(h + 1) * ML_DV])
            zg = _silu(z_ref[rows, h * ML_DV:(h + 1) * ML_DV])
            y_ref[rows, h * ML_DV:(h + 1) * ML_DV] = (hn * og * zg).astype(BF16)
        return carry

    lax.fori_loop(0, lb // L, chunk, 0)


def _mlstm(u, gates, conv_w, ib, fb, head_w, B, S, lb=256):
    T = B * S
    nsb = S // lb
    qkw = ML_HEADS * ML_DQK
    vw = ML_HEADS * ML_DV
    rowmap = lambda col: (lambda b, s: (b * nsb + s, col))
    const = lambda b, s: (0, 0)
    return pl.pallas_call(
        functools.partial(_mlstm_kernel, lb=lb),
        grid=(B, nsb),
        in_specs=[
            pl.BlockSpec((lb, qkw), rowmap(0)),
            pl.BlockSpec((lb, qkw), rowmap(1)),
            pl.BlockSpec((lb, vw), rowmap(1)),
            pl.BlockSpec((lb, vw), rowmap(2)),
            pl.BlockSpec((lb, vw), rowmap(3)),
            pl.BlockSpec((lb, 2 * LANES), rowmap(0)),
            pl.BlockSpec((ML_CONV, 2 * qkw), const),
            pl.BlockSpec((1, LANES), const),
            pl.BlockSpec((1, LANES), const),
            pl.BlockSpec((ML_HEADS, ML_DV), const),
        ],
        out_specs=pl.BlockSpec((lb, vw), rowmap(0)),
        out_shape=jax.ShapeDtypeStruct((T, vw), BF16),
        scratch_shapes=[
            pltpu.VMEM((lb + 8, 2 * qkw), F32),
            pltpu.VMEM((lb, qkw), BF16),
            pltpu.VMEM((lb, qkw), F32),
            pltpu.VMEM((ML_HEADS, ML_DQK, ML_DV + LANES), F32),
            pltpu.VMEM((ML_HEADS, LANES), F32),
        ],
        compiler_params=pltpu.CompilerParams(dimension_semantics=("arbitrary", "arbitrary"),
                                             vmem_limit_bytes=VMEM_LIMIT),
        name="mlstm",
    )(u, u, u, u, u, gates, conv_w, ib, fb, head_w)


def _rms_heads(x, w, scale=1.0):
    outs = []
    for h in range(x.shape[1] // NSA_HD):
        xh = x[:, h * NSA_HD:(h + 1) * NSA_HD]
        ms = jnp.mean(xh * xh, axis=-1, keepdims=True)
        outs.append(xh * lax.rsqrt(ms + EPS) * w * scale)
    return jnp.concatenate(outs, axis=1)


def _nsa_prep_kernel(q_ref, ks_ref, vs_ref, kw_ref, vw_ref, qw_ref, kn_ref,
                     qn_ref, ksn_ref, vsb_ref, kwn_ref, vwb_ref):
    qn_ref[...] = _rms_heads(q_ref[...], qw_ref[...], NSA_HD ** -0.5).astype(BF16)
    ksn_ref[...] = _rms_heads(ks_ref[...], kn_ref[1:2, :]).astype(BF16)
    kwn_ref[...] = _rms_heads(kw_ref[...], kn_ref[2:3, :]).astype(BF16)
    vsb_ref[...] = vs_ref[...].astype(BF16)
    vwb_ref[...] = vw_ref[...].astype(BF16)


def _nsa_prep(u, q_norm_w, k_norm_w, col_q, col_kv, rb=256):
    T = u.shape[0]
    qw = NSA_HEADS * NSA_HD
    kvw = NSA_GROUPS * NSA_HD
    cq = col_q // qw
    ck = col_kv // kvw
    kv_spec = lambda idx: pl.BlockSpec((rb, kvw), lambda i: (i, ck + idx))
    out_kv = pl.BlockSpec((rb, kvw), lambda i: (i, 0))
    return pl.pallas_call(
        _nsa_prep_kernel,
        grid=(T // rb,),
        in_specs=[
            pl.BlockSpec((rb, qw), lambda i: (i, cq)),
            kv_spec(2), kv_spec(3), kv_spec(4), kv_spec(5),
            pl.BlockSpec((1, NSA_HD), lambda i: (0, 0)),
            pl.BlockSpec((N_BRANCH, NSA_HD), lambda i: (0, 0)),
        ],
        out_specs=[pl.BlockSpec((rb, qw), lambda i: (i, 0)), out_kv, out_kv, out_kv, out_kv],
        out_shape=[jax.ShapeDtypeStruct((T, qw), BF16)] + [jax.ShapeDtypeStruct((T, kvw), BF16)] * 4,
        compiler_params=pltpu.CompilerParams(dimension_semantics=("arbitrary",), vmem_limit_bytes=VMEM_LIMIT),
        name="nsa_prep",
    )(u, u, u, u, u, q_norm_w, k_norm_w)


CMP_PAD = 16


def _compress_kernel(kc_ref, vc_ref, pek_ref, pev_ref, kw1_ref, kw2_ref, vw1_ref, vw2_ref, kn_ref,
                     kcmp_ref, vcmp_ref, *, ns):
    def mlp(x_ref, pe_ref, w1_ref, w2_ref):
        hid = CMP_LEN // 2
        a = jnp.zeros((ns, w1_ref.shape[1]), F32)
        bsum = jnp.zeros((ns, w1_ref.shape[1]), F32)
        for l in range(hid):
            xl = x_ref[pl.ds(l, ns, stride=CMP_STRIDE), :]
            a = a + _dot((xl + pe_ref[l:l + 1, :]).astype(BF16), w1_ref[l * NSA_HD:(l + 1) * NSA_HD, :])
            bsum = bsum + _dot((xl + pe_ref[hid + l:hid + l + 1, :]).astype(BF16),
                               w1_ref[(hid + l) * NSA_HD:(hid + l + 1) * NSA_HD, :])
        pre = a + pltpu.roll(bsum, ns - 1, 0)
        return _dot(_silu(pre).astype(BF16), w2_ref[...])

    kc = mlp(kc_ref, pek_ref, kw1_ref, kw2_ref)
    ms = jnp.mean(kc * kc, axis=-1, keepdims=True)
    kc = kc * lax.rsqrt(ms + EPS) * kn_ref[0:1, :]
    vc = mlp(vc_ref, pev_ref, vw1_ref, vw2_ref)
    zpad = jnp.zeros((CMP_PAD, NSA_HD), F32)
    kcmp_ref[...] = jnp.concatenate([zpad, kc, zpad], axis=0)
    vcmp_ref[...] = jnp.concatenate([zpad, vc, zpad], axis=0)


def _compress(u, pe_k, pe_v, kw1, kw2, vw1, vw2, k_norm_w, col_kv, B, S):
    ns = S // CMP_STRIDE
    nsp = ns + 2 * CMP_PAD
    ck = col_kv // NSA_HD
    hidden = kw1.shape[1]
    const = lambda b, g: (0, 0)
    out_spec = pl.BlockSpec((None, None, nsp, NSA_HD), lambda b, g: (b, g, 0, 0))
    return pl.pallas_call(
        functools.partial(_compress_kernel, ns=ns),
        grid=(B, NSA_GROUPS),
        in_specs=[
            pl.BlockSpec((S, NSA_HD), lambda b, g: (b, ck + g)),
            pl.BlockSpec((S, NSA_HD), lambda b, g: (b, ck + NSA_GROUPS + g)),
            pl.BlockSpec((CMP_LEN, NSA_HD), const),
            pl.BlockSpec((CMP_LEN, NSA_HD), const),
            pl.BlockSpec((CMP_LEN * NSA_HD, hidden), const),
            pl.BlockSpec((hidden, NSA_HD), const),
            pl.BlockSpec((CMP_LEN * NSA_HD, hidden), const),
            pl.BlockSpec((hidden, NSA_HD), const),
            pl.BlockSpec((N_BRANCH, NSA_HD), const),
        ],
        out_specs=[out_spec, out_spec],
        out_shape=[jax.ShapeDtypeStruct((B, NSA_GROUPS, nsp, NSA_HD), F32)] * 2,
        compiler_params=pltpu.CompilerParams(dimension_semantics=("arbitrary", "arbitrary"),
                                             vmem_limit_bytes=VMEM_LIMIT),
        name="compress",
    )(u, u, pe_k, pe_v, kw1, kw2, vw1, vw2, k_norm_w)


NEAR = 32


def _stack_heads(q4):
    return jnp.concatenate([q4[:, h * NSA_HD:(h + 1) * NSA_HD] for h in range(NSA_HPG)], axis=0)


def _cmp_sel_kernel(tbl_ref, q_ref, kc_ref, vc_ref, oc_ref, sel_ref, pat_ref, *, ns, nsel):
    g = pl.program_id(1)
    i = pl.program_id(2)
    t0 = i * TQ
    c0 = pl.multiple_of(i * (TQ // CMP_STRIDE), TQ // CMP_STRIDE)
    R = NSA_HPG * TQ

    d_near = _iota((TQ, NEAR), 0) - CMP_STRIDE * (_iota((TQ, NEAR), 1) - CMP_PAD) - (CMP_LEN - 1)

    @pl.when(i == 0)
    def _():
        for h in range(NSA_HPG):
            pat_ref[h * TQ:(h + 1) * TQ, :] = _rel_bias_pattern(d_near, tbl_ref, g * NSA_HPG + h)

    q = _stack_heads(q_ref[...])
    k_far = kc_ref[CMP_PAD:CMP_PAD + ns, :]
    v_far = vc_ref[CMP_PAD:CMP_PAD + ns, :]
    k_near = kc_ref[pl.ds(c0, NEAR), :]
    v_near = vc_ref[pl.ds(c0, NEAR), :]

    rloc = jnp.bitwise_and(_iota((R, ns), 0), TQ - 1)
    d_far = t0 + rloc - CMP_STRIDE * _iota((R, ns), 1) - (CMP_LEN - 1)
    mask_far = d_far >= REL_MAX_DIST
    rloc_n = jnp.bitwise_and(_iota((R, NEAR), 0), TQ - 1)
    u_n = _iota((R, NEAR), 1)
    d_n = rloc_n - CMP_STRIDE * (u_n - CMP_PAD) - (CMP_LEN - 1)
    mask_near = (d_n >= 0) & (d_n < REL_MAX_DIST) & (c0 - CMP_PAD + u_n >= 0)

    s_far = jnp.where(mask_far, _dot_nt(q, k_far.astype(BF16)), NEG)
    s_near = jnp.where(mask_near, _dot_nt(q, k_near.astype(BF16)) + pat_ref[...], NEG)
    m = jnp.maximum(jnp.max(s_far, axis=-1, keepdims=True), jnp.max(s_near, axis=-1, keepdims=True))
    e_far = jnp.where(mask_far, jnp.exp(s_far - m), 0.0)
    e_near = jnp.where(mask_near, jnp.exp(s_near - m), 0.0)
    l = jnp.sum(e_far, axis=-1, keepdims=True) + jnp.sum(e_near, axis=-1, keepdims=True)
    inv = jnp.where(l > 0.0, 1.0 / jnp.where(l > 0.0, l, 1.0), 0.0)
    p_far = e_far * inv
    p_near = e_near * inv
    o = _dot(p_far.astype(BF16), v_far.astype(BF16)) + _dot(p_near.astype(BF16), v_near.astype(BF16))
    for h in range(NSA_HPG):
        oc_ref[:, h * NSA_HD:(h + 1) * NSA_HD] = o[h * TQ:(h + 1) * TQ, :]

    ps_far = p_far[0:TQ]
    ps_near = p_near[0:TQ]
    for h in range(1, NSA_HPG):
        ps_far = ps_far + p_far[h * TQ:(h + 1) * TQ]
        ps_near = ps_near + p_near[h * TQ:(h + 1) * TQ]
    ratio = SEL_LEN // CMP_STRIDE

    def overlap(n, c):
        return ((c >= ratio * n - (CMP_LEN // CMP_STRIDE - 1)) & (c <= ratio * n + ratio - 1)).astype(F32)

    ov_far = overlap(_iota((nsel, ns), 0), _iota((nsel, ns), 1))
    ov_near = overlap(_iota((nsel, NEAR), 0), c0 - CMP_PAD + _iota((nsel, NEAR), 1))
    imp = _dot_nt(ov_far, ps_far, precision=HIGHEST) + _dot_nt(ov_near, ps_near, precision=HIGHEST)

    blk = _iota((nsel, TQ), 0)
    cur = (t0 + _iota((nsel, TQ), 1)) // SEL_LEN
    imp = jnp.where((blk == 0) | (blk == cur) | (blk == cur - 1), FORCE_SCORE, imp)
    imp = jnp.where(blk > cur, NEG, imp)
    rank = jnp.zeros((nsel, TQ), jnp.int32)
    for jb in range(nsel):
        vj = imp[jb:jb + 1, :]
        ahead = (vj > imp) | ((vj == imp) & (blk > jb))
        rank = rank + jnp.where(ahead, 1, 0)
    sel_t = (rank < min(SEL_TOPK, nsel)).astype(BF16)
    eye_q = (_iota((TQ, TQ), 0) == _iota((TQ, TQ), 1)).astype(BF16)
    sel_ref[...] = _dot_nt(eye_q, sel_t).astype(BF16)


def _cmp_sel(rel_bias, qn, kcmp, vcmp, B, S):
    ns = S // CMP_STRIDE
    nsp = ns + 2 * CMP_PAD
    nsel = S // SEL_LEN
    nt = S // TQ
    gw = NSA_HPG * NSA_HD
    cmp_spec = pl.BlockSpec((None, None, nsp, NSA_HD), lambda b, g, i: (b, g, 0, 0))
    return pl.pallas_call(
        functools.partial(_cmp_sel_kernel, ns=ns, nsel=nsel),
        grid=(B, NSA_GROUPS, nt),
        in_specs=[
            pl.BlockSpec(memory_space=pltpu.SMEM),
            pl.BlockSpec((TQ, gw), lambda b, g, i: (b * nt + i, g)),
            cmp_spec, cmp_spec,
        ],
        out_specs=[
            pl.BlockSpec((TQ, gw), lambda b, g, i: (b * nt + i, g)),
            pl.BlockSpec((None, None, TQ, nsel), lambda b, g, i: (b, g, i, 0)),
        ],
        out_shape=[jax.ShapeDtypeStruct((B * S, NSA_HEADS * NSA_HD), F32),
                   jax.ShapeDtypeStruct((B, NSA_GROUPS, S, nsel), BF16)],
        scratch_shapes=[pltpu.VMEM((NSA_HPG * TQ, NEAR), F32)],
        compiler_params=pltpu.CompilerParams(dimension_semantics=("arbitrary", "arbitrary", "arbitrary"),
                                             vmem_limit_bytes=VMEM_LIMIT),
        name="cmp_sel",
    )(rel_bias, qn, kcmp, vcmp)


def _attend_kernel(tbl_ref, q_ref, ks_ref, vs_ref, kw_ref, vw_ref, sel_ref, oc_ref, g_ref, z_ref, y_ref,
                   pd_ref, pp_ref, *, nsel, gate_col):
    g = pl.program_id(1)
    i = pl.program_id(2)
    R = NSA_HPG * TQ
    n_back = WIN // TQ

    qrow = _iota((TQ, TQ), 0)
    kcol = _iota((TQ, TQ), 1)

    @pl.when(i == 0)
    def _():
        for h in range(NSA_HPG):
            head = g * NSA_HPG + h
            pd_ref[h * TQ:(h + 1) * TQ, :] = _rel_bias_pattern(qrow - kcol, tbl_ref, head)
            pp_ref[h * TQ:(h + 1) * TQ, :] = _rel_bias_pattern(qrow - kcol + TQ, tbl_ref, head)

    q = _stack_heads(q_ref[...])
    causal = kcol <= qrow

    def stack4(x):
        return jnp.concatenate([x] * NSA_HPG, axis=0)

    def update(carry, s, v):
        m, l, acc = carry
        m_new = jnp.maximum(m, jnp.max(s, axis=-1, keepdims=True))
        alpha = jnp.exp(m - m_new)
        p = jnp.exp(s - m_new)
        l = alpha * l + jnp.sum(p, axis=-1, keepdims=True)
        acc = alpha * acc + _dot(p.astype(BF16), v)
        return m_new, l, acc

    def key_rows(j):
        return pl.ds(pl.multiple_of(j * TQ, TQ), TQ)

    init = (jnp.full((R, 1), NEG, F32), jnp.zeros((R, 1), F32), jnp.zeros((R, NSA_HD), F32))

    sel = sel_ref[...]
    blk_n = _iota((nsel, TQ), 0)
    blk_k = _iota((nsel, TQ), 1) // SEL_LEN

    def sel_neg(j, extra=None):
        expand = (blk_n == j * (TQ // SEL_LEN) + blk_k).astype(BF16)
        keep = _dot(sel, expand) > 0.5
        if extra is not None:
            keep = keep & extra
        return stack4(jnp.where(keep, 0.0, NEG))

    s = _dot_nt(q, ks_ref[key_rows(i), :]) + pd_ref[...] + sel_neg(i, causal)
    carry = update(init, s, vs_ref[key_rows(i), :])
    jp = jnp.maximum(i - 1, 0)
    s = _dot_nt(q, ks_ref[key_rows(jp), :]) + pp_ref[...] + sel_neg(jp, jnp.broadcast_to(i >= 1, (TQ, TQ)))
    carry = update(carry, s, vs_ref[key_rows(jp), :])

    def far(j, carry):
        s = _dot_nt(q, ks_ref[key_rows(j), :]) + sel_neg(j)
        return update(carry, s, vs_ref[key_rows(j), :])

    m_s, l_s, acc_s = lax.fori_loop(0, jnp.maximum(i - 1, 0), far, carry)
    o_s = acc_s / l_s

    s = _dot_nt(q, kw_ref[key_rows(i), :]) + pd_ref[...] + stack4(jnp.where(causal, 0.0, NEG))
    carry = update(init, s, vw_ref[key_rows(i), :])
    for back in range(1, n_back + 1):
        jb = jnp.maximum(i - back, 0)
        ok = jnp.broadcast_to(i >= back, (TQ, TQ))
        if back == n_back:
            ok = ok & (kcol > qrow)
        s = _dot_nt(q, kw_ref[key_rows(jb), :]) + stack4(jnp.where(ok, 0.0, NEG))
        if back == 1:
            s = s + pp_ref[...]
        carry = update(carry, s, vw_ref[key_rows(jb), :])
    m_w, l_w, acc_w = carry
    o_w = acc_w / l_w

    gs = _sigmoid(g_ref[:, 0:LANES])
    lane = _iota((TQ, LANES), 1)
    for h in range(NSA_HPG):
        base = gate_col + (g * NSA_HPG + h) * N_BRANCH
        gate = [jnp.sum(jnp.where(lane == base + br, gs, 0.0), axis=-1, keepdims=True) for br in range(N_BRANCH)]
        cols = slice(h * NSA_HD, (h + 1) * NSA_HD)
        rows = slice(h * TQ, (h + 1) * TQ)
        o = gate[0] * oc_ref[:, cols] + gate[1] * o_s[rows] + gate[2] * o_w[rows]
        y_ref[:, cols] = (o * _silu(z_ref[:, cols])).astype(BF16)


def _attend(rel_bias, qn, ksn, vsb, kwn, vwb, sel, o_c, gates, u, col_z, gate_col, B, S):
    nt = S // TQ
    nsel = S // SEL_LEN
    gw = NSA_HPG * NSA_HD
    cz = col_z // gw
    tile = lambda b, g, i: (b * nt + i, g)
    kv_spec = pl.BlockSpec((S, NSA_HD), lambda b, g, i: (b, g))
    return pl.pallas_call(
        functools.partial(_attend_kernel, nsel=nsel, gate_col=gate_col),
        grid=(B, NSA_GROUPS, nt),
        in_specs=[
            pl.BlockSpec(memory_space=pltpu.SMEM),
            pl.BlockSpec((TQ, gw), tile),
            kv_spec, kv_spec, kv_spec, kv_spec,
            pl.BlockSpec((None, None, TQ, nsel), lambda b, g, i: (b, g, i, 0)),
            pl.BlockSpec((TQ, gw), tile),
            pl.BlockSpec((TQ, 2 * LANES), lambda b, g, i: (b * nt + i, 0)),
            pl.BlockSpec((TQ, gw), lambda b, g, i: (b * nt + i, cz + g)),
        ],
        out_specs=pl.BlockSpec((TQ, gw), tile),
        out_shape=jax.ShapeDtypeStruct((B * S, NSA_HEADS * NSA_HD), BF16),
        scratch_shapes=[pltpu.VMEM((NSA_HPG * TQ, TQ), F32), pltpu.VMEM((NSA_HPG * TQ, TQ), F32)],
        compiler_params=pltpu.CompilerParams(dimension_semantics=("arbitrary", "arbitrary", "arbitrary"),
                                             vmem_limit_bytes=VMEM_LIMIT),
        name="attend",
    )(rel_bias, qn, ksn, vsb, kwn, vwb, sel, o_c, gates, u)


def _outproj_kernel(yml_ref, yns_ref, w_ref, x_ref, o_ref):
    half = yml_ref.shape[1]
    o_ref[...] = x_ref[...] + _dot(yml_ref[...], w_ref[0:half, :]) + _dot(yns_ref[...], w_ref[half:2 * half, :])


def _outproj(y_ml, y_ns, w_out, x2, tm=512, tn=512):
    T, D = x2.shape
    half = y_ml.shape[1]
    return pl.pallas_call(
        _outproj_kernel,
        grid=(T // tm, D // tn),
        in_specs=[
            pl.BlockSpec((tm, half), lambda i, j: (i, 0)),
            pl.BlockSpec((tm, half), lambda i, j: (i, 0)),
            pl.BlockSpec((2 * half, tn), lambda i, j: (0, j)),
            pl.BlockSpec((tm, tn), lambda i, j: (i, j)),
        ],
        out_specs=pl.BlockSpec((tm, tn), lambda i, j: (i, j)),
        out_shape=jax.ShapeDtypeStruct((T, D), F32),
        compiler_params=pltpu.CompilerParams(dimension_semantics=("arbitrary", "arbitrary"),
                                             vmem_limit_bytes=VMEM_LIMIT),
        name="outproj",
    )(y_ml, y_ns, w_out, x2)


def _ple_kernel(x_ref, p_ref, wg_ref, wp_ref, o_ref, xb_ref, pb_ref, *, tn):
    j = pl.program_id(1)

    @pl.when(j == 0)
    def _():
        xb_ref[...] = x_ref[...].astype(BF16)
        pb_ref[...] = p_ref[...].astype(BF16)

    gate = _sigmoid(_dot(xb_ref[...], wg_ref[...]))
    emb = _dot(pb_ref[...], wp_ref[...])
    o_ref[...] = x_ref[:, pl.ds(pl.multiple_of(j * tn, tn), tn)] + gate * emb


def _ple(x1, p2, wg, wp, tm=512, tn=512):
    T, D = x1.shape
    P = p2.shape[1]
    return pl.pallas_call(
        functools.partial(_ple_kernel, tn=tn),
        grid=(T // tm, D // tn),
        in_specs=[
            pl.BlockSpec((tm, D), lambda i, j: (i, 0)),
            pl.BlockSpec((tm, P), lambda i, j: (i, 0)),
            pl.BlockSpec((D, tn), lambda i, j: (0, j)),
            pl.BlockSpec((P, tn), lambda i, j: (0, j)),
        ],
        out_specs=pl.BlockSpec((tm, tn), lambda i, j: (i, j)),
        out_shape=jax.ShapeDtypeStruct((T, D), F32),
        scratch_shapes=[pltpu.VMEM((tm, D), BF16), pltpu.VMEM((tm, P), BF16)],
        compiler_params=pltpu.CompilerParams(dimension_semantics=("arbitrary", "arbitrary"),
                                             vmem_limit_bytes=VMEM_LIMIT),
        name="ple",
    )(x1, p2, wg, wp)


def _layer(x2, p2, norm_w, w_in, conv_w, i_bias, f_bias, head_norm_w, q_norm_w, k_norm_w,
           pe_k, pe_v, kw1, kw2, vw1, vw2, rel_bias, w_out, ple_proj, ple_gate, B, S):
    D = x2.shape[1]
    qkw = ML_HEADS * ML_DQK
    vw = ML_HEADS * ML_DV
    nq = NSA_HEADS * NSA_HD
    nkv = NSA_GROUPS * NSA_HD
    o_i = 2 * qkw + 3 * vw
    o_f = o_i + ML_HEADS
    o_nq = o_f + ML_HEADS
    o_g = o_nq + nq + 6 * nkv
    o_z = o_g + NSA_HEADS * N_BRANCH
    w_main = jnp.concatenate([w_in[:, :o_i], w_in[:, o_nq:o_g], w_in[:, o_z:]], axis=1).astype(BF16)
    ngate = NSA_HEADS * N_BRANCH
    zeros = lambda n: jnp.zeros((D, n), w_in.dtype)
    w_small = jnp.concatenate([w_in[:, o_i:o_f], w_in[:, o_g:o_z], zeros(LANES - ML_HEADS - ngate),
                               w_in[:, o_f:o_nq], zeros(LANES - ML_HEADS)], axis=1).astype(BF16)
    col_nq = o_i
    col_kv = col_nq + nq
    col_z = col_kv + 6 * nkv

    u, gates = _inproj(x2, norm_w.reshape(1, D), w_main, w_small)

    pad_h = lambda v: jnp.concatenate([v, jnp.zeros((LANES - ML_HEADS,), v.dtype)]).reshape(1, LANES)
    y_ml = _mlstm(u, gates, conv_w, pad_h(i_bias), pad_h(f_bias), head_norm_w, B, S)

    qn, ksn, vsb, kwn, vwb = _nsa_prep(u, q_norm_w.reshape(1, NSA_HD), k_norm_w, col_nq, col_kv)
    kcmp, vcmp = _compress(u, pe_k, pe_v, kw1.astype(BF16), kw2.astype(BF16), vw1.astype(BF16), vw2.astype(BF16),
                           k_norm_w, col_kv, B, S)
    o_c, sel = _cmp_sel(rel_bias, qn, kcmp, vcmp, B, S)
    y_ns = _attend(rel_bias, qn, ksn, vsb, kwn, vwb, sel, o_c, gates, u, col_z, ML_HEADS, B, S)

    x1 = _outproj(y_ml, y_ns, w_out.astype(BF16), x2)
    return _ple(x1, p2, ple_gate.astype(BF16), ple_proj.astype(BF16))


def kernel(x, p, norm_w, w_in, ml_conv_w, ml_i_bias, ml_f_bias, ml_head_norm_w, nsa_q_norm_w, nsa_k_norm_w,
           cmp_pe_k, cmp_pe_v, cmp_k_w1, cmp_k_w2, cmp_v_w1, cmp_v_w2, rel_bias, w_out, ple_proj, ple_gate):
    B, S, D = x.shape
    assert S % max(WIN, 256) == 0 and S // SEL_LEN >= 1
    x2 = x.reshape(B * S, D)
    for layer in range(w_in.shape[0]):
        x2 = _layer(x2, p[layer].reshape(B * S, -1), norm_w[layer], w_in[layer], ml_conv_w[layer],
                    ml_i_bias[layer], ml_f_bias[layer], ml_head_norm_w[layer], nsa_q_norm_w[layer],
                    nsa_k_norm_w[layer], cmp_pe_k[layer], cmp_pe_v[layer], cmp_k_w1[layer], cmp_k_w2[layer],
                    cmp_v_w1[layer], cmp_v_w2[layer], rel_bias, w_out[layer], ple_proj[layer], ple_gate[layer],
                    B, S)
    return x2.reshape(B, S, D)
```

```python
import functools
import math

import numpy as np
import jax
import jax.numpy as jnp
from jax import lax
from jax.experimental import pallas as pl
from jax.experimental.pallas import tpu as pltpu

F32 = jnp.float32
BF16 = jnp.bfloat16
HIGHEST = lax.Precision.HIGHEST

ML_HEADS = 8
ML_DQK = 128
ML_DV = 256
ML_CHUNK = 64
ML_CONV = 4
NSA_HEADS = 16
NSA_HD = 128
NSA_GROUPS = 4
NSA_HPG = 4
N_BRANCH = 3
CMP_STRIDE = 16
CMP_LEN = 32
SEL_LEN = 64
SEL_TOPK = 16
WIN = 512
REL_BUCKETS = 32
REL_MAX_DIST = 128
EPS = 1e-6
NEG = -1e30
FORCE_SCORE = 1e4

LANES = 128
TQ = 128
VMEM_LIMIT = 48 * 1024 * 1024


def _bucket_thresholds():
    n = np.arange(0, 4 * REL_MAX_DIST, dtype=np.int64)
    max_exact = REL_BUCKETS // 2
    nf = np.maximum(n, 1).astype(np.float32)
    large = max_exact + (np.log(nf / np.float32(max_exact)) / np.float32(math.log(REL_MAX_DIST / max_exact))
                         * np.float32(REL_BUCKETS - max_exact)).astype(np.int32)
    large = np.minimum(large, REL_BUCKETS - 1)
    bucket = np.where(n < max_exact, n, large)
    assert np.all(np.diff(bucket) >= 0)
    thr = [int(np.argmax(bucket >= b)) for b in range(REL_BUCKETS)]
    assert thr[REL_BUCKETS - 1] <= REL_MAX_DIST
    return thr


BUCKET_THR = _bucket_thresholds()


def _dot(a, b, precision=None):
    return jnp.dot(a, b, preferred_element_type=F32, precision=precision)


def _dot_nt(a, b, precision=None):
    return lax.dot_general(a, b, (((1,), (1,)), ((), ())), preferred_element_type=F32, precision=precision)


def _sigmoid(x):
    return 1.0 / (1.0 + jnp.exp(-x))


def _silu(x):
    return x * _sigmoid(x)


def _iota(shape, dim):
    return lax.broadcasted_iota(jnp.int32, shape, dim)


def _rel_bias_pattern(dist, tbl_ref, head):
    val = jnp.full(dist.shape, tbl_ref[0, head], F32)
    for b in range(1, REL_BUCKETS):
        val = jnp.where(dist >= BUCKET_THR[b], tbl_ref[b, head], val)
    return val - tbl_ref[REL_BUCKETS - 1, head]


def _inproj_kernel(x_ref, nw_ref, w_ref, ws_ref, u_ref, g_ref, h_ref, *, row_chunk):
    j = pl.program_id(1)

    @pl.when(j == 0)
    def _():
        def body(c, carry):
            rows = pl.ds(pl.multiple_of(c * row_chunk, row_chunk), row_chunk)
            x = x_ref[rows, :]
            ms = jnp.mean(x * x, axis=-1, keepdims=True)
            h_ref[rows, :] = (x * lax.rsqrt(ms + EPS) * nw_ref[...]).astype(BF16)
            return carry
        lax.fori_loop(0, x_ref.shape[0] // row_chunk, body, 0)
        g_ref[...] = _dot(h_ref[...], ws_ref[...])

    u_ref[...] = _dot(h_ref[...], w_ref[...])


def _inproj(x2, norm_w, w_main, w_small, tm=512, tn=512):
    T, D = x2.shape
    N = w_main.shape[1]
    NS = w_small.shape[1]
    return pl.pallas_call(
        functools.partial(_inproj_kernel, row_chunk=64),
        grid=(T // tm, N // tn),
        in_specs=[
            pl.BlockSpec((tm, D), lambda i, j: (i, 0)),
            pl.BlockSpec((1, D), lambda i, j: (0, 0)),
            pl.BlockSpec((D, tn), lambda i, j: (0, j)),
            pl.BlockSpec((D, NS), lambda i, j: (0, 0)),
        ],
        out_specs=[
            pl.BlockSpec((tm, tn), lambda i, j: (i, j)),
            pl.BlockSpec((tm, NS), lambda i, j: (i, 0)),
        ],
        out_shape=[jax.ShapeDtypeStruct((T, N), F32), jax.ShapeDtypeStruct((T, NS), F32)],
        scratch_shapes=[pltpu.VMEM((tm, D), BF16)],
        compiler_params=pltpu.CompilerParams(dimension_semantics=("arbitrary", "arbitrary"),
                                             vmem_limit_bytes=VMEM_LIMIT),
        name="inproj",
    )(x2, norm_w, w_main, w_small)


def _mlstm_kernel(q_ref, k_ref, v_ref, o_ref, z_ref, g_ref, cw_ref, ib_ref, fb_ref, hw_ref, y_ref,
                  xbuf, qa, ka, c_ref, m_ref, *, lb):
    L = ML_CHUNK
    qkw = ML_HEADS * ML_DQK

    @pl.when(pl.program_id(1) == 0)
    def _():
        xbuf[0:8, :] = jnp.zeros((8, 2 * qkw), F32)
        c_ref[...] = jnp.zeros(c_ref.shape, F32)
        m_ref[...] = jnp.zeros(m_ref.shape, F32)

    xbuf[8:8 + lb, 0:qkw] = q_ref[...]
    xbuf[8:8 + lb, qkw:2 * qkw] = k_ref[...]
    acc = cw_ref[ML_CONV - 1:ML_CONV, :] * xbuf[8:8 + lb, :]
    for j in range(1, ML_CONV):
        acc = acc + cw_ref[ML_CONV - 1 - j:ML_CONV - j, :] * xbuf[8 - j:8 - j + lb, :]
    act = _silu(acc)
    qa[...] = act[:, 0:qkw].astype(BF16)
    ka[...] = act[:, qkw:2 * qkw] * (ML_DQK ** -0.5)
    xbuf[0:8, :] = xbuf[lb:lb + 8, :]

    row = _iota((L, L), 0)
    col = _iota((L, L), 1)
    tril = col <= row
    rowg = _iota((L, LANES), 0)
    eye_h = (_iota((8, LANES), 0) == _iota((8, LANES), 1)).astype(F32)
    eye_k = (_iota((ML_DQK, ML_DQK), 0) == _iota((ML_DQK, ML_DQK), 1)).astype(BF16)
    ones_col = (_iota((L, LANES), 1) == 0).astype(BF16)

    def chunk(c, carry):
        rows = pl.ds(pl.multiple_of(c * L, L), L)
        G = g_ref[rows, :]
        li = G[:, 0:LANES] + ib_ref[...]
        fp = G[:, LANES:2 * LANES] + fb_ref[...]
        lf = jnp.minimum(fp, 0.0) - jnp.log(1.0 + jnp.exp(-jnp.abs(fp)))
        b = lf
        for sh in (1, 2, 4, 8, 16, 32):
            b = b + jnp.where(rowg >= sh, pltpu.roll(b, sh, 0), 0.0)
        g_rows = _dot_nt(eye_h, li - b, precision=HIGHEST)

        for h in range(ML_HEADS):
            li_c = li[:, h:h + 1]
            b_c = b[:, h:h + 1]
            m = m_ref[h:h + 1, 0:1]
            D = jnp.where(tril, b_c + g_rows[h:h + 1, :], NEG)
            a_c = b_c + m
            m_t = jnp.maximum(a_c, jnp.max(D, axis=-1, keepdims=True))
            Dw = jnp.exp(D - m_t)
            inter = jnp.exp(a_c - m_t)
            q = qa[rows, h * ML_DQK:(h + 1) * ML_DQK]
            kf = ka[rows, h * ML_DQK:(h + 1) * ML_DQK]
            sc = (_dot_nt(q, kf.astype(BF16)) * Dw).astype(BF16)
            vaug = jnp.concatenate([v_ref[rows, h * ML_DV:(h + 1) * ML_DV].astype(BF16), ones_col], axis=1)
            cst = c_ref[h]
            num_aug = inter * _dot(q, cst.astype(BF16)) + _dot(sc, vaug)
            num = num_aug[:, 0:ML_DV]
            den = num_aug[:, ML_DV:ML_DV + 1]
            hh = num / jnp.maximum(jnp.abs(den), jnp.exp(-m_t))

            bL = b_c[L - 1:L, :]
            w_c = bL - b_c + li_c
            m_new = jnp.maximum(bL + m, jnp.max(w_c, axis=0, keepdims=True))
            wk = jnp.exp(w_c - m_new)
            decay = jnp.exp(bL + m - m_new)
            kw_t = _dot_nt(eye_k, (wk * kf).astype(BF16)).astype(BF16)
            c_ref[h] = decay * cst + _dot(kw_t, vaug)
            m_ref[h:h + 1, :] = jnp.broadcast_to(m_new, (1, LANES))

            ms = jnp.mean(hh * hh, axis=-1, keepdims=True)
            hn = hh * lax.rsqrt(ms + EPS) * hw_ref[h:h + 1, :]
            og = _sigmoid(o_ref[rows, h * ML_DV:(h + 1) * ML_DV])
            zg = _silu(z_ref[rows, h * ML_DV:(h + 1) * ML_DV])
            y_ref[rows, h * ML_DV:(h + 1) * ML_DV] = (hn * og * zg).astype(BF16)
        return carry

    lax.fori_loop(0, lb // L, chunk, 0)


def _mlstm(u, gates, conv_w, ib, fb, head_w, B, S, lb=256):
    T = B * S
    nsb = S // lb
    qkw = ML_HEADS * ML_DQK
    vw = ML_HEADS * ML_DV
    rowmap = lambda col: (lambda b, s: (b * nsb + s, col))
    const = lambda b, s: (0, 0)
    return pl.pallas_call(
        functools.partial(_mlstm_kernel, lb=lb),
        grid=(B, nsb),
        in_specs=[
            pl.BlockSpec((lb, qkw), rowmap(0)),
            pl.BlockSpec((lb, qkw), rowmap(1)),
            pl.BlockSpec((lb, vw), rowmap(1)),
            pl.BlockSpec((lb, vw), rowmap(2)),
            pl.BlockSpec((lb, vw), rowmap(3)),
            pl.BlockSpec((lb, 2 * LANES), rowmap(0)),
            pl.BlockSpec((ML_CONV, 2 * qkw), const),
            pl.BlockSpec((1, LANES), const),
            pl.BlockSpec((1, LANES), const),
            pl.BlockSpec((ML_HEADS, ML_DV), const),
        ],
        out_specs=pl.BlockSpec((lb, vw), rowmap(0)),
        out_shape=jax.ShapeDtypeStruct((T, vw), BF16),
        scratch_shapes=[
            pltpu.VMEM((lb + 8, 2 * qkw), F32),
            pltpu.VMEM((lb, qkw), BF16),
            pltpu.VMEM((lb, qkw), F32),
            pltpu.VMEM((ML_HEADS, ML_DQK, ML_DV + LANES), F32),
            pltpu.VMEM((ML_HEADS, LANES), F32),
        ],
        compiler_params=pltpu.CompilerParams(dimension_semantics=("arbitrary", "arbitrary"),
                                             vmem_limit_bytes=VMEM_LIMIT),
        name="mlstm",
    )(u, u, u, u, u, gates, conv_w, ib, fb, head_w)


def _rms_heads(x, w, scale=1.0):
    outs = []
    for h in range(x.shape[1] // NSA_HD):
        xh = x[:, h * NSA_HD:(h + 1) * NSA_HD]
        ms = jnp.mean(xh * xh, axis=-1, keepdims=True)
        outs.append(xh * lax.rsqrt(ms + EPS) * w * scale)
    return jnp.concatenate(outs, axis=1)


def _nsa_prep_kernel(q_ref, ks_ref, vs_ref, kw_ref, vw_ref, qw_ref, kn_ref,
                     qn_ref, ksn_ref, vst_ref, kwn_ref, vwt_ref):
    qn_ref[...] = _rms_heads(q_ref[...], qw_ref[...], NSA_HD ** -0.5).astype(BF16)
    ksn_ref[...] = _rms_heads(ks_ref[...], kn_ref[1:2, :]).astype(BF16)
    kwn_ref[...] = _rms_heads(kw_ref[...], kn_ref[2:3, :]).astype(BF16)
    for g in range(NSA_GROUPS):
        cols = slice(g * NSA_HD, (g + 1) * NSA_HD)
        vst_ref[g] = vs_ref[:, cols].T.astype(BF16)
        vwt_ref[g] = vw_ref[:, cols].T.astype(BF16)


def _nsa_prep(u, q_norm_w, k_norm_w, col_q, col_kv, B, S, rb=256):
    T = u.shape[0]
    nsb = S // rb
    qw = NSA_HEADS * NSA_HD
    kvw = NSA_GROUPS * NSA_HD
    cq = col_q // qw
    ck = col_kv // kvw
    kv_spec = lambda idx: pl.BlockSpec((rb, kvw), lambda i: (i, ck + idx))
    out_k = pl.BlockSpec((rb, kvw), lambda i: (i, 0))
    out_vt = pl.BlockSpec((None, NSA_GROUPS, NSA_HD, rb), lambda i: (i // nsb, 0, 0, i % nsb))
    vt_shape = jax.ShapeDtypeStruct((B, NSA_GROUPS, NSA_HD, S), BF16)
    return pl.pallas_call(
        _nsa_prep_kernel,
        grid=(T // rb,),
        in_specs=[
            pl.BlockSpec((rb, qw), lambda i: (i, cq)),
            kv_spec(2), kv_spec(3), kv_spec(4), kv_spec(5),
            pl.BlockSpec((1, NSA_HD), lambda i: (0, 0)),
            pl.BlockSpec((N_BRANCH, NSA_HD), lambda i: (0, 0)),
        ],
        out_specs=[pl.BlockSpec((rb, qw), lambda i: (i, 0)), out_k, out_vt, out_k, out_vt],
        out_shape=[jax.ShapeDtypeStruct((T, qw), BF16), jax.ShapeDtypeStruct((T, kvw), BF16), vt_shape,
                   jax.ShapeDtypeStruct((T, kvw), BF16), vt_shape],
        compiler_params=pltpu.CompilerParams(dimension_semantics=("arbitrary",), vmem_limit_bytes=VMEM_LIMIT),
        name="nsa_prep",
    )(u, u, u, u, u, q_norm_w, k_norm_w)


CMP_PAD = 16


def _compress_kernel(kc_ref, vc_ref, pek_ref, pev_ref, kw1_ref, kw2_ref, vw1_ref, vw2_ref, kn_ref,
                     kcmp_ref, vcmp_ref, *, ns):
    def mlp(x_ref, pe_ref, w1_ref, w2_ref):
        hid = CMP_LEN // 2
        a = jnp.zeros((ns, w1_ref.shape[1]), F32)
        bsum = jnp.zeros((ns, w1_ref.shape[1]), F32)
        for l in range(hid):
            xl = x_ref[pl.ds(l, ns, stride=CMP_STRIDE), :]
            a = a + _dot((xl + pe_ref[l:l + 1, :]).astype(BF16), w1_ref[l * NSA_HD:(l + 1) * NSA_HD, :])
            bsum = bsum + _dot((xl + pe_ref[hid + l:hid + l + 1, :]).astype(BF16),
                               w1_ref[(hid + l) * NSA_HD:(hid + l + 1) * NSA_HD, :])
        pre = a + pltpu.roll(bsum, ns - 1, 0)
        return _dot(_silu(pre).astype(BF16), w2_ref[...])

    kc = mlp(kc_ref, pek_ref, kw1_ref, kw2_ref)
    ms = jnp.mean(kc * kc, axis=-1, keepdims=True)
    kc = kc * lax.rsqrt(ms + EPS) * kn_ref[0:1, :]
    vc = mlp(vc_ref, pev_ref, vw1_ref, vw2_ref)
    zpad = jnp.zeros((CMP_PAD, NSA_HD), F32)
    kcmp_ref[...] = jnp.concatenate([zpad, kc, zpad], axis=0)
    vcmp_ref[...] = jnp.concatenate([zpad, vc, zpad], axis=0)


def _compress(u, pe_k, pe_v, kw1, kw2, vw1, vw2, k_norm_w, col_kv, B, S):
    ns = S // CMP_STRIDE
    nsp = ns + 2 * CMP_PAD
    ck = col_kv // NSA_HD
    hidden = kw1.shape[1]
    const = lambda b, g: (0, 0)
    out_spec = pl.BlockSpec((None, None, nsp, NSA_HD), lambda b, g: (b, g, 0, 0))
    return pl.pallas_call(
        functools.partial(_compress_kernel, ns=ns),
        grid=(B, NSA_GROUPS),
        in_specs=[
            pl.BlockSpec((S, NSA_HD), lambda b, g: (b, ck + g)),
            pl.BlockSpec((S, NSA_HD), lambda b, g: (b, ck + NSA_GROUPS + g)),
            pl.BlockSpec((CMP_LEN, NSA_HD), const),
            pl.BlockSpec((CMP_LEN, NSA_HD), const),
            pl.BlockSpec((CMP_LEN * NSA_HD, hidden), const),
            pl.BlockSpec((hidden, NSA_HD), const),
            pl.BlockSpec((CMP_LEN * NSA_HD, hidden), const),
            pl.BlockSpec((hidden, NSA_HD), const),
            pl.BlockSpec((N_BRANCH, NSA_HD), const),
        ],
        out_specs=[out_spec, out_spec],
        out_shape=[jax.ShapeDtypeStruct((B, NSA_GROUPS, nsp, NSA_HD), F32)] * 2,
        compiler_params=pltpu.CompilerParams(dimension_semantics=("arbitrary", "arbitrary"),
                                             vmem_limit_bytes=VMEM_LIMIT),
        name="compress",
    )(u, u, pe_k, pe_v, kw1, kw2, vw1, vw2, k_norm_w)


NEAR = 32


def _stack_heads(q4):
    return jnp.concatenate([q4[:, h * NSA_HD:(h + 1) * NSA_HD] for h in range(NSA_HPG)], axis=0)


def _cmp_sel_kernel(tbl_ref, q_ref, kc_ref, vc_ref, oc_ref, sel_ref, pat_ref, *, ns, nsel):
    g = pl.program_id(1)
    i = pl.program_id(2)
    t0 = i * TQ
    c0 = pl.multiple_of(i * (TQ // CMP_STRIDE), TQ // CMP_STRIDE)
    R = NSA_HPG * TQ

    d_near = _iota((TQ, NEAR), 0) - CMP_STRIDE * (_iota((TQ, NEAR), 1) - CMP_PAD) - (CMP_LEN - 1)

    @pl.when(i == 0)
    def _():
        for h in range(NSA_HPG):
            pat_ref[h * TQ:(h + 1) * TQ, :] = _rel_bias_pattern(d_near, tbl_ref, g * NSA_HPG + h)

    q = _stack_heads(q_ref[...])
    k_far = kc_ref[CMP_PAD:CMP_PAD + ns, :]
    v_far = vc_ref[CMP_PAD:CMP_PAD + ns, :]
    k_near = kc_ref[pl.ds(c0, NEAR), :]
    v_near = vc_ref[pl.ds(c0, NEAR), :]

    rloc = jnp.bitwise_and(_iota((R, ns), 0), TQ - 1)
    d_far = t0 + rloc - CMP_STRIDE * _iota((R, ns), 1) - (CMP_LEN - 1)
    mask_far = d_far >= REL_MAX_DIST
    rloc_n = jnp.bitwise_and(_iota((R, NEAR), 0), TQ - 1)
    u_n = _iota((R, NEAR), 1)
    d_n = rloc_n - CMP_STRIDE * (u_n - CMP_PAD) - (CMP_LEN - 1)
    mask_near = (d_n >= 0) & (d_n < REL_MAX_DIST) & (c0 - CMP_PAD + u_n >= 0)

    s_far = jnp.where(mask_far, _dot_nt(q, k_far.astype(BF16)), NEG)
    s_near = jnp.where(mask_near, _dot_nt(q, k_near.astype(BF16)) + pat_ref[...], NEG)
    m = jnp.maximum(jnp.max(s_far, axis=-1, keepdims=True), jnp.max(s_near, axis=-1, keepdims=True))
    e_far = jnp.where(mask_far, jnp.exp(s_far - m), 0.0)
    e_near = jnp.where(mask_near, jnp.exp(s_near - m), 0.0)
    l = jnp.sum(e_far, axis=-1, keepdims=True) + jnp.sum(e_near, axis=-1, keepdims=True)
    inv = jnp.where(l > 0.0, 1.0 / jnp.where(l > 0.0, l, 1.0), 0.0)
    p_far = e_far * inv
    p_near = e_near * inv
    o = _dot(p_far.astype(BF16), v_far.astype(BF16)) + _dot(p_near.astype(BF16), v_near.astype(BF16))
    for h in range(NSA_HPG):
        oc_ref[:, h * NSA_HD:(h + 1) * NSA_HD] = o[h * TQ:(h + 1) * TQ, :]

    ps_far = p_far[0:TQ]
    ps_near = p_near[0:TQ]
    for h in range(1, NSA_HPG):
        ps_far = ps_far + p_far[h * TQ:(h + 1) * TQ]
        ps_near = ps_near + p_near[h * TQ:(h + 1) * TQ]
    ratio = SEL_LEN // CMP_STRIDE

    def overlap(n, c):
        return ((c >= ratio * n - (CMP_LEN // CMP_STRIDE - 1)) & (c <= ratio * n + ratio - 1)).astype(F32)

    ov_far = overlap(_iota((nsel, ns), 0), _iota((nsel, ns), 1))
    ov_near = overlap(_iota((nsel, NEAR), 0), c0 - CMP_PAD + _iota((nsel, NEAR), 1))
    imp = _dot_nt(ov_far, ps_far, precision=HIGHEST) + _dot_nt(ov_near, ps_near, precision=HIGHEST)

    blk = _iota((nsel, TQ), 0)
    cur = (t0 + _iota((nsel, TQ), 1)) // SEL_LEN
    imp = jnp.where((blk == 0) | (blk == cur) | (blk == cur - 1), FORCE_SCORE, imp)
    imp = jnp.where(blk > cur, NEG, imp)
    rank = jnp.zeros((nsel, TQ), jnp.int32)
    for jb in range(nsel):
        vj = imp[jb:jb + 1, :]
        ahead = (vj > imp) | ((vj == imp) & (blk > jb))
        rank = rank + jnp.where(ahead, 1, 0)
    sel_ref[...] = jnp.where(rank < min(SEL_TOPK, nsel), 0.0, NEG)


def _cmp_sel(rel_bias, qn, kcmp, vcmp, B, S):
    ns = S // CMP_STRIDE
    nsp = ns + 2 * CMP_PAD
    nsel = S // SEL_LEN
    nt = S // TQ
    gw = NSA_HPG * NSA_HD
    cmp_spec = pl.BlockSpec((None, None, nsp, NSA_HD), lambda b, g, i: (b, g, 0, 0))
    return pl.pallas_call(
        functools.partial(_cmp_sel_kernel, ns=ns, nsel=nsel),
        grid=(B, NSA_GROUPS, nt),
        in_specs=[
            pl.BlockSpec(memory_space=pltpu.SMEM),
            pl.BlockSpec((TQ, gw), lambda b, g, i: (b * nt + i, g)),
            cmp_spec, cmp_spec,
        ],
        out_specs=[
            pl.BlockSpec((TQ, gw), lambda b, g, i: (b * nt + i, g)),
            pl.BlockSpec((None, None, nsel, TQ), lambda b, g, i: (b, g, 0, i)),
        ],
        out_shape=[jax.ShapeDtypeStruct((B * S, NSA_HEADS * NSA_HD), F32),
                   jax.ShapeDtypeStruct((B, NSA_GROUPS, nsel, S), F32)],
        scratch_shapes=[pltpu.VMEM((NSA_HPG * TQ, NEAR), F32)],
        compiler_params=pltpu.CompilerParams(dimension_semantics=("arbitrary", "arbitrary", "arbitrary"),
                                             vmem_limit_bytes=VMEM_LIMIT),
        name="cmp_sel",
    )(rel_bias, qn, kcmp, vcmp)


TA = 256


def _attend_kernel(tbl_ref, q_ref, ks_ref, vst_ref, kw_ref, vwt_ref, sel_ref, oc_ref, g_ref, z_ref, y_ref,
                   pd_ref, pp_ref, m_ref, l_ref, acc_ref, *, gate_col):
    g = pl.program_id(1)
    i = pl.program_id(2)
    W = NSA_HPG * TA
    blocks_per_tile = TA // SEL_LEN

    krow = _iota((TA, TA), 0)
    qcol = _iota((TA, TA), 1)

    @pl.when(i == 0)
    def _():
        for h in range(NSA_HPG):
            head = g * NSA_HPG + h
            lanes = slice(h * TA, (h + 1) * TA)
            pd_ref[:, lanes] = jnp.where(krow <= qcol, _rel_bias_pattern(qcol - krow, tbl_ref, head), NEG)
            pp_ref[:, lanes] = _rel_bias_pattern(qcol - krow + TA, tbl_ref, head)

    q = _stack_heads(q_ref[...])

    def lanes4(x):
        return jnp.concatenate([x] * NSA_HPG, axis=1)

    def key_rows(j):
        return pl.ds(pl.multiple_of(j * TA, TA), TA)

    def reset():
        m_ref[...] = jnp.full((1, W), NEG, F32)
        l_ref[...] = jnp.zeros((1, W), F32)
        acc_ref[...] = jnp.zeros((NSA_HD, W), F32)

    def update(s, vt):
        m_old = m_ref[...]
        m_new = jnp.maximum(m_old, jnp.max(s, axis=0, keepdims=True))
        alpha = jnp.exp(m_old - m_new)
        p = jnp.exp(s - m_new)
        l_ref[...] = alpha * l_ref[...] + jnp.sum(p, axis=0, keepdims=True)
        acc_ref[...] = alpha * acc_ref[...] + _dot(vt, p.astype(BF16))
        m_ref[...] = m_new

    def sel_scores(j):
        parts = []
        for bb in range(blocks_per_tile):
            k = ks_ref[pl.ds(pl.multiple_of(j * TA + bb * SEL_LEN, SEL_LEN), SEL_LEN), :]
            neg = sel_ref[pl.ds(j * blocks_per_tile + bb, 1), :]
            parts.append(_dot_nt(k, q) + lanes4(neg))
        return jnp.concatenate(parts, axis=0)

    reset()
    update(sel_scores(i) + pd_ref[...], vst_ref[:, key_rows(i)])

    @pl.when(i >= 1)
    def _():
        update(sel_scores(i - 1) + pp_ref[...], vst_ref[:, key_rows(i - 1)])

    def far(j, carry):
        update(sel_scores(j), vst_ref[:, key_rows(j)])
        return carry

    lax.fori_loop(0, jnp.maximum(i - 1, 0), far, 0)
    o_s = acc_ref[...] / l_ref[...]

    n_back = WIN // TA
    reset()
    update(_dot_nt(kw_ref[key_rows(i), :], q) + pd_ref[...], vwt_ref[:, key_rows(i)])
    for back in range(1, n_back + 1):
        @pl.when(i >= back)
        def _():
            s = _dot_nt(kw_ref[key_rows(i - back), :], q)
            if back == 1:
                s = s + pp_ref[...]
            if back == n_back:
                s = s + lanes4(jnp.where(krow > qcol, 0.0, NEG))
            update(s, vwt_ref[:, key_rows(i - back)])
    o_w = acc_ref[...] / l_ref[...]

    gs = _sigmoid(g_ref[:, 0:LANES])
    lane = _iota((TA, LANES), 1)
    for h in range(NSA_HPG):
        base = gate_col + (g * NSA_HPG + h) * N_BRANCH
        gate = [jnp.sum(jnp.where(lane == base + br, gs, 0.0), axis=-1, keepdims=True) for br in range(N_BRANCH)]
        cols = slice(h * NSA_HD, (h + 1) * NSA_HD)
        lanes = slice(h * TA, (h + 1) * TA)
        o = gate[0] * oc_ref[:, cols] + gate[1] * o_s[:, lanes].T + gate[2] * o_w[:, lanes].T
        y_ref[:, cols] = (o * _silu(z_ref[:, cols])).astype(BF16)


def _attend(rel_bias, qn, ksn, vst, kwn, vwt, sel, o_c, gates, u, col_z, gate_col, B, S):
    nt = S // TA
    nsel = S // SEL_LEN
    gw = NSA_HPG * NSA_HD
    cz = col_z // gw
    tile = lambda b, g, i: (b * nt + i, g)
    k_spec = pl.BlockSpec((S, NSA_HD), lambda b, g, i: (b, g))
    vt_spec = pl.BlockSpec((None, None, NSA_HD, S), lambda b, g, i: (b, g, 0, 0))
    return pl.pallas_call(
        functools.partial(_attend_kernel, gate_col=gate_col),
        grid=(B, NSA_GROUPS, nt),
        in_specs=[
            pl.BlockSpec(memory_space=pltpu.SMEM),
            pl.BlockSpec((TA, gw), tile),
            k_spec, vt_spec, k_spec, vt_spec,
            pl.BlockSpec((None, None, nsel, TA), lambda b, g, i: (b, g, 0, i)),
            pl.BlockSpec((TA, gw), tile),
            pl.BlockSpec((TA, 2 * LANES), lambda b, g, i: (b * nt + i, 0)),
            pl.BlockSpec((TA, gw), lambda b, g, i: (b * nt + i, cz + g)),
        ],
        out_specs=pl.BlockSpec((TA, gw), tile),
        out_shape=jax.ShapeDtypeStruct((B * S, NSA_HEADS * NSA_HD), BF16),
        scratch_shapes=[pltpu.VMEM((TA, NSA_HPG * TA), F32), pltpu.VMEM((TA, NSA_HPG * TA), F32),
                        pltpu.VMEM((1, NSA_HPG * TA), F32), pltpu.VMEM((1, NSA_HPG * TA), F32),
                        pltpu.VMEM((NSA_HD, NSA_HPG * TA), F32)],
        compiler_params=pltpu.CompilerParams(dimension_semantics=("arbitrary", "arbitrary", "arbitrary"),
                                             vmem_limit_bytes=VMEM_LIMIT),
        name="attend",
    )(rel_bias, qn, ksn, vst, kwn, vwt, sel, o_c, gates, u)


def _outproj_kernel(yml_ref, yns_ref, w_ref, x_ref, o_ref):
    half = yml_ref.shape[1]
    o_ref[...] = x_ref[...] + _dot(yml_ref[...], w_ref[0:half, :]) + _dot(yns_ref[...], w_ref[half:2 * half, :])


def _outproj(y_ml, y_ns, w_out, x2, tm=512, tn=512):
    T, D = x2.shape
    half = y_ml.shape[1]
    return pl.pallas_call(
        _outproj_kernel,
        grid=(T // tm, D // tn),
        in_specs=[
            pl.BlockSpec((tm, half), lambda i, j: (i, 0)),
            pl.BlockSpec((tm, half), lambda i, j: (i, 0)),
            pl.BlockSpec((2 * half, tn), lambda i, j: (0, j)),
            pl.BlockSpec((tm, tn), lambda i, j: (i, j)),
        ],
        out_specs=pl.BlockSpec((tm, tn), lambda i, j: (i, j)),
        out_shape=jax.ShapeDtypeStruct((T, D), F32),
        compiler_params=pltpu.CompilerParams(dimension_semantics=("arbitrary", "arbitrary"),
                                             vmem_limit_bytes=VMEM_LIMIT),
        name="outproj",
    )(y_ml, y_ns, w_out, x2)


def _ple_kernel(x_ref, p_ref, wg_ref, wp_ref, o_ref, xb_ref, pb_ref, *, tn):
    j = pl.program_id(1)

    @pl.when(j == 0)
    def _():
        xb_ref[...] = x_ref[...].astype(BF16)
        pb_ref[...] = p_ref[...].astype(BF16)

    gate = _sigmoid(_dot(xb_ref[...], wg_ref[...]))
    emb = _dot(pb_ref[...], wp_ref[...])
    o_ref[...] = x_ref[:, pl.ds(pl.multiple_of(j * tn, tn), tn)] + gate * emb


def _ple(x1, p2, wg, wp, tm=512, tn=512):
    T, D = x1.shape
    P = p2.shape[1]
    return pl.pallas_call(
        functools.partial(_ple_kernel, tn=tn),
        grid=(T // tm, D // tn),
        in_specs=[
            pl.BlockSpec((tm, D), lambda i, j: (i, 0)),
            pl.BlockSpec((tm, P), lambda i, j: (i, 0)),
            pl.BlockSpec((D, tn), lambda i, j: (0, j)),
            pl.BlockSpec((P, tn), lambda i, j: (0, j)),
        ],
        out_specs=pl.BlockSpec((tm, tn), lambda i, j: (i, j)),
        out_shape=jax.ShapeDtypeStruct((T, D), F32),
        scratch_shapes=[pltpu.VMEM((tm, D), BF16), pltpu.VMEM((tm, P), BF16)],
        compiler_params=pltpu.CompilerParams(dimension_semantics=("arbitrary", "arbitrary"),
                                             vmem_limit_bytes=VMEM_LIMIT),
        name="ple",
    )(x1, p2, wg, wp)


def _layer(x2, p2, norm_w, w_in, conv_w, i_bias, f_bias, head_norm_w, q_norm_w, k_norm_w,
           pe_k, pe_v, kw1, kw2, vw1, vw2, rel_bias, w_out, ple_proj, ple_gate, B, S):
    D = x2.shape[1]
    qkw = ML_HEADS * ML_DQK
    vw = ML_HEADS * ML_DV
    nq = NSA_HEADS * NSA_HD
    nkv = NSA_GROUPS * NSA_HD
    o_i = 2 * qkw + 3 * vw
    o_f = o_i + ML_HEADS
    o_nq = o_f + ML_HEADS
    o_g = o_nq + nq + 6 * nkv
    o_z = o_g + NSA_HEADS * N_BRANCH
    w_main = jnp.concatenate([w_in[:, :o_i], w_in[:, o_nq:o_g], w_in[:, o_z:]], axis=1).astype(BF16)
    ngate = NSA_HEADS * N_BRANCH
    zeros = lambda n: jnp.zeros((D, n), w_in.dtype)
    w_small = jnp.concatenate([w_in[:, o_i:o_f], w_in[:, o_g:o_z], zeros(LANES - ML_HEADS - ngate),
                               w_in[:, o_f:o_nq], zeros(LANES - ML_HEADS)], axis=1).astype(BF16)
    col_nq = o_i
    col_kv = col_nq + nq
    col_z = col_kv + 6 * nkv

    u, gates = _inproj(x2, norm_w.reshape(1, D), w_main, w_small)

    pad_h = lambda v: jnp.concatenate([v, jnp.zeros((LANES - ML_HEADS,), v.dtype)]).reshape(1, LANES)
    y_ml = _mlstm(u, gates, conv_w, pad_h(i_bias), pad_h(f_bias), head_norm_w, B, S)

    qn, ksn, vst, kwn, vwt = _nsa_prep(u, q_norm_w.reshape(1, NSA_HD), k_norm_w, col_nq, col_kv, B, S)
    kcmp, vcmp = _compress(u, pe_k, pe_v, kw1.astype(BF16), kw2.astype(BF16), vw1.astype(BF16), vw2.astype(BF16),
                           k_norm_w, col_kv, B, S)
    o_c, sel = _cmp_sel(rel_bias, qn, kcmp, vcmp, B, S)
    y_ns = _attend(rel_bias, qn, ksn, vst, kwn, vwt, sel, o_c, gates, u, col_z, ML_HEADS, B, S)

    x1 = _outproj(y_ml, y_ns, w_out.astype(BF16), x2)
    return _ple(x1, p2, ple_gate.astype(BF16), ple_proj.astype(BF16))


def kernel(x, p, norm_w, w_in, ml_conv_w, ml_i_bias, ml_f_bias, ml_head_norm_w, nsa_q_norm_w, nsa_k_norm_w,
           cmp_pe_k, cmp_pe_v, cmp_k_w1, cmp_k_w2, cmp_v_w1, cmp_v_w2, rel_bias, w_out, ple_proj, ple_gate):
    B, S, D = x.shape
    assert S % max(WIN, 256) == 0 and S // SEL_LEN >= 1
    x2 = x.reshape(B * S, D)
    for layer in range(w_in.shape[0]):
        x2 = _layer(x2, p[layer].reshape(B * S, -1), norm_w[layer], w_in[layer], ml_conv_w[layer],
                    ml_i_bias[layer], ml_f_bias[layer], ml_head_norm_w[layer], nsa_q_norm_w[layer],
                    nsa_k_norm_w[layer], cmp_pe_k[layer], cmp_pe_v[layer], cmp_k_w1[layer], cmp_k_w2[layer],
                    cmp_v_w1[layer], cmp_v_w2[layer], rel_bias, w_out[layer], ple_proj[layer], ple_gate[layer],
                    B, S)
    return x2.reshape(B, S, D)
```

```python
import functools
import math

import numpy as np
import jax
import jax.numpy as jnp
from jax import lax
from jax.experimental import pallas as pl
from jax.experimental.pallas import tpu as pltpu

F32 = jnp.float32
BF16 = jnp.bfloat16
HIGHEST = lax.Precision.HIGHEST

ML_HEADS = 8
ML_DQK = 128
ML_DV = 256
ML_CHUNK = 64
ML_CONV = 4
NSA_HEADS = 16
NSA_HD = 128
NSA_GROUPS = 4
NSA_HPG = 4
N_BRANCH = 3
CMP_STRIDE = 16
CMP_LEN = 32
SEL_LEN = 64
SEL_SHIFT = 6
SEL_TOPK = 16
WIN = 512
REL_BUCKETS = 32
REL_MAX_DIST = 128
EPS = 1e-6
NEG = -1e30
FORCE_SCORE = 1e4
LOG2E = math.log2(math.e)

LANES = 128
TQ = 128
VMEM_LIMIT = 48 * 1024 * 1024


def _bucket_thresholds():
    n = np.arange(0, 4 * REL_MAX_DIST, dtype=np.int64)
    max_exact = REL_BUCKETS // 2
    nf = np.maximum(n, 1).astype(np.float32)
    large = max_exact + (np.log(nf / np.float32(max_exact)) / np.float32(math.log(REL_MAX_DIST / max_exact))
                         * np.float32(REL_BUCKETS - max_exact)).astype(np.int32)
    large = np.minimum(large, REL_BUCKETS - 1)
    bucket = np.where(n < max_exact, n, large)
    assert np.all(np.diff(bucket) >= 0)
    thr = [int(np.argmax(bucket >= b)) for b in range(REL_BUCKETS)]
    assert thr[REL_BUCKETS - 1] <= REL_MAX_DIST
    return thr


BUCKET_THR = _bucket_thresholds()


def _dot(a, b, precision=None):
    return jnp.dot(a, b, preferred_element_type=F32, precision=precision)


def _dot_nt(a, b, precision=None):
    return lax.dot_general(a, b, (((1,), (1,)), ((), ())), preferred_element_type=F32, precision=precision)


def _sigmoid(x):
    return 1.0 / (1.0 + jnp.exp(-x))


def _silu(x):
    return x * _sigmoid(x)


def _iota(shape, dim):
    return lax.broadcasted_iota(jnp.int32, shape, dim)


def _rel_bias_pattern(dist, tbl_ref, head):
    val = jnp.full(dist.shape, tbl_ref[0, head], F32)
    for b in range(1, REL_BUCKETS):
        val = jnp.where(dist >= BUCKET_THR[b], tbl_ref[b, head], val)
    return (val - tbl_ref[REL_BUCKETS - 1, head]) * LOG2E


def _inproj_kernel(x_ref, nw_ref, w_ref, ws_ref, u_ref, g_ref, h_ref, *, row_chunk):
    j = pl.program_id(1)

    @pl.when(j == 0)
    def _():
        def body(c, carry):
            rows = pl.ds(pl.multiple_of(c * row_chunk, row_chunk), row_chunk)
            x = x_ref[rows, :]
            ms = jnp.mean(x * x, axis=-1, keepdims=True)
            h_ref[rows, :] = (x * lax.rsqrt(ms + EPS) * nw_ref[...]).astype(BF16)
            return carry
        lax.fori_loop(0, x_ref.shape[0] // row_chunk, body, 0)
        g_ref[...] = _dot(h_ref[...], ws_ref[...])

    u_ref[...] = _dot(h_ref[...], w_ref[...])


def _inproj(x2, norm_w, w_main, w_small, tm=512, tn=512):
    T, D = x2.shape
    N = w_main.shape[1]
    NS = w_small.shape[1]
    return pl.pallas_call(
        functools.partial(_inproj_kernel, row_chunk=64),
        grid=(T // tm, N // tn),
        in_specs=[
            pl.BlockSpec((tm, D), lambda i, j: (i, 0)),
            pl.BlockSpec((1, D), lambda i, j: (0, 0)),
            pl.BlockSpec((D, tn), lambda i, j: (0, j)),
            pl.BlockSpec((D, NS), lambda i, j: (0, 0)),
        ],
        out_specs=[
            pl.BlockSpec((tm, tn), lambda i, j: (i, j)),
            pl.BlockSpec((tm, NS), lambda i, j: (i, 0)),
        ],
        out_shape=[jax.ShapeDtypeStruct((T, N), F32), jax.ShapeDtypeStruct((T, NS), F32)],
        scratch_shapes=[pltpu.VMEM((tm, D), BF16)],
        compiler_params=pltpu.CompilerParams(dimension_semantics=("arbitrary", "arbitrary"),
                                             vmem_limit_bytes=VMEM_LIMIT),
        name="inproj",
    )(x2, norm_w, w_main, w_small)


def _mlstm_kernel(q_ref, k_ref, v_ref, o_ref, z_ref, g_ref, cw_ref, ib_ref, fb_ref, hw_ref, y_ref,
                  xbuf, qa, ka, c_ref, m_ref, *, lb):
    L = ML_CHUNK
    qkw = ML_HEADS * ML_DQK

    @pl.when(pl.program_id(1) == 0)
    def _():
        xbuf[0:8, :] = jnp.zeros((8, 2 * qkw), F32)
        c_ref[...] = jnp.zeros(c_ref.shape, F32)
        m_ref[...] = jnp.zeros(m_ref.shape, F32)

    xbuf[8:8 + lb, 0:qkw] = q_ref[...]
    xbuf[8:8 + lb, qkw:2 * qkw] = k_ref[...]
    acc = cw_ref[ML_CONV - 1:ML_CONV, :] * xbuf[8:8 + lb, :]
    for j in range(1, ML_CONV):
        acc = acc + cw_ref[ML_CONV - 1 - j:ML_CONV - j, :] * xbuf[8 - j:8 - j + lb, :]
    act = _silu(acc)
    qa[...] = act[:, 0:qkw].astype(BF16)
    ka[...] = act[:, qkw:2 * qkw] * (ML_DQK ** -0.5)
    xbuf[0:8, :] = xbuf[lb:lb + 8, :]

    row = _iota((L, L), 0)
    col = _iota((L, L), 1)
    tril = col <= row
    rowg = _iota((L, LANES), 0)
    eye_h = (_iota((8, LANES), 0) == _iota((8, LANES), 1)).astype(F32)
    eye_k = (_iota((ML_DQK, ML_DQK), 0) == _iota((ML_DQK, ML_DQK), 1)).astype(BF16)
    ones_col = (_iota((L, LANES), 1) == 0).astype(BF16)

    def chunk(c, carry):
        rows = pl.ds(pl.multiple_of(c * L, L), L)
        G = g_ref[rows, :]
        li = G[:, 0:LANES] + ib_ref[...]
        fp = G[:, LANES:2 * LANES] + fb_ref[...]
        lf = jnp.minimum(fp, 0.0) - jnp.log(1.0 + jnp.exp(-jnp.abs(fp)))
        b = lf
        for sh in (1, 2, 4, 8, 16, 32):
            b = b + jnp.where(rowg >= sh, pltpu.roll(b, sh, 0), 0.0)
        g_rows = _dot_nt(eye_h, li - b, precision=HIGHEST)

        for h in range(ML_HEADS):
            li_c = li[:, h:h + 1]
            b_c = b[:, h:h + 1]
            m = m_ref[h:h + 1, 0:1]
            D = jnp.where(tril, b_c + g_rows[h:h + 1, :], NEG)
            a_c = b_c + m
            m_t = jnp.maximum(a_c, jnp.max(D, axis=-1, keepdims=True))
            Dw = jnp.exp(D - m_t)
            inter = jnp.exp(a_c - m_t)
            q = qa[rows, h * ML_DQK:(h + 1) * ML_DQK]
            kf = ka[rows, h * ML_DQK:(h + 1) * ML_DQK]
            sc = (_dot_nt(q, kf.astype(BF16)) * Dw).astype(BF16)
            vaug = jnp.concatenate([v_ref[rows, h * ML_DV:(h + 1) * ML_DV].astype(BF16), ones_col], axis=1)
            cst = c_ref[h]
            num_aug = inter * _dot(q, cst.astype(BF16)) + _dot(sc, vaug)
            num = num_aug[:, 0:ML_DV]
            den = num_aug[:, ML_DV:ML_DV + 1]
            hh = num / jnp.maximum(jnp.abs(den), jnp.exp(-m_t))

            bL = b_c[L - 1:L, :]
            w_c = bL - b_c + li_c
            m_new = jnp.maximum(bL + m, jnp.max(w_c, axis=0, keepdims=True))
            wk = jnp.exp(w_c - m_new)
            decay = jnp.exp(bL + m - m_new)
            kw_t = _dot_nt(eye_k, (wk * kf).astype(BF16)).astype(BF16)
            c_ref[h] = decay * cst + _dot(kw_t, vaug)
            m_ref[h:h + 1, :] = jnp.broadcast_to(m_new, (1, LANES))

            ms = jnp.mean(hh * hh, axis=-1, keepdims=True)
            hn = hh * lax.rsqrt(ms + EPS) * hw_ref[h:h + 1, :]
            og = _sigmoid(o_ref[rows, h * ML_DV:(h + 1) * ML_DV])
            zg = _silu(z_ref[rows, h * ML_DV:(h + 1) * ML_DV])
            y_ref[rows, h * ML_DV:(h + 1) * ML_DV] = (hn * og * zg).astype(BF16)
        return carry

    lax.fori_loop(0, lb // L, chunk, 0)


def _mlstm(u, gates, conv_w, ib, fb, head_w, B, S, lb=256):
    T = B * S
    nsb = S // lb
    qkw = ML_HEADS * ML_DQK
    vw = ML_HEADS * ML_DV
    rowmap = lambda col: (lambda b, s: (b * nsb + s, col))
    const = lambda b, s: (0, 0)
    return pl.pallas_call(
        functools.partial(_mlstm_kernel, lb=lb),
        grid=(B, nsb),
        in_specs=[
            pl.BlockSpec((lb, qkw), rowmap(0)),
            pl.BlockSpec((lb, qkw), rowmap(1)),
            pl.BlockSpec((lb, vw), rowmap(1)),
            pl.BlockSpec((lb, vw), rowmap(2)),
            pl.BlockSpec((lb, vw), rowmap(3)),
            pl.BlockSpec((lb, 2 * LANES), rowmap(0)),
            pl.BlockSpec((ML_CONV, 2 * qkw), const),
            pl.BlockSpec((1, LANES), const),
            pl.BlockSpec((1, LANES), const),
            pl.BlockSpec((ML_HEADS, ML_DV), const),
        ],
        out_specs=pl.BlockSpec((lb, vw), rowmap(0)),
        out_shape=jax.ShapeDtypeStruct((T, vw), BF16),
        scratch_shapes=[
            pltpu.VMEM((lb + 8, 2 * qkw), F32),
            pltpu.VMEM((lb, qkw), BF16),
            pltpu.VMEM((lb, qkw), F32),
            pltpu.VMEM((ML_HEADS, ML_DQK, ML_DV + LANES), F32),
            pltpu.VMEM((ML_HEADS, LANES), F32),
        ],
        compiler_params=pltpu.CompilerParams(dimension_semantics=("arbitrary", "arbitrary"),
                                             vmem_limit_bytes=VMEM_LIMIT),
        name="mlstm",
    )(u, u, u, u, u, gates, conv_w, ib, fb, head_w)


def _rms_heads(x, w, scale=1.0):
    outs = []
    for h in range(x.shape[1] // NSA_HD):
        xh = x[:, h * NSA_HD:(h + 1) * NSA_HD]
        ms = jnp.mean(xh * xh, axis=-1, keepdims=True)
        outs.append(xh * lax.rsqrt(ms + EPS) * w * scale)
    return jnp.concatenate(outs, axis=1)


VT_ROWS = NSA_HD + 16


def _nsa_prep_kernel(q_ref, ks_ref, vs_ref, kw_ref, vw_ref, qw_ref, kn_ref,
                     qn_ref, ksa_ref, vst_ref, kwn_ref, vwt_ref, *, rb, nsb):
    qn_ref[...] = _rms_heads(q_ref[...], qw_ref[...], NSA_HD ** -0.5 * LOG2E).astype(BF16)
    ksn = _rms_heads(ks_ref[...], kn_ref[1:2, :]).astype(BF16)
    kwn_ref[...] = _rms_heads(kw_ref[...], kn_ref[2:3, :]).astype(BF16)
    t = (pl.program_id(0) % nsb) * rb + _iota((rb, NSA_HD), 0)
    onehot = (_iota((rb, NSA_HD), 1) == jnp.right_shift(t, SEL_SHIFT)).astype(BF16)
    tail = (_iota((VT_ROWS - NSA_HD, rb), 0) == 0).astype(BF16)
    for g in range(NSA_GROUPS):
        cols = slice(g * NSA_HD, (g + 1) * NSA_HD)
        ksa_ref[:, 2 * g * NSA_HD:(2 * g + 1) * NSA_HD] = ksn[:, cols]
        ksa_ref[:, (2 * g + 1) * NSA_HD:(2 * g + 2) * NSA_HD] = onehot
        vst_ref[g] = jnp.concatenate([vs_ref[:, cols].T.astype(BF16), tail], axis=0)
        vwt_ref[g] = jnp.concatenate([vw_ref[:, cols].T.astype(BF16), tail], axis=0)


def _nsa_prep(u, q_norm_w, k_norm_w, col_q, col_kv, B, S, rb=256):
    T = u.shape[0]
    nsb = S // rb
    qw = NSA_HEADS * NSA_HD
    kvw = NSA_GROUPS * NSA_HD
    cq = col_q // qw
    ck = col_kv // kvw
    assert S // SEL_LEN <= NSA_HD
    kv_spec = lambda idx: pl.BlockSpec((rb, kvw), lambda i: (i, ck + idx))
    out_k = pl.BlockSpec((rb, kvw), lambda i: (i, 0))
    out_ka = pl.BlockSpec((rb, 2 * kvw), lambda i: (i, 0))
    out_vt = pl.BlockSpec((None, NSA_GROUPS, VT_ROWS, rb), lambda i: (i // nsb, 0, 0, i % nsb))
    vt_shape = jax.ShapeDtypeStruct((B, NSA_GROUPS, VT_ROWS, S), BF16)
    return pl.pallas_call(
        functools.partial(_nsa_prep_kernel, rb=rb, nsb=nsb),
        grid=(T // rb,),
        in_specs=[
            pl.BlockSpec((rb, qw), lambda i: (i, cq)),
            kv_spec(2), kv_spec(3), kv_spec(4), kv_spec(5),
            pl.BlockSpec((1, NSA_HD), lambda i: (0, 0)),
            pl.BlockSpec((N_BRANCH, NSA_HD), lambda i: (0, 0)),
        ],
        out_specs=[pl.BlockSpec((rb, qw), lambda i: (i, 0)), out_ka, out_vt, out_k, out_vt],
        out_shape=[jax.ShapeDtypeStruct((T, qw), BF16), jax.ShapeDtypeStruct((T, 2 * kvw), BF16), vt_shape,
                   jax.ShapeDtypeStruct((T, kvw), BF16), vt_shape],
        compiler_params=pltpu.CompilerParams(dimension_semantics=("arbitrary",), vmem_limit_bytes=VMEM_LIMIT),
        name="nsa_prep",
    )(u, u, u, u, u, q_norm_w, k_norm_w)


CMP_PAD = 16


def _compress_kernel(kc_ref, vc_ref, pek_ref, pev_ref, kw1_ref, kw2_ref, vw1_ref, vw2_ref, kn_ref,
                     kcmp_ref, vcmp_ref, *, ns):
    def mlp(x_ref, pe_ref, w1_ref, w2_ref):
        hid = CMP_LEN // 2
        a = jnp.zeros((ns, w1_ref.shape[1]), F32)
        bsum = jnp.zeros((ns, w1_ref.shape[1]), F32)
        for l in range(hid):
            xl = x_ref[pl.ds(l, ns, stride=CMP_STRIDE), :]
            a = a + _dot((xl + pe_ref[l:l + 1, :]).astype(BF16), w1_ref[l * NSA_HD:(l + 1) * NSA_HD, :])
            bsum = bsum + _dot((xl + pe_ref[hid + l:hid + l + 1, :]).astype(BF16),
                               w1_ref[(hid + l) * NSA_HD:(hid + l + 1) * NSA_HD, :])
        pre = a + pltpu.roll(bsum, ns - 1, 0)
        return _dot(_silu(pre).astype(BF16), w2_ref[...])

    kc = mlp(kc_ref, pek_ref, kw1_ref, kw2_ref)
    ms = jnp.mean(kc * kc, axis=-1, keepdims=True)
    kc = kc * lax.rsqrt(ms + EPS) * kn_ref[0:1, :]
    vc = mlp(vc_ref, pev_ref, vw1_ref, vw2_ref)
    zpad = jnp.zeros((CMP_PAD, NSA_HD), F32)
    kcmp_ref[...] = jnp.concatenate([zpad, kc, zpad], axis=0)
    vcmp_ref[...] = jnp.concatenate([zpad, vc, zpad], axis=0)


def _compress(u, pe_k, pe_v, kw1, kw2, vw1, vw2, k_norm_w, col_kv, B, S):
    ns = S // CMP_STRIDE
    nsp = ns + 2 * CMP_PAD
    ck = col_kv // NSA_HD
    hidden = kw1.shape[1]
    const = lambda b, g: (0, 0)
    out_spec = pl.BlockSpec((None, None, nsp, NSA_HD), lambda b, g: (b, g, 0, 0))
    return pl.pallas_call(
        functools.partial(_compress_kernel, ns=ns),
        grid=(B, NSA_GROUPS),
        in_specs=[
            pl.BlockSpec((S, NSA_HD), lambda b, g: (b, ck + g)),
            pl.BlockSpec((S, NSA_HD), lambda b, g: (b, ck + NSA_GROUPS + g)),
            pl.BlockSpec((CMP_LEN, NSA_HD), const),
            pl.BlockSpec((CMP_LEN, NSA_HD), const),
            pl.BlockSpec((CMP_LEN * NSA_HD, hidden), const),
            pl.BlockSpec((hidden, NSA_HD), const),
            pl.BlockSpec((CMP_LEN * NSA_HD, hidden), const),
            pl.BlockSpec((hidden, NSA_HD), const),
            pl.BlockSpec((N_BRANCH, NSA_HD), const),
        ],
        out_specs=[out_spec, out_spec],
        out_shape=[jax.ShapeDtypeStruct((B, NSA_GROUPS, nsp, NSA_HD), F32)] * 2,
        compiler_params=pltpu.CompilerParams(dimension_semantics=("arbitrary", "arbitrary"),
                                             vmem_limit_bytes=VMEM_LIMIT),
        name="compress",
    )(u, u, pe_k, pe_v, kw1, kw2, vw1, vw2, k_norm_w)


NEAR = 32


def _stack_heads(q4):
    return jnp.concatenate([q4[:, h * NSA_HD:(h + 1) * NSA_HD] for h in range(NSA_HPG)], axis=0)


def _cmp_sel_kernel(tbl_ref, q_ref, kc_ref, vc_ref, oc_ref, sel_ref, pat_ref, *, ns, nsel):
    g = pl.program_id(1)
    i = pl.program_id(2)
    t0 = i * TQ
    c0 = pl.multiple_of(i * (TQ // CMP_STRIDE), TQ // CMP_STRIDE)
    R = NSA_HPG * TQ

    d_near = _iota((TQ, NEAR), 0) - CMP_STRIDE * (_iota((TQ, NEAR), 1) - CMP_PAD) - (CMP_LEN - 1)

    @pl.when(i == 0)
    def _():
        for h in range(NSA_HPG):
            pat_ref[h * TQ:(h + 1) * TQ, :] = _rel_bias_pattern(d_near, tbl_ref, g * NSA_HPG + h)

    q = _stack_heads(q_ref[...])
    k_far = kc_ref[CMP_PAD:CMP_PAD + ns, :]
    v_far = vc_ref[CMP_PAD:CMP_PAD + ns, :]
    k_near = kc_ref[pl.ds(c0, NEAR), :]
    v_near = vc_ref[pl.ds(c0, NEAR), :]

    rloc = jnp.bitwise_and(_iota((R, ns), 0), TQ - 1)
    d_far = t0 + rloc - CMP_STRIDE * _iota((R, ns), 1) - (CMP_LEN - 1)
    mask_far = d_far >= REL_MAX_DIST
    rloc_n = jnp.bitwise_and(_iota((R, NEAR), 0), TQ - 1)
    u_n = _iota((R, NEAR), 1)
    d_n = rloc_n - CMP_STRIDE * (u_n - CMP_PAD) - (CMP_LEN - 1)
    mask_near = (d_n >= 0) & (d_n < REL_MAX_DIST) & (c0 - CMP_PAD + u_n >= 0)

    s_far = jnp.where(mask_far, _dot_nt(q, k_far.astype(BF16)), NEG)
    s_near = jnp.where(mask_near, _dot_nt(q, k_near.astype(BF16)) + pat_ref[...], NEG)
    m = jnp.maximum(jnp.max(s_far, axis=-1, keepdims=True), jnp.max(s_near, axis=-1, keepdims=True))
    e_far = jnp.where(mask_far, jnp.exp2(s_far - m), 0.0)
    e_near = jnp.where(mask_near, jnp.exp2(s_near - m), 0.0)
    l = jnp.sum(e_far, axis=-1, keepdims=True) + jnp.sum(e_near, axis=-1, keepdims=True)
    inv = jnp.where(l > 0.0, 1.0 / jnp.where(l > 0.0, l, 1.0), 0.0)
    p_far = e_far * inv
    p_near = e_near * inv
    o = _dot(p_far.astype(BF16), v_far.astype(BF16)) + _dot(p_near.astype(BF16), v_near.astype(BF16))
    for h in range(NSA_HPG):
        oc_ref[:, h * NSA_HD:(h + 1) * NSA_HD] = o[h * TQ:(h + 1) * TQ, :]

    ps_far = p_far[0:TQ]
    ps_near = p_near[0:TQ]
    for h in range(1, NSA_HPG):
        ps_far = ps_far + p_far[h * TQ:(h + 1) * TQ]
        ps_near = ps_near + p_near[h * TQ:(h + 1) * TQ]
    ratio = SEL_LEN // CMP_STRIDE

    def overlap(n, c):
        return ((c >= ratio * n - (CMP_LEN // CMP_STRIDE - 1)) & (c <= ratio * n + ratio - 1)).astype(F32)

    ov_far = overlap(_iota((nsel, ns), 0), _iota((nsel, ns), 1))
    ov_near = overlap(_iota((nsel, NEAR), 0), c0 - CMP_PAD + _iota((nsel, NEAR), 1))
    imp = _dot_nt(ov_far, ps_far, precision=HIGHEST) + _dot_nt(ov_near, ps_near, precision=HIGHEST)

    blk = _iota((nsel, TQ), 0)
    cur = jnp.right_shift(t0 + _iota((nsel, TQ), 1), SEL_SHIFT)
    imp = jnp.where((blk == 0) | (blk == cur) | (blk == cur - 1), FORCE_SCORE, imp)
    imp = jnp.where(blk > cur, NEG, imp)
    rank = jnp.zeros((nsel, TQ), jnp.int32)
    for jb in range(nsel):
        vj = imp[jb:jb + 1, :]
        ahead = (vj > imp) | ((vj == imp) & (blk > jb))
        rank = rank + jnp.where(ahead, 1, 0)
    chosen = jnp.where(rank < min(SEL_TOPK, nsel), 1.0, 0.0).astype(BF16)
    chosen = jnp.concatenate([chosen, jnp.zeros((LANES - nsel, TQ), BF16)], axis=0)
    eye_q = (_iota((TQ, TQ), 0) == _iota((TQ, TQ), 1)).astype(BF16)
    sel_ref[...] = ((_dot_nt(eye_q, chosen) - 1.0) * (-NEG)).astype(BF16)


def _cmp_sel(rel_bias, qn, kcmp, vcmp, B, S):
    ns = S // CMP_STRIDE
    nsp = ns + 2 * CMP_PAD
    nsel = S // SEL_LEN
    nt = S // TQ
    gw = NSA_HPG * NSA_HD
    cmp_spec = pl.BlockSpec((None, None, nsp, NSA_HD), lambda b, g, i: (b, g, 0, 0))
    return pl.pallas_call(
        functools.partial(_cmp_sel_kernel, ns=ns, nsel=nsel),
        grid=(B, NSA_GROUPS, nt),
        in_specs=[
            pl.BlockSpec(memory_space=pltpu.SMEM),
            pl.BlockSpec((TQ, gw), lambda b, g, i: (b * nt + i, g)),
            cmp_spec, cmp_spec,
        ],
        out_specs=[
            pl.BlockSpec((TQ, gw), lambda b, g, i: (b * nt + i, g)),
            pl.BlockSpec((None, None, TQ, LANES), lambda b, g, i: (b, g, i, 0)),
        ],
        out_shape=[jax.ShapeDtypeStruct((B * S, NSA_HEADS * NSA_HD), F32),
                   jax.ShapeDtypeStruct((B, NSA_GROUPS, S, LANES), BF16)],
        scratch_shapes=[pltpu.VMEM((NSA_HPG * TQ, NEAR), F32)],
        compiler_params=pltpu.CompilerParams(dimension_semantics=("arbitrary", "arbitrary", "arbitrary"),
                                             vmem_limit_bytes=VMEM_LIMIT),
        name="cmp_sel",
    )(rel_bias, qn, kcmp, vcmp)


TA = 512
SUB = 128


def _attend_kernel(tbl_ref, q_ref, ksa_ref, vst_ref, kw_ref, vwt_ref, sel_ref, oc_ref, g_ref, z_ref, y_ref,
                   pd_ref, pp_ref, sa_ref, sb_ref, ms_ref, accs_ref, mw_ref, accw_ref, *, gate_col):
    g = pl.program_id(1)
    i = pl.program_id(2)
    nsub = TA // SUB

    krow = _iota((SUB, SUB), 0)
    qcol = _iota((SUB, SUB), 1)

    @pl.when(i == 0)
    def _():
        for h in range(NSA_HPG):
            head = g * NSA_HPG + h
            lanes = slice(h * SUB, (h + 1) * SUB)
            pd_ref[:, lanes] = jnp.where(krow <= qcol, _rel_bias_pattern(qcol - krow, tbl_ref, head), NEG)
            pp_ref[:, lanes] = _rel_bias_pattern(qcol - krow + SUB, tbl_ref, head)

    q4 = q_ref[...]
    sel = sel_ref[...]
    q_win = _stack_heads(q4)
    q_sel = jnp.concatenate([jnp.concatenate([q4[:, h * NSA_HD:(h + 1) * NSA_HD], sel], axis=1)
                             for h in range(NSA_HPG)], axis=0)

    def chunk(c):
        return pl.ds(pl.multiple_of(c * TA, TA), TA)

    def sub(h, kb, qb):
        return slice(kb * SUB, (kb + 1) * SUB), slice(h * TA + qb * SUB, h * TA + (qb + 1) * SUB)

    def band_diag(s_ref):
        for h in range(NSA_HPG):
            lanes = slice(h * SUB, (h + 1) * SUB)
            for qb in range(nsub):
                for kb in range(nsub):
                    if kb > qb:
                        s_ref[sub(h, kb, qb)] = jnp.full((SUB, SUB), NEG, F32)
                    elif kb == qb:
                        s_ref[sub(h, kb, qb)] += pd_ref[:, lanes]
                    elif kb == qb - 1:
                        s_ref[sub(h, kb, qb)] += pp_ref[:, lanes]

    def band_prev(s_ref, window):
        for h in range(NSA_HPG):
            s_ref[sub(h, nsub - 1, 0)] += pp_ref[:, h * SUB:(h + 1) * SUB]
            if window:
                for qb in range(nsub):
                    for kb in range(nsub):
                        if kb < qb:
                            s_ref[sub(h, kb, qb)] = jnp.full((SUB, SUB), NEG, F32)
                        elif kb == qb:
                            s_ref[sub(h, kb, qb)] += jnp.where(krow > qcol, 0.0, NEG)

    def absorb(s_ref, vt, m_ref, acc_ref, first=False):
        mx = jnp.max(s_ref[...], axis=0, keepdims=True)
        if first:
            m_new = mx
        else:
            m_old = m_ref[...]
            m_new = jnp.maximum(m_old, mx)
        p = jnp.exp2(s_ref[...] - m_new).astype(BF16)
        pv = _dot(vt, p)
        if first:
            acc_ref[...] = pv
        else:
            acc_ref[...] = jnp.exp2(m_old - m_new) * acc_ref[...] + pv
        m_ref[...] = m_new

    def qk_sel(s_ref, c):
        s_ref[...] = _dot_nt(ksa_ref[chunk(c), :], q_sel)

    def qk_win(s_ref, c):
        s_ref[...] = _dot_nt(kw_ref[chunk(c), :], q_win)

    @pl.when(i == 0)
    def _():
        qk_sel(sa_ref, 0)
        band_diag(sa_ref)
        qk_win(sb_ref, 0)
        band_diag(sb_ref)
        absorb(sa_ref, vst_ref[:, chunk(0)], ms_ref, accs_ref, first=True)
        absorb(sb_ref, vwt_ref[:, chunk(0)], mw_ref, accw_ref, first=True)

    @pl.when(i >= 1)
    def _():
        qk_sel(sa_ref, i)
        band_diag(sa_ref)
        qk_sel(sb_ref, i - 1)
        band_prev(sb_ref, window=False)
        absorb(sa_ref, vst_ref[:, chunk(i)], ms_ref, accs_ref, first=True)
        qk_win(sa_ref, i)
        band_diag(sa_ref)
        absorb(sb_ref, vst_ref[:, chunk(i - 1)], ms_ref, accs_ref)
        qk_win(sb_ref, i - 1)
        band_prev(sb_ref, window=True)
        absorb(sa_ref, vwt_ref[:, chunk(i)], mw_ref, accw_ref, first=True)
        absorb(sb_ref, vwt_ref[:, chunk(i - 1)], mw_ref, accw_ref)

    nfar = jnp.maximum(i - 1, 0)
    odd = nfar % 2

    @pl.when(odd == 1)
    def _():
        qk_sel(sa_ref, 0)
        absorb(sa_ref, vst_ref[:, chunk(0)], ms_ref, accs_ref)

    npair = nfar // 2

    @pl.when(npair > 0)
    def _():
        qk_sel(sa_ref, odd)

    def pair(t, carry):
        c = odd + 2 * t
        qk_sel(sb_ref, c + 1)
        absorb(sa_ref, vst_ref[:, chunk(c)], ms_ref, accs_ref)
        qk_sel(sa_ref, jnp.minimum(c + 2, nfar - 1))
        absorb(sb_ref, vst_ref[:, chunk(c + 1)], ms_ref, accs_ref)
        return carry

    lax.fori_loop(0, npair, pair, 0)

    o_s = accs_ref[0:NSA_HD, :] / accs_ref[NSA_HD:NSA_HD + 1, :]
    o_w = accw_ref[0:NSA_HD, :] / accw_ref[NSA_HD:NSA_HD + 1, :]
    gs = _sigmoid(g_ref[:, 0:LANES])
    lane = _iota((TA, LANES), 1)
    for h in range(NSA_HPG):
        base = gate_col + (g * NSA_HPG + h) * N_BRANCH
        gate = [jnp.sum(jnp.where(lane == base + br, gs, 0.0), axis=-1, keepdims=True) for br in range(N_BRANCH)]
        cols = slice(h * NSA_HD, (h + 1) * NSA_HD)
        lanes = slice(h * TA, (h + 1) * TA)
        o = gate[0] * oc_ref[:, cols] + gate[1] * o_s[:, lanes].T + gate[2] * o_w[:, lanes].T
        y_ref[:, cols] = (o * _silu(z_ref[:, cols])).astype(BF16)


def _attend(rel_bias, qn, ksa, vst, kwn, vwt, sel, o_c, gates, u, col_z, gate_col, B, S):
    nt = S // TA
    gw = NSA_HPG * NSA_HD
    W = NSA_HPG * TA
    cz = col_z // gw
    assert WIN == TA
    tile = lambda b, g, i: (b * nt + i, g)
    vt_spec = pl.BlockSpec((None, None, VT_ROWS, S), lambda b, g, i: (b, g, 0, 0))
    return pl.pallas_call(
        functools.partial(_attend_kernel, gate_col=gate_col),
        grid=(B, NSA_GROUPS, nt),
        in_specs=[
            pl.BlockSpec(memory_space=pltpu.SMEM),
            pl.BlockSpec((TA, gw), tile),
            pl.BlockSpec((S, 2 * NSA_HD), lambda b, g, i: (b, g)), vt_spec,
            pl.BlockSpec((S, NSA_HD), lambda b, g, i: (b, g)), vt_spec,
            pl.BlockSpec((None, None, TA, LANES), lambda b, g, i: (b, g, i, 0)),
            pl.BlockSpec((TA, gw), tile),
            pl.BlockSpec((TA, 2 * LANES), lambda b, g, i: (b * nt + i, 0)),
            pl.BlockSpec((TA, gw), lambda b, g, i: (b * nt + i, cz + g)),
        ],
        out_specs=pl.BlockSpec((TA, gw), tile),
        out_shape=jax.ShapeDtypeStruct((B * S, NSA_HEADS * NSA_HD), BF16),
        scratch_shapes=[pltpu.VMEM((SUB, NSA_HPG * SUB), F32), pltpu.VMEM((SUB, NSA_HPG * SUB), F32),
                        pltpu.VMEM((TA, W), F32), pltpu.VMEM((TA, W), F32),
                        pltpu.VMEM((1, W), F32), pltpu.VMEM((VT_ROWS, W), F32),
                        pltpu.VMEM((1, W), F32), pltpu.VMEM((VT_ROWS, W), F32)],
        compiler_params=pltpu.CompilerParams(dimension_semantics=("arbitrary", "arbitrary", "arbitrary"),
                                             vmem_limit_bytes=VMEM_LIMIT),
        name="attend",
    )(rel_bias, qn, ksa, vst, kwn, vwt, sel, o_c, gates, u)


def _outproj_kernel(yml_ref, yns_ref, w_ref, x_ref, o_ref):
    half = yml_ref.shape[1]
    o_ref[...] = x_ref[...] + _dot(yml_ref[...], w_ref[0:half, :]) + _dot(yns_ref[...], w_ref[half:2 * half, :])


def _outproj(y_ml, y_ns, w_out, x2, tm=512, tn=512):
    T, D = x2.shape
    half = y_ml.shape[1]
    return pl.pallas_call(
        _outproj_kernel,
        grid=(T // tm, D // tn),
        in_specs=[
            pl.BlockSpec((tm, half), lambda i, j: (i, 0)),
            pl.BlockSpec((tm, half), lambda i, j: (i, 0)),
            pl.BlockSpec((2 * half, tn), lambda i, j: (0, j)),
            pl.BlockSpec((tm, tn), lambda i, j: (i, j)),
        ],
        out_specs=pl.BlockSpec((tm, tn), lambda i, j: (i, j)),
        out_shape=jax.ShapeDtypeStruct((T, D), F32),
        compiler_params=pltpu.CompilerParams(dimension_semantics=("arbitrary", "arbitrary"),
                                             vmem_limit_bytes=VMEM_LIMIT),
        name="outproj",
    )(y_ml, y_ns, w_out, x2)


def _ple_kernel(x_ref, p_ref, wg_ref, wp_ref, o_ref, xb_ref, pb_ref, *, tn):
    j = pl.program_id(1)

    @pl.when(j == 0)
    def _():
        xb_ref[...] = x_ref[...].astype(BF16)
        pb_ref[...] = p_ref[...].astype(BF16)

    gate = _sigmoid(_dot(xb_ref[...], wg_ref[...]))
    emb = _dot(pb_ref[...], wp_ref[...])
    o_ref[...] = x_ref[:, pl.ds(pl.multiple_of(j * tn, tn), tn)] + gate * emb


def _ple(x1, p2, wg, wp, tm=512, tn=512):
    T, D = x1.shape
    P = p2.shape[1]
    return pl.pallas_call(
        functools.partial(_ple_kernel, tn=tn),
        grid=(T // tm, D // tn),
        in_specs=[
            pl.BlockSpec((tm, D), lambda i, j: (i, 0)),
            pl.BlockSpec((tm, P), lambda i, j: (i, 0)),
            pl.BlockSpec((D, tn), lambda i, j: (0, j)),
            pl.BlockSpec((P, tn), lambda i, j: (0, j)),
        ],
        out_specs=pl.BlockSpec((tm, tn), lambda i, j: (i, j)),
        out_shape=jax.ShapeDtypeStruct((T, D), F32),
        scratch_shapes=[pltpu.VMEM((tm, D), BF16), pltpu.VMEM((tm, P), BF16)],
        compiler_params=pltpu.CompilerParams(dimension_semantics=("arbitrary", "arbitrary"),
                                             vmem_limit_bytes=VMEM_LIMIT),
        name="ple",
    )(x1, p2, wg, wp)


def _layer(x2, p2, norm_w, w_in, conv_w, i_bias, f_bias, head_norm_w, q_norm_w, k_norm_w,
           pe_k, pe_v, kw1, kw2, vw1, vw2, rel_bias, w_out, ple_proj, ple_gate, B, S):
    D = x2.shape[1]
    qkw = ML_HEADS * ML_DQK
    vw = ML_HEADS * ML_DV
    nq = NSA_HEADS * NSA_HD
    nkv = NSA_GROUPS * NSA_HD
    o_i = 2 * qkw + 3 * vw
    o_f = o_i + ML_HEADS
    o_nq = o_f + ML_HEADS
    o_g = o_nq + nq + 6 * nkv
    o_z = o_g + NSA_HEADS * N_BRANCH
    w_main = jnp.concatenate([w_in[:, :o_i], w_in[:, o_nq:o_g], w_in[:, o_z:]], axis=1).astype(BF16)
    ngate = NSA_HEADS * N_BRANCH
    zeros = lambda n: jnp.zeros((D, n), w_in.dtype)
    w_small = jnp.concatenate([w_in[:, o_i:o_f], w_in[:, o_g:o_z], zeros(LANES - ML_HEADS - ngate),
                               w_in[:, o_f:o_nq], zeros(LANES - ML_HEADS)], axis=1).astype(BF16)
    col_nq = o_i
    col_kv = col_nq + nq
    col_z = col_kv + 6 * nkv

    u, gates = _inproj(x2, norm_w.reshape(1, D), w_main, w_small)

    pad_h = lambda v: jnp.concatenate([v, jnp.zeros((LANES - ML_HEADS,), v.dtype)]).reshape(1, LANES)
    y_ml = _mlstm(u, gates, conv_w, pad_h(i_bias), pad_h(f_bias), head_norm_w, B, S)

    qn, ksn, vst, kwn, vwt = _nsa_prep(u, q_norm_w.reshape(1, NSA_HD), k_norm_w, col_nq, col_kv, B, S)
    kcmp, vcmp = _compress(u, pe_k, pe_v, kw1.astype(BF16), kw2.astype(BF16), vw1.astype(BF16), vw2.astype(BF16),
                           k_norm_w, col_kv, B, S)
    o_c, sel = _cmp_sel(rel_bias, qn, kcmp, vcmp, B, S)
    y_ns = _attend(rel_bias, qn, ksn, vst, kwn, vwt, sel, o_c, gates, u, col_z, ML_HEADS, B, S)

    x1 = _outproj(y_ml, y_ns, w_out.astype(BF16), x2)
    return _ple(x1, p2, ple_gate.astype(BF16), ple_proj.astype(BF16))


def kernel(x, p, norm_w, w_in, ml_conv_w, ml_i_bias, ml_f_bias, ml_head_norm_w, nsa_q_norm_w, nsa_k_norm_w,
           cmp_pe_k, cmp_pe_v, cmp_k_w1, cmp_k_w2, cmp_v_w1, cmp_v_w2, rel_bias, w_out, ple_proj, ple_gate):
    B, S, D = x.shape
    assert S % max(WIN, 256) == 0 and S // SEL_LEN >= 1
    x2 = x.reshape(B * S, D)
    for layer in range(w_in.shape[0]):
        x2 = _layer(x2, p[layer].reshape(B * S, -1), norm_w[layer], w_in[layer], ml_conv_w[layer],
                    ml_i_bias[layer], ml_f_bias[layer], ml_head_norm_w[layer], nsa_q_norm_w[layer],
                    nsa_k_norm_w[layer], cmp_pe_k[layer], cmp_pe_v[layer], cmp_k_w1[layer], cmp_k_w2[layer],
                    cmp_v_w1[layer], cmp_v_w2[layer], rel_bias, w_out[layer], ple_proj[layer], ple_gate[layer],
                    B, S)
    return x2.reshape(B, S, D)
```

```python
import functools
import math

import numpy as np
import jax
import jax.numpy as jnp
from jax import lax
from jax.experimental import pallas as pl
from jax.experimental.pallas import tpu as pltpu

F32 = jnp.float32
BF16 = jnp.bfloat16
HIGHEST = lax.Precision.HIGHEST

ML_HEADS = 8
ML_DQK = 128
ML_DV = 256
ML_CHUNK = 64
ML_CONV = 4
NSA_HEADS = 16
NSA_HD = 128
NSA_GROUPS = 4
NSA_HPG = 4
N_BRANCH = 3
CMP_STRIDE = 16
CMP_LEN = 32
SEL_LEN = 64
SEL_SHIFT = 6
SEL_TOPK = 16
WIN = 512
REL_BUCKETS = 32
REL_MAX_DIST = 128
EPS = 1e-6
NEG = -1e30
FORCE_SCORE = 1e4
LOG2E = math.log2(math.e)

LANES = 128
VMEM_LIMIT = 48 * 1024 * 1024


def _bucket_thresholds():
    n = np.arange(0, 4 * REL_MAX_DIST, dtype=np.int64)
    max_exact = REL_BUCKETS // 2
    nf = np.maximum(n, 1).astype(np.float32)
    large = max_exact + (np.log(nf / np.float32(max_exact)) / np.float32(math.log(REL_MAX_DIST / max_exact))
                         * np.float32(REL_BUCKETS - max_exact)).astype(np.int32)
    large = np.minimum(large, REL_BUCKETS - 1)
    bucket = np.where(n < max_exact, n, large)
    assert np.all(np.diff(bucket) >= 0)
    thr = [int(np.argmax(bucket >= b)) for b in range(REL_BUCKETS)]
    assert thr[REL_BUCKETS - 1] <= REL_MAX_DIST
    return thr


BUCKET_THR = _bucket_thresholds()


def _dot(a, b, precision=None):
    return jnp.dot(a, b, preferred_element_type=F32, precision=precision)


def _dot_nt(a, b, precision=None):
    return lax.dot_general(a, b, (((1,), (1,)), ((), ())), preferred_element_type=F32, precision=precision)


def _sigmoid(x):
    return 1.0 / (1.0 + jnp.exp(-x))


def _silu(x):
    return x * _sigmoid(x)


def _iota(shape, dim):
    return lax.broadcasted_iota(jnp.int32, shape, dim)


def _rel_bias_pattern(dist, tbl_ref, head):
    val = jnp.full(dist.shape, tbl_ref[0, head], F32)
    for b in range(1, REL_BUCKETS):
        val = jnp.where(dist >= BUCKET_THR[b], tbl_ref[b, head], val)
    return (val - tbl_ref[REL_BUCKETS - 1, head]) * LOG2E


def _rmsnorm_kernel(x_ref, nw_ref, h_ref):
    x = x_ref[...]
    ms = jnp.mean(x * x, axis=-1, keepdims=True)
    h_ref[...] = (x * lax.rsqrt(ms + EPS) * nw_ref[...]).astype(BF16)


def _rmsnorm(x2, norm_w, rb=128):
    T, D = x2.shape
    return pl.pallas_call(
        _rmsnorm_kernel,
        grid=(T // rb,),
        in_specs=[pl.BlockSpec((rb, D), lambda i: (i, 0)), pl.BlockSpec((1, D), lambda i: (0, 0))],
        out_specs=pl.BlockSpec((rb, D), lambda i: (i, 0)),
        out_shape=jax.ShapeDtypeStruct((T, D), BF16),
        compiler_params=pltpu.CompilerParams(dimension_semantics=("arbitrary",), vmem_limit_bytes=VMEM_LIMIT),
        name="rmsnorm",
    )(x2, norm_w)


def _inproj_kernel(h_ref, w_ref, u_ref, *scratch):
    if scratch:
        wb_ref, = scratch

        @pl.when(pl.program_id(1) == 0)
        def _():
            wb_ref[...] = w_ref[...].astype(BF16)
        u_ref[...] = _dot(h_ref[...], wb_ref[...])
    else:
        u_ref[...] = _dot(h_ref[...], w_ref[...])


def _inproj(h, w, ncols, tm, tn, name):
    T, D = h.shape
    cast = w.dtype != BF16
    return pl.pallas_call(
        _inproj_kernel,
        grid=(ncols // tn, T // tm),
        in_specs=[
            pl.BlockSpec((tm, D), lambda j, i: (i, 0)),
            pl.BlockSpec((D, tn), lambda j, i: (0, j)),
        ],
        out_specs=pl.BlockSpec((tm, tn), lambda j, i: (i, j)),
        out_shape=jax.ShapeDtypeStruct((T, ncols), F32),
        scratch_shapes=[pltpu.VMEM((D, tn), BF16)] if cast else [],
        compiler_params=pltpu.CompilerParams(dimension_semantics=("arbitrary", "arbitrary"),
                                             vmem_limit_bytes=VMEM_LIMIT),
        name=name,
    )(h, w)


def _mlstm_kernel(q_ref, k_ref, v_ref, o_ref, z_ref, g_ref, cw_ref, ib_ref, fb_ref, hw_ref, y_ref,
                  xbuf, qa, ka, c_ref, m_ref, *, lb):
    L = ML_CHUNK
    qkw = ML_HEADS * ML_DQK

    @pl.when(pl.program_id(1) == 0)
    def _():
        xbuf[0:8, :] = jnp.zeros((8, 2 * qkw), F32)
        c_ref[...] = jnp.zeros(c_ref.shape, F32)
        m_ref[...] = jnp.zeros(m_ref.shape, F32)

    xbuf[8:8 + lb, 0:qkw] = q_ref[...]
    xbuf[8:8 + lb, qkw:2 * qkw] = k_ref[...]
    acc = cw_ref[ML_CONV - 1:ML_CONV, :] * xbuf[8:8 + lb, :]
    for j in range(1, ML_CONV):
        acc = acc + cw_ref[ML_CONV - 1 - j:ML_CONV - j, :] * xbuf[8 - j:8 - j + lb, :]
    act = _silu(acc)
    qa[...] = act[:, 0:qkw].astype(BF16)
    ka[...] = act[:, qkw:2 * qkw] * (ML_DQK ** -0.5)
    xbuf[0:8, :] = xbuf[lb:lb + 8, :]

    row = _iota((L, L), 0)
    col = _iota((L, L), 1)
    tril = col <= row
    rowg = _iota((L, LANES), 0)
    eye_h = (_iota((8, LANES), 0) == _iota((8, LANES), 1)).astype(F32)
    eye_k = (_iota((ML_DQK, ML_DQK), 0) == _iota((ML_DQK, ML_DQK), 1)).astype(BF16)
    ones_col = (_iota((L, LANES), 1) == 0).astype(BF16)

    def chunk(c, carry):
        rows = pl.ds(pl.multiple_of(c * L, L), L)
        G = g_ref[rows, :]
        li = G[:, 0:LANES] + ib_ref[...]
        fp = G[:, LANES:2 * LANES] + fb_ref[...]
        lf = jnp.minimum(fp, 0.0) - jnp.log(1.0 + jnp.exp(-jnp.abs(fp)))
        b = lf
        for sh in (1, 2, 4, 8, 16, 32):
            b = b + jnp.where(rowg >= sh, pltpu.roll(b, sh, 0), 0.0)
        g_rows = _dot_nt(eye_h, li - b, precision=HIGHEST)

        for h in range(ML_HEADS):
            li_c = li[:, h:h + 1]
            b_c = b[:, h:h + 1]
            m = m_ref[h:h + 1, 0:1]
            D = jnp.where(tril, b_c + g_rows[h:h + 1, :], NEG)
            a_c = b_c + m
            m_t = jnp.maximum(a_c, jnp.max(D, axis=-1, keepdims=True))
            Dw = jnp.exp(D - m_t)
            inter = jnp.exp(a_c - m_t)
            q = qa[rows, h * ML_DQK:(h + 1) * ML_DQK]
            kf = ka[rows, h * ML_DQK:(h + 1) * ML_DQK]
            sc = (_dot_nt(q, kf.astype(BF16)) * Dw).astype(BF16)
            vaug = jnp.concatenate([v_ref[rows, h * ML_DV:(h + 1) * ML_DV].astype(BF16), ones_col], axis=1)
            cst = c_ref[h]
            num_aug = inter * _dot(q, cst.astype(BF16)) + _dot(sc, vaug)
            num = num_aug[:, 0:ML_DV]
            den = num_aug[:, ML_DV:ML_DV + 1]
            hh = num / jnp.maximum(jnp.abs(den), jnp.exp(-m_t))

            bL = b_c[L - 1:L, :]
            w_c = bL - b_c + li_c
            m_new = jnp.maximum(bL + m, jnp.max(w_c, axis=0, keepdims=True))
            wk = jnp.exp(w_c - m_new)
            decay = jnp.exp(bL + m - m_new)
            kw_t = _dot_nt(eye_k, (wk * kf).astype(BF16)).astype(BF16)
            c_ref[h] = decay * cst + _dot(kw_t, vaug)
            m_ref[h:h + 1, :] = jnp.broadcast_to(m_new, (1, LANES))

            ms = jnp.mean(hh * hh, axis=-1, keepdims=True)
            hn = hh * lax.rsqrt(ms + EPS) * hw_ref[h:h + 1, :]
            og = _sigmoid(o_ref[rows, h * ML_DV:(h + 1) * ML_DV])
            zg = _silu(z_ref[rows, h * ML_DV:(h + 1) * ML_DV])
            y_ref[rows, h * ML_DV:(h + 1) * ML_DV] = (hn * og * zg).astype(BF16)
        return carry

    lax.fori_loop(0, lb // L, chunk, 0)


def _mlstm(u, gates, gate_blk, conv_w, ib, fb, head_w, B, S, lb=256):
    T = B * S
    nsb = S // lb
    qkw = ML_HEADS * ML_DQK
    vw = ML_HEADS * ML_DV
    rowmap = lambda col: (lambda b, s: (b * nsb + s, col))
    const = lambda b, s: (0, 0)
    return pl.pallas_call(
        functools.partial(_mlstm_kernel, lb=lb),
        grid=(B, nsb),
        in_specs=[
            pl.BlockSpec((lb, qkw), rowmap(0)),
            pl.BlockSpec((lb, qkw), rowmap(1)),
            pl.BlockSpec((lb, vw), rowmap(1)),
            pl.BlockSpec((lb, vw), rowmap(2)),
            pl.BlockSpec((lb, vw), rowmap(3)),
            pl.BlockSpec((lb, 2 * LANES), rowmap(gate_blk)),
            pl.BlockSpec((ML_CONV, 2 * qkw), const),
            pl.BlockSpec((1, LANES), const),
            pl.BlockSpec((1, LANES), const),
            pl.BlockSpec((ML_HEADS, ML_DV), const),
        ],
        out_specs=pl.BlockSpec((lb, vw), rowmap(0)),
        out_shape=jax.ShapeDtypeStruct((T, vw), BF16),
        scratch_shapes=[
            pltpu.VMEM((lb + 8, 2 * qkw), F32),
            pltpu.VMEM((lb, qkw), BF16),
            pltpu.VMEM((lb, qkw), F32),
            pltpu.VMEM((ML_HEADS, ML_DQK, ML_DV + LANES), F32),
            pltpu.VMEM((ML_HEADS, LANES), F32),
        ],
        compiler_params=pltpu.CompilerParams(dimension_semantics=("arbitrary", "arbitrary"),
                                             vmem_limit_bytes=VMEM_LIMIT),
        name="mlstm",
    )(u, u, u, u, u, gates, conv_w, ib, fb, head_w)


def _rms_heads(x, w, scale=1.0):
    outs = []
    for h in range(x.shape[1] // NSA_HD):
        xh = x[:, h * NSA_HD:(h + 1) * NSA_HD]
        ms = jnp.mean(xh * xh, axis=-1, keepdims=True)
        outs.append(xh * lax.rsqrt(ms + EPS) * w * scale)
    return jnp.concatenate(outs, axis=1)


VT_ROWS = NSA_HD + 16


def _nsa_prep_kernel(q_ref, ks_ref, vs_ref, kw_ref, vw_ref, qw_ref, kn_ref,
                     qn_ref, ksa_ref, vst_ref, kwn_ref, vwt_ref, *, rb, nsb):
    qn_ref[...] = _rms_heads(q_ref[...], qw_ref[...], NSA_HD ** -0.5 * LOG2E).astype(BF16)
    ksn = _rms_heads(ks_ref[...], kn_ref[1:2, :]).astype(BF16)
    kwn_ref[...] = _rms_heads(kw_ref[...], kn_ref[2:3, :]).astype(BF16)
    t = (pl.program_id(0) % nsb) * rb + _iota((rb, NSA_HD), 0)
    onehot = (_iota((rb, NSA_HD), 1) == jnp.right_shift(t, SEL_SHIFT)).astype(BF16)
    tail = (_iota((VT_ROWS - NSA_HD, rb), 0) == 0).astype(BF16)
    for g in range(NSA_GROUPS):
        cols = slice(g * NSA_HD, (g + 1) * NSA_HD)
        ksa_ref[:, 2 * g * NSA_HD:(2 * g + 1) * NSA_HD] = ksn[:, cols]
        ksa_ref[:, (2 * g + 1) * NSA_HD:(2 * g + 2) * NSA_HD] = onehot
        vst_ref[g] = jnp.concatenate([vs_ref[:, cols].T.astype(BF16), tail], axis=0)
        vwt_ref[g] = jnp.concatenate([vw_ref[:, cols].T.astype(BF16), tail], axis=0)


def _nsa_prep(u, q_norm_w, k_norm_w, col_q, col_kv, B, S, rb=256):
    T = u.shape[0]
    nsb = S // rb
    qw = NSA_HEADS * NSA_HD
    kvw = NSA_GROUPS * NSA_HD
    cq = col_q // qw
    ck = col_kv // kvw
    assert S // SEL_LEN <= NSA_HD
    kv_spec = lambda idx: pl.BlockSpec((rb, kvw), lambda i: (i, ck + idx))
    out_k = pl.BlockSpec((rb, kvw), lambda i: (i, 0))
    out_ka = pl.BlockSpec((rb, 2 * kvw), lambda i: (i, 0))
    out_vt = pl.BlockSpec((None, NSA_GROUPS, VT_ROWS, rb), lambda i: (i // nsb, 0, 0, i % nsb))
    vt_shape = jax.ShapeDtypeStruct((B, NSA_GROUPS, VT_ROWS, S), BF16)
    return pl.pallas_call(
        functools.partial(_nsa_prep_kernel, rb=rb, nsb=nsb),
        grid=(T // rb,),
        in_specs=[
            pl.BlockSpec((rb, qw), lambda i: (i, cq)),
            kv_spec(2), kv_spec(3), kv_spec(4), kv_spec(5),
            pl.BlockSpec((1, NSA_HD), lambda i: (0, 0)),
            pl.BlockSpec((N_BRANCH, NSA_HD), lambda i: (0, 0)),
        ],
        out_specs=[pl.BlockSpec((rb, qw), lambda i: (i, 0)), out_ka, out_vt, out_k, out_vt],
        out_shape=[jax.ShapeDtypeStruct((T, qw), BF16), jax.ShapeDtypeStruct((T, 2 * kvw), BF16), vt_shape,
                   jax.ShapeDtypeStruct((T, kvw), BF16), vt_shape],
        compiler_params=pltpu.CompilerParams(dimension_semantics=("arbitrary",), vmem_limit_bytes=VMEM_LIMIT),
        name="nsa_prep",
    )(u, u, u, u, u, q_norm_w, k_norm_w)


def _compress_kernel(kc_ref, vc_ref, pek_ref, pev_ref, kw1_ref, kw2_ref, vw1_ref, vw2t_ref, kn_ref,
                     kcmp_ref, vcmp_ref, *, ns):
    def hidden(x_ref, pe_ref, w1_ref):
        hid = CMP_LEN // 2
        a = jnp.zeros((ns, w1_ref.shape[1]), F32)
        bsum = jnp.zeros((ns, w1_ref.shape[1]), F32)
        for l in range(hid):
            xl = x_ref[pl.ds(l, ns, stride=CMP_STRIDE), :]
            a = a + _dot((xl + pe_ref[l:l + 1, :]).astype(BF16), w1_ref[l * NSA_HD:(l + 1) * NSA_HD, :])
            bsum = bsum + _dot((xl + pe_ref[hid + l:hid + l + 1, :]).astype(BF16),
                               w1_ref[(hid + l) * NSA_HD:(hid + l + 1) * NSA_HD, :])
        pre = a + pltpu.roll(bsum, ns - 1, 0)
        return _silu(pre).astype(BF16)

    kc = _dot(hidden(kc_ref, pek_ref, kw1_ref), kw2_ref[...])
    ms = jnp.mean(kc * kc, axis=-1, keepdims=True)
    kcmp_ref[...] = (kc * lax.rsqrt(ms + EPS) * kn_ref[0:1, :]).astype(BF16)
    vcmp_ref[...] = _dot_nt(vw2t_ref[...], hidden(vc_ref, pev_ref, vw1_ref)).astype(BF16)


def _compress(u, pe_k, pe_v, kw1, kw2, vw1, vw2, k_norm_w, col_kv, B, S):
    ns = S // CMP_STRIDE
    ck = col_kv // NSA_HD
    hidden = kw1.shape[1]
    const = lambda b, g: (0, 0)
    return pl.pallas_call(
        functools.partial(_compress_kernel, ns=ns),
        grid=(B, NSA_GROUPS),
        in_specs=[
            pl.BlockSpec((S, NSA_HD), lambda b, g: (b, ck + g)),
            pl.BlockSpec((S, NSA_HD), lambda b, g: (b, ck + NSA_GROUPS + g)),
            pl.BlockSpec((CMP_LEN, NSA_HD), const),
            pl.BlockSpec((CMP_LEN, NSA_HD), const),
            pl.BlockSpec((CMP_LEN * NSA_HD, hidden), const),
            pl.BlockSpec((hidden, NSA_HD), const),
            pl.BlockSpec((CMP_LEN * NSA_HD, hidden), const),
            pl.BlockSpec((NSA_HD, hidden), const),
            pl.BlockSpec((N_BRANCH, NSA_HD), const),
        ],
        out_specs=[pl.BlockSpec((None, None, ns, NSA_HD), lambda b, g: (b, g, 0, 0)),
                   pl.BlockSpec((None, None, NSA_HD, ns), lambda b, g: (b, g, 0, 0))],
        out_shape=[jax.ShapeDtypeStruct((B, NSA_GROUPS, ns, NSA_HD), BF16),
                   jax.ShapeDtypeStruct((B, NSA_GROUPS, NSA_HD, ns), BF16)],
        compiler_params=pltpu.CompilerParams(dimension_semantics=("arbitrary", "arbitrary"),
                                             vmem_limit_bytes=VMEM_LIMIT),
        name="compress",
    )(u, u, pe_k, pe_v, kw1, kw2, vw1, vw2.T, k_norm_w)


TQC = 256
NEAR = 32
NEAR_BACK = 16


def _stack_heads(q4):
    return jnp.concatenate([q4[:, h * NSA_HD:(h + 1) * NSA_HD] for h in range(NSA_HPG)], axis=0)


def _cmp_sel_kernel(tbl_ref, q_ref, kc_ref, vct_ref, oc_ref, sel_ref, pat_ref, *, ns, nsel):
    g = pl.program_id(1)
    i = pl.program_id(2)
    t0 = i * TQC
    c0 = i * (TQC // CMP_STRIDE)
    W = NSA_HPG * TQC
    assert (TQC - CMP_LEN) // CMP_STRIDE < NEAR - NEAR_BACK and NEAR_BACK * CMP_STRIDE >= REL_MAX_DIST + CMP_LEN

    @pl.when(i == 0)
    def _():
        r = _iota((TQC, LANES), 0)
        lane = _iota((TQC, LANES), 1)
        d = r - CMP_STRIDE * (jnp.bitwise_and(lane, NEAR - 1) - NEAR_BACK) - (CMP_LEN - 1)
        for h in range(NSA_HPG):
            val = jnp.where(d < 0, NEG, _rel_bias_pattern(d, tbl_ref, g * NSA_HPG + h))
            hi = val.astype(BF16).astype(F32)
            lo = jnp.where(d < 0, 0.0, val - hi)
            ext = jnp.where(lane < NEAR, hi, jnp.where(lane < 2 * NEAR, lo, jnp.where(lane == 2 * NEAR, NEG, 0.0)))
            pat_ref[h * TQC:(h + 1) * TQC, :] = ext.astype(BF16)

    q4 = q_ref[...]
    q_aug = jnp.concatenate([jnp.concatenate([q4[:, h * NSA_HD:(h + 1) * NSA_HD],
                                              pat_ref[h * TQC:(h + 1) * TQC, :]], axis=1)
                             for h in range(NSA_HPG)], axis=0)
    rel = _iota((ns, LANES), 0) - (c0 - NEAR_BACK)
    lane = _iota((ns, LANES), 1)
    ext = ((lane < 2 * NEAR) & (rel == jnp.bitwise_and(lane, NEAR - 1))) | ((lane == 2 * NEAR) & (rel >= NEAR))
    k_aug = jnp.concatenate([kc_ref[...], jnp.where(ext, 1.0, 0.0).astype(BF16)], axis=1)

    s = _dot_nt(k_aug, q_aug)
    m = jnp.max(s, axis=0, keepdims=True)
    e = jnp.exp2(s - m)
    l = jnp.sum(e, axis=0, keepdims=True)
    tq = t0 + jnp.bitwise_and(_iota((1, W), 1), TQC - 1)
    inv = jnp.where(tq >= CMP_LEN - 1, 1.0 / l, 0.0)
    o_t = _dot(vct_ref[...], e.astype(BF16)) * inv
    for h in range(NSA_HPG):
        oc_ref[:, h * NSA_HD:(h + 1) * NSA_HD] = o_t[:, h * TQC:(h + 1) * TQC].T

    p = e * inv
    ps = p[:, 0:TQC]
    for h in range(1, NSA_HPG):
        ps = ps + p[:, h * TQC:(h + 1) * TQC]
    ratio = SEL_LEN // CMP_STRIDE
    blk_n = _iota((nsel, ns), 0)
    tok_c = _iota((nsel, ns), 1)
    ov = ((tok_c >= ratio * blk_n - (CMP_LEN // CMP_STRIDE - 1)) & (tok_c <= ratio * blk_n + ratio - 1)).astype(F32)
    imp = _dot(ov, ps, precision=HIGHEST)

    blk = _iota((nsel, TQC), 0)
    cur = jnp.right_shift(t0 + _iota((nsel, TQC), 1), SEL_SHIFT)
    imp = jnp.where((blk == 0) | (blk == cur) | (blk == cur - 1), FORCE_SCORE, imp)
    imp = jnp.where(blk > cur, NEG, imp)
    tiles = [imp[8 * k:8 * k + 8, :] for k in range(nsel // 8)]
    counts = [jnp.zeros((8, TQC), jnp.int32) for _ in tiles]
    sub = _iota((8, TQC), 0)
    for jb in range(nsel):
        vj = imp[jb:jb + 1, :]
        for k, tile in enumerate(tiles):
            if 8 * k > jb:
                ahead = vj >= tile
            elif 8 * k + 7 < jb:
                ahead = vj > tile
            else:
                ahead = (vj > tile) | ((vj == tile) & (sub + 8 * k > jb))
            counts[k] = counts[k] + jnp.where(ahead, 1, 0)
    rank = jnp.concatenate(counts, axis=0)
    chosen = jnp.where(rank < min(SEL_TOPK, nsel), 1.0, 0.0).astype(BF16)
    chosen = jnp.concatenate([chosen, jnp.zeros((LANES - nsel, TQC), BF16)], axis=0)
    eye_q = (_iota((TQC, TQC), 0) == _iota((TQC, TQC), 1)).astype(BF16)
    sel_ref[...] = ((_dot_nt(eye_q, chosen) - 1.0) * (-NEG)).astype(BF16)


def _cmp_sel(rel_bias, qn, kcmp, vcmp_t, B, S):
    ns = S // CMP_STRIDE
    nsel = S // SEL_LEN
    nt = S // TQC
    gw = NSA_HPG * NSA_HD
    assert nsel % 8 == 0 and nsel <= LANES
    return pl.pallas_call(
        functools.partial(_cmp_sel_kernel, ns=ns, nsel=nsel),
        grid=(B, NSA_GROUPS, nt),
        in_specs=[
            pl.BlockSpec(memory_space=pltpu.SMEM),
            pl.BlockSpec((TQC, gw), lambda b, g, i: (b * nt + i, g)),
            pl.BlockSpec((None, None, ns, NSA_HD), lambda b, g, i: (b, g, 0, 0)),
            pl.BlockSpec((None, None, NSA_HD, ns), lambda b, g, i: (b, g, 0, 0)),
        ],
        out_specs=[
            pl.BlockSpec((TQC, gw), lambda b, g, i: (b * nt + i, g)),
            pl.BlockSpec((None, None, TQC, LANES), lambda b, g, i: (b, g, i, 0)),
        ],
        out_shape=[jax.ShapeDtypeStruct((B * S, NSA_HEADS * NSA_HD), F32),
                   jax.ShapeDtypeStruct((B, NSA_GROUPS, S, LANES), BF16)],
        scratch_shapes=[pltpu.VMEM((NSA_HPG * TQC, LANES), BF16)],
        compiler_params=pltpu.CompilerParams(dimension_semantics=("arbitrary", "arbitrary", "arbitrary"),
                                             vmem_limit_bytes=VMEM_LIMIT),
        name="cmp_sel",
    )(rel_bias, qn, kcmp, vcmp_t)


TA = 512
SUB = 128


def _attend_kernel(tbl_ref, q_ref, ksa_ref, vst_ref, kw_ref, vwt_ref, sel_ref, oc_ref, g_ref, z_ref, y_ref,
                   pd_ref, pp_ref, sa_ref, sb_ref, ms_ref, accs_ref, mw_ref, accw_ref, *, gate_col):
    g = pl.program_id(1)
    i = pl.program_id(2)
    nsub = TA // SUB

    krow = _iota((SUB, SUB), 0)
    qcol = _iota((SUB, SUB), 1)

    @pl.when(i == 0)
    def _():
        for h in range(NSA_HPG):
            head = g * NSA_HPG + h
            lanes = slice(h * SUB, (h + 1) * SUB)
            pd_ref[:, lanes] = jnp.where(krow <= qcol, _rel_bias_pattern(qcol - krow, tbl_ref, head), NEG)
            pp_ref[:, lanes] = _rel_bias_pattern(qcol - krow + SUB, tbl_ref, head)

    q4 = q_ref[...]
    sel = sel_ref[...]
    q_win = _stack_heads(q4)
    q_sel = jnp.concatenate([jnp.concatenate([q4[:, h * NSA_HD:(h + 1) * NSA_HD], sel], axis=1)
                             for h in range(NSA_HPG)], axis=0)

    def chunk(c):
        return pl.ds(pl.multiple_of(c * TA, TA), TA)

    def sub(h, kb, qb):
        return slice(kb * SUB, (kb + 1) * SUB), slice(h * TA + qb * SUB, h * TA + (qb + 1) * SUB)

    def band_diag(s_ref):
        for h in range(NSA_HPG):
            lanes = slice(h * SUB, (h + 1) * SUB)
            for qb in range(nsub):
                for kb in range(nsub):
                    if kb > qb:
                        s_ref[sub(h, kb, qb)] = jnp.full((SUB, SUB), NEG, F32)
                    elif kb == qb:
                        s_ref[sub(h, kb, qb)] += pd_ref[:, lanes]
                    elif kb == qb - 1:
                        s_ref[sub(h, kb, qb)] += pp_ref[:, lanes]

    def band_prev(s_ref, window):
        for h in range(NSA_HPG):
            s_ref[sub(h, nsub - 1, 0)] += pp_ref[:, h * SUB:(h + 1) * SUB]
            if window:
                for qb in range(nsub):
                    for kb in range(nsub):
                        if kb < qb:
                            s_ref[sub(h, kb, qb)] = jnp.full((SUB, SUB), NEG, F32)
                        elif kb == qb:
                            s_ref[sub(h, kb, qb)] += jnp.where(krow > qcol, 0.0, NEG)

    def absorb(s_ref, vt, m_ref, acc_ref, first=False):
        mx = jnp.max(s_ref[...], axis=0, keepdims=True)
        if first:
            m_new = mx
        else:
            m_old = m_ref[...]
            m_new = jnp.maximum(m_old, mx)
        p = jnp.exp2(s_ref[...] - m_new).astype(BF16)
        pv = _dot(vt, p)
        if first:
            acc_ref[...] = pv
        else:
            acc_ref[...] = jnp.exp2(m_old - m_new) * acc_ref[...] + pv
        m_ref[...] = m_new

    def qk_sel(s_ref, c):
        s_ref[...] = _dot_nt(ksa_ref[chunk(c), :], q_sel)

    def qk_win(s_ref, c):
        s_ref[...] = _dot_nt(kw_ref[chunk(c), :], q_win)

    @pl.when(i == 0)
    def _():
        qk_sel(sa_ref, 0)
        band_diag(sa_ref)
        qk_win(sb_ref, 0)
        band_diag(sb_ref)
        absorb(sa_ref, vst_ref[:, chunk(0)], ms_ref, accs_ref, first=True)
        absorb(sb_ref, vwt_ref[:, chunk(0)], mw_ref, accw_ref, first=True)

    @pl.when(i >= 1)
    def _():
        qk_sel(sa_ref, i)
        band_diag(sa_ref)
        qk_sel(sb_ref, i - 1)
        band_prev(sb_ref, window=False)
        absorb(sa_ref, vst_ref[:, chunk(i)], ms_ref, accs_ref, first=True)
        qk_win(sa_ref, i)
        band_diag(sa_ref)
        absorb(sb_ref, vst_ref[:, chunk(i - 1)], ms_ref, accs_ref)
        qk_win(sb_ref, i - 1)
        band_prev(sb_ref, window=True)
        absorb(sa_ref, vwt_ref[:, chunk(i)], mw_ref, accw_ref, first=True)
        absorb(sb_ref, vwt_ref[:, chunk(i - 1)], mw_ref, accw_ref)

    nfar = jnp.maximum(i - 1, 0)
    odd = nfar % 2

    @pl.when(odd == 1)
    def _():
        qk_sel(sa_ref, 0)
        absorb(sa_ref, vst_ref[:, chunk(0)], ms_ref, accs_ref)

    npair = nfar // 2

    @pl.when(npair > 0)
    def _():
        qk_sel(sa_ref, odd)

    def pair(t, carry):
        c = odd + 2 * t
        qk_sel(sb_ref, c + 1)
        absorb(sa_ref, vst_ref[:, chunk(c)], ms_ref, accs_ref)
        qk_sel(sa_ref, jnp.minimum(c + 2, nfar - 1))
        absorb(sb_ref, vst_ref[:, chunk(c + 1)], ms_ref, accs_ref)
        return carry

    lax.fori_loop(0, npair, pair, 0)

    o_s = accs_ref[0:NSA_HD, :] / accs_ref[NSA_HD:NSA_HD + 1, :]
    o_w = accw_ref[0:NSA_HD, :] / accw_ref[NSA_HD:NSA_HD + 1, :]
    gs = _sigmoid(g_ref[:, 0:LANES])
    lane = _iota((TA, LANES), 1)
    for h in range(NSA_HPG):
        base = gate_col + (g * NSA_HPG + h) * N_BRANCH
        gate = [jnp.sum(jnp.where(lane == base + br, gs, 0.0), axis=-1, keepdims=True) for br in range(N_BRANCH)]
        cols = slice(h * NSA_HD, (h + 1) * NSA_HD)
        lanes = slice(h * TA, (h + 1) * TA)
        o = gate[0] * oc_ref[:, cols] + gate[1] * o_s[:, lanes].T + gate[2] * o_w[:, lanes].T
        y_ref[:, cols] = (o * _silu(z_ref[:, cols])).astype(BF16)


def _attend(rel_bias, qn, ksa, vst, kwn, vwt, sel, o_c, gates, gate_blk, u, col_z, gate_col, B, S):
    nt = S // TA
    gw = NSA_HPG * NSA_HD
    W = NSA_HPG * TA
    cz = col_z // gw
    assert WIN == TA
    tile = lambda b, g, i: (b * nt + i, g)
    vt_spec = pl.BlockSpec((None, None, VT_ROWS, S), lambda b, g, i: (b, g, 0, 0))
    return pl.pallas_call(
        functools.partial(_attend_kernel, gate_col=gate_col),
        grid=(B, NSA_GROUPS, nt),
        in_specs=[
            pl.BlockSpec(memory_space=pltpu.SMEM),
            pl.BlockSpec((TA, gw), tile),
            pl.BlockSpec((S, 2 * NSA_HD), lambda b, g, i: (b, g)), vt_spec,
            pl.BlockSpec((S, NSA_HD), lambda b, g, i: (b, g)), vt_spec,
            pl.BlockSpec((None, None, TA, LANES), lambda b, g, i: (b, g, i, 0)),
            pl.BlockSpec((TA, gw), tile),
            pl.BlockSpec((TA, 2 * LANES), lambda b, g, i: (b * nt + i, gate_blk)),
            pl.BlockSpec((TA, gw), lambda b, g, i: (b * nt + i, cz + g)),
        ],
        out_specs=pl.BlockSpec((TA, gw), tile),
        out_shape=jax.ShapeDtypeStruct((B * S, NSA_HEADS * NSA_HD), BF16),
        scratch_shapes=[pltpu.VMEM((SUB, NSA_HPG * SUB), F32), pltpu.VMEM((SUB, NSA_HPG * SUB), F32),
                        pltpu.VMEM((TA, W), F32), pltpu.VMEM((TA, W), F32),
                        pltpu.VMEM((1, W), F32), pltpu.VMEM((VT_ROWS, W), F32),
                        pltpu.VMEM((1, W), F32), pltpu.VMEM((VT_ROWS, W), F32)],
        compiler_params=pltpu.CompilerParams(dimension_semantics=("arbitrary", "arbitrary", "arbitrary"),
                                             vmem_limit_bytes=VMEM_LIMIT),
        name="attend",
    )(rel_bias, qn, ksa, vst, kwn, vwt, sel, o_c, gates, u)


def _outproj_kernel(yml_ref, yns_ref, w_ref, x_ref, o_ref, ob_ref, wb_ref):
    @pl.when(pl.program_id(1) == 0)
    def _():
        wb_ref[...] = w_ref[...].astype(BF16)

    half = yml_ref.shape[1]
    o = x_ref[...] + _dot(yml_ref[...], wb_ref[0:half, :]) + _dot(yns_ref[...], wb_ref[half:2 * half, :])
    o_ref[...] = o
    ob_ref[...] = o.astype(BF16)


def _outproj(y_ml, y_ns, w_out, x2, tm=512, tn=512):
    T, D = x2.shape
    half = y_ml.shape[1]
    tile = pl.BlockSpec((tm, tn), lambda j, i: (i, j))
    return pl.pallas_call(
        _outproj_kernel,
        grid=(D // tn, T // tm),
        in_specs=[
            pl.BlockSpec((tm, half), lambda j, i: (i, 0)),
            pl.BlockSpec((tm, half), lambda j, i: (i, 0)),
            pl.BlockSpec((2 * half, tn), lambda j, i: (0, j)),
            tile,
        ],
        out_specs=[tile, tile],
        out_shape=[jax.ShapeDtypeStruct((T, D), F32), jax.ShapeDtypeStruct((T, D), BF16)],
        scratch_shapes=[pltpu.VMEM((2 * half, tn), BF16)],
        compiler_params=pltpu.CompilerParams(dimension_semantics=("arbitrary", "arbitrary"),
                                             vmem_limit_bytes=VMEM_LIMIT),
        name="outproj",
    )(y_ml, y_ns, w_out, x2)


def _ple_kernel(xb_ref, x_ref, p_ref, wg_ref, wp_ref, o_ref, wgb_ref, wpb_ref):
    @pl.when(pl.program_id(1) == 0)
    def _():
        wgb_ref[...] = wg_ref[...].astype(BF16)
        wpb_ref[...] = wp_ref[...].astype(BF16)

    gate = _sigmoid(_dot(xb_ref[...], wgb_ref[...]))
    emb = _dot(p_ref[...].astype(BF16), wpb_ref[...])
    o_ref[...] = x_ref[...] + gate * emb


def _ple(x1b, x1, p2, wg, wp, tm=512, tn=512):
    T, D = x1.shape
    P = p2.shape[1]
    tile = pl.BlockSpec((tm, tn), lambda j, i: (i, j))
    return pl.pallas_call(
        _ple_kernel,
        grid=(D // tn, T // tm),
        in_specs=[
            pl.BlockSpec((tm, D), lambda j, i: (i, 0)),
            tile,
            pl.BlockSpec((tm, P), lambda j, i: (i, 0)),
            pl.BlockSpec((D, tn), lambda j, i: (0, j)),
            pl.BlockSpec((P, tn), lambda j, i: (0, j)),
        ],
        out_specs=tile,
        out_shape=jax.ShapeDtypeStruct((T, D), F32),
        scratch_shapes=[pltpu.VMEM((D, tn), BF16), pltpu.VMEM((P, tn), BF16)],
        compiler_params=pltpu.CompilerParams(dimension_semantics=("arbitrary", "arbitrary"),
                                             vmem_limit_bytes=VMEM_LIMIT),
        name="ple",
    )(x1b, x1, p2, wg, wp)


def _layer(x2, p2, norm_w, w_in, conv_w, i_bias, f_bias, head_norm_w, q_norm_w, k_norm_w,
           pe_k, pe_v, kw1, kw2, vw1, vw2, rel_bias, w_out, ple_proj, ple_gate, B, S):
    D = x2.shape[1]
    qkw = ML_HEADS * ML_DQK
    vw = ML_HEADS * ML_DV
    nq = NSA_HEADS * NSA_HD
    nkv = NSA_GROUPS * NSA_HD
    o_i = 2 * qkw + 3 * vw
    o_f = o_i + ML_HEADS
    o_nq = o_f + ML_HEADS
    o_g = o_nq + nq + 6 * nkv
    o_z = o_g + NSA_HEADS * N_BRANCH
    ngate = NSA_HEADS * N_BRANCH
    zeros = lambda n: jnp.zeros((D, n), w_in.dtype)
    w_nsa = jnp.concatenate([w_in[:, o_nq:o_g], w_in[:, o_z:],
                             w_in[:, o_i:o_f], w_in[:, o_g:o_z], zeros(LANES - ML_HEADS - ngate),
                             w_in[:, o_f:o_nq], zeros(LANES - ML_HEADS), zeros(2 * LANES)], axis=1).astype(BF16)
    col_kv = nq
    col_z = col_kv + 6 * nkv
    gate_blk = (col_z + nq) // (2 * LANES)

    h = _rmsnorm(x2, norm_w.reshape(1, D))
    u_ml = _inproj(h, w_in, o_i, 1024, 512, "inproj_ml")
    u_ns = _inproj(h, w_nsa, w_nsa.shape[1], 1024, 512, "inproj_nsa")

    pad_h = lambda v: jnp.concatenate([v, jnp.zeros((LANES - ML_HEADS,), v.dtype)]).reshape(1, LANES)
    y_ml = _mlstm(u_ml, u_ns, gate_blk, conv_w, pad_h(i_bias), pad_h(f_bias), head_norm_w, B, S)

    qn, ksa, vst, kwn, vwt = _nsa_prep(u_ns, q_norm_w.reshape(1, NSA_HD), k_norm_w, 0, col_kv, B, S)
    kcmp, vcmp_t = _compress(u_ns, pe_k, pe_v, kw1.astype(BF16), kw2.astype(BF16), vw1.astype(BF16),
                             vw2.astype(BF16), k_norm_w, col_kv, B, S)
    o_c, sel = _cmp_sel(rel_bias, qn, kcmp, vcmp_t, B, S)
    y_ns = _attend(rel_bias, qn, ksa, vst, kwn, vwt, sel, o_c, u_ns, gate_blk, u_ns, col_z, ML_HEADS, B, S)

    x1, x1b = _outproj(y_ml, y_ns, w_out, x2)
    return _ple(x1b, x1, p2, ple_gate, ple_proj)


def kernel(x, p, norm_w, w_in, ml_conv_w, ml_i_bias, ml_f_bias, ml_head_norm_w, nsa_q_norm_w, nsa_k_norm_w,
           cmp_pe_k, cmp_pe_v, cmp_k_w1, cmp_k_w2, cmp_v_w1, cmp_v_w2, rel_bias, w_out, ple_proj, ple_gate):
    B, S, D = x.shape
    assert S % max(WIN, 256) == 0 and S // SEL_LEN >= 1
    x2 = x.reshape(B * S, D)
    for layer in range(w_in.shape[0]):
        x2 = _layer(x2, p[layer].reshape(B * S, -1), norm_w[layer], w_in[layer], ml_conv_w[layer],
                    ml_i_bias[layer], ml_f_bias[layer], ml_head_norm_w[layer], nsa_q_norm_w[layer],
                    nsa_k_norm_w[layer], cmp_pe_k[layer], cmp_pe_v[layer], cmp_k_w1[layer], cmp_k_w2[layer],
                    cmp_v_w1[layer], cmp_v_w2[layer], rel_bias, w_out[layer], ple_proj[layer], ple_gate[layer],
                    B, S)
    return x2.reshape(B, S, D)
```

```python
import functools
import math

import numpy as np
import jax
import jax.numpy as jnp
from jax import lax
from jax.experimental import pallas as pl
from jax.experimental.pallas import tpu as pltpu

F32 = jnp.float32
BF16 = jnp.bfloat16
HIGHEST = lax.Precision.HIGHEST

ML_HEADS = 8
ML_DQK = 128
ML_DV = 256
ML_CHUNK = 64
ML_CONV = 4
NSA_HEADS = 16
NSA_HD = 128
NSA_GROUPS = 4
NSA_HPG = 4
N_BRANCH = 3
CMP_STRIDE = 16
CMP_LEN = 32
SEL_LEN = 64
SEL_SHIFT = 6
SEL_TOPK = 16
WIN = 512
REL_BUCKETS = 32
REL_MAX_DIST = 128
EPS = 1e-6
NEG = -1e30
FORCE_SCORE = 1e4
LOG2E = math.log2(math.e)

LANES = 128
VMEM_LIMIT = 48 * 1024 * 1024


def _bucket_thresholds():
    n = np.arange(0, 4 * REL_MAX_DIST, dtype=np.int64)
    max_exact = REL_BUCKETS // 2
    nf = np.maximum(n, 1).astype(np.float32)
    large = max_exact + (np.log(nf / np.float32(max_exact)) / np.float32(math.log(REL_MAX_DIST / max_exact))
                         * np.float32(REL_BUCKETS - max_exact)).astype(np.int32)
    large = np.minimum(large, REL_BUCKETS - 1)
    bucket = np.where(n < max_exact, n, large)
    assert np.all(np.diff(bucket) >= 0)
    thr = [int(np.argmax(bucket >= b)) for b in range(REL_BUCKETS)]
    assert thr[REL_BUCKETS - 1] <= REL_MAX_DIST
    return thr


BUCKET_THR = _bucket_thresholds()


def _dot(a, b, precision=None):
    return jnp.dot(a, b, preferred_element_type=F32, precision=precision)


def _dot_nt(a, b, precision=None):
    return lax.dot_general(a, b, (((1,), (1,)), ((), ())), preferred_element_type=F32, precision=precision)


def _sigmoid(x):
    return 1.0 / (1.0 + jnp.exp(-x))


def _silu(x):
    return x * _sigmoid(x)


def _iota(shape, dim):
    return lax.broadcasted_iota(jnp.int32, shape, dim)


def _rel_bias_pattern(dist, tbl_ref, head):
    val = jnp.full(dist.shape, tbl_ref[0, head], F32)
    for b in range(1, REL_BUCKETS):
        val = jnp.where(dist >= BUCKET_THR[b], tbl_ref[b, head], val)
    return (val - tbl_ref[REL_BUCKETS - 1, head]) * LOG2E


def _rmsnorm_kernel(x_ref, nw_ref, wg_ref, h_ref, g_ref, wgb_ref):
    @pl.when(pl.program_id(0) == 0)
    def _():
        wgb_ref[...] = wg_ref[...].astype(BF16)

    x = x_ref[...]
    ms = jnp.mean(x * x, axis=-1, keepdims=True)
    h = (x * lax.rsqrt(ms + EPS) * nw_ref[...]).astype(BF16)
    h_ref[...] = h
    g_ref[...] = _dot_nt(h, wgb_ref[...])


def _rmsnorm(x2, norm_w, wg_t, rb=128):
    T, D = x2.shape
    NG = wg_t.shape[0]
    return pl.pallas_call(
        _rmsnorm_kernel,
        grid=(T // rb,),
        in_specs=[pl.BlockSpec((rb, D), lambda i: (i, 0)), pl.BlockSpec((1, D), lambda i: (0, 0)),
                  pl.BlockSpec((NG, D), lambda i: (0, 0))],
        out_specs=[pl.BlockSpec((rb, D), lambda i: (i, 0)), pl.BlockSpec((rb, NG), lambda i: (i, 0))],
        out_shape=[jax.ShapeDtypeStruct((T, D), BF16), jax.ShapeDtypeStruct((T, NG), F32)],
        scratch_shapes=[pltpu.VMEM((NG, D), BF16)],
        compiler_params=pltpu.CompilerParams(dimension_semantics=("arbitrary",), vmem_limit_bytes=VMEM_LIMIT),
        name="rmsnorm",
    )(x2, norm_w, wg_t)


def _inproj_kernel(h_ref, wt_ref, u_ref, wb_ref):
    @pl.when(pl.program_id(1) == 0)
    def _():
        wb_ref[...] = wt_ref[...].astype(BF16)

    u_ref[...] = _dot_nt(h_ref[...], wb_ref[...])


def _inproj(h, w_t, pieces, tm=1024, tn=512):
    T, D = h.shape
    starts = []
    for start, length in pieces:
        assert length % tn == 0 and start % 8 == 0
        starts += [start + k * tn for k in range(length // tn)]
    nblk = len(starts)
    bounds = [(b, s) for b, s in enumerate(starts) if b == 0 or s != starts[b - 1] + tn]

    def w_rows(j, i):
        row8 = (bounds[0][1] + j * tn) // 8
        for b, s in bounds[1:]:
            row8 = jnp.where(j >= b, (s + (j - b) * tn) // 8, row8)
        return row8 * 8, 0

    return pl.pallas_call(
        _inproj_kernel,
        grid=(nblk, T // tm),
        in_specs=[
            pl.BlockSpec((tm, D), lambda j, i: (i, 0)),
            pl.BlockSpec((pl.Element(tn), pl.Element(D)), w_rows),
        ],
        out_specs=pl.BlockSpec((tm, tn), lambda j, i: (i, j)),
        out_shape=jax.ShapeDtypeStruct((T, nblk * tn), F32),
        scratch_shapes=[pltpu.VMEM((tn, D), BF16)],
        compiler_params=pltpu.CompilerParams(dimension_semantics=("arbitrary", "arbitrary"),
                                             vmem_limit_bytes=VMEM_LIMIT),
        name="inproj",
    )(h, w_t)


def _mlstm_kernel(q_ref, k_ref, v_ref, o_ref, z_ref, g_ref, cw_ref, ib_ref, fb_ref, hw_ref, y_ref,
                  xbuf, qa, ka, c_ref, m_ref, *, lb):
    L = ML_CHUNK
    qkw = ML_HEADS * ML_DQK

    @pl.when(pl.program_id(1) == 0)
    def _():
        xbuf[0:8, :] = jnp.zeros((8, 2 * qkw), F32)
        c_ref[...] = jnp.zeros(c_ref.shape, F32)
        m_ref[...] = jnp.zeros(m_ref.shape, F32)

    xbuf[8:8 + lb, 0:qkw] = q_ref[...]
    xbuf[8:8 + lb, qkw:2 * qkw] = k_ref[...]
    acc = cw_ref[ML_CONV - 1:ML_CONV, :] * xbuf[8:8 + lb, :]
    for j in range(1, ML_CONV):
        acc = acc + cw_ref[ML_CONV - 1 - j:ML_CONV - j, :] * xbuf[8 - j:8 - j + lb, :]
    act = _silu(acc)
    qa[...] = act[:, 0:qkw].astype(BF16)
    ka[...] = act[:, qkw:2 * qkw] * (ML_DQK ** -0.5)
    xbuf[0:8, :] = xbuf[lb:lb + 8, :]

    row = _iota((L, L), 0)
    col = _iota((L, L), 1)
    tril = col <= row
    rowg = _iota((L, LANES), 0)
    eye_h = (_iota((8, LANES), 0) == _iota((8, LANES), 1)).astype(F32)
    eye_k = (_iota((ML_DQK, ML_DQK), 0) == _iota((ML_DQK, ML_DQK), 1)).astype(BF16)
    ones_col = (_iota((L, LANES), 1) == 0).astype(BF16)

    def chunk(c, carry):
        rows = pl.ds(pl.multiple_of(c * L, L), L)
        G = g_ref[rows, :]
        li = G[:, 0:LANES] + ib_ref[...]
        fp = G[:, LANES:2 * LANES] + fb_ref[...]
        lf = jnp.minimum(fp, 0.0) - jnp.log(1.0 + jnp.exp(-jnp.abs(fp)))
        b = lf
        for sh in (1, 2, 4, 8, 16, 32):
            b = b + jnp.where(rowg >= sh, pltpu.roll(b, sh, 0), 0.0)
        g_rows = _dot_nt(eye_h, li - b, precision=HIGHEST)

        for h in range(ML_HEADS):
            li_c = li[:, h:h + 1]
            b_c = b[:, h:h + 1]
            m = m_ref[h:h + 1, 0:1]
            D = jnp.where(tril, b_c + g_rows[h:h + 1, :], NEG)
            a_c = b_c + m
            m_t = jnp.maximum(a_c, jnp.max(D, axis=-1, keepdims=True))
            Dw = jnp.exp(D - m_t)
            inter = jnp.exp(a_c - m_t)
            q = qa[rows, h * ML_DQK:(h + 1) * ML_DQK]
            kf = ka[rows, h * ML_DQK:(h + 1) * ML_DQK]
            sc = (_dot_nt(q, kf.astype(BF16)) * Dw).astype(BF16)
            vaug = jnp.concatenate([v_ref[rows, h * ML_DV:(h + 1) * ML_DV].astype(BF16), ones_col], axis=1)
            cst = c_ref[h]
            num_aug = inter * _dot(q, cst.astype(BF16)) + _dot(sc, vaug)
            num = num_aug[:, 0:ML_DV]
            den = num_aug[:, ML_DV:ML_DV + 1]
            hh = num / jnp.maximum(jnp.abs(den), jnp.exp(-m_t))

            bL = b_c[L - 1:L, :]
            w_c = bL - b_c + li_c
            m_new = jnp.maximum(bL + m, jnp.max(w_c, axis=0, keepdims=True))
            wk = jnp.exp(w_c - m_new)
            decay = jnp.exp(bL + m - m_new)
            kw_t = _dot_nt(eye_k, (wk * kf).astype(BF16)).astype(BF16)
            c_ref[h] = decay * cst + _dot(kw_t, vaug)
            m_ref[h:h + 1, :] = jnp.broadcast_to(m_new, (1, LANES))

            ms = jnp.mean(hh * hh, axis=-1, keepdims=True)
            hn = hh * lax.rsqrt(ms + EPS) * hw_ref[h:h + 1, :]
            og = _sigmoid(o_ref[rows, h * ML_DV:(h + 1) * ML_DV])
            zg = _silu(z_ref[rows, h * ML_DV:(h + 1) * ML_DV])
            y_ref[rows, h * ML_DV:(h + 1) * ML_DV] = (hn * og * zg).astype(BF16)
        return carry

    lax.fori_loop(0, lb // L, chunk, 0)


def _mlstm(u, gates, gate_blk, conv_w, ib, fb, head_w, B, S, lb=256):
    T = B * S
    nsb = S // lb
    qkw = ML_HEADS * ML_DQK
    vw = ML_HEADS * ML_DV
    rowmap = lambda col: (lambda b, s: (b * nsb + s, col))
    const = lambda b, s: (0, 0)
    return pl.pallas_call(
        functools.partial(_mlstm_kernel, lb=lb),
        grid=(B, nsb),
        in_specs=[
            pl.BlockSpec((lb, qkw), rowmap(0)),
            pl.BlockSpec((lb, qkw), rowmap(1)),
            pl.BlockSpec((lb, vw), rowmap(1)),
            pl.BlockSpec((lb, vw), rowmap(2)),
            pl.BlockSpec((lb, vw), rowmap(3)),
            pl.BlockSpec((lb, 2 * LANES), rowmap(gate_blk)),
            pl.BlockSpec((ML_CONV, 2 * qkw), const),
            pl.BlockSpec((1, LANES), const),
            pl.BlockSpec((1, LANES), const),
            pl.BlockSpec((ML_HEADS, ML_DV), const),
        ],
        out_specs=pl.BlockSpec((lb, vw), rowmap(0)),
        out_shape=jax.ShapeDtypeStruct((T, vw), BF16),
        scratch_shapes=[
            pltpu.VMEM((lb + 8, 2 * qkw), F32),
            pltpu.VMEM((lb, qkw), BF16),
            pltpu.VMEM((lb, qkw), F32),
            pltpu.VMEM((ML_HEADS, ML_DQK, ML_DV + LANES), F32),
            pltpu.VMEM((ML_HEADS, LANES), F32),
        ],
        compiler_params=pltpu.CompilerParams(dimension_semantics=("arbitrary", "arbitrary"),
                                             vmem_limit_bytes=VMEM_LIMIT),
        name="mlstm",
    )(u, u, u, u, u, gates, conv_w, ib, fb, head_w)


def _rms_heads(x, w, scale=1.0):
    outs = []
    for h in range(x.shape[1] // NSA_HD):
        xh = x[:, h * NSA_HD:(h + 1) * NSA_HD]
        ms = jnp.mean(xh * xh, axis=-1, keepdims=True)
        outs.append(xh * lax.rsqrt(ms + EPS) * w * scale)
    return jnp.concatenate(outs, axis=1)


VT_ROWS = NSA_HD + 16


def _nsa_prep_kernel(q_ref, ks_ref, vs_ref, kw_ref, vw_ref, qw_ref, kn_ref,
                     qn_ref, ksa_ref, vst_ref, kwn_ref, vwt_ref, *, rb, nsb):
    qn_ref[...] = _rms_heads(q_ref[...], qw_ref[...], NSA_HD ** -0.5 * LOG2E).astype(BF16)
    ksn = _rms_heads(ks_ref[...], kn_ref[1:2, :]).astype(BF16)
    kwn_ref[...] = _rms_heads(kw_ref[...], kn_ref[2:3, :]).astype(BF16)
    t = (pl.program_id(0) % nsb) * rb + _iota((rb, NSA_HD), 0)
    onehot = (_iota((rb, NSA_HD), 1) == jnp.right_shift(t, SEL_SHIFT)).astype(BF16)
    tail = (_iota((VT_ROWS - NSA_HD, rb), 0) == 0).astype(BF16)
    for g in range(NSA_GROUPS):
        cols = slice(g * NSA_HD, (g + 1) * NSA_HD)
        ksa_ref[:, 2 * g * NSA_HD:(2 * g + 1) * NSA_HD] = ksn[:, cols]
        ksa_ref[:, (2 * g + 1) * NSA_HD:(2 * g + 2) * NSA_HD] = onehot
        vst_ref[g] = jnp.concatenate([vs_ref[:, cols].T.astype(BF16), tail], axis=0)
        vwt_ref[g] = jnp.concatenate([vw_ref[:, cols].T.astype(BF16), tail], axis=0)


def _nsa_prep(u, q_norm_w, k_norm_w, col_q, col_kv, B, S, rb=256):
    T = u.shape[0]
    nsb = S // rb
    qw = NSA_HEADS * NSA_HD
    kvw = NSA_GROUPS * NSA_HD
    cq = col_q // qw
    ck = col_kv // kvw
    assert S // SEL_LEN <= NSA_HD
    kv_spec = lambda idx: pl.BlockSpec((rb, kvw), lambda i: (i, ck + idx))
    out_k = pl.BlockSpec((rb, kvw), lambda i: (i, 0))
    out_ka = pl.BlockSpec((rb, 2 * kvw), lambda i: (i, 0))
    out_vt = pl.BlockSpec((None, NSA_GROUPS, VT_ROWS, rb), lambda i: (i // nsb, 0, 0, i % nsb))
    vt_shape = jax.ShapeDtypeStruct((B, NSA_GROUPS, VT_ROWS, S), BF16)
    return pl.pallas_call(
        functools.partial(_nsa_prep_kernel, rb=rb, nsb=nsb),
        grid=(T // rb,),
        in_specs=[
            pl.BlockSpec((rb, qw), lambda i: (i, cq)),
            kv_spec(2), kv_spec(3), kv_spec(4), kv_spec(5),
            pl.BlockSpec((1, NSA_HD), lambda i: (0, 0)),
            pl.BlockSpec((N_BRANCH, NSA_HD), lambda i: (0, 0)),
        ],
        out_specs=[pl.BlockSpec((rb, qw), lambda i: (i, 0)), out_ka, out_vt, out_k, out_vt],
        out_shape=[jax.ShapeDtypeStruct((T, qw), BF16), jax.ShapeDtypeStruct((T, 2 * kvw), BF16), vt_shape,
                   jax.ShapeDtypeStruct((T, kvw), BF16), vt_shape],
        compiler_params=pltpu.CompilerParams(dimension_semantics=("arbitrary",), vmem_limit_bytes=VMEM_LIMIT),
        name="nsa_prep",
    )(u, u, u, u, u, q_norm_w, k_norm_w)


def _compress_kernel(kc_ref, vc_ref, pek_ref, pev_ref, kw1_ref, kw2_ref, vw1_ref, vw2t_ref, kn_ref,
                     kcmp_ref, vcmp_ref, *, ns):
    def hidden(x_ref, pe_ref, w1_ref):
        hid = CMP_LEN // 2
        a = jnp.zeros((ns, w1_ref.shape[1]), F32)
        bsum = jnp.zeros((ns, w1_ref.shape[1]), F32)
        for l in range(hid):
            xl = x_ref[pl.ds(l, ns, stride=CMP_STRIDE), :]
            a = a + _dot((xl + pe_ref[l:l + 1, :]).astype(BF16), w1_ref[l * NSA_HD:(l + 1) * NSA_HD, :])
            bsum = bsum + _dot((xl + pe_ref[hid + l:hid + l + 1, :]).astype(BF16),
                               w1_ref[(hid + l) * NSA_HD:(hid + l + 1) * NSA_HD, :])
        pre = a + pltpu.roll(bsum, ns - 1, 0)
        return _silu(pre).astype(BF16)

    kc = _dot(hidden(kc_ref, pek_ref, kw1_ref), kw2_ref[...])
    ms = jnp.mean(kc * kc, axis=-1, keepdims=True)
    kcmp_ref[...] = (kc * lax.rsqrt(ms + EPS) * kn_ref[0:1, :]).astype(BF16)
    vcmp_ref[...] = _dot_nt(vw2t_ref[...], hidden(vc_ref, pev_ref, vw1_ref)).astype(BF16)


def _compress(u, pe_k, pe_v, kw1, kw2, vw1, vw2, k_norm_w, col_kv, B, S):
    ns = S // CMP_STRIDE
    ck = col_kv // NSA_HD
    hidden = kw1.shape[1]
    const = lambda b, g: (0, 0)
    return pl.pallas_call(
        functools.partial(_compress_kernel, ns=ns),
        grid=(B, NSA_GROUPS),
        in_specs=[
            pl.BlockSpec((S, NSA_HD), lambda b, g: (b, ck + g)),
            pl.BlockSpec((S, NSA_HD), lambda b, g: (b, ck + NSA_GROUPS + g)),
            pl.BlockSpec((CMP_LEN, NSA_HD), const),
            pl.BlockSpec((CMP_LEN, NSA_HD), const),
            pl.BlockSpec((CMP_LEN * NSA_HD, hidden), const),
            pl.BlockSpec((hidden, NSA_HD), const),
            pl.BlockSpec((CMP_LEN * NSA_HD, hidden), const),
            pl.BlockSpec((NSA_HD, hidden), const),
            pl.BlockSpec((N_BRANCH, NSA_HD), const),
        ],
        out_specs=[pl.BlockSpec((None, None, ns, NSA_HD), lambda b, g: (b, g, 0, 0)),
                   pl.BlockSpec((None, None, NSA_HD, ns), lambda b, g: (b, g, 0, 0))],
        out_shape=[jax.ShapeDtypeStruct((B, NSA_GROUPS, ns, NSA_HD), BF16),
                   jax.ShapeDtypeStruct((B, NSA_GROUPS, NSA_HD, ns), BF16)],
        compiler_params=pltpu.CompilerParams(dimension_semantics=("arbitrary", "arbitrary"),
                                             vmem_limit_bytes=VMEM_LIMIT),
        name="compress",
    )(u, u, pe_k, pe_v, kw1, kw2, vw1, vw2.T, k_norm_w)


TQC = 256
NEAR = 32
NEAR_BACK = 16


def _stack_heads(q4):
    return jnp.concatenate([q4[:, h * NSA_HD:(h + 1) * NSA_HD] for h in range(NSA_HPG)], axis=0)


def _cmp_sel_kernel(tbl_ref, q_ref, kc_ref, vct_ref, oc_ref, sel_ref, pat_ref, *, ns, nsel):
    g = pl.program_id(1)
    i = pl.program_id(2)
    t0 = i * TQC
    c0 = i * (TQC // CMP_STRIDE)
    W = NSA_HPG * TQC
    assert (TQC - CMP_LEN) // CMP_STRIDE < NEAR - NEAR_BACK and NEAR_BACK * CMP_STRIDE >= REL_MAX_DIST + CMP_LEN

    @pl.when(i == 0)
    def _():
        r = _iota((TQC, LANES), 0)
        lane = _iota((TQC, LANES), 1)
        d = r - CMP_STRIDE * (jnp.bitwise_and(lane, NEAR - 1) - NEAR_BACK) - (CMP_LEN - 1)
        for h in range(NSA_HPG):
            val = jnp.where(d < 0, NEG, _rel_bias_pattern(d, tbl_ref, g * NSA_HPG + h))
            hi = val.astype(BF16).astype(F32)
            lo = jnp.where(d < 0, 0.0, val - hi)
            ext = jnp.where(lane < NEAR, hi, jnp.where(lane < 2 * NEAR, lo, jnp.where(lane == 2 * NEAR, NEG, 0.0)))
            pat_ref[h * TQC:(h + 1) * TQC, :] = ext.astype(BF16)

    q4 = q_ref[...]
    q_aug = jnp.concatenate([jnp.concatenate([q4[:, h * NSA_HD:(h + 1) * NSA_HD],
                                              pat_ref[h * TQC:(h + 1) * TQC, :]], axis=1)
                             for h in range(NSA_HPG)], axis=0)
    rel = _iota((ns, LANES), 0) - (c0 - NEAR_BACK)
    lane = _iota((ns, LANES), 1)
    ext = ((lane < 2 * NEAR) & (rel == jnp.bitwise_and(lane, NEAR - 1))) | ((lane == 2 * NEAR) & (rel >= NEAR))
    k_aug = jnp.concatenate([kc_ref[...], jnp.where(ext, 1.0, 0.0).astype(BF16)], axis=1)

    s = _dot_nt(k_aug, q_aug)
    m = jnp.max(s, axis=0, keepdims=True)
    e = jnp.exp2(s - m)
    l = jnp.sum(e, axis=0, keepdims=True)
    tq = t0 + jnp.bitwise_and(_iota((1, W), 1), TQC - 1)
    inv = jnp.where(tq >= CMP_LEN - 1, 1.0 / l, 0.0)
    o_t = _dot(vct_ref[...], e.astype(BF16)) * inv
    for h in range(NSA_HPG):
        oc_ref[:, h * NSA_HD:(h + 1) * NSA_HD] = o_t[:, h * TQC:(h + 1) * TQC].T

    p = e * inv
    ps = p[:, 0:TQC]
    for h in range(1, NSA_HPG):
        ps = ps + p[:, h * TQC:(h + 1) * TQC]
    ratio = SEL_LEN // CMP_STRIDE
    blk_n = _iota((nsel, ns), 0)
    tok_c = _iota((nsel, ns), 1)
    ov = ((tok_c >= ratio * blk_n - (CMP_LEN // CMP_STRIDE - 1)) & (tok_c <= ratio * blk_n + ratio - 1)).astype(F32)
    imp = _dot(ov, ps, precision=HIGHEST)

    blk = _iota((nsel, TQC), 0)
    cur = jnp.right_shift(t0 + _iota((nsel, TQC), 1), SEL_SHIFT)
    imp = jnp.where((blk == 0) | (blk == cur) | (blk == cur - 1), FORCE_SCORE, imp)
    imp = jnp.where(blk > cur, NEG, imp)
    tiles = [imp[8 * k:8 * k + 8, :] for k in range(nsel // 8)]
    counts = [jnp.zeros((8, TQC), jnp.int32) for _ in tiles]
    sub = _iota((8, TQC), 0)
    for jb in range(nsel):
        vj = imp[jb:jb + 1, :]
        for k, tile in enumerate(tiles):
            if 8 * k > jb:
                ahead = vj >= tile
            elif 8 * k + 7 < jb:
                ahead = vj > tile
            else:
                ahead = (vj > tile) | ((vj == tile) & (sub + 8 * k > jb))
            counts[k] = counts[k] + jnp.where(ahead, 1, 0)
    rank = jnp.concatenate(counts, axis=0)
    chosen = jnp.where(rank < min(SEL_TOPK, nsel), 1.0, 0.0).astype(BF16)
    chosen = jnp.concatenate([chosen, jnp.zeros((LANES - nsel, TQC), BF16)], axis=0)
    eye_q = (_iota((TQC, TQC), 0) == _iota((TQC, TQC), 1)).astype(BF16)
    sel_ref[...] = ((_dot_nt(eye_q, chosen) - 1.0) * (-NEG)).astype(BF16)


def _cmp_sel(rel_bias, qn, kcmp, vcmp_t, B, S):
    ns = S // CMP_STRIDE
    nsel = S // SEL_LEN
    nt = S // TQC
    gw = NSA_HPG * NSA_HD
    assert nsel % 8 == 0 and nsel <= LANES
    return pl.pallas_call(
        functools.partial(_cmp_sel_kernel, ns=ns, nsel=nsel),
        grid=(B, NSA_GROUPS, nt),
        in_specs=[
            pl.BlockSpec(memory_space=pltpu.SMEM),
            pl.BlockSpec((TQC, gw), lambda b, g, i: (b * nt + i, g)),
            pl.BlockSpec((None, None, ns, NSA_HD), lambda b, g, i: (b, g, 0, 0)),
            pl.BlockSpec((None, None, NSA_HD, ns), lambda b, g, i: (b, g, 0, 0)),
        ],
        out_specs=[
            pl.BlockSpec((TQC, gw), lambda b, g, i: (b * nt + i, g)),
            pl.BlockSpec((None, None, TQC, LANES), lambda b, g, i: (b, g, i, 0)),
        ],
        out_shape=[jax.ShapeDtypeStruct((B * S, NSA_HEADS * NSA_HD), F32),
                   jax.ShapeDtypeStruct((B, NSA_GROUPS, S, LANES), BF16)],
        scratch_shapes=[pltpu.VMEM((NSA_HPG * TQC, LANES), BF16)],
        compiler_params=pltpu.CompilerParams(dimension_semantics=("arbitrary", "arbitrary", "arbitrary"),
                                             vmem_limit_bytes=VMEM_LIMIT),
        name="cmp_sel",
    )(rel_bias, qn, kcmp, vcmp_t)


TA = 512
SUB = 128


def _attend_kernel(tbl_ref, q_ref, ksa_ref, vst_ref, kw_ref, vwt_ref, sel_ref, oc_ref, g_ref, z_ref, y_ref,
                   pd_ref, pp_ref, sa_ref, sb_ref, ms_ref, accs_ref, mw_ref, accw_ref, *, gate_col):
    g = pl.program_id(1)
    i = pl.program_id(2)
    nsub = TA // SUB

    krow = _iota((SUB, SUB), 0)
    qcol = _iota((SUB, SUB), 1)

    @pl.when(i == 0)
    def _():
        for h in range(NSA_HPG):
            head = g * NSA_HPG + h
            lanes = slice(h * SUB, (h + 1) * SUB)
            pd_ref[:, lanes] = jnp.where(krow <= qcol, _rel_bias_pattern(qcol - krow, tbl_ref, head), NEG)
            pp_ref[:, lanes] = _rel_bias_pattern(qcol - krow + SUB, tbl_ref, head)

    q4 = q_ref[...]
    sel = sel_ref[...]
    q_win = _stack_heads(q4)
    q_sel = jnp.concatenate([jnp.concatenate([q4[:, h * NSA_HD:(h + 1) * NSA_HD], sel], axis=1)
                             for h in range(NSA_HPG)], axis=0)

    def chunk(c):
        return pl.ds(pl.multiple_of(c * TA, TA), TA)

    def sub(h, kb, qb):
        return slice(kb * SUB, (kb + 1) * SUB), slice(h * TA + qb * SUB, h * TA + (qb + 1) * SUB)

    def band_diag(s_ref):
        for h in range(NSA_HPG):
            lanes = slice(h * SUB, (h + 1) * SUB)
            for qb in range(nsub):
                for kb in range(nsub):
                    if kb > qb:
                        s_ref[sub(h, kb, qb)] = jnp.full((SUB, SUB), NEG, F32)
                    elif kb == qb:
                        s_ref[sub(h, kb, qb)] += pd_ref[:, lanes]
                    elif kb == qb - 1:
                        s_ref[sub(h, kb, qb)] += pp_ref[:, lanes]

    def band_prev(s_ref, window):
        for h in range(NSA_HPG):
            s_ref[sub(h, nsub - 1, 0)] += pp_ref[:, h * SUB:(h + 1) * SUB]
            if window:
                for qb in range(nsub):
                    for kb in range(nsub):
                        if kb < qb:
                            s_ref[sub(h, kb, qb)] = jnp.full((SUB, SUB), NEG, F32)
                        elif kb == qb:
                            s_ref[sub(h, kb, qb)] += jnp.where(krow > qcol, 0.0, NEG)

    def absorb(s_ref, vt, m_ref, acc_ref, first=False):
        mx = jnp.max(s_ref[...], axis=0, keepdims=True)
        if first:
            m_new = mx
        else:
            m_old = m_ref[...]
            m_new = jnp.maximum(m_old, mx)
        p = jnp.exp2(s_ref[...] - m_new).astype(BF16)
        pv = _dot(vt, p)
        if first:
            acc_ref[...] = pv
        else:
            acc_ref[...] = jnp.exp2(m_old - m_new) * acc_ref[...] + pv
        m_ref[...] = m_new

    def qk_sel(s_ref, c):
        s_ref[...] = _dot_nt(ksa_ref[chunk(c), :], q_sel)

    def qk_win(s_ref, c):
        s_ref[...] = _dot_nt(kw_ref[chunk(c), :], q_win)

    @pl.when(i == 0)
    def _():
        qk_sel(sa_ref, 0)
        band_diag(sa_ref)
        qk_win(sb_ref, 0)
        band_diag(sb_ref)
        absorb(sa_ref, vst_ref[:, chunk(0)], ms_ref, accs_ref, first=True)
        absorb(sb_ref, vwt_ref[:, chunk(0)], mw_ref, accw_ref, first=True)

    @pl.when(i >= 1)
    def _():
        qk_sel(sa_ref, i)
        band_diag(sa_ref)
        qk_sel(sb_ref, i - 1)
        band_prev(sb_ref, window=False)
        absorb(sa_ref, vst_ref[:, chunk(i)], ms_ref, accs_ref, first=True)
        qk_win(sa_ref, i)
        band_diag(sa_ref)
        absorb(sb_ref, vst_ref[:, chunk(i - 1)], ms_ref, accs_ref)
        qk_win(sb_ref, i - 1)
        band_prev(sb_ref, window=True)
        absorb(sa_ref, vwt_ref[:, chunk(i)], mw_ref, accw_ref, first=True)
        absorb(sb_ref, vwt_ref[:, chunk(i - 1)], mw_ref, accw_ref)

    nfar = jnp.maximum(i - 1, 0)
    odd = nfar % 2

    @pl.when(odd == 1)
    def _():
        qk_sel(sa_ref, 0)
        absorb(sa_ref, vst_ref[:, chunk(0)], ms_ref, accs_ref)

    npair = nfar // 2

    @pl.when(npair > 0)
    def _():
        qk_sel(sa_ref, odd)

    def pair(t, carry):
        c = odd + 2 * t
        qk_sel(sb_ref, c + 1)
        absorb(sa_ref, vst_ref[:, chunk(c)], ms_ref, accs_ref)
        qk_sel(sa_ref, jnp.minimum(c + 2, nfar - 1))
        absorb(sb_ref, vst_ref[:, chunk(c + 1)], ms_ref, accs_ref)
        return carry

    lax.fori_loop(0, npair, pair, 0)

    o_s = accs_ref[0:NSA_HD, :] / accs_ref[NSA_HD:NSA_HD + 1, :]
    o_w = accw_ref[0:NSA_HD, :] / accw_ref[NSA_HD:NSA_HD + 1, :]
    gs = _sigmoid(g_ref[:, 0:LANES])
    lane = _iota((TA, LANES), 1)
    for h in range(NSA_HPG):
        base = gate_col + (g * NSA_HPG + h) * N_BRANCH
        gate = [jnp.sum(jnp.where(lane == base + br, gs, 0.0), axis=-1, keepdims=True) for br in range(N_BRANCH)]
        cols = slice(h * NSA_HD, (h + 1) * NSA_HD)
        lanes = slice(h * TA, (h + 1) * TA)
        o = gate[0] * oc_ref[:, cols] + gate[1] * o_s[:, lanes].T + gate[2] * o_w[:, lanes].T
        y_ref[:, cols] = (o * _silu(z_ref[:, cols])).astype(BF16)


def _attend(rel_bias, qn, ksa, vst, kwn, vwt, sel, o_c, gates, gate_blk, u, col_z, gate_col, B, S):
    nt = S // TA
    gw = NSA_HPG * NSA_HD
    W = NSA_HPG * TA
    cz = col_z // gw
    assert WIN == TA
    tile = lambda b, g, i: (b * nt + i, g)
    vt_spec = pl.BlockSpec((None, None, VT_ROWS, S), lambda b, g, i: (b, g, 0, 0))
    return pl.pallas_call(
        functools.partial(_attend_kernel, gate_col=gate_col),
        grid=(B, NSA_GROUPS, nt),
        in_specs=[
            pl.BlockSpec(memory_space=pltpu.SMEM),
            pl.BlockSpec((TA, gw), tile),
            pl.BlockSpec((S, 2 * NSA_HD), lambda b, g, i: (b, g)), vt_spec,
            pl.BlockSpec((S, NSA_HD), lambda b, g, i: (b, g)), vt_spec,
            pl.BlockSpec((None, None, TA, LANES), lambda b, g, i: (b, g, i, 0)),
            pl.BlockSpec((TA, gw), tile),
            pl.BlockSpec((TA, 2 * LANES), lambda b, g, i: (b * nt + i, gate_blk)),
            pl.BlockSpec((TA, gw), lambda b, g, i: (b * nt + i, cz + g)),
        ],
        out_specs=pl.BlockSpec((TA, gw), tile),
        out_shape=jax.ShapeDtypeStruct((B * S, NSA_HEADS * NSA_HD), BF16),
        scratch_shapes=[pltpu.VMEM((SUB, NSA_HPG * SUB), F32), pltpu.VMEM((SUB, NSA_HPG * SUB), F32),
                        pltpu.VMEM((TA, W), F32), pltpu.VMEM((TA, W), F32),
                        pltpu.VMEM((1, W), F32), pltpu.VMEM((VT_ROWS, W), F32),
                        pltpu.VMEM((1, W), F32), pltpu.VMEM((VT_ROWS, W), F32)],
        compiler_params=pltpu.CompilerParams(dimension_semantics=("arbitrary", "arbitrary", "arbitrary"),
                                             vmem_limit_bytes=VMEM_LIMIT),
        name="attend",
    )(rel_bias, qn, ksa, vst, kwn, vwt, sel, o_c, gates, u)


def _outproj_kernel(yml_ref, yns_ref, w_ref, x_ref, o_ref, ob_ref, wb_ref):
    @pl.when(pl.program_id(1) == 0)
    def _():
        wb_ref[...] = w_ref[...].astype(BF16)

    half = yml_ref.shape[1]
    o = x_ref[...] + _dot(yml_ref[...], wb_ref[0:half, :]) + _dot(yns_ref[...], wb_ref[half:2 * half, :])
    o_ref[...] = o
    ob_ref[...] = o.astype(BF16)


def _outproj(y_ml, y_ns, w_out, x2, tm=512, tn=512):
    T, D = x2.shape
    half = y_ml.shape[1]
    tile = pl.BlockSpec((tm, tn), lambda j, i: (i, j))
    return pl.pallas_call(
        _outproj_kernel,
        grid=(D // tn, T // tm),
        in_specs=[
            pl.BlockSpec((tm, half), lambda j, i: (i, 0)),
            pl.BlockSpec((tm, half), lambda j, i: (i, 0)),
            pl.BlockSpec((2 * half, tn), lambda j, i: (0, j)),
            tile,
        ],
        out_specs=[tile, tile],
        out_shape=[jax.ShapeDtypeStruct((T, D), F32), jax.ShapeDtypeStruct((T, D), BF16)],
        scratch_shapes=[pltpu.VMEM((2 * half, tn), BF16)],
        compiler_params=pltpu.CompilerParams(dimension_semantics=("arbitrary", "arbitrary"),
                                             vmem_limit_bytes=VMEM_LIMIT),
        name="outproj",
    )(y_ml, y_ns, w_out, x2)


def _ple_kernel(xb_ref, x_ref, p_ref, wg_ref, wp_ref, o_ref, wgb_ref, wpb_ref):
    @pl.when(pl.program_id(1) == 0)
    def _():
        wgb_ref[...] = wg_ref[...].astype(BF16)
        wpb_ref[...] = wp_ref[...].astype(BF16)

    gate = _sigmoid(_dot(xb_ref[...], wgb_ref[...]))
    emb = _dot(p_ref[...].astype(BF16), wpb_ref[...])
    o_ref[...] = x_ref[...] + gate * emb


def _ple(x1b, x1, p2, wg, wp, tm=512, tn=512):
    T, D = x1.shape
    P = p2.shape[1]
    tile = pl.BlockSpec((tm, tn), lambda j, i: (i, j))
    return pl.pallas_call(
        _ple_kernel,
        grid=(D // tn, T // tm),
        in_specs=[
            pl.BlockSpec((tm, D), lambda j, i: (i, 0)),
            tile,
            pl.BlockSpec((tm, P), lambda j, i: (i, 0)),
            pl.BlockSpec((D, tn), lambda j, i: (0, j)),
            pl.BlockSpec((P, tn), lambda j, i: (0, j)),
        ],
        out_specs=tile,
        out_shape=jax.ShapeDtypeStruct((T, D), F32),
        scratch_shapes=[pltpu.VMEM((D, tn), BF16), pltpu.VMEM((P, tn), BF16)],
        compiler_params=pltpu.CompilerParams(dimension_semantics=("arbitrary", "arbitrary"),
                                             vmem_limit_bytes=VMEM_LIMIT),
        name="ple",
    )(x1b, x1, p2, wg, wp)


def _layer(x2, p2, norm_w, w_in, conv_w, i_bias, f_bias, head_norm_w, q_norm_w, k_norm_w,
           pe_k, pe_v, kw1, kw2, vw1, vw2, rel_bias, w_out, ple_proj, ple_gate, B, S):
    D = x2.shape[1]
    qkw = ML_HEADS * ML_DQK
    vw = ML_HEADS * ML_DV
    nq = NSA_HEADS * NSA_HD
    nkv = NSA_GROUPS * NSA_HD
    o_i = 2 * qkw + 3 * vw
    o_f = o_i + ML_HEADS
    o_nq = o_f + ML_HEADS
    o_g = o_nq + nq + 6 * nkv
    o_z = o_g + NSA_HEADS * N_BRANCH
    w_t = w_in.T
    ngate = NSA_HEADS * N_BRANCH
    zeros = lambda n: jnp.zeros((n, D), w_in.dtype)
    wg_t = jnp.concatenate([w_t[o_i:o_f], w_t[o_g:o_z], zeros(LANES - ML_HEADS - ngate),
                            w_t[o_f:o_nq], zeros(LANES - ML_HEADS)], axis=0)
    col_nq = o_i
    col_kv = col_nq + nq
    col_z = col_kv + 6 * nkv

    h, gates = _rmsnorm(x2, norm_w.reshape(1, D), wg_t)
    u = _inproj(h, w_t, [(0, o_i), (o_nq, o_g - o_nq), (o_z, nq)])

    pad_h = lambda v: jnp.concatenate([v, jnp.zeros((LANES - ML_HEADS,), v.dtype)]).reshape(1, LANES)
    y_ml = _mlstm(u, gates, 0, conv_w, pad_h(i_bias), pad_h(f_bias), head_norm_w, B, S)

    qn, ksa, vst, kwn, vwt = _nsa_prep(u, q_norm_w.reshape(1, NSA_HD), k_norm_w, col_nq, col_kv, B, S)
    kcmp, vcmp_t = _compress(u, pe_k, pe_v, kw1.astype(BF16), kw2.astype(BF16), vw1.astype(BF16),
                             vw2.astype(BF16), k_norm_w, col_kv, B, S)
    o_c, sel = _cmp_sel(rel_bias, qn, kcmp, vcmp_t, B, S)
    y_ns = _attend(rel_bias, qn, ksa, vst, kwn, vwt, sel, o_c, gates, 0, u, col_z, ML_HEADS, B, S)

    x1, x1b = _outproj(y_ml, y_ns, w_out, x2)
    return _ple(x1b, x1, p2, ple_gate, ple_proj)


def kernel(x, p, norm_w, w_in, ml_conv_w, ml_i_bias, ml_f_bias, ml_head_norm_w, nsa_q_norm_w, nsa_k_norm_w,
           cmp_pe_k, cmp_pe_v, cmp_k_w1, cmp_k_w2, cmp_v_w1, cmp_v_w2, rel_bias, w_out, ple_proj, ple_gate):
    B, S, D = x.shape
    assert S % max(WIN, 256) == 0 and S // SEL_LEN >= 1
    x2 = x.reshape(B * S, D)
    for layer in range(w_in.shape[0]):
        x2 = _layer(x2, p[layer].reshape(B * S, -1), norm_w[layer], w_in[layer], ml_conv_w[layer],
                    ml_i_bias[layer], ml_f_bias[layer], ml_head_norm_w[layer], nsa_q_norm_w[layer],
                    nsa_k_norm_w[layer], cmp_pe_k[layer], cmp_pe_v[layer], cmp_k_w1[layer], cmp_k_w2[layer],
                    cmp_v_w1[layer], cmp_v_w2[layer], rel_bias, w_out[layer], ple_proj[layer], ple_gate[layer],
                    B, S)
    return x2.reshape(B, S, D)
```

```python
import functools
import math

import numpy as np
import jax
import jax.numpy as jnp
from jax import lax
from jax.experimental import pallas as pl
from jax.experimental.pallas import tpu as pltpu

F32 = jnp.float32
BF16 = jnp.bfloat16
HIGHEST = lax.Precision.HIGHEST

ML_HEADS = 8
ML_DQK = 128
ML_DV = 256
ML_CHUNK = 64
ML_CONV = 4
NSA_HEADS = 16
NSA_HD = 128
NSA_GROUPS = 4
NSA_HPG = 4
N_BRANCH = 3
CMP_STRIDE = 16
CMP_LEN = 32
SEL_LEN = 64
SEL_SHIFT = 6
SEL_TOPK = 16
WIN = 512
REL_BUCKETS = 32
REL_MAX_DIST = 128
EPS = 1e-6
NEG = -1e30
FORCE_SCORE = 1e4
LOG2E = math.log2(math.e)

LANES = 128
VMEM_LIMIT = 48 * 1024 * 1024


def _bucket_thresholds():
    n = np.arange(0, 4 * REL_MAX_DIST, dtype=np.int64)
    max_exact = REL_BUCKETS // 2
    nf = np.maximum(n, 1).astype(np.float32)
    large = max_exact + (np.log(nf / np.float32(max_exact)) / np.float32(math.log(REL_MAX_DIST / max_exact))
                         * np.float32(REL_BUCKETS - max_exact)).astype(np.int32)
    large = np.minimum(large, REL_BUCKETS - 1)
    bucket = np.where(n < max_exact, n, large)
    assert np.all(np.diff(bucket) >= 0)
    thr = [int(np.argmax(bucket >= b)) for b in range(REL_BUCKETS)]
    assert thr[REL_BUCKETS - 1] <= REL_MAX_DIST
    return thr


BUCKET_THR = _bucket_thresholds()


def _dot(a, b, precision=None):
    return jnp.dot(a, b, preferred_element_type=F32, precision=precision)


def _dot_nt(a, b, precision=None):
    return lax.dot_general(a, b, (((1,), (1,)), ((), ())), preferred_element_type=F32, precision=precision)


def _sigmoid(x):
    return 1.0 / (1.0 + jnp.exp(-x))


def _silu(x):
    return x * _sigmoid(x)


def _sigmoid_t(x):
    return 0.5 * jnp.tanh(0.5 * x) + 0.5


def _iota(shape, dim):
    return lax.broadcasted_iota(jnp.int32, shape, dim)


def _rel_bias_pattern(dist, tbl_ref, head):
    val = jnp.full(dist.shape, tbl_ref[0, head], F32)
    for b in range(1, REL_BUCKETS):
        val = jnp.where(dist >= BUCKET_THR[b], tbl_ref[b, head], val)
    return (val - tbl_ref[REL_BUCKETS - 1, head]) * LOG2E


def _rmsnorm_kernel(x_ref, nw_ref, wg_ref, h_ref, g_ref, wgb_ref):
    @pl.when(pl.program_id(0) == 0)
    def _():
        wgb_ref[...] = wg_ref[...].astype(BF16)

    x = x_ref[...]
    ms = jnp.mean(x * x, axis=-1, keepdims=True)
    h = (x * lax.rsqrt(ms + EPS) * nw_ref[...]).astype(BF16)
    h_ref[...] = h
    g_ref[...] = _dot_nt(h, wgb_ref[...])


def _rmsnorm(x2, norm_w, wg_t, rb=128):
    T, D = x2.shape
    NG = wg_t.shape[0]
    return pl.pallas_call(
        _rmsnorm_kernel,
        grid=(T // rb,),
        in_specs=[pl.BlockSpec((rb, D), lambda i: (i, 0)), pl.BlockSpec((1, D), lambda i: (0, 0)),
                  pl.BlockSpec((NG, D), lambda i: (0, 0))],
        out_specs=[pl.BlockSpec((rb, D), lambda i: (i, 0)), pl.BlockSpec((rb, NG), lambda i: (i, 0))],
        out_shape=[jax.ShapeDtypeStruct((T, D), BF16), jax.ShapeDtypeStruct((T, NG), F32)],
        scratch_shapes=[pltpu.VMEM((NG, D), BF16)],
        compiler_params=pltpu.CompilerParams(dimension_semantics=("arbitrary",), vmem_limit_bytes=VMEM_LIMIT),
        name="rmsnorm",
    )(x2, norm_w, wg_t)


def _inproj_kernel(h_ref, wt_ref, u_ref, wb_ref):
    @pl.when(pl.program_id(1) == 0)
    def _():
        wb_ref[...] = wt_ref[...].astype(BF16)

    u_ref[...] = _dot_nt(h_ref[...], wb_ref[...])


def _inproj(h, w_t, pieces, tm=1024, tn=512):
    T, D = h.shape
    starts = []
    for start, length in pieces:
        assert length % tn == 0 and start % 8 == 0
        starts += [start + k * tn for k in range(length // tn)]
    nblk = len(starts)
    bounds = [(b, s) for b, s in enumerate(starts) if b == 0 or s != starts[b - 1] + tn]

    def w_rows(j, i):
        row8 = (bounds[0][1] + j * tn) // 8
        for b, s in bounds[1:]:
            row8 = jnp.where(j >= b, (s + (j - b) * tn) // 8, row8)
        return row8 * 8, 0

    return pl.pallas_call(
        _inproj_kernel,
        grid=(nblk, T // tm),
        in_specs=[
            pl.BlockSpec((tm, D), lambda j, i: (i, 0)),
            pl.BlockSpec((pl.Element(tn), pl.Element(D)), w_rows),
        ],
        out_specs=pl.BlockSpec((tm, tn), lambda j, i: (i, j)),
        out_shape=jax.ShapeDtypeStruct((T, nblk * tn), F32),
        scratch_shapes=[pltpu.VMEM((tn, D), BF16)],
        compiler_params=pltpu.CompilerParams(dimension_semantics=("arbitrary", "arbitrary"),
                                             vmem_limit_bytes=VMEM_LIMIT),
        name="inproj",
    )(h, w_t)


def _mlstm_kernel(q_ref, k_ref, v_ref, o_ref, z_ref, g_ref, cw_ref, ib_ref, fb_ref, hw_ref, y_ref,
                  xbuf, qa, ka, c_ref, m_ref, *, lb):
    L = ML_CHUNK
    qkw = ML_HEADS * ML_DQK

    @pl.when(pl.program_id(1) == 0)
    def _():
        xbuf[0:8, :] = jnp.zeros((8, 2 * qkw), F32)
        c_ref[...] = jnp.zeros(c_ref.shape, F32)
        m_ref[...] = jnp.zeros(m_ref.shape, F32)

    xbuf[8:8 + lb, 0:qkw] = q_ref[...]
    xbuf[8:8 + lb, qkw:2 * qkw] = k_ref[...]
    acc = cw_ref[ML_CONV - 1:ML_CONV, :] * xbuf[8:8 + lb, :]
    for j in range(1, ML_CONV):
        acc = acc + cw_ref[ML_CONV - 1 - j:ML_CONV - j, :] * xbuf[8 - j:8 - j + lb, :]
    act = _silu(acc)
    qa[...] = act[:, 0:qkw].astype(BF16)
    ka[...] = act[:, qkw:2 * qkw] * (ML_DQK ** -0.5)
    xbuf[0:8, :] = xbuf[lb:lb + 8, :]

    row = _iota((L, L), 0)
    col = _iota((L, L), 1)
    tril = col <= row
    rowg = _iota((L, LANES), 0)
    eye_h = (_iota((8, LANES), 0) == _iota((8, LANES), 1)).astype(F32)
    ones_col = (_iota((L, LANES), 1) == 0).astype(BF16)
    heads = range(ML_HEADS)

    def chunk(c, carry):
        rows = pl.ds(pl.multiple_of(c * L, L), L)
        qk_cols = lambda h: slice(h * ML_DQK, (h + 1) * ML_DQK)
        v_cols = lambda h: slice(h * ML_DV, (h + 1) * ML_DV)
        col_of = lambda x, h: x[:, h:h + 1]

        G = g_ref[rows, :]
        li = G[:, 0:LANES] + ib_ref[...]
        fp = G[:, LANES:2 * LANES] + fb_ref[...]
        lf = jnp.minimum(fp, 0.0) - jnp.log(1.0 + jnp.exp(-jnp.abs(fp)))
        b = lf
        for sh in (1, 2, 4, 8, 16, 32):
            b = b + jnp.where(rowg >= sh, pltpu.roll(b, sh, 0), 0.0)
        g = li - b
        gmax = g
        for sh in (1, 2, 4, 8, 16, 32):
            gmax = jnp.maximum(gmax, jnp.where(rowg >= sh, pltpu.roll(gmax, sh, 0), NEG))
        m_old = m_ref[...]
        top = jnp.maximum(m_old, gmax)
        m_t = b + top
        inter = jnp.exp(m_old - top)
        floor = jnp.exp(-m_t)
        bL = b[L - 1:L, :]
        w = bL - b + li
        m_new = jnp.maximum(bL + m_old, jnp.max(w, axis=0, keepdims=True))
        wk = jnp.exp(w - m_new)
        decay = jnp.exp(bL + m_old - m_new)
        m_ref[...] = m_new
        g_rows = _dot_nt(eye_h, g, precision=HIGHEST)

        q = [qa[rows, qk_cols(h)] for h in heads]
        kf = [ka[rows, qk_cols(h)] for h in heads]
        vaug = [jnp.concatenate([v_ref[rows, v_cols(h)].astype(BF16), ones_col], axis=1) for h in heads]
        cst = [c_ref[h] for h in heads]
        lhs = []
        for h in heads:
            dw = jnp.exp(jnp.where(tril, g_rows[h:h + 1, :] - col_of(top, h), NEG))
            sc = (_dot_nt(q[h], kf[h].astype(BF16)) * dw).astype(BF16)
            q_in = (q[h].astype(F32) * col_of(inter, h)).astype(BF16)
            lhs.append(jnp.concatenate([q_in, sc], axis=1))
        num_aug = [_dot(lhs[h], jnp.concatenate([cst[h].astype(BF16), vaug[h]], axis=0)) for h in heads]
        for h in heads:
            kw = (kf[h] * col_of(wk, h)).astype(BF16)
            upd = lax.dot_general(kw, vaug[h], (((0,), (0,)), ((), ())), preferred_element_type=F32)
            c_ref[h] = decay[:, h:h + 1] * cst[h] + upd
        for h in heads:
            num = num_aug[h][:, 0:ML_DV]
            den = num_aug[h][:, ML_DV:ML_DV + 1]
            r = 1.0 / jnp.maximum(jnp.abs(den), col_of(floor, h))
            f = r * lax.rsqrt(r * r * jnp.mean(num * num, axis=-1, keepdims=True) + EPS)
            o = o_ref[rows, v_cols(h)]
            z = z_ref[rows, v_cols(h)]
            gate = _sigmoid_t(o) * z * _sigmoid_t(z)
            y_ref[rows, v_cols(h)] = (num * f * hw_ref[h:h + 1, :] * gate).astype(BF16)
        return carry

    lax.fori_loop(0, lb // L, chunk, 0)


def _mlstm(u, gates, gate_blk, conv_w, ib, fb, head_w, B, S, lb=256):
    T = B * S
    nsb = S // lb
    qkw = ML_HEADS * ML_DQK
    vw = ML_HEADS * ML_DV
    rowmap = lambda col: (lambda b, s: (b * nsb + s, col))
    const = lambda b, s: (0, 0)
    return pl.pallas_call(
        functools.partial(_mlstm_kernel, lb=lb),
        grid=(B, nsb),
        in_specs=[
            pl.BlockSpec((lb, qkw), rowmap(0)),
            pl.BlockSpec((lb, qkw), rowmap(1)),
            pl.BlockSpec((lb, vw), rowmap(1)),
            pl.BlockSpec((lb, vw), rowmap(2)),
            pl.BlockSpec((lb, vw), rowmap(3)),
            pl.BlockSpec((lb, 2 * LANES), rowmap(gate_blk)),
            pl.BlockSpec((ML_CONV, 2 * qkw), const),
            pl.BlockSpec((1, LANES), const),
            pl.BlockSpec((1, LANES), const),
            pl.BlockSpec((ML_HEADS, ML_DV), const),
        ],
        out_specs=pl.BlockSpec((lb, vw), rowmap(0)),
        out_shape=jax.ShapeDtypeStruct((T, vw), BF16),
        scratch_shapes=[
            pltpu.VMEM((lb + 8, 2 * qkw), F32),
            pltpu.VMEM((lb, qkw), BF16),
            pltpu.VMEM((lb, qkw), F32),
            pltpu.VMEM((ML_HEADS, ML_DQK, ML_DV + LANES), F32),
            pltpu.VMEM((1, LANES), F32),
        ],
        compiler_params=pltpu.CompilerParams(dimension_semantics=("arbitrary", "arbitrary"),
                                             vmem_limit_bytes=VMEM_LIMIT),
        name="mlstm",
    )(u, u, u, u, u, gates, conv_w, ib, fb, head_w)


def _rms_heads(x, w, scale=1.0):
    outs = []
    for h in range(x.shape[1] // NSA_HD):
        xh = x[:, h * NSA_HD:(h + 1) * NSA_HD]
        ms = jnp.mean(xh * xh, axis=-1, keepdims=True)
        outs.append(xh * lax.rsqrt(ms + EPS) * w * scale)
    return jnp.concatenate(outs, axis=1)


VT_ROWS = NSA_HD + 16


def _nsa_prep_kernel(q_ref, ks_ref, vs_ref, kw_ref, vw_ref, qw_ref, kn_ref,
                     qn_ref, ksa_ref, vst_ref, kwn_ref, vwt_ref, *, rb, nsb):
    qn_ref[...] = _rms_heads(q_ref[...], qw_ref[...], NSA_HD ** -0.5 * LOG2E).astype(BF16)
    ksn = _rms_heads(ks_ref[...], kn_ref[1:2, :]).astype(BF16)
    kwn_ref[...] = _rms_heads(kw_ref[...], kn_ref[2:3, :]).astype(BF16)
    t = (pl.program_id(0) % nsb) * rb + _iota((rb, NSA_HD), 0)
    onehot = (_iota((rb, NSA_HD), 1) == jnp.right_shift(t, SEL_SHIFT)).astype(BF16)
    tail = (_iota((VT_ROWS - NSA_HD, rb), 0) == 0).astype(BF16)
    for g in range(NSA_GROUPS):
        cols = slice(g * NSA_HD, (g + 1) * NSA_HD)
        ksa_ref[:, 2 * g * NSA_HD:(2 * g + 1) * NSA_HD] = ksn[:, cols]
        ksa_ref[:, (2 * g + 1) * NSA_HD:(2 * g + 2) * NSA_HD] = onehot
        vst_ref[g] = jnp.concatenate([vs_ref[:, cols].T.astype(BF16), tail], axis=0)
        vwt_ref[g] = jnp.concatenate([vw_ref[:, cols].T.astype(BF16), tail], axis=0)


def _nsa_prep(u, q_norm_w, k_norm_w, col_q, col_kv, B, S, rb=256):
    T = u.shape[0]
    nsb = S // rb
    qw = NSA_HEADS * NSA_HD
    kvw = NSA_GROUPS * NSA_HD
    cq = col_q // qw
    ck = col_kv // kvw
    assert S // SEL_LEN <= NSA_HD
    kv_spec = lambda idx: pl.BlockSpec((rb, kvw), lambda i: (i, ck + idx))
    out_k = pl.BlockSpec((rb, kvw), lambda i: (i, 0))
    out_ka = pl.BlockSpec((rb, 2 * kvw), lambda i: (i, 0))
    out_vt = pl.BlockSpec((None, NSA_GROUPS, VT_ROWS, rb), lambda i: (i // nsb, 0, 0, i % nsb))
    vt_shape = jax.ShapeDtypeStruct((B, NSA_GROUPS, VT_ROWS, S), BF16)
    return pl.pallas_call(
        functools.partial(_nsa_prep_kernel, rb=rb, nsb=nsb),
        grid=(T // rb,),
        in_specs=[
            pl.BlockSpec((rb, qw), lambda i: (i, cq)),
            kv_spec(2), kv_spec(3), kv_spec(4), kv_spec(5),
            pl.BlockSpec((1, NSA_HD), lambda i: (0, 0)),
            pl.BlockSpec((N_BRANCH, NSA_HD), lambda i: (0, 0)),
        ],
        out_specs=[pl.BlockSpec((rb, qw), lambda i: (i, 0)), out_ka, out_vt, out_k, out_vt],
        out_shape=[jax.ShapeDtypeStruct((T, qw), BF16), jax.ShapeDtypeStruct((T, 2 * kvw), BF16), vt_shape,
                   jax.ShapeDtypeStruct((T, kvw), BF16), vt_shape],
        compiler_params=pltpu.CompilerParams(dimension_semantics=("arbitrary",), vmem_limit_bytes=VMEM_LIMIT),
        name="nsa_prep",
    )(u, u, u, u, u, q_norm_w, k_norm_w)


def _compress_kernel(kc_ref, vc_ref, pek_ref, pev_ref, kw1_ref, kw2_ref, vw1_ref, vw2t_ref, kn_ref,
                     kcmp_ref, vcmp_ref, *, ns):
    def hidden(x_ref, pe_ref, w1_ref):
        hid = CMP_LEN // 2
        a = jnp.zeros((ns, w1_ref.shape[1]), F32)
        bsum = jnp.zeros((ns, w1_ref.shape[1]), F32)
        for l in range(hid):
            xl = x_ref[pl.ds(l, ns, stride=CMP_STRIDE), :]
            a = a + _dot((xl + pe_ref[l:l + 1, :]).astype(BF16), w1_ref[l * NSA_HD:(l + 1) * NSA_HD, :])
            bsum = bsum + _dot((xl + pe_ref[hid + l:hid + l + 1, :]).astype(BF16),
                               w1_ref[(hid + l) * NSA_HD:(hid + l + 1) * NSA_HD, :])
        pre = a + pltpu.roll(bsum, ns - 1, 0)
        return _silu(pre).astype(BF16)

    kc = _dot(hidden(kc_ref, pek_ref, kw1_ref), kw2_ref[...])
    ms = jnp.mean(kc * kc, axis=-1, keepdims=True)
    kcmp_ref[...] = (kc * lax.rsqrt(ms + EPS) * kn_ref[0:1, :]).astype(BF16)
    vcmp_ref[...] = _dot_nt(vw2t_ref[...], hidden(vc_ref, pev_ref, vw1_ref)).astype(BF16)


def _compress(u, pe_k, pe_v, kw1, kw2, vw1, vw2, k_norm_w, col_kv, B, S):
    ns = S // CMP_STRIDE
    ck = col_kv // NSA_HD
    hidden = kw1.shape[1]
    const = lambda b, g: (0, 0)
    return pl.pallas_call(
        functools.partial(_compress_kernel, ns=ns),
        grid=(B, NSA_GROUPS),
        in_specs=[
            pl.BlockSpec((S, NSA_HD), lambda b, g: (b, ck + g)),
            pl.BlockSpec((S, NSA_HD), lambda b, g: (b, ck + NSA_GROUPS + g)),
            pl.BlockSpec((CMP_LEN, NSA_HD), const),
            pl.BlockSpec((CMP_LEN, NSA_HD), const),
            pl.BlockSpec((CMP_LEN * NSA_HD, hidden), const),
            pl.BlockSpec((hidden, NSA_HD), const),
            pl.BlockSpec((CMP_LEN * NSA_HD, hidden), const),
            pl.BlockSpec((NSA_HD, hidden), const),
            pl.BlockSpec((N_BRANCH, NSA_HD), const),
        ],
        out_specs=[pl.BlockSpec((None, None, ns, NSA_HD), lambda b, g: (b, g, 0, 0)),
                   pl.BlockSpec((None, None, NSA_HD, ns), lambda b, g: (b, g, 0, 0))],
        out_shape=[jax.ShapeDtypeStruct((B, NSA_GROUPS, ns, NSA_HD), BF16),
                   jax.ShapeDtypeStruct((B, NSA_GROUPS, NSA_HD, ns), BF16)],
        compiler_params=pltpu.CompilerParams(dimension_semantics=("arbitrary", "arbitrary"),
                                             vmem_limit_bytes=VMEM_LIMIT),
        name="compress",
    )(u, u, pe_k, pe_v, kw1, kw2, vw1, vw2.T, k_norm_w)


TQC = 256
NEAR = 32
NEAR_BACK = 16


def _stack_heads(q4):
    return jnp.concatenate([q4[:, h * NSA_HD:(h + 1) * NSA_HD] for h in range(NSA_HPG)], axis=0)


def _cmp_sel_kernel(tbl_ref, q_ref, kc_ref, vct_ref, oc_ref, sel_ref, pat_ref, *, ns, nsel):
    g = pl.program_id(1)
    i = pl.program_id(2)
    t0 = i * TQC
    c0 = i * (TQC // CMP_STRIDE)
    W = NSA_HPG * TQC
    assert (TQC - CMP_LEN) // CMP_STRIDE < NEAR - NEAR_BACK and NEAR_BACK * CMP_STRIDE >= REL_MAX_DIST + CMP_LEN

    @pl.when(i == 0)
    def _():
        r = _iota((TQC, LANES), 0)
        lane = _iota((TQC, LANES), 1)
        d = r - CMP_STRIDE * (jnp.bitwise_and(lane, NEAR - 1) - NEAR_BACK) - (CMP_LEN - 1)
        for h in range(NSA_HPG):
            val = jnp.where(d < 0, NEG, _rel_bias_pattern(d, tbl_ref, g * NSA_HPG + h))
            hi = val.astype(BF16).astype(F32)
            lo = jnp.where(d < 0, 0.0, val - hi)
            ext = jnp.where(lane < NEAR, hi, jnp.where(lane < 2 * NEAR, lo, jnp.where(lane == 2 * NEAR, NEG, 0.0)))
            pat_ref[h * TQC:(h + 1) * TQC, :] = ext.astype(BF16)

    q4 = q_ref[...]
    q_aug = jnp.concatenate([jnp.concatenate([q4[:, h * NSA_HD:(h + 1) * NSA_HD],
                                              pat_ref[h * TQC:(h + 1) * TQC, :]], axis=1)
                             for h in range(NSA_HPG)], axis=0)
    rel = _iota((ns, LANES), 0) - (c0 - NEAR_BACK)
    lane = _iota((ns, LANES), 1)
    ext = ((lane < 2 * NEAR) & (rel == jnp.bitwise_and(lane, NEAR - 1))) | ((lane == 2 * NEAR) & (rel >= NEAR))
    k_aug = jnp.concatenate([kc_ref[...], jnp.where(ext, 1.0, 0.0).astype(BF16)], axis=1)

    s = _dot_nt(k_aug, q_aug)
    m = jnp.max(s, axis=0, keepdims=True)
    e = jnp.exp2(s - m)
    l = jnp.sum(e, axis=0, keepdims=True)
    tq = t0 + jnp.bitwise_and(_iota((1, W), 1), TQC - 1)
    inv = jnp.where(tq >= CMP_LEN - 1, 1.0 / l, 0.0)
    o_t = _dot(vct_ref[...], e.astype(BF16)) * inv
    for h in range(NSA_HPG):
        oc_ref[:, h * NSA_HD:(h + 1) * NSA_HD] = o_t[:, h * TQC:(h + 1) * TQC].T

    p = e * inv
    ps = p[:, 0:TQC]
    for h in range(1, NSA_HPG):
        ps = ps + p[:, h * TQC:(h + 1) * TQC]
    ratio = SEL_LEN // CMP_STRIDE
    blk_n = _iota((nsel, ns), 0)
    tok_c = _iota((nsel, ns), 1)
    ov = ((tok_c >= ratio * blk_n - (CMP_LEN // CMP_STRIDE - 1)) & (tok_c <= ratio * blk_n + ratio - 1)).astype(F32)
    imp = _dot(ov, ps, precision=HIGHEST)

    blk = _iota((nsel, TQC), 0)
    cur = jnp.right_shift(t0 + _iota((nsel, TQC), 1), SEL_SHIFT)
    imp = jnp.where((blk == 0) | (blk == cur) | (blk == cur - 1), FORCE_SCORE, imp)
    imp = jnp.where(blk > cur, NEG, imp)
    tiles = [imp[8 * k:8 * k + 8, :] for k in range(nsel // 8)]
    counts = [jnp.zeros((8, TQC), jnp.int32) for _ in tiles]
    sub = _iota((8, TQC), 0)
    for jb in range(nsel):
        vj = imp[jb:jb + 1, :]
        for k, tile in enumerate(tiles):
            if 8 * k > jb:
                ahead = vj >= tile
            elif 8 * k + 7 < jb:
                ahead = vj > tile
            else:
                ahead = (vj > tile) | ((vj == tile) & (sub + 8 * k > jb))
            counts[k] = counts[k] + jnp.where(ahead, 1, 0)
    rank = jnp.concatenate(counts, axis=0)
    chosen = jnp.where(rank < min(SEL_TOPK, nsel), 1.0, 0.0).astype(BF16)
    chosen = jnp.concatenate([chosen, jnp.zeros((LANES - nsel, TQC), BF16)], axis=0)
    eye_q = (_iota((TQC, TQC), 0) == _iota((TQC, TQC), 1)).astype(BF16)
    sel_ref[...] = ((_dot_nt(eye_q, chosen) - 1.0) * (-NEG)).astype(BF16)


def _cmp_sel(rel_bias, qn, kcmp, vcmp_t, B, S):
    ns = S // CMP_STRIDE
    nsel = S // SEL_LEN
    nt = S // TQC
    gw = NSA_HPG * NSA_HD
    assert nsel % 8 == 0 and nsel <= LANES
    return pl.pallas_call(
        functools.partial(_cmp_sel_kernel, ns=ns, nsel=nsel),
        grid=(B, NSA_GROUPS, nt),
        in_specs=[
            pl.BlockSpec(memory_space=pltpu.SMEM),
            pl.BlockSpec((TQC, gw), lambda b, g, i: (b * nt + i, g)),
            pl.BlockSpec((None, None, ns, NSA_HD), lambda b, g, i: (b, g, 0, 0)),
            pl.BlockSpec((None, None, NSA_HD, ns), lambda b, g, i: (b, g, 0, 0)),
        ],
        out_specs=[
            pl.BlockSpec((TQC, gw), lambda b, g, i: (b * nt + i, g)),
            pl.BlockSpec((None, None, TQC, LANES), lambda b, g, i: (b, g, i, 0)),
        ],
        out_shape=[jax.ShapeDtypeStruct((B * S, NSA_HEADS * NSA_HD), F32),
                   jax.ShapeDtypeStruct((B, NSA_GROUPS, S, LANES), BF16)],
        scratch_shapes=[pltpu.VMEM((NSA_HPG * TQC, LANES), BF16)],
        compiler_params=pltpu.CompilerParams(dimension_semantics=("arbitrary", "arbitrary", "arbitrary"),
                                             vmem_limit_bytes=VMEM_LIMIT),
        name="cmp_sel",
    )(rel_bias, qn, kcmp, vcmp_t)


TA = 512
SUB = 128


def _attend_kernel(tbl_ref, q_ref, ksa_ref, vst_ref, kw_ref, vwt_ref, sel_ref, oc_ref, g_ref, z_ref, y_ref,
                   pd_ref, pp_ref, sa_ref, sb_ref, ms_ref, accs_ref, mw_ref, accw_ref, *, gate_col):
    g = pl.program_id(1)
    i = pl.program_id(2)
    nsub = TA // SUB

    krow = _iota((SUB, SUB), 0)
    qcol = _iota((SUB, SUB), 1)

    @pl.when(i == 0)
    def _():
        for h in range(NSA_HPG):
            head = g * NSA_HPG + h
            lanes = slice(h * SUB, (h + 1) * SUB)
            pd_ref[:, lanes] = jnp.where(krow <= qcol, _rel_bias_pattern(qcol - krow, tbl_ref, head), NEG)
            pp_ref[:, lanes] = _rel_bias_pattern(qcol - krow + SUB, tbl_ref, head)

    q4 = q_ref[...]
    sel = sel_ref[...]
    q_win = _stack_heads(q4)
    q_sel = jnp.concatenate([jnp.concatenate([q4[:, h * NSA_HD:(h + 1) * NSA_HD], sel], axis=1)
                             for h in range(NSA_HPG)], axis=0)

    def chunk(c):
        return pl.ds(pl.multiple_of(c * TA, TA), TA)

    def sub(h, kb, qb):
        return slice(kb * SUB, (kb + 1) * SUB), slice(h * TA + qb * SUB, h * TA + (qb + 1) * SUB)

    def band_diag(s_ref):
        for h in range(NSA_HPG):
            lanes = slice(h * SUB, (h + 1) * SUB)
            for qb in range(nsub):
                for kb in range(nsub):
                    if kb > qb:
                        s_ref[sub(h, kb, qb)] = jnp.full((SUB, SUB), NEG, F32)
                    elif kb == qb:
                        s_ref[sub(h, kb, qb)] += pd_ref[:, lanes]
                    elif kb == qb - 1:
                        s_ref[sub(h, kb, qb)] += pp_ref[:, lanes]

    def band_prev(s_ref, window):
        for h in range(NSA_HPG):
            s_ref[sub(h, nsub - 1, 0)] += pp_ref[:, h * SUB:(h + 1) * SUB]
            if window:
                for qb in range(nsub):
                    for kb in range(nsub):
                        if kb < qb:
                            s_ref[sub(h, kb, qb)] = jnp.full((SUB, SUB), NEG, F32)
                        elif kb == qb:
                            s_ref[sub(h, kb, qb)] += jnp.where(krow > qcol, 0.0, NEG)

    def absorb(s_ref, vt, m_ref, acc_ref, first=False):
        mx = jnp.max(s_ref[...], axis=0, keepdims=True)
        if first:
            m_new = mx
        else:
            m_old = m_ref[...]
            m_new = jnp.maximum(m_old, mx)
        p = jnp.exp2(s_ref[...] - m_new).astype(BF16)
        pv = _dot(vt, p)
        if first:
            acc_ref[...] = pv
        else:
            acc_ref[...] = jnp.exp2(m_old - m_new) * acc_ref[...] + pv
        m_ref[...] = m_new

    def qk_sel(s_ref, c):
        s_ref[...] = _dot_nt(ksa_ref[chunk(c), :], q_sel)

    def qk_win(s_ref, c):
        s_ref[...] = _dot_nt(kw_ref[chunk(c), :], q_win)

    @pl.when(i == 0)
    def _():
        qk_sel(sa_ref, 0)
        band_diag(sa_ref)
        qk_win(sb_ref, 0)
        band_diag(sb_ref)
        absorb(sa_ref, vst_ref[:, chunk(0)], ms_ref, accs_ref, first=True)
        absorb(sb_ref, vwt_ref[:, chunk(0)], mw_ref, accw_ref, first=True)

    @pl.when(i >= 1)
    def _():
        qk_sel(sa_ref, i)
        band_diag(sa_ref)
        qk_sel(sb_ref, i - 1)
        band_prev(sb_ref, window=False)
        absorb(sa_ref, vst_ref[:, chunk(i)], ms_ref, accs_ref, first=True)
        qk_win(sa_ref, i)
        band_diag(sa_ref)
        absorb(sb_ref, vst_ref[:, chunk(i - 1)], ms_ref, accs_ref)
        qk_win(sb_ref, i - 1)
        band_prev(sb_ref, window=True)
        absorb(sa_ref, vwt_ref[:, chunk(i)], mw_ref, accw_ref, first=True)
        absorb(sb_ref, vwt_ref[:, chunk(i - 1)], mw_ref, accw_ref)

    nfar = jnp.maximum(i - 1, 0)
    odd = nfar % 2

    @pl.when(odd == 1)
    def _():
        qk_sel(sa_ref, 0)
        absorb(sa_ref, vst_ref[:, chunk(0)], ms_ref, accs_ref)

    npair = nfar // 2

    @pl.when(npair > 0)
    def _():
        qk_sel(sa_ref, odd)

    def pair(t, carry):
        c = odd + 2 * t
        qk_sel(sb_ref, c + 1)
        absorb(sa_ref, vst_ref[:, chunk(c)], ms_ref, accs_ref)
        qk_sel(sa_ref, jnp.minimum(c + 2, nfar - 1))
        absorb(sb_ref, vst_ref[:, chunk(c + 1)], ms_ref, accs_ref)
        return carry

    lax.fori_loop(0, npair, pair, 0)

    o_s = accs_ref[0:NSA_HD, :] / accs_ref[NSA_HD:NSA_HD + 1, :]
    o_w = accw_ref[0:NSA_HD, :] / accw_ref[NSA_HD:NSA_HD + 1, :]
    gs = _sigmoid(g_ref[:, 0:LANES])
    lane = _iota((TA, LANES), 1)
    for h in range(NSA_HPG):
        base = gate_col + (g * NSA_HPG + h) * N_BRANCH
        gate = [jnp.sum(jnp.where(lane == base + br, gs, 0.0), axis=-1, keepdims=True) for br in range(N_BRANCH)]
        cols = slice(h * NSA_HD, (h + 1) * NSA_HD)
        lanes = slice(h * TA, (h + 1) * TA)
        o = gate[0] * oc_ref[:, cols] + gate[1] * o_s[:, lanes].T + gate[2] * o_w[:, lanes].T
        y_ref[:, cols] = (o * _silu(z_ref[:, cols])).astype(BF16)


def _attend(rel_bias, qn, ksa, vst, kwn, vwt, sel, o_c, gates, gate_blk, u, col_z, gate_col, B, S):
    nt = S // TA
    gw = NSA_HPG * NSA_HD
    W = NSA_HPG * TA
    cz = col_z // gw
    assert WIN == TA
    tile = lambda b, g, i: (b * nt + i, g)
    vt_spec = pl.BlockSpec((None, None, VT_ROWS, S), lambda b, g, i: (b, g, 0, 0))
    return pl.pallas_call(
        functools.partial(_attend_kernel, gate_col=gate_col),
        grid=(B, NSA_GROUPS, nt),
        in_specs=[
            pl.BlockSpec(memory_space=pltpu.SMEM),
            pl.BlockSpec((TA, gw), tile),
            pl.BlockSpec((S, 2 * NSA_HD), lambda b, g, i: (b, g)), vt_spec,
            pl.BlockSpec((S, NSA_HD), lambda b, g, i: (b, g)), vt_spec,
            pl.BlockSpec((None, None, TA, LANES), lambda b, g, i: (b, g, i, 0)),
            pl.BlockSpec((TA, gw), tile),
            pl.BlockSpec((TA, 2 * LANES), lambda b, g, i: (b * nt + i, gate_blk)),
            pl.BlockSpec((TA, gw), lambda b, g, i: (b * nt + i, cz + g)),
        ],
        out_specs=pl.BlockSpec((TA, gw), tile),
        out_shape=jax.ShapeDtypeStruct((B * S, NSA_HEADS * NSA_HD), BF16),
        scratch_shapes=[pltpu.VMEM((SUB, NSA_HPG * SUB), F32), pltpu.VMEM((SUB, NSA_HPG * SUB), F32),
                        pltpu.VMEM((TA, W), F32), pltpu.VMEM((TA, W), F32),
                        pltpu.VMEM((1, W), F32), pltpu.VMEM((VT_ROWS, W), F32),
                        pltpu.VMEM((1, W), F32), pltpu.VMEM((VT_ROWS, W), F32)],
        compiler_params=pltpu.CompilerParams(dimension_semantics=("arbitrary", "arbitrary", "arbitrary"),
                                             vmem_limit_bytes=VMEM_LIMIT),
        name="attend",
    )(rel_bias, qn, ksa, vst, kwn, vwt, sel, o_c, gates, u)


def _outproj_kernel(yml_ref, yns_ref, w_ref, x_ref, o_ref, ob_ref, wb_ref):
    @pl.when(pl.program_id(1) == 0)
    def _():
        wb_ref[...] = w_ref[...].astype(BF16)

    half = yml_ref.shape[1]
    o = x_ref[...] + _dot(yml_ref[...], wb_ref[0:half, :]) + _dot(yns_ref[...], wb_ref[half:2 * half, :])
    o_ref[...] = o
    ob_ref[...] = o.astype(BF16)


def _outproj(y_ml, y_ns, w_out, x2, tm=512, tn=512):
    T, D = x2.shape
    half = y_ml.shape[1]
    tile = pl.BlockSpec((tm, tn), lambda j, i: (i, j))
    return pl.pallas_call(
        _outproj_kernel,
        grid=(D // tn, T // tm),
        in_specs=[
            pl.BlockSpec((tm, half), lambda j, i: (i, 0)),
            pl.BlockSpec((tm, half), lambda j, i: (i, 0)),
            pl.BlockSpec((2 * half, tn), lambda j, i: (0, j)),
            tile,
        ],
        out_specs=[tile, tile],
        out_shape=[jax.ShapeDtypeStruct((T, D), F32), jax.ShapeDtypeStruct((T, D), BF16)],
        scratch_shapes=[pltpu.VMEM((2 * half, tn), BF16)],
        compiler_params=pltpu.CompilerParams(dimension_semantics=("arbitrary", "arbitrary"),
                                             vmem_limit_bytes=VMEM_LIMIT),
        name="outproj",
    )(y_ml, y_ns, w_out, x2)


def _ple_kernel(xb_ref, x_ref, p_ref, wg_ref, wp_ref, o_ref, wgb_ref, wpb_ref):
    @pl.when(pl.program_id(1) == 0)
    def _():
        wgb_ref[...] = wg_ref[...].astype(BF16)
        wpb_ref[...] = wp_ref[...].astype(BF16)

    gate = _sigmoid(_dot(xb_ref[...], wgb_ref[...]))
    emb = _dot(p_ref[...].astype(BF16), wpb_ref[...])
    o_ref[...] = x_ref[...] + gate * emb


def _ple(x1b, x1, p2, wg, wp, tm=512, tn=512):
    T, D = x1.shape
    P = p2.shape[1]
    tile = pl.BlockSpec((tm, tn), lambda j, i: (i, j))
    return pl.pallas_call(
        _ple_kernel,
        grid=(D // tn, T // tm),
        in_specs=[
            pl.BlockSpec((tm, D), lambda j, i: (i, 0)),
            tile,
            pl.BlockSpec((tm, P), lambda j, i: (i, 0)),
            pl.BlockSpec((D, tn), lambda j, i: (0, j)),
            pl.BlockSpec((P, tn), lambda j, i: (0, j)),
        ],
        out_specs=tile,
        out_shape=jax.ShapeDtypeStruct((T, D), F32),
        scratch_shapes=[pltpu.VMEM((D, tn), BF16), pltpu.VMEM((P, tn), BF16)],
        compiler_params=pltpu.CompilerParams(dimension_semantics=("arbitrary", "arbitrary"),
                                             vmem_limit_bytes=VMEM_LIMIT),
        name="ple",
    )(x1b, x1, p2, wg, wp)


def _layer(x2, p2, norm_w, w_in, conv_w, i_bias, f_bias, head_norm_w, q_norm_w, k_norm_w,
           pe_k, pe_v, kw1, kw2, vw1, vw2, rel_bias, w_out, ple_proj, ple_gate, B, S):
    D = x2.shape[1]
    qkw = ML_HEADS * ML_DQK
    vw = ML_HEADS * ML_DV
    nq = NSA_HEADS * NSA_HD
    nkv = NSA_GROUPS * NSA_HD
    o_i = 2 * qkw + 3 * vw
    o_f = o_i + ML_HEADS
    o_nq = o_f + ML_HEADS
    o_g = o_nq + nq + 6 * nkv
    o_z = o_g + NSA_HEADS * N_BRANCH
    w_t = w_in.T
    ngate = NSA_HEADS * N_BRANCH
    zeros = lambda n: jnp.zeros((n, D), w_in.dtype)
    wg_t = jnp.concatenate([w_t[o_i:o_f], w_t[o_g:o_z], zeros(LANES - ML_HEADS - ngate),
                            w_t[o_f:o_nq], zeros(LANES - ML_HEADS)], axis=0)
    col_nq = o_i
    col_kv = col_nq + nq
    col_z = col_kv + 6 * nkv

    h, gates = _rmsnorm(x2, norm_w.reshape(1, D), wg_t)
    u = _inproj(h, w_t, [(0, o_i), (o_nq, o_g - o_nq), (o_z, nq)])

    pad_h = lambda v: jnp.concatenate([v, jnp.zeros((LANES - ML_HEADS,), v.dtype)]).reshape(1, LANES)
    y_ml = _mlstm(u, gates, 0, conv_w, pad_h(i_bias), pad_h(f_bias), head_norm_w, B, S)

    qn, ksa, vst, kwn, vwt = _nsa_prep(u, q_norm_w.reshape(1, NSA_HD), k_norm_w, col_nq, col_kv, B, S)
    kcmp, vcmp_t = _compress(u, pe_k, pe_v, kw1.astype(BF16), kw2.astype(BF16), vw1.astype(BF16),
                             vw2.astype(BF16), k_norm_w, col_kv, B, S)
    o_c, sel = _cmp_sel(rel_bias, qn, kcmp, vcmp_t, B, S)
    y_ns = _attend(rel_bias, qn, ksa, vst, kwn, vwt, sel, o_c, gates, 0, u, col_z, ML_HEADS, B, S)

    x1, x1b = _outproj(y_ml, y_ns, w_out, x2)
    return _ple(x1b, x1, p2, ple_gate, ple_proj)


def kernel(x, p, norm_w, w_in, ml_conv_w, ml_i_bias, ml_f_bias, ml_head_norm_w, nsa_q_norm_w, nsa_k_norm_w,
           cmp_pe_k, cmp_pe_v, cmp_k_w1, cmp_k_w2, cmp_v_w1, cmp_v_w2, rel_bias, w_out, ple_proj, ple_gate):
    B, S, D = x.shape
    assert S % max(WIN, 256) == 0 and S // SEL_LEN >= 1
    x2 = x.reshape(B * S, D)
    for layer in range(w_in.shape[0]):
        x2 = _layer(x2, p[layer].reshape(B * S, -1), norm_w[layer], w_in[layer], ml_conv_w[layer],
                    ml_i_bias[layer], ml_f_bias[layer], ml_head_norm_w[layer], nsa_q_norm_w[layer],
                    nsa_k_norm_w[layer], cmp_pe_k[layer], cmp_pe_v[layer], cmp_k_w1[layer], cmp_k_w2[layer],
                    cmp_v_w1[layer], cmp_v_w2[layer], rel_bias, w_out[layer], ple_proj[layer], ple_gate[layer],
                    B, S)
    return x2.reshape(B, S, D)
```

```python
import functools
import math

import numpy as np
import jax
import jax.numpy as jnp
from jax import lax
from jax.experimental import pallas as pl
from jax.experimental.pallas import tpu as pltpu

F32 = jnp.float32
BF16 = jnp.bfloat16
HIGHEST = lax.Precision.HIGHEST

ML_HEADS = 8
ML_DQK = 128
ML_DV = 256
ML_CHUNK = 64
ML_CONV = 4
NSA_HEADS = 16
NSA_HD = 128
NSA_GROUPS = 4
NSA_HPG = 4
N_BRANCH = 3
CMP_STRIDE = 16
CMP_LEN = 32
SEL_LEN = 64
SEL_SHIFT = 6
SEL_TOPK = 16
WIN = 512
REL_BUCKETS = 32
REL_MAX_DIST = 128
EPS = 1e-6
NEG = -1e30
FORCE_SCORE = 1e4
LOG2E = math.log2(math.e)

LANES = 128
VMEM_LIMIT = 56 * 1024 * 1024


def _bucket_thresholds():
    n = np.arange(0, 4 * REL_MAX_DIST, dtype=np.int64)
    max_exact = REL_BUCKETS // 2
    nf = np.maximum(n, 1).astype(np.float32)
    large = max_exact + (np.log(nf / np.float32(max_exact)) / np.float32(math.log(REL_MAX_DIST / max_exact))
                         * np.float32(REL_BUCKETS - max_exact)).astype(np.int32)
    large = np.minimum(large, REL_BUCKETS - 1)
    bucket = np.where(n < max_exact, n, large)
    assert np.all(np.diff(bucket) >= 0)
    thr = [int(np.argmax(bucket >= b)) for b in range(REL_BUCKETS)]
    assert thr[REL_BUCKETS - 1] <= REL_MAX_DIST
    return thr


BUCKET_THR = _bucket_thresholds()


def _dot(a, b, precision=None):
    return jnp.dot(a, b, preferred_element_type=F32, precision=precision)


def _dot_nt(a, b, precision=None):
    return lax.dot_general(a, b, (((1,), (1,)), ((), ())), preferred_element_type=F32, precision=precision)


def _sigmoid(x):
    return 1.0 / (1.0 + jnp.exp(-x))


def _silu(x):
    return x * _sigmoid(x)


def _sigmoid_t(x):
    return 0.5 * jnp.tanh(0.5 * x) + 0.5


def _iota(shape, dim):
    return lax.broadcasted_iota(jnp.int32, shape, dim)


def _rel_bias_pattern(dist, tbl_ref, head):
    val = jnp.full(dist.shape, tbl_ref[0, head], F32)
    for b in range(1, REL_BUCKETS):
        val = jnp.where(dist >= BUCKET_THR[b], tbl_ref[b, head], val)
    return (val - tbl_ref[REL_BUCKETS - 1, head]) * LOG2E


def _rmsnorm_kernel(x_ref, nw_ref, wg_ref, h_ref, g_ref, wgb_ref):
    @pl.when(pl.program_id(0) == 0)
    def _():
        wgb_ref[...] = wg_ref[...].astype(BF16)

    x = x_ref[...]
    ms = jnp.mean(x * x, axis=-1, keepdims=True)
    h = (x * lax.rsqrt(ms + EPS) * nw_ref[...]).astype(BF16)
    h_ref[...] = h
    g_ref[...] = _dot_nt(h, wgb_ref[...])


def _rmsnorm(x2, norm_w, wg_t, rb=128):
    T, D = x2.shape
    NG = wg_t.shape[0]
    return pl.pallas_call(
        _rmsnorm_kernel,
        grid=(T // rb,),
        in_specs=[pl.BlockSpec((rb, D), lambda i: (i, 0)), pl.BlockSpec((1, D), lambda i: (0, 0)),
                  pl.BlockSpec((NG, D), lambda i: (0, 0))],
        out_specs=[pl.BlockSpec((rb, D), lambda i: (i, 0)), pl.BlockSpec((rb, NG), lambda i: (i, 0))],
        out_shape=[jax.ShapeDtypeStruct((T, D), BF16), jax.ShapeDtypeStruct((T, NG), F32)],
        scratch_shapes=[pltpu.VMEM((NG, D), BF16)],
        compiler_params=pltpu.CompilerParams(dimension_semantics=("arbitrary",), vmem_limit_bytes=VMEM_LIMIT),
        name="rmsnorm",
    )(x2, norm_w, wg_t)


def _inproj_kernel(h_ref, wt_ref, u_ref, wb_ref):
    @pl.when(pl.program_id(1) == 0)
    def _():
        wb_ref[...] = wt_ref[...].astype(BF16)

    u_ref[...] = _dot_nt(h_ref[...], wb_ref[...])


def _inproj(h, w_t, pieces, tm=512, tn=1024):
    T, D = h.shape
    starts = []
    for start, length in pieces:
        assert length % tn == 0 and start % 8 == 0
        starts += [start + k * tn for k in range(length // tn)]
    nblk = len(starts)
    bounds = [(b, s) for b, s in enumerate(starts) if b == 0 or s != starts[b - 1] + tn]

    def w_rows(j, i):
        row8 = (bounds[0][1] + j * tn) // 8
        for b, s in bounds[1:]:
            row8 = jnp.where(j >= b, (s + (j - b) * tn) // 8, row8)
        return row8 * 8, 0

    return pl.pallas_call(
        _inproj_kernel,
        grid=(nblk, T // tm),
        in_specs=[
            pl.BlockSpec((tm, D), lambda j, i: (i, 0)),
            pl.BlockSpec((pl.Element(tn), pl.Element(D)), w_rows),
        ],
        out_specs=pl.BlockSpec((tm, tn), lambda j, i: (i, j)),
        out_shape=jax.ShapeDtypeStruct((T, nblk * tn), F32),
        scratch_shapes=[pltpu.VMEM((tn, D), BF16)],
        compiler_params=pltpu.CompilerParams(dimension_semantics=("arbitrary", "arbitrary"),
                                             vmem_limit_bytes=VMEM_LIMIT),
        name="inproj",
    )(h, w_t)


def _mlstm_kernel(q_ref, k_ref, v_ref, o_ref, z_ref, g_ref, cw_ref, ib_ref, fb_ref, hw_ref, y_ref,
                  xbuf, qa, ka, c_ref, m_ref, *, lb):
    L = ML_CHUNK
    qkw = ML_HEADS * ML_DQK

    @pl.when(pl.program_id(1) == 0)
    def _():
        xbuf[0:8, :] = jnp.zeros((8, 2 * qkw), F32)
        c_ref[...] = jnp.zeros(c_ref.shape, F32)
        m_ref[...] = jnp.zeros(m_ref.shape, F32)

    xbuf[8:8 + lb, 0:qkw] = q_ref[...]
    xbuf[8:8 + lb, qkw:2 * qkw] = k_ref[...]
    acc = cw_ref[ML_CONV - 1:ML_CONV, :] * xbuf[8:8 + lb, :]
    for j in range(1, ML_CONV):
        acc = acc + cw_ref[ML_CONV - 1 - j:ML_CONV - j, :] * xbuf[8 - j:8 - j + lb, :]
    act = _silu(acc)
    qa[...] = act[:, 0:qkw].astype(BF16)
    ka[...] = act[:, qkw:2 * qkw] * (ML_DQK ** -0.5)
    xbuf[0:8, :] = xbuf[lb:lb + 8, :]

    row = _iota((L, L), 0)
    col = _iota((L, L), 1)
    tril = col <= row
    rowg = _iota((L, LANES), 0)
    eye_h = (_iota((8, LANES), 0) == _iota((8, LANES), 1)).astype(F32)
    ones_col = (_iota((L, LANES), 1) == 0).astype(BF16)
    heads = range(ML_HEADS)

    def chunk(c, carry):
        rows = pl.ds(pl.multiple_of(c * L, L), L)
        qk_cols = lambda h: slice(h * ML_DQK, (h + 1) * ML_DQK)
        v_cols = lambda h: slice(h * ML_DV, (h + 1) * ML_DV)
        col_of = lambda x, h: x[:, h:h + 1]

        G = g_ref[rows, :]
        li = G[:, 0:LANES] + ib_ref[...]
        fp = G[:, LANES:2 * LANES] + fb_ref[...]
        lf = jnp.minimum(fp, 0.0) - jnp.log(1.0 + jnp.exp(-jnp.abs(fp)))
        b = lf
        for sh in (1, 2, 4, 8, 16, 32):
            b = b + jnp.where(rowg >= sh, pltpu.roll(b, sh, 0), 0.0)
        g = li - b
        gmax = g
        for sh in (1, 2, 4, 8, 16, 32):
            gmax = jnp.maximum(gmax, jnp.where(rowg >= sh, pltpu.roll(gmax, sh, 0), NEG))
        m_old = m_ref[...]
        top = jnp.maximum(m_old, gmax)
        m_t = b + top
        inter = jnp.exp(m_old - top)
        floor = jnp.exp(-m_t)
        bL = b[L - 1:L, :]
        w = bL - b + li
        m_new = jnp.maximum(bL + m_old, jnp.max(w, axis=0, keepdims=True))
        wk = jnp.exp(w - m_new)
        decay = jnp.exp(bL + m_old - m_new)
        m_ref[...] = m_new
        g_rows = _dot_nt(eye_h, g, precision=HIGHEST)

        q = [qa[rows, qk_cols(h)] for h in heads]
        kf = [ka[rows, qk_cols(h)] for h in heads]
        vaug = [jnp.concatenate([v_ref[rows, v_cols(h)].astype(BF16), ones_col], axis=1) for h in heads]
        cst = [c_ref[h] for h in heads]
        lhs = []
        for h in heads:
            dw = jnp.exp(jnp.where(tril, g_rows[h:h + 1, :] - col_of(top, h), NEG))
            sc = (_dot_nt(q[h], kf[h].astype(BF16)) * dw).astype(BF16)
            q_in = (q[h].astype(F32) * col_of(inter, h)).astype(BF16)
            lhs.append(jnp.concatenate([q_in, sc], axis=1))
        num_aug = [_dot(lhs[h], jnp.concatenate([cst[h].astype(BF16), vaug[h]], axis=0)) for h in heads]
        for h in heads:
            kw = (kf[h] * col_of(wk, h)).astype(BF16)
            upd = lax.dot_general(kw, vaug[h], (((0,), (0,)), ((), ())), preferred_element_type=F32)
            c_ref[h] = decay[:, h:h + 1] * cst[h] + upd
        for h in heads:
            num = num_aug[h][:, 0:ML_DV]
            den = num_aug[h][:, ML_DV:ML_DV + 1]
            r = 1.0 / jnp.maximum(jnp.abs(den), col_of(floor, h))
            f = r * lax.rsqrt(r * r * jnp.mean(num * num, axis=-1, keepdims=True) + EPS)
            o = o_ref[rows, v_cols(h)]
            z = z_ref[rows, v_cols(h)]
            gate = _sigmoid_t(o) * z * _sigmoid_t(z)
            y_ref[rows, v_cols(h)] = (num * f * hw_ref[h:h + 1, :] * gate).astype(BF16)
        return carry

    lax.fori_loop(0, lb // L, chunk, 0)


def _mlstm(u, gates, gate_blk, conv_w, ib, fb, head_w, B, S, lb=256):
    T = B * S
    nsb = S // lb
    qkw = ML_HEADS * ML_DQK
    vw = ML_HEADS * ML_DV
    rowmap = lambda col: (lambda b, s: (b * nsb + s, col))
    const = lambda b, s: (0, 0)
    return pl.pallas_call(
        functools.partial(_mlstm_kernel, lb=lb),
        grid=(B, nsb),
        in_specs=[
            pl.BlockSpec((lb, qkw), rowmap(0)),
            pl.BlockSpec((lb, qkw), rowmap(1)),
            pl.BlockSpec((lb, vw), rowmap(1)),
            pl.BlockSpec((lb, vw), rowmap(2)),
            pl.BlockSpec((lb, vw), rowmap(3)),
            pl.BlockSpec((lb, 2 * LANES), rowmap(gate_blk)),
            pl.BlockSpec((ML_CONV, 2 * qkw), const),
            pl.BlockSpec((1, LANES), const),
            pl.BlockSpec((1, LANES), const),
            pl.BlockSpec((ML_HEADS, ML_DV), const),
        ],
        out_specs=pl.BlockSpec((lb, vw), rowmap(0)),
        out_shape=jax.ShapeDtypeStruct((T, vw), BF16),
        scratch_shapes=[
            pltpu.VMEM((lb + 8, 2 * qkw), F32),
            pltpu.VMEM((lb, qkw), BF16),
            pltpu.VMEM((lb, qkw), F32),
            pltpu.VMEM((ML_HEADS, ML_DQK, ML_DV + LANES), F32),
            pltpu.VMEM((1, LANES), F32),
        ],
        compiler_params=pltpu.CompilerParams(dimension_semantics=("arbitrary", "arbitrary"),
                                             vmem_limit_bytes=VMEM_LIMIT),
        name="mlstm",
    )(u, u, u, u, u, gates, conv_w, ib, fb, head_w)


def _rms_heads(x, w, scale=1.0):
    outs = []
    for h in range(x.shape[1] // NSA_HD):
        xh = x[:, h * NSA_HD:(h + 1) * NSA_HD]
        ms = jnp.mean(xh * xh, axis=-1, keepdims=True)
        outs.append(xh * lax.rsqrt(ms + EPS) * w * scale)
    return jnp.concatenate(outs, axis=1)


VT_ROWS = NSA_HD + 16


def _nsa_prep_kernel(q_ref, ks_ref, vs_ref, kw_ref, vw_ref, qw_ref, kn_ref,
                     qn_ref, ksa_ref, vst_ref, kwn_ref, vwt_ref, *, rb, nsb):
    qn_ref[...] = _rms_heads(q_ref[...], qw_ref[...], NSA_HD ** -0.5 * LOG2E).astype(BF16)
    ksn = _rms_heads(ks_ref[...], kn_ref[1:2, :]).astype(BF16)
    kwn_ref[...] = _rms_heads(kw_ref[...], kn_ref[2:3, :]).astype(BF16)
    t = (pl.program_id(0) % nsb) * rb + _iota((rb, NSA_HD), 0)
    onehot = (_iota((rb, NSA_HD), 1) == jnp.right_shift(t, SEL_SHIFT)).astype(BF16)
    tail = (_iota((VT_ROWS - NSA_HD, rb), 0) == 0).astype(BF16)
    for g in range(NSA_GROUPS):
        cols = slice(g * NSA_HD, (g + 1) * NSA_HD)
        ksa_ref[:, 2 * g * NSA_HD:(2 * g + 1) * NSA_HD] = ksn[:, cols]
        ksa_ref[:, (2 * g + 1) * NSA_HD:(2 * g + 2) * NSA_HD] = onehot
        vst_ref[g] = jnp.concatenate([vs_ref[:, cols].T.astype(BF16), tail], axis=0)
        vwt_ref[g] = jnp.concatenate([vw_ref[:, cols].T.astype(BF16), tail], axis=0)


def _nsa_prep(u, q_norm_w, k_norm_w, col_q, col_kv, B, S, rb=256):
    T = u.shape[0]
    nsb = S // rb
    qw = NSA_HEADS * NSA_HD
    kvw = NSA_GROUPS * NSA_HD
    cq = col_q // qw
    ck = col_kv // kvw
    assert S // SEL_LEN <= NSA_HD
    kv_spec = lambda idx: pl.BlockSpec((rb, kvw), lambda i: (i, ck + idx))
    out_k = pl.BlockSpec((rb, kvw), lambda i: (i, 0))
    out_ka = pl.BlockSpec((rb, 2 * kvw), lambda i: (i, 0))
    out_vt = pl.BlockSpec((None, NSA_GROUPS, VT_ROWS, rb), lambda i: (i // nsb, 0, 0, i % nsb))
    vt_shape = jax.ShapeDtypeStruct((B, NSA_GROUPS, VT_ROWS, S), BF16)
    return pl.pallas_call(
        functools.partial(_nsa_prep_kernel, rb=rb, nsb=nsb),
        grid=(T // rb,),
        in_specs=[
            pl.BlockSpec((rb, qw), lambda i: (i, cq)),
            kv_spec(2), kv_spec(3), kv_spec(4), kv_spec(5),
            pl.BlockSpec((1, NSA_HD), lambda i: (0, 0)),
            pl.BlockSpec((N_BRANCH, NSA_HD), lambda i: (0, 0)),
        ],
        out_specs=[pl.BlockSpec((rb, qw), lambda i: (i, 0)), out_ka, out_vt, out_k, out_vt],
        out_shape=[jax.ShapeDtypeStruct((T, qw), BF16), jax.ShapeDtypeStruct((T, 2 * kvw), BF16), vt_shape,
                   jax.ShapeDtypeStruct((T, kvw), BF16), vt_shape],
        compiler_params=pltpu.CompilerParams(dimension_semantics=("arbitrary",), vmem_limit_bytes=VMEM_LIMIT),
        name="nsa_prep",
    )(u, u, u, u, u, q_norm_w, k_norm_w)


def _compress_kernel(kc_ref, vc_ref, pek_ref, pev_ref, kw1_ref, kw2_ref, vw1_ref, vw2t_ref, kn_ref,
                     kcmp_ref, vcmp_ref, *, ns):
    def hidden(x_ref, pe_ref, w1_ref):
        hid = CMP_LEN // 2
        a = jnp.zeros((ns, w1_ref.shape[1]), F32)
        bsum = jnp.zeros((ns, w1_ref.shape[1]), F32)
        for l in range(hid):
            xl = x_ref[pl.ds(l, ns, stride=CMP_STRIDE), :]
            a = a + _dot((xl + pe_ref[l:l + 1, :]).astype(BF16), w1_ref[l * NSA_HD:(l + 1) * NSA_HD, :])
            bsum = bsum + _dot((xl + pe_ref[hid + l:hid + l + 1, :]).astype(BF16),
                               w1_ref[(hid + l) * NSA_HD:(hid + l + 1) * NSA_HD, :])
        pre = a + pltpu.roll(bsum, ns - 1, 0)
        return _silu(pre).astype(BF16)

    kc = _dot(hidden(kc_ref, pek_ref, kw1_ref), kw2_ref[...])
    ms = jnp.mean(kc * kc, axis=-1, keepdims=True)
    kcmp_ref[...] = (kc * lax.rsqrt(ms + EPS) * kn_ref[0:1, :]).astype(BF16)
    vcmp_ref[...] = _dot_nt(vw2t_ref[...], hidden(vc_ref, pev_ref, vw1_ref)).astype(BF16)


def _compress(u, pe_k, pe_v, kw1, kw2, vw1, vw2, k_norm_w, col_kv, B, S):
    ns = S // CMP_STRIDE
    ck = col_kv // NSA_HD
    hidden = kw1.shape[1]
    const = lambda b, g: (0, 0)
    return pl.pallas_call(
        functools.partial(_compress_kernel, ns=ns),
        grid=(B, NSA_GROUPS),
        in_specs=[
            pl.BlockSpec((S, NSA_HD), lambda b, g: (b, ck + g)),
            pl.BlockSpec((S, NSA_HD), lambda b, g: (b, ck + NSA_GROUPS + g)),
            pl.BlockSpec((CMP_LEN, NSA_HD), const),
            pl.BlockSpec((CMP_LEN, NSA_HD), const),
            pl.BlockSpec((CMP_LEN * NSA_HD, hidden), const),
            pl.BlockSpec((hidden, NSA_HD), const),
            pl.BlockSpec((CMP_LEN * NSA_HD, hidden), const),
            pl.BlockSpec((NSA_HD, hidden), const),
            pl.BlockSpec((N_BRANCH, NSA_HD), const),
        ],
        out_specs=[pl.BlockSpec((None, None, ns, NSA_HD), lambda b, g: (b, g, 0, 0)),
                   pl.BlockSpec((None, None, NSA_HD, ns), lambda b, g: (b, g, 0, 0))],
        out_shape=[jax.ShapeDtypeStruct((B, NSA_GROUPS, ns, NSA_HD), BF16),
                   jax.ShapeDtypeStruct((B, NSA_GROUPS, NSA_HD, ns), BF16)],
        compiler_params=pltpu.CompilerParams(dimension_semantics=("arbitrary", "arbitrary"),
                                             vmem_limit_bytes=VMEM_LIMIT),
        name="compress",
    )(u, u, pe_k, pe_v, kw1, kw2, vw1, vw2.T, k_norm_w)


TQC = 256
NEAR = 32
NEAR_BACK = 16


def _stack_heads(q4):
    return jnp.concatenate([q4[:, h * NSA_HD:(h + 1) * NSA_HD] for h in range(NSA_HPG)], axis=0)


def _cmp_sel_kernel(tbl_ref, q_ref, kc_ref, vct_ref, oc_ref, sel_ref, pat_ref, *, ns, nsel):
    g = pl.program_id(1)
    i = pl.program_id(2)
    t0 = i * TQC
    c0 = i * (TQC // CMP_STRIDE)
    W = NSA_HPG * TQC
    assert (TQC - CMP_LEN) // CMP_STRIDE < NEAR - NEAR_BACK and NEAR_BACK * CMP_STRIDE >= REL_MAX_DIST + CMP_LEN

    @pl.when(i == 0)
    def _():
        r = _iota((TQC, LANES), 0)
        lane = _iota((TQC, LANES), 1)
        d = r - CMP_STRIDE * (jnp.bitwise_and(lane, NEAR - 1) - NEAR_BACK) - (CMP_LEN - 1)
        for h in range(NSA_HPG):
            val = jnp.where(d < 0, NEG, _rel_bias_pattern(d, tbl_ref, g * NSA_HPG + h))
            hi = val.astype(BF16).astype(F32)
            lo = jnp.where(d < 0, 0.0, val - hi)
            ext = jnp.where(lane < NEAR, hi, jnp.where(lane < 2 * NEAR, lo, jnp.where(lane == 2 * NEAR, NEG, 0.0)))
            pat_ref[h * TQC:(h + 1) * TQC, :] = ext.astype(BF16)

    q4 = q_ref[...]
    q_aug = jnp.concatenate([jnp.concatenate([q4[:, h * NSA_HD:(h + 1) * NSA_HD],
                                              pat_ref[h * TQC:(h + 1) * TQC, :]], axis=1)
                             for h in range(NSA_HPG)], axis=0)
    rel = _iota((ns, LANES), 0) - (c0 - NEAR_BACK)
    lane = _iota((ns, LANES), 1)
    ext = ((lane < 2 * NEAR) & (rel == jnp.bitwise_and(lane, NEAR - 1))) | ((lane == 2 * NEAR) & (rel >= NEAR))
    k_aug = jnp.concatenate([kc_ref[...], jnp.where(ext, 1.0, 0.0).astype(BF16)], axis=1)

    s = _dot_nt(k_aug, q_aug)
    m = jnp.max(s, axis=0, keepdims=True)
    e = jnp.exp2(s - m)
    l = jnp.sum(e, axis=0, keepdims=True)
    tq = t0 + jnp.bitwise_and(_iota((1, W), 1), TQC - 1)
    inv = jnp.where(tq >= CMP_LEN - 1, 1.0 / l, 0.0)
    o_t = _dot(vct_ref[...], e.astype(BF16)) * inv
    for h in range(NSA_HPG):
        oc_ref[:, h * NSA_HD:(h + 1) * NSA_HD] = o_t[:, h * TQC:(h + 1) * TQC].T

    p = e * inv
    ps = p[:, 0:TQC]
    for h in range(1, NSA_HPG):
        ps = ps + p[:, h * TQC:(h + 1) * TQC]
    ratio = SEL_LEN // CMP_STRIDE
    blk_n = _iota((nsel, ns), 0)
    tok_c = _iota((nsel, ns), 1)
    ov = ((tok_c >= ratio * blk_n - (CMP_LEN // CMP_STRIDE - 1)) & (tok_c <= ratio * blk_n + ratio - 1)).astype(F32)
    imp = _dot(ov, ps, precision=HIGHEST)

    blk = _iota((nsel, TQC), 0)
    cur = jnp.right_shift(t0 + _iota((nsel, TQC), 1), SEL_SHIFT)
    imp = jnp.where((blk == 0) | (blk == cur) | (blk == cur - 1), FORCE_SCORE, imp)
    imp = jnp.where(blk > cur, NEG, imp)
    tiles = [imp[8 * k:8 * k + 8, :] for k in range(nsel // 8)]
    counts = [jnp.zeros((8, TQC), jnp.int32) for _ in tiles]
    sub = _iota((8, TQC), 0)
    for jb in range(nsel):
        vj = imp[jb:jb + 1, :]
        for k, tile in enumerate(tiles):
            if 8 * k > jb:
                ahead = vj >= tile
            elif 8 * k + 7 < jb:
                ahead = vj > tile
            else:
                ahead = (vj > tile) | ((vj == tile) & (sub + 8 * k > jb))
            counts[k] = counts[k] + jnp.where(ahead, 1, 0)
    rank = jnp.concatenate(counts, axis=0)
    chosen = jnp.where(rank < min(SEL_TOPK, nsel), 1.0, 0.0).astype(BF16)
    chosen = jnp.concatenate([chosen, jnp.zeros((LANES - nsel, TQC), BF16)], axis=0)
    eye_q = (_iota((TQC, TQC), 0) == _iota((TQC, TQC), 1)).astype(BF16)
    sel_ref[...] = ((_dot_nt(eye_q, chosen) - 1.0) * (-NEG)).astype(BF16)


def _cmp_sel(rel_bias, qn, kcmp, vcmp_t, B, S):
    ns = S // CMP_STRIDE
    nsel = S // SEL_LEN
    nt = S // TQC
    gw = NSA_HPG * NSA_HD
    assert nsel % 8 == 0 and nsel <= LANES
    return pl.pallas_call(
        functools.partial(_cmp_sel_kernel, ns=ns, nsel=nsel),
        grid=(B, NSA_GROUPS, nt),
        in_specs=[
            pl.BlockSpec(memory_space=pltpu.SMEM),
            pl.BlockSpec((TQC, gw), lambda b, g, i: (b * nt + i, g)),
            pl.BlockSpec((None, None, ns, NSA_HD), lambda b, g, i: (b, g, 0, 0)),
            pl.BlockSpec((None, None, NSA_HD, ns), lambda b, g, i: (b, g, 0, 0)),
        ],
        out_specs=[
            pl.BlockSpec((TQC, gw), lambda b, g, i: (b * nt + i, g)),
            pl.BlockSpec((None, None, TQC, LANES), lambda b, g, i: (b, g, i, 0)),
        ],
        out_shape=[jax.ShapeDtypeStruct((B * S, NSA_HEADS * NSA_HD), F32),
                   jax.ShapeDtypeStruct((B, NSA_GROUPS, S, LANES), BF16)],
        scratch_shapes=[pltpu.VMEM((NSA_HPG * TQC, LANES), BF16)],
        compiler_params=pltpu.CompilerParams(dimension_semantics=("arbitrary", "arbitrary", "arbitrary"),
                                             vmem_limit_bytes=VMEM_LIMIT),
        name="cmp_sel",
    )(rel_bias, qn, kcmp, vcmp_t)


TA = 512
SUB = 128


def _attend_kernel(tbl_ref, q_ref, ksa_ref, vst_ref, kw_ref, vwt_ref, sel_ref, oc_ref, g_ref, z_ref, y_ref,
                   pd_ref, pp_ref, sa_ref, sb_ref, ms_ref, accs_ref, mw_ref, accw_ref, *, gate_col):
    g = pl.program_id(1)
    i = pl.program_id(2)
    nsub = TA // SUB

    krow = _iota((SUB, SUB), 0)
    qcol = _iota((SUB, SUB), 1)

    @pl.when(i == 0)
    def _():
        for h in range(NSA_HPG):
            head = g * NSA_HPG + h
            lanes = slice(h * SUB, (h + 1) * SUB)
            pd_ref[:, lanes] = jnp.where(krow <= qcol, _rel_bias_pattern(qcol - krow, tbl_ref, head), NEG)
            pp_ref[:, lanes] = _rel_bias_pattern(qcol - krow + SUB, tbl_ref, head)

    q4 = q_ref[...]
    sel = sel_ref[...]
    q_win = _stack_heads(q4)
    q_sel = jnp.concatenate([jnp.concatenate([q4[:, h * NSA_HD:(h + 1) * NSA_HD], sel], axis=1)
                             for h in range(NSA_HPG)], axis=0)

    def chunk(c):
        return pl.ds(pl.multiple_of(c * TA, TA), TA)

    def sub(h, kb, qb):
        return slice(kb * SUB, (kb + 1) * SUB), slice(h * TA + qb * SUB, h * TA + (qb + 1) * SUB)

    def band_diag(s_ref):
        for h in range(NSA_HPG):
            lanes = slice(h * SUB, (h + 1) * SUB)
            for qb in range(nsub):
                for kb in range(nsub):
                    if kb > qb:
                        s_ref[sub(h, kb, qb)] = jnp.full((SUB, SUB), NEG, F32)
                    elif kb == qb:
                        s_ref[sub(h, kb, qb)] += pd_ref[:, lanes]
                    elif kb == qb - 1:
                        s_ref[sub(h, kb, qb)] += pp_ref[:, lanes]

    def band_prev(s_ref, window):
        for h in range(NSA_HPG):
            s_ref[sub(h, nsub - 1, 0)] += pp_ref[:, h * SUB:(h + 1) * SUB]
            if window:
                for qb in range(nsub):
                    for kb in range(nsub):
                        if kb < qb:
                            s_ref[sub(h, kb, qb)] = jnp.full((SUB, SUB), NEG, F32)
                        elif kb == qb:
                            s_ref[sub(h, kb, qb)] += jnp.where(krow > qcol, 0.0, NEG)

    def absorb(s_ref, vt, m_ref, acc_ref, first=False):
        mx = jnp.max(s_ref[...], axis=0, keepdims=True)
        if first:
            m_new = mx
        else:
            m_old = m_ref[...]
            m_new = jnp.maximum(m_old, mx)
        p = jnp.exp2(s_ref[...] - m_new).astype(BF16)
        pv = _dot(vt, p)
        if first:
            acc_ref[...] = pv
        else:
            acc_ref[...] = jnp.exp2(m_old - m_new) * acc_ref[...] + pv
        m_ref[...] = m_new

    def qk_sel(s_ref, c):
        s_ref[...] = _dot_nt(ksa_ref[chunk(c), :], q_sel)

    def qk_win(s_ref, c):
        s_ref[...] = _dot_nt(kw_ref[chunk(c), :], q_win)

    @pl.when(i == 0)
    def _():
        qk_sel(sa_ref, 0)
        band_diag(sa_ref)
        qk_win(sb_ref, 0)
        band_diag(sb_ref)
        absorb(sa_ref, vst_ref[:, chunk(0)], ms_ref, accs_ref, first=True)
        absorb(sb_ref, vwt_ref[:, chunk(0)], mw_ref, accw_ref, first=True)

    @pl.when(i >= 1)
    def _():
        qk_sel(sa_ref, i)
        band_diag(sa_ref)
        qk_sel(sb_ref, i - 1)
        band_prev(sb_ref, window=False)
        absorb(sa_ref, vst_ref[:, chunk(i)], ms_ref, accs_ref, first=True)
        qk_win(sa_ref, i)
        band_diag(sa_ref)
        absorb(sb_ref, vst_ref[:, chunk(i - 1)], ms_ref, accs_ref)
        qk_win(sb_ref, i - 1)
        band_prev(sb_ref, window=True)
        absorb(sa_ref, vwt_ref[:, chunk(i)], mw_ref, accw_ref, first=True)
        absorb(sb_ref, vwt_ref[:, chunk(i - 1)], mw_ref, accw_ref)

    nfar = jnp.maximum(i - 1, 0)
    odd = nfar % 2

    @pl.when(odd == 1)
    def _():
        qk_sel(sa_ref, 0)
        absorb(sa_ref, vst_ref[:, chunk(0)], ms_ref, accs_ref)

    npair = nfar // 2

    @pl.when(npair > 0)
    def _():
        qk_sel(sa_ref, odd)

    def pair(t, carry):
        c = odd + 2 * t
        qk_sel(sb_ref, c + 1)
        absorb(sa_ref, vst_ref[:, chunk(c)], ms_ref, accs_ref)
        qk_sel(sa_ref, jnp.minimum(c + 2, nfar - 1))
        absorb(sb_ref, vst_ref[:, chunk(c + 1)], ms_ref, accs_ref)
        return carry

    lax.fori_loop(0, npair, pair, 0)

    o_s = accs_ref[0:NSA_HD, :] / accs_ref[NSA_HD:NSA_HD + 1, :]
    o_w = accw_ref[0:NSA_HD, :] / accw_ref[NSA_HD:NSA_HD + 1, :]
    gs = _sigmoid(g_ref[:, 0:LANES])
    lane = _iota((TA, LANES), 1)
    for h in range(NSA_HPG):
        base = gate_col + (g * NSA_HPG + h) * N_BRANCH
        gate = [jnp.sum(jnp.where(lane == base + br, gs, 0.0), axis=-1, keepdims=True) for br in range(N_BRANCH)]
        cols = slice(h * NSA_HD, (h + 1) * NSA_HD)
        lanes = slice(h * TA, (h + 1) * TA)
        o = gate[0] * oc_ref[:, cols] + gate[1] * o_s[:, lanes].T + gate[2] * o_w[:, lanes].T
        y_ref[:, cols] = (o * _silu(z_ref[:, cols])).astype(BF16)


def _attend(rel_bias, qn, ksa, vst, kwn, vwt, sel, o_c, gates, gate_blk, u, col_z, gate_col, B, S):
    nt = S // TA
    gw = NSA_HPG * NSA_HD
    W = NSA_HPG * TA
    cz = col_z // gw
    assert WIN == TA
    tile = lambda b, g, i: (b * nt + i, g)
    vt_spec = pl.BlockSpec((None, None, VT_ROWS, S), lambda b, g, i: (b, g, 0, 0))
    return pl.pallas_call(
        functools.partial(_attend_kernel, gate_col=gate_col),
        grid=(B, NSA_GROUPS, nt),
        in_specs=[
            pl.BlockSpec(memory_space=pltpu.SMEM),
            pl.BlockSpec((TA, gw), tile),
            pl.BlockSpec((S, 2 * NSA_HD), lambda b, g, i: (b, g)), vt_spec,
            pl.BlockSpec((S, NSA_HD), lambda b, g, i: (b, g)), vt_spec,
            pl.BlockSpec((None, None, TA, LANES), lambda b, g, i: (b, g, i, 0)),
            pl.BlockSpec((TA, gw), tile),
            pl.BlockSpec((TA, 2 * LANES), lambda b, g, i: (b * nt + i, gate_blk)),
            pl.BlockSpec((TA, gw), lambda b, g, i: (b * nt + i, cz + g)),
        ],
        out_specs=pl.BlockSpec((TA, gw), tile),
        out_shape=jax.ShapeDtypeStruct((B * S, NSA_HEADS * NSA_HD), BF16),
        scratch_shapes=[pltpu.VMEM((SUB, NSA_HPG * SUB), F32), pltpu.VMEM((SUB, NSA_HPG * SUB), F32),
                        pltpu.VMEM((TA, W), F32), pltpu.VMEM((TA, W), F32),
                        pltpu.VMEM((1, W), F32), pltpu.VMEM((VT_ROWS, W), F32),
                        pltpu.VMEM((1, W), F32), pltpu.VMEM((VT_ROWS, W), F32)],
        compiler_params=pltpu.CompilerParams(dimension_semantics=("arbitrary", "arbitrary", "arbitrary"),
                                             vmem_limit_bytes=VMEM_LIMIT),
        name="attend",
    )(rel_bias, qn, ksa, vst, kwn, vwt, sel, o_c, gates, u)


def _outproj_kernel(yml_ref, yns_ref, w_ref, x_ref, o_ref, ob_ref, wb_ref):
    @pl.when(pl.program_id(1) == 0)
    def _():
        wb_ref[...] = w_ref[...].astype(BF16)

    half = yml_ref.shape[1]
    o = x_ref[...] + _dot(yml_ref[...], wb_ref[0:half, :]) + _dot(yns_ref[...], wb_ref[half:2 * half, :])
    o_ref[...] = o
    ob_ref[...] = o.astype(BF16)


def _outproj(y_ml, y_ns, w_out, x2, tm=256, tn=1024):
    T, D = x2.shape
    half = y_ml.shape[1]
    tile = pl.BlockSpec((tm, tn), lambda j, i: (i, j))
    return pl.pallas_call(
        _outproj_kernel,
        grid=(D // tn, T // tm),
        in_specs=[
            pl.BlockSpec((tm, half), lambda j, i: (i, 0)),
            pl.BlockSpec((tm, half), lambda j, i: (i, 0)),
            pl.BlockSpec((2 * half, tn), lambda j, i: (0, j)),
            tile,
        ],
        out_specs=[tile, tile],
        out_shape=[jax.ShapeDtypeStruct((T, D), F32), jax.ShapeDtypeStruct((T, D), BF16)],
        scratch_shapes=[pltpu.VMEM((2 * half, tn), BF16)],
        compiler_params=pltpu.CompilerParams(dimension_semantics=("arbitrary", "arbitrary"),
                                             vmem_limit_bytes=VMEM_LIMIT),
        name="outproj",
    )(y_ml, y_ns, w_out, x2)


def _ple_kernel(xb_ref, x_ref, p_ref, wg_ref, wp_ref, o_ref, wgb_ref, wpb_ref):
    @pl.when(pl.program_id(1) == 0)
    def _():
        wgb_ref[...] = wg_ref[...].astype(BF16)
        wpb_ref[...] = wp_ref[...].astype(BF16)

    gate = _sigmoid(_dot(xb_ref[...], wgb_ref[...]))
    emb = _dot(p_ref[...].astype(BF16), wpb_ref[...])
    o_ref[...] = x_ref[...] + gate * emb


def _ple(x1b, x1, p2, wg, wp, tm=256, tn=1024):
    T, D = x1.shape
    P = p2.shape[1]
    tile = pl.BlockSpec((tm, tn), lambda j, i: (i, j))
    return pl.pallas_call(
        _ple_kernel,
        grid=(D // tn, T // tm),
        in_specs=[
            pl.BlockSpec((tm, D), lambda j, i: (i, 0)),
            tile,
            pl.BlockSpec((tm, P), lambda j, i: (i, 0)),
            pl.BlockSpec((D, tn), lambda j, i: (0, j)),
            pl.BlockSpec((P, tn), lambda j, i: (0, j)),
        ],
        out_specs=tile,
        out_shape=jax.ShapeDtypeStruct((T, D), F32),
        scratch_shapes=[pltpu.VMEM((D, tn), BF16), pltpu.VMEM((P, tn), BF16)],
        compiler_params=pltpu.CompilerParams(dimension_semantics=("arbitrary", "arbitrary"),
                                             vmem_limit_bytes=VMEM_LIMIT),
        name="ple",
    )(x1b, x1, p2, wg, wp)


def _layer(x2, p2, norm_w, w_in, conv_w, i_bias, f_bias, head_norm_w, q_norm_w, k_norm_w,
           pe_k, pe_v, kw1, kw2, vw1, vw2, rel_bias, w_out, ple_proj, ple_gate, B, S):
    D = x2.shape[1]
    qkw = ML_HEADS * ML_DQK
    vw = ML_HEADS * ML_DV
    nq = NSA_HEADS * NSA_HD
    nkv = NSA_GROUPS * NSA_HD
    o_i = 2 * qkw + 3 * vw
    o_f = o_i + ML_HEADS
    o_nq = o_f + ML_HEADS
    o_g = o_nq + nq + 6 * nkv
    o_z = o_g + NSA_HEADS * N_BRANCH
    w_t = w_in.T
    ngate = NSA_HEADS * N_BRANCH
    zeros = lambda n: jnp.zeros((n, D), w_in.dtype)
    wg_t = jnp.concatenate([w_t[o_i:o_f], w_t[o_g:o_z], zeros(LANES - ML_HEADS - ngate),
                            w_t[o_f:o_nq], zeros(LANES - ML_HEADS)], axis=0)
    col_nq = o_i
    col_kv = col_nq + nq
    col_z = col_kv + 6 * nkv

    h, gates = _rmsnorm(x2, norm_w.reshape(1, D), wg_t)
    u = _inproj(h, w_t, [(0, o_i), (o_nq, o_g - o_nq), (o_z, nq)])

    pad_h = lambda v: jnp.concatenate([v, jnp.zeros((LANES - ML_HEADS,), v.dtype)]).reshape(1, LANES)
    y_ml = _mlstm(u, gates, 0, conv_w, pad_h(i_bias), pad_h(f_bias), head_norm_w, B, S)

    qn, ksa, vst, kwn, vwt = _nsa_prep(u, q_norm_w.reshape(1, NSA_HD), k_norm_w, col_nq, col_kv, B, S)
    kcmp, vcmp_t = _compress(u, pe_k, pe_v, kw1.astype(BF16), kw2.astype(BF16), vw1.astype(BF16),
                             vw2.astype(BF16), k_norm_w, col_kv, B, S)
    o_c, sel = _cmp_sel(rel_bias, qn, kcmp, vcmp_t, B, S)
    y_ns = _attend(rel_bias, qn, ksa, vst, kwn, vwt, sel, o_c, gates, 0, u, col_z, ML_HEADS, B, S)

    x1, x1b = _outproj(y_ml, y_ns, w_out, x2)
    return _ple(x1b, x1, p2, ple_gate, ple_proj)


def kernel(x, p, norm_w, w_in, ml_conv_w, ml_i_bias, ml_f_bias, ml_head_norm_w, nsa_q_norm_w, nsa_k_norm_w,
           cmp_pe_k, cmp_pe_v, cmp_k_w1, cmp_k_w2, cmp_v_w1, cmp_v_w2, rel_bias, w_out, ple_proj, ple_gate):
    B, S, D = x.shape
    assert S % max(WIN, 256) == 0 and S // SEL_LEN >= 1
    x2 = x.reshape(B * S, D)
    for layer in range(w_in.shape[0]):
        x2 = _layer(x2, p[layer].reshape(B * S, -1), norm_w[layer], w_in[layer], ml_conv_w[layer],
                    ml_i_bias[layer], ml_f_bias[layer], ml_head_norm_w[layer], nsa_q_norm_w[layer],
                    nsa_k_norm_w[layer], cmp_pe_k[layer], cmp_pe_v[layer], cmp_k_w1[layer], cmp_k_w2[layer],
                    cmp_v_w1[layer], cmp_v_w2[layer], rel_bias, w_out[layer], ple_proj[layer], ple_gate[layer],
                    B, S)
    return x2.reshape(B, S, D)
```

```python
import functools
import math

import numpy as np
import jax
import jax.numpy as jnp
from jax import lax
from jax.experimental import pallas as pl
from jax.experimental.pallas import tpu as pltpu

F32 = jnp.float32
BF16 = jnp.bfloat16
HIGHEST = lax.Precision.HIGHEST

ML_HEADS = 8
ML_DQK = 128
ML_DV = 256
ML_CHUNK = 64
ML_CONV = 4
NSA_HEADS = 16
NSA_HD = 128
NSA_GROUPS = 4
NSA_HPG = 4
N_BRANCH = 3
CMP_STRIDE = 16
CMP_LEN = 32
SEL_LEN = 64
SEL_SHIFT = 6
SEL_TOPK = 16
WIN = 512
REL_BUCKETS = 32
REL_MAX_DIST = 128
EPS = 1e-6
NEG = -1e30
FORCE_SCORE = 1e4
LOG2E = math.log2(math.e)

LANES = 128
VMEM_LIMIT = 56 * 1024 * 1024


def _bucket_thresholds():
    n = np.arange(0, 4 * REL_MAX_DIST, dtype=np.int64)
    max_exact = REL_BUCKETS // 2
    nf = np.maximum(n, 1).astype(np.float32)
    large = max_exact + (np.log(nf / np.float32(max_exact)) / np.float32(math.log(REL_MAX_DIST / max_exact))
                         * np.float32(REL_BUCKETS - max_exact)).astype(np.int32)
    large = np.minimum(large, REL_BUCKETS - 1)
    bucket = np.where(n < max_exact, n, large)
    assert np.all(np.diff(bucket) >= 0)
    thr = [int(np.argmax(bucket >= b)) for b in range(REL_BUCKETS)]
    assert thr[REL_BUCKETS - 1] <= REL_MAX_DIST
    return thr


BUCKET_THR = _bucket_thresholds()


def _dot(a, b, precision=None):
    return jnp.dot(a, b, preferred_element_type=F32, precision=precision)


def _dot_nt(a, b, precision=None):
    return lax.dot_general(a, b, (((1,), (1,)), ((), ())), preferred_element_type=F32, precision=precision)


def _sigmoid(x):
    return 1.0 / (1.0 + jnp.exp(-x))


def _silu(x):
    return x * _sigmoid(x)


def _sigmoid_t(x):
    return 0.5 * jnp.tanh(0.5 * x) + 0.5


def _iota(shape, dim):
    return lax.broadcasted_iota(jnp.int32, shape, dim)


def _rel_bias_pattern(dist, tbl_ref, head):
    val = jnp.full(dist.shape, tbl_ref[0, head], F32)
    for b in range(1, REL_BUCKETS):
        val = jnp.where(dist >= BUCKET_THR[b], tbl_ref[b, head], val)
    return (val - tbl_ref[REL_BUCKETS - 1, head]) * LOG2E


def _rmsnorm_kernel(x_ref, nw_ref, wg_ref, h_ref, g_ref, wgb_ref):
    @pl.when(pl.program_id(0) == 0)
    def _():
        wgb_ref[...] = wg_ref[...].astype(BF16)

    x = x_ref[...]
    ms = jnp.mean(x * x, axis=-1, keepdims=True)
    h = (x * lax.rsqrt(ms + EPS) * nw_ref[...]).astype(BF16)
    h_ref[...] = h
    g_ref[...] = _dot_nt(h, wgb_ref[...])


def _rmsnorm(x2, norm_w, wg_t, rb=256):
    T, D = x2.shape
    NG = wg_t.shape[0]
    return pl.pallas_call(
        _rmsnorm_kernel,
        grid=(T // rb,),
        in_specs=[pl.BlockSpec((rb, D), lambda i: (i, 0)), pl.BlockSpec((1, D), lambda i: (0, 0)),
                  pl.BlockSpec((NG, D), lambda i: (0, 0))],
        out_specs=[pl.BlockSpec((rb, D), lambda i: (i, 0)), pl.BlockSpec((rb, NG), lambda i: (i, 0))],
        out_shape=[jax.ShapeDtypeStruct((T, D), BF16), jax.ShapeDtypeStruct((T, NG), F32)],
        scratch_shapes=[pltpu.VMEM((NG, D), BF16)],
        compiler_params=pltpu.CompilerParams(dimension_semantics=("arbitrary",), vmem_limit_bytes=VMEM_LIMIT),
        name="rmsnorm",
    )(x2, norm_w, wg_t)


def _inproj_kernel(h_ref, wt_ref, u_ref, wb_ref):
    @pl.when(pl.program_id(1) == 0)
    def _():
        wb_ref[...] = wt_ref[...].astype(BF16)

    u_ref[...] = _dot_nt(h_ref[...], wb_ref[...])


def _inproj(h, w_t, pieces, tm=512, tn=1024):
    T, D = h.shape
    starts = []
    for start, length in pieces:
        assert length % tn == 0 and start % 8 == 0
        starts += [start + k * tn for k in range(length // tn)]
    nblk = len(starts)
    bounds = [(b, s) for b, s in enumerate(starts) if b == 0 or s != starts[b - 1] + tn]

    def w_rows(j, i):
        row8 = (bounds[0][1] + j * tn) // 8
        for b, s in bounds[1:]:
            row8 = jnp.where(j >= b, (s + (j - b) * tn) // 8, row8)
        return row8 * 8, 0

    return pl.pallas_call(
        _inproj_kernel,
        grid=(nblk, T // tm),
        in_specs=[
            pl.BlockSpec((tm, D), lambda j, i: (i, 0)),
            pl.BlockSpec((pl.Element(tn), pl.Element(D)), w_rows),
        ],
        out_specs=pl.BlockSpec((tm, tn), lambda j, i: (i, j)),
        out_shape=jax.ShapeDtypeStruct((T, nblk * tn), F32),
        scratch_shapes=[pltpu.VMEM((tn, D), BF16)],
        compiler_params=pltpu.CompilerParams(dimension_semantics=("arbitrary", "arbitrary"),
                                             vmem_limit_bytes=VMEM_LIMIT),
        name="inproj",
    )(h, w_t)


def _mlstm_kernel(q_ref, k_ref, v_ref, o_ref, z_ref, g_ref, cw_ref, ib_ref, fb_ref, hw_ref, y_ref,
                  xbuf, qa, ka, c_ref, m_ref, *, lb):
    L = ML_CHUNK
    qkw = ML_HEADS * ML_DQK

    @pl.when(pl.program_id(1) == 0)
    def _():
        xbuf[0:8, :] = jnp.zeros((8, 2 * qkw), F32)
        c_ref[...] = jnp.zeros(c_ref.shape, F32)
        m_ref[...] = jnp.zeros(m_ref.shape, F32)

    xbuf[8:8 + lb, 0:qkw] = q_ref[...]
    xbuf[8:8 + lb, qkw:2 * qkw] = k_ref[...]
    acc = cw_ref[ML_CONV - 1:ML_CONV, :] * xbuf[8:8 + lb, :]
    for j in range(1, ML_CONV):
        acc = acc + cw_ref[ML_CONV - 1 - j:ML_CONV - j, :] * xbuf[8 - j:8 - j + lb, :]
    act = _silu(acc)
    qa[...] = act[:, 0:qkw].astype(BF16)
    ka[...] = act[:, qkw:2 * qkw] * (ML_DQK ** -0.5)
    xbuf[0:8, :] = xbuf[lb:lb + 8, :]

    row = _iota((L, L), 0)
    col = _iota((L, L), 1)
    tril = col <= row
    rowg = _iota((L, LANES), 0)
    eye_h = (_iota((8, LANES), 0) == _iota((8, LANES), 1)).astype(F32)
    ones_col = (_iota((L, LANES), 1) == 0).astype(BF16)
    heads = range(ML_HEADS)

    def chunk(c, carry):
        rows = pl.ds(pl.multiple_of(c * L, L), L)
        qk_cols = lambda h: slice(h * ML_DQK, (h + 1) * ML_DQK)
        v_cols = lambda h: slice(h * ML_DV, (h + 1) * ML_DV)
        col_of = lambda x, h: x[:, h:h + 1]

        G = g_ref[rows, :]
        li = G[:, 0:LANES] + ib_ref[...]
        fp = G[:, LANES:2 * LANES] + fb_ref[...]
        lf = jnp.minimum(fp, 0.0) - jnp.log(1.0 + jnp.exp(-jnp.abs(fp)))
        b = lf
        for sh in (1, 2, 4, 8, 16, 32):
            b = b + jnp.where(rowg >= sh, pltpu.roll(b, sh, 0), 0.0)
        g = li - b
        gmax = g
        for sh in (1, 2, 4, 8, 16, 32):
            gmax = jnp.maximum(gmax, jnp.where(rowg >= sh, pltpu.roll(gmax, sh, 0), NEG))
        m_old = m_ref[...]
        top = jnp.maximum(m_old, gmax)
        m_t = b + top
        inter = jnp.exp(m_old - top)
        floor = jnp.exp(-m_t)
        bL = b[L - 1:L, :]
        w = bL - b + li
        m_new = jnp.maximum(bL + m_old, jnp.max(w, axis=0, keepdims=True))
        wk = jnp.exp(w - m_new)
        decay = jnp.exp(bL + m_old - m_new)
        m_ref[...] = m_new
        g_rows = _dot_nt(eye_h, g, precision=HIGHEST)

        q = [qa[rows, qk_cols(h)] for h in heads]
        kf = [ka[rows, qk_cols(h)] for h in heads]
        vaug = [jnp.concatenate([v_ref[rows, v_cols(h)].astype(BF16), ones_col], axis=1) for h in heads]
        cst = [c_ref[h] for h in heads]
        lhs = []
        for h in heads:
            dw = jnp.exp(jnp.where(tril, g_rows[h:h + 1, :] - col_of(top, h), NEG))
            sc = (_dot_nt(q[h], kf[h].astype(BF16)) * dw).astype(BF16)
            q_in = (q[h].astype(F32) * col_of(inter, h)).astype(BF16)
            lhs.append(jnp.concatenate([q_in, sc], axis=1))
        num_aug = [_dot(lhs[h], jnp.concatenate([cst[h].astype(BF16), vaug[h]], axis=0)) for h in heads]
        for h in heads:
            kw = (kf[h] * col_of(wk, h)).astype(BF16)
            upd = lax.dot_general(kw, vaug[h], (((0,), (0,)), ((), ())), preferred_element_type=F32)
            c_ref[h] = decay[:, h:h + 1] * cst[h] + upd
        for h in heads:
            num = num_aug[h][:, 0:ML_DV]
            den = num_aug[h][:, ML_DV:ML_DV + 1]
            r = 1.0 / jnp.maximum(jnp.abs(den), col_of(floor, h))
            f = r * lax.rsqrt(r * r * jnp.mean(num * num, axis=-1, keepdims=True) + EPS)
            o = o_ref[rows, v_cols(h)]
            z = z_ref[rows, v_cols(h)]
            gate = _sigmoid_t(o) * z * _sigmoid_t(z)
            y_ref[rows, v_cols(h)] = (num * f * hw_ref[h:h + 1, :] * gate).astype(BF16)
        return carry

    lax.fori_loop(0, lb // L, chunk, 0)


def _mlstm(u, gates, gate_blk, conv_w, ib, fb, head_w, B, S, lb=256):
    T = B * S
    nsb = S // lb
    qkw = ML_HEADS * ML_DQK
    vw = ML_HEADS * ML_DV
    rowmap = lambda col: (lambda b, s: (b * nsb + s, col))
    const = lambda b, s: (0, 0)
    return pl.pallas_call(
        functools.partial(_mlstm_kernel, lb=lb),
        grid=(B, nsb),
        in_specs=[
            pl.BlockSpec((lb, qkw), rowmap(0)),
            pl.BlockSpec((lb, qkw), rowmap(1)),
            pl.BlockSpec((lb, vw), rowmap(1)),
            pl.BlockSpec((lb, vw), rowmap(2)),
            pl.BlockSpec((lb, vw), rowmap(3)),
            pl.BlockSpec((lb, 2 * LANES), rowmap(gate_blk)),
            pl.BlockSpec((ML_CONV, 2 * qkw), const),
            pl.BlockSpec((1, LANES), const),
            pl.BlockSpec((1, LANES), const),
            pl.BlockSpec((ML_HEADS, ML_DV), const),
        ],
        out_specs=pl.BlockSpec((lb, vw), rowmap(0)),
        out_shape=jax.ShapeDtypeStruct((T, vw), BF16),
        scratch_shapes=[
            pltpu.VMEM((lb + 8, 2 * qkw), F32),
            pltpu.VMEM((lb, qkw), BF16),
            pltpu.VMEM((lb, qkw), F32),
            pltpu.VMEM((ML_HEADS, ML_DQK, ML_DV + LANES), F32),
            pltpu.VMEM((1, LANES), F32),
        ],
        compiler_params=pltpu.CompilerParams(dimension_semantics=("arbitrary", "arbitrary"),
                                             vmem_limit_bytes=VMEM_LIMIT),
        name="mlstm",
    )(u, u, u, u, u, gates, conv_w, ib, fb, head_w)


def _rms_heads(x, w, scale=1.0):
    outs = []
    for h in range(x.shape[1] // NSA_HD):
        xh = x[:, h * NSA_HD:(h + 1) * NSA_HD]
        ms = jnp.mean(xh * xh, axis=-1, keepdims=True)
        outs.append(xh * lax.rsqrt(ms + EPS) * w * scale)
    return jnp.concatenate(outs, axis=1)


VT_ROWS = NSA_HD + 16


def _nsa_prep_kernel(q_ref, ks_ref, vs_ref, kw_ref, vw_ref, qw_ref, kn_ref,
                     qn_ref, ksa_ref, vst_ref, kwn_ref, vwt_ref, *, rb, nsb):
    qn_ref[...] = _rms_heads(q_ref[...], qw_ref[...], NSA_HD ** -0.5 * LOG2E).astype(BF16)
    ksn = _rms_heads(ks_ref[...], kn_ref[1:2, :]).astype(BF16)
    kwn_ref[...] = _rms_heads(kw_ref[...], kn_ref[2:3, :]).astype(BF16)
    t = (pl.program_id(0) % nsb) * rb + _iota((rb, NSA_HD), 0)
    onehot = (_iota((rb, NSA_HD), 1) == jnp.right_shift(t, SEL_SHIFT)).astype(BF16)
    tail = (_iota((VT_ROWS - NSA_HD, rb), 0) == 0).astype(BF16)
    for g in range(NSA_GROUPS):
        cols = slice(g * NSA_HD, (g + 1) * NSA_HD)
        ksa_ref[:, 2 * g * NSA_HD:(2 * g + 1) * NSA_HD] = ksn[:, cols]
        ksa_ref[:, (2 * g + 1) * NSA_HD:(2 * g + 2) * NSA_HD] = onehot
        vst_ref[g] = jnp.concatenate([vs_ref[:, cols].T.astype(BF16), tail], axis=0)
        vwt_ref[g] = jnp.concatenate([vw_ref[:, cols].T.astype(BF16), tail], axis=0)


def _nsa_prep(u, q_norm_w, k_norm_w, col_q, col_kv, B, S, rb=256):
    T = u.shape[0]
    nsb = S // rb
    qw = NSA_HEADS * NSA_HD
    kvw = NSA_GROUPS * NSA_HD
    cq = col_q // qw
    ck = col_kv // kvw
    assert S // SEL_LEN <= NSA_HD
    kv_spec = lambda idx: pl.BlockSpec((rb, kvw), lambda i: (i, ck + idx))
    out_k = pl.BlockSpec((rb, kvw), lambda i: (i, 0))
    out_ka = pl.BlockSpec((rb, 2 * kvw), lambda i: (i, 0))
    out_vt = pl.BlockSpec((None, NSA_GROUPS, VT_ROWS, rb), lambda i: (i // nsb, 0, 0, i % nsb))
    vt_shape = jax.ShapeDtypeStruct((B, NSA_GROUPS, VT_ROWS, S), BF16)
    return pl.pallas_call(
        functools.partial(_nsa_prep_kernel, rb=rb, nsb=nsb),
        grid=(T // rb,),
        in_specs=[
            pl.BlockSpec((rb, qw), lambda i: (i, cq)),
            kv_spec(2), kv_spec(3), kv_spec(4), kv_spec(5),
            pl.BlockSpec((1, NSA_HD), lambda i: (0, 0)),
            pl.BlockSpec((N_BRANCH, NSA_HD), lambda i: (0, 0)),
        ],
        out_specs=[pl.BlockSpec((rb, qw), lambda i: (i, 0)), out_ka, out_vt, out_k, out_vt],
        out_shape=[jax.ShapeDtypeStruct((T, qw), BF16), jax.ShapeDtypeStruct((T, 2 * kvw), BF16), vt_shape,
                   jax.ShapeDtypeStruct((T, kvw), BF16), vt_shape],
        compiler_params=pltpu.CompilerParams(dimension_semantics=("arbitrary",), vmem_limit_bytes=VMEM_LIMIT),
        name="nsa_prep",
    )(u, u, u, u, u, q_norm_w, k_norm_w)


def _compress_kernel(kc_ref, vc_ref, pek_ref, pev_ref, kw1_ref, kw2_ref, vw1_ref, vw2t_ref, kn_ref,
                     kcmp_ref, vcmp_ref, *, ns):
    def hidden(x_ref, pe_ref, w1_ref):
        hid = CMP_LEN // 2
        a = jnp.zeros((ns, w1_ref.shape[1]), F32)
        bsum = jnp.zeros((ns, w1_ref.shape[1]), F32)
        for l in range(hid):
            xl = x_ref[pl.ds(l, ns, stride=CMP_STRIDE), :]
            a = a + _dot((xl + pe_ref[l:l + 1, :]).astype(BF16), w1_ref[l * NSA_HD:(l + 1) * NSA_HD, :])
            bsum = bsum + _dot((xl + pe_ref[hid + l:hid + l + 1, :]).astype(BF16),
                               w1_ref[(hid + l) * NSA_HD:(hid + l + 1) * NSA_HD, :])
        pre = a + pltpu.roll(bsum, ns - 1, 0)
        return _silu(pre).astype(BF16)

    kc = _dot(hidden(kc_ref, pek_ref, kw1_ref), kw2_ref[...])
    ms = jnp.mean(kc * kc, axis=-1, keepdims=True)
    kcmp_ref[...] = (kc * lax.rsqrt(ms + EPS) * kn_ref[0:1, :]).astype(BF16)
    vcmp_ref[...] = _dot_nt(vw2t_ref[...], hidden(vc_ref, pev_ref, vw1_ref)).astype(BF16)


def _compress(u, pe_k, pe_v, kw1, kw2, vw1, vw2, k_norm_w, col_kv, B, S):
    ns = S // CMP_STRIDE
    ck = col_kv // NSA_HD
    hidden = kw1.shape[1]
    const = lambda b, g: (0, 0)
    return pl.pallas_call(
        functools.partial(_compress_kernel, ns=ns),
        grid=(B, NSA_GROUPS),
        in_specs=[
            pl.BlockSpec((S, NSA_HD), lambda b, g: (b, ck + g)),
            pl.BlockSpec((S, NSA_HD), lambda b, g: (b, ck + NSA_GROUPS + g)),
            pl.BlockSpec((CMP_LEN, NSA_HD), const),
            pl.BlockSpec((CMP_LEN, NSA_HD), const),
            pl.BlockSpec((CMP_LEN * NSA_HD, hidden), const),
            pl.BlockSpec((hidden, NSA_HD), const),
            pl.BlockSpec((CMP_LEN * NSA_HD, hidden), const),
            pl.BlockSpec((NSA_HD, hidden), const),
            pl.BlockSpec((N_BRANCH, NSA_HD), const),
        ],
        out_specs=[pl.BlockSpec((None, None, ns, NSA_HD), lambda b, g: (b, g, 0, 0)),
                   pl.BlockSpec((None, None, NSA_HD, ns), lambda b, g: (b, g, 0, 0))],
        out_shape=[jax.ShapeDtypeStruct((B, NSA_GROUPS, ns, NSA_HD), BF16),
                   jax.ShapeDtypeStruct((B, NSA_GROUPS, NSA_HD, ns), BF16)],
        compiler_params=pltpu.CompilerParams(dimension_semantics=("arbitrary", "arbitrary"),
                                             vmem_limit_bytes=VMEM_LIMIT),
        name="compress",
    )(u, u, pe_k, pe_v, kw1, kw2, vw1, vw2.T, k_norm_w)


TQC = 256
NEAR = 32
NEAR_BACK = 16


def _stack_heads(q4):
    return jnp.concatenate([q4[:, h * NSA_HD:(h + 1) * NSA_HD] for h in range(NSA_HPG)], axis=0)


def _cmp_sel_kernel(tbl_ref, q_ref, kc_ref, vct_ref, oc_ref, sel_ref, pat_ref, *, ns, nsel):
    g = pl.program_id(1)
    i = pl.program_id(2)
    t0 = i * TQC
    c0 = i * (TQC // CMP_STRIDE)
    W = NSA_HPG * TQC
    assert (TQC - CMP_LEN) // CMP_STRIDE < NEAR - NEAR_BACK and NEAR_BACK * CMP_STRIDE >= REL_MAX_DIST + CMP_LEN

    @pl.when(i == 0)
    def _():
        r = _iota((TQC, LANES), 0)
        lane = _iota((TQC, LANES), 1)
        d = r - CMP_STRIDE * (jnp.bitwise_and(lane, NEAR - 1) - NEAR_BACK) - (CMP_LEN - 1)
        for h in range(NSA_HPG):
            val = jnp.where(d < 0, NEG, _rel_bias_pattern(d, tbl_ref, g * NSA_HPG + h))
            hi = val.astype(BF16).astype(F32)
            lo = jnp.where(d < 0, 0.0, val - hi)
            ext = jnp.where(lane < NEAR, hi, jnp.where(lane < 2 * NEAR, lo, jnp.where(lane == 2 * NEAR, NEG, 0.0)))
            pat_ref[h * TQC:(h + 1) * TQC, :] = ext.astype(BF16)

    q4 = q_ref[...]
    q_aug = jnp.concatenate([jnp.concatenate([q4[:, h * NSA_HD:(h + 1) * NSA_HD],
                                              pat_ref[h * TQC:(h + 1) * TQC, :]], axis=1)
                             for h in range(NSA_HPG)], axis=0)
    rel = _iota((ns, LANES), 0) - (c0 - NEAR_BACK)
    lane = _iota((ns, LANES), 1)
    ext = ((lane < 2 * NEAR) & (rel == jnp.bitwise_and(lane, NEAR - 1))) | ((lane == 2 * NEAR) & (rel >= NEAR))
    k_aug = jnp.concatenate([kc_ref[...], jnp.where(ext, 1.0, 0.0).astype(BF16)], axis=1)

    s = _dot_nt(k_aug, q_aug)
    m = jnp.max(s, axis=0, keepdims=True)
    e = jnp.exp2(s - m)
    l = jnp.sum(e, axis=0, keepdims=True)
    tq = t0 + jnp.bitwise_and(_iota((1, W), 1), TQC - 1)
    inv = jnp.where(tq >= CMP_LEN - 1, 1.0 / l, 0.0)
    o_t = _dot(vct_ref[...], e.astype(BF16)) * inv
    for h in range(NSA_HPG):
        oc_ref[:, h * NSA_HD:(h + 1) * NSA_HD] = o_t[:, h * TQC:(h + 1) * TQC].T

    p = e * inv
    ps = p[:, 0:TQC]
    for h in range(1, NSA_HPG):
        ps = ps + p[:, h * TQC:(h + 1) * TQC]
    ratio = SEL_LEN // CMP_STRIDE
    blk_n = _iota((nsel, ns), 0)
    tok_c = _iota((nsel, ns), 1)
    ov = ((tok_c >= ratio * blk_n - (CMP_LEN // CMP_STRIDE - 1)) & (tok_c <= ratio * blk_n + ratio - 1)).astype(F32)
    imp = _dot(ov, ps, precision=HIGHEST)

    blk = _iota((nsel, TQC), 0)
    cur = jnp.right_shift(t0 + _iota((nsel, TQC), 1), SEL_SHIFT)
    imp = jnp.where((blk == 0) | (blk == cur) | (blk == cur - 1), FORCE_SCORE, imp)
    imp = jnp.where(blk > cur, NEG, imp)
    k_top = min(SEL_TOPK, nsel)

    def emit(chosen):
        chosen = jnp.concatenate([chosen.astype(BF16), jnp.zeros((LANES - nsel, TQC), BF16)], axis=0)
        eye_q = (_iota((TQC, TQC), 0) == _iota((TQC, TQC), 1)).astype(BF16)
        sel_ref[...] = ((_dot_nt(eye_q, chosen) - 1.0) * (-NEG)).astype(BF16)

    few = (i + 1) * (TQC // SEL_LEN) <= k_top

    @pl.when(few)
    def _():
        emit(jnp.where(blk <= cur, 1.0, 0.0))

    @pl.when(jnp.logical_not(few))
    def _():
        tiles = [imp[8 * k:8 * k + 8, :] for k in range(nsel // 8)]
        counts = [jnp.zeros((8, TQC), jnp.int32) for _ in tiles]
        sub = _iota((8, TQC), 0)
        for jb in range(nsel):
            vj = imp[jb:jb + 1, :]
            for k, tile in enumerate(tiles):
                if 8 * k > jb:
                    ahead = vj >= tile
                elif 8 * k + 7 < jb:
                    ahead = vj > tile
                else:
                    ahead = (vj > tile) | ((vj == tile) & (sub + 8 * k > jb))
                counts[k] = counts[k] + jnp.where(ahead, 1, 0)
        rank = jnp.concatenate(counts, axis=0)
        emit(jnp.where(rank < k_top, 1.0, 0.0))


def _cmp_sel(rel_bias, qn, kcmp, vcmp_t, B, S):
    ns = S // CMP_STRIDE
    nsel = S // SEL_LEN
    nt = S // TQC
    gw = NSA_HPG * NSA_HD
    assert nsel % 8 == 0 and nsel <= LANES
    return pl.pallas_call(
        functools.partial(_cmp_sel_kernel, ns=ns, nsel=nsel),
        grid=(B, NSA_GROUPS, nt),
        in_specs=[
            pl.BlockSpec(memory_space=pltpu.SMEM),
            pl.BlockSpec((TQC, gw), lambda b, g, i: (b * nt + i, g)),
            pl.BlockSpec((None, None, ns, NSA_HD), lambda b, g, i: (b, g, 0, 0)),
            pl.BlockSpec((None, None, NSA_HD, ns), lambda b, g, i: (b, g, 0, 0)),
        ],
        out_specs=[
            pl.BlockSpec((TQC, gw), lambda b, g, i: (b * nt + i, g)),
            pl.BlockSpec((None, None, TQC, LANES), lambda b, g, i: (b, g, i, 0)),
        ],
        out_shape=[jax.ShapeDtypeStruct((B * S, NSA_HEADS * NSA_HD), F32),
                   jax.ShapeDtypeStruct((B, NSA_GROUPS, S, LANES), BF16)],
        scratch_shapes=[pltpu.VMEM((NSA_HPG * TQC, LANES), BF16)],
        compiler_params=pltpu.CompilerParams(dimension_semantics=("arbitrary", "arbitrary", "arbitrary"),
                                             vmem_limit_bytes=VMEM_LIMIT),
        name="cmp_sel",
    )(rel_bias, qn, kcmp, vcmp_t)


TA = 512
SUB = 128


def _attend_kernel(tbl_ref, q_ref, ksa_ref, vst_ref, kw_ref, vwt_ref, sel_ref, oc_ref, g_ref, z_ref, y_ref,
                   pd_ref, pp_ref, sa_ref, sb_ref, ms_ref, accs_ref, mw_ref, accw_ref, gt_ref, *, gate_col):
    g = pl.program_id(1)
    i = pl.program_id(2)
    nsub = TA // SUB

    krow = _iota((SUB, SUB), 0)
    qcol = _iota((SUB, SUB), 1)

    @pl.when(i == 0)
    def _():
        for h in range(NSA_HPG):
            head = g * NSA_HPG + h
            lanes = slice(h * SUB, (h + 1) * SUB)
            pd_ref[:, lanes] = jnp.where(krow <= qcol, _rel_bias_pattern(qcol - krow, tbl_ref, head), NEG)
            pp_ref[:, lanes] = _rel_bias_pattern(qcol - krow + SUB, tbl_ref, head)

    SLAB = NSA_HPG * SUB
    q4 = q_ref[...]
    sel = sel_ref[...]
    q_win = jnp.concatenate([q4[qb * SUB:(qb + 1) * SUB, h * NSA_HD:(h + 1) * NSA_HD]
                             for qb in range(nsub) for h in range(NSA_HPG)], axis=0)
    q_sel = jnp.concatenate([jnp.concatenate([q4[qb * SUB:(qb + 1) * SUB, h * NSA_HD:(h + 1) * NSA_HD],
                                              sel[qb * SUB:(qb + 1) * SUB, :]], axis=1)
                             for qb in range(nsub) for h in range(NSA_HPG)], axis=0)
    strict = jnp.concatenate([jnp.where(krow > qcol, 0.0, NEG)] * NSA_HPG, axis=1)

    def chunk(c):
        return pl.ds(pl.multiple_of(c * TA, TA), TA)

    def keys(c, lo, hi):
        return pl.ds(pl.multiple_of(c * TA + lo * SUB, SUB), (hi - lo) * SUB)

    def rows(kb):
        return slice(kb * SUB, (kb + 1) * SUB)

    def slab(qb):
        return slice(qb * SLAB, (qb + 1) * SLAB)

    def absorb(s_ref, vt_ref, c, m_ref, acc_ref, first=False, lo=0, hi=nsub, qb=None):
        lanes = slice(None) if qb is None else slab(qb)
        krows = slice(lo * SUB, hi * SUB)
        mx = jnp.max(s_ref[krows, lanes], axis=0, keepdims=True)
        if first:
            m_new = mx
        else:
            m_old = m_ref[:, lanes]
            m_new = jnp.maximum(m_old, mx)
        p = jnp.exp2(s_ref[krows, lanes] - m_new).astype(BF16)
        pv = _dot(vt_ref[:, keys(c, lo, hi)], p)
        if first:
            acc_ref[:, lanes] = pv
        else:
            acc_ref[:, lanes] = jnp.exp2(m_old - m_new) * acc_ref[:, lanes] + pv
        m_ref[:, lanes] = m_new

    def qk_sel(s_ref, c):
        s_ref[...] = _dot_nt(ksa_ref[chunk(c), :], q_sel)

    def diag(s_ref, k_ref, q_all, vt_ref, m_ref, acc_ref):
        for qb in range(nsub):
            s_ref[0:(qb + 1) * SUB, slab(qb)] = _dot_nt(k_ref[keys(i, 0, qb + 1), :], q_all[slab(qb), :])
            s_ref[rows(qb), slab(qb)] += pd_ref[...]
            if qb >= 1:
                s_ref[rows(qb - 1), slab(qb)] += pp_ref[...]
        for qb in range(nsub):
            absorb(s_ref, vt_ref, i, m_ref, acc_ref, first=True, lo=0, hi=qb + 1, qb=qb)

    def win_prev(s_ref):
        for qb in range(nsub):
            s_ref[qb * SUB:TA, slab(qb)] = _dot_nt(kw_ref[keys(i - 1, qb, nsub), :], q_win[slab(qb), :])
            s_ref[rows(qb), slab(qb)] += strict
        s_ref[rows(nsub - 1), slab(0)] += pp_ref[...]
        for qb in range(nsub):
            absorb(s_ref, vwt_ref, i - 1, mw_ref, accw_ref, lo=qb, hi=nsub, qb=qb)

    @pl.when(i == 0)
    def _():
        diag(sa_ref, ksa_ref, q_sel, vst_ref, ms_ref, accs_ref)
        diag(sb_ref, kw_ref, q_win, vwt_ref, mw_ref, accw_ref)

    @pl.when(i >= 1)
    def _():
        qk_sel(sb_ref, i - 1)
        sb_ref[rows(nsub - 1), slab(0)] += pp_ref[...]
        diag(sa_ref, ksa_ref, q_sel, vst_ref, ms_ref, accs_ref)
        absorb(sb_ref, vst_ref, i - 1, ms_ref, accs_ref)
        diag(sa_ref, kw_ref, q_win, vwt_ref, mw_ref, accw_ref)
        win_prev(sb_ref)

    nfar = jnp.maximum(i - 1, 0)
    odd = nfar % 2

    @pl.when(odd == 1)
    def _():
        qk_sel(sa_ref, 0)
        absorb(sa_ref, vst_ref, 0, ms_ref, accs_ref)

    npair = nfar // 2

    @pl.when(npair > 0)
    def _():
        qk_sel(sa_ref, odd)

    def pair(t, carry):
        c = odd + 2 * t
        qk_sel(sb_ref, c + 1)
        absorb(sa_ref, vst_ref, c, ms_ref, accs_ref)
        qk_sel(sa_ref, jnp.minimum(c + 2, nfar - 1))
        absorb(sb_ref, vst_ref, c + 1, ms_ref, accs_ref)
        return carry

    lax.fori_loop(0, npair, pair, 0)

    gs = _sigmoid_t(g_ref[:, 0:LANES])
    gt_ref[...] = gs.T
    inv_s = 1.0 / accs_ref[NSA_HD:NSA_HD + 1, :]
    inv_w = 1.0 / accw_ref[NSA_HD:NSA_HD + 1, :]
    lane = _iota((TA, LANES), 1)
    for h in range(NSA_HPG):
        base = gate_col + (g * NSA_HPG + h) * N_BRANCH
        cols = slice(h * NSA_HD, (h + 1) * NSA_HD)
        gate_c = jnp.sum(jnp.where(lane == base, gs, 0.0), axis=-1, keepdims=True)
        gs_row = gt_ref[pl.ds(base + 1, 1), :]
        gw_row = gt_ref[pl.ds(base + 2, 1), :]
        o_sw = []
        for qb in range(nsub):
            lanes = slice(qb * SLAB + h * SUB, qb * SLAB + (h + 1) * SUB)
            gate_s = gs_row[:, rows(qb)] * inv_s[:, lanes]
            gate_w = gw_row[:, rows(qb)] * inv_w[:, lanes]
            o_sw.append((gate_s * accs_ref[0:NSA_HD, lanes] + gate_w * accw_ref[0:NSA_HD, lanes]).T)
        o = gate_c * oc_ref[:, cols] + jnp.concatenate(o_sw, axis=0)
        z = z_ref[:, cols]
        y_ref[:, cols] = (o * z * _sigmoid_t(z)).astype(BF16)


def _attend(rel_bias, qn, ksa, vst, kwn, vwt, sel, o_c, gates, gate_blk, u, col_z, gate_col, B, S):
    nt = S // TA
    gw = NSA_HPG * NSA_HD
    W = NSA_HPG * TA
    cz = col_z // gw
    assert WIN == TA
    tile = lambda b, g, i: (b * nt + i, g)
    vt_spec = pl.BlockSpec((None, None, VT_ROWS, S), lambda b, g, i: (b, g, 0, 0))
    return pl.pallas_call(
        functools.partial(_attend_kernel, gate_col=gate_col),
        grid=(B, NSA_GROUPS, nt),
        in_specs=[
            pl.BlockSpec(memory_space=pltpu.SMEM),
            pl.BlockSpec((TA, gw), tile),
            pl.BlockSpec((S, 2 * NSA_HD), lambda b, g, i: (b, g)), vt_spec,
            pl.BlockSpec((S, NSA_HD), lambda b, g, i: (b, g)), vt_spec,
            pl.BlockSpec((None, None, TA, LANES), lambda b, g, i: (b, g, i, 0)),
            pl.BlockSpec((TA, gw), tile),
            pl.BlockSpec((TA, 2 * LANES), lambda b, g, i: (b * nt + i, gate_blk)),
            pl.BlockSpec((TA, gw), lambda b, g, i: (b * nt + i, cz + g)),
        ],
        out_specs=pl.BlockSpec((TA, gw), tile),
        out_shape=jax.ShapeDtypeStruct((B * S, NSA_HEADS * NSA_HD), BF16),
        scratch_shapes=[pltpu.VMEM((SUB, NSA_HPG * SUB), F32), pltpu.VMEM((SUB, NSA_HPG * SUB), F32),
                        pltpu.VMEM((TA, W), F32), pltpu.VMEM((TA, W), F32),
                        pltpu.VMEM((1, W), F32), pltpu.VMEM((VT_ROWS, W), F32),
                        pltpu.VMEM((1, W), F32), pltpu.VMEM((VT_ROWS, W), F32),
                        pltpu.VMEM((LANES, TA), F32)],
        compiler_params=pltpu.CompilerParams(dimension_semantics=("arbitrary", "arbitrary", "arbitrary"),
                                             vmem_limit_bytes=VMEM_LIMIT),
        name="attend",
    )(rel_bias, qn, ksa, vst, kwn, vwt, sel, o_c, gates, u)


def _outproj_kernel(yml_ref, yns_ref, w_ref, x_ref, o_ref, ob_ref, wb_ref):
    @pl.when(pl.program_id(1) == 0)
    def _():
        wb_ref[...] = w_ref[...].astype(BF16)

    half = yml_ref.shape[1]
    o = x_ref[...] + _dot(yml_ref[...], wb_ref[0:half, :]) + _dot(yns_ref[...], wb_ref[half:2 * half, :])
    o_ref[...] = o
    ob_ref[...] = o.astype(BF16)


def _outproj(y_ml, y_ns, w_out, x2, tm=256, tn=1024):
    T, D = x2.shape
    half = y_ml.shape[1]
    tile = pl.BlockSpec((tm, tn), lambda j, i: (i, j))
    return pl.pallas_call(
        _outproj_kernel,
        grid=(D // tn, T // tm),
        in_specs=[
            pl.BlockSpec((tm, half), lambda j, i: (i, 0)),
            pl.BlockSpec((tm, half), lambda j, i: (i, 0)),
            pl.BlockSpec((2 * half, tn), lambda j, i: (0, j)),
            tile,
        ],
        out_specs=[tile, tile],
        out_shape=[jax.ShapeDtypeStruct((T, D), F32), jax.ShapeDtypeStruct((T, D), BF16)],
        scratch_shapes=[pltpu.VMEM((2 * half, tn), BF16)],
        compiler_params=pltpu.CompilerParams(dimension_semantics=("arbitrary", "arbitrary"),
                                             vmem_limit_bytes=VMEM_LIMIT),
        name="outproj",
    )(y_ml, y_ns, w_out, x2)


def _ple_kernel(xb_ref, x_ref, p_ref, wg_ref, wp_ref, o_ref, wgb_ref, wpb_ref):
    @pl.when(pl.program_id(1) == 0)
    def _():
        wgb_ref[...] = wg_ref[...].astype(BF16)
        wpb_ref[...] = wp_ref[...].astype(BF16)

    gate = _sigmoid(_dot(xb_ref[...], wgb_ref[...]))
    emb = _dot(p_ref[...].astype(BF16), wpb_ref[...])
    o_ref[...] = x_ref[...] + gate * emb


def _ple(x1b, x1, p2, wg, wp, tm=256, tn=1024):
    T, D = x1.shape
    P = p2.shape[1]
    tile = pl.BlockSpec((tm, tn), lambda j, i: (i, j))
    return pl.pallas_call(
        _ple_kernel,
        grid=(D // tn, T // tm),
        in_specs=[
            pl.BlockSpec((tm, D), lambda j, i: (i, 0)),
            tile,
            pl.BlockSpec((tm, P), lambda j, i: (i, 0)),
            pl.BlockSpec((D, tn), lambda j, i: (0, j)),
            pl.BlockSpec((P, tn), lambda j, i: (0, j)),
        ],
        out_specs=tile,
        out_shape=jax.ShapeDtypeStruct((T, D), F32),
        scratch_shapes=[pltpu.VMEM((D, tn), BF16), pltpu.VMEM((P, tn), BF16)],
        compiler_params=pltpu.CompilerParams(dimension_semantics=("arbitrary", "arbitrary"),
                                             vmem_limit_bytes=VMEM_LIMIT),
        name="ple",
    )(x1b, x1, p2, wg, wp)


def _layer(x2, p2, norm_w, w_in, conv_w, i_bias, f_bias, head_norm_w, q_norm_w, k_norm_w,
           pe_k, pe_v, kw1, kw2, vw1, vw2, rel_bias, w_out, ple_proj, ple_gate, B, S):
    D = x2.shape[1]
    qkw = ML_HEADS * ML_DQK
    vw = ML_HEADS * ML_DV
    nq = NSA_HEADS * NSA_HD
    nkv = NSA_GROUPS * NSA_HD
    o_i = 2 * qkw + 3 * vw
    o_f = o_i + ML_HEADS
    o_nq = o_f + ML_HEADS
    o_g = o_nq + nq + 6 * nkv
    o_z = o_g + NSA_HEADS * N_BRANCH
    w_t = w_in.T
    ngate = NSA_HEADS * N_BRANCH
    zeros = lambda n: jnp.zeros((n, D), w_in.dtype)
    wg_t = jnp.concatenate([w_t[o_i:o_f], w_t[o_g:o_z], zeros(LANES - ML_HEADS - ngate),
                            w_t[o_f:o_nq], zeros(LANES - ML_HEADS)], axis=0)
    col_nq = o_i
    col_kv = col_nq + nq
    col_z = col_kv + 6 * nkv

    h, gates = _rmsnorm(x2, norm_w.reshape(1, D), wg_t)
    u = _inproj(h, w_t, [(0, o_i), (o_nq, o_g - o_nq), (o_z, nq)])

    pad_h = lambda v: jnp.concatenate([v, jnp.zeros((LANES - ML_HEADS,), v.dtype)]).reshape(1, LANES)
    y_ml = _mlstm(u, gates, 0, conv_w, pad_h(i_bias), pad_h(f_bias), head_norm_w, B, S)

    qn, ksa, vst, kwn, vwt = _nsa_prep(u, q_norm_w.reshape(1, NSA_HD), k_norm_w, col_nq, col_kv, B, S)
    kcmp, vcmp_t = _compress(u, pe_k, pe_v, kw1.astype(BF16), kw2.astype(BF16), vw1.astype(BF16),
                             vw2.astype(BF16), k_norm_w, col_kv, B, S)
    o_c, sel = _cmp_sel(rel_bias, qn, kcmp, vcmp_t, B, S)
    y_ns = _attend(rel_bias, qn, ksa, vst, kwn, vwt, sel, o_c, gates, 0, u, col_z, ML_HEADS, B, S)

    x1, x1b = _outproj(y_ml, y_ns, w_out, x2)
    return _ple(x1b, x1, p2, ple_gate, ple_proj)


def kernel(x, p, norm_w, w_in, ml_conv_w, ml_i_bias, ml_f_bias, ml_head_norm_w, nsa_q_norm_w, nsa_k_norm_w,
           cmp_pe_k, cmp_pe_v, cmp_k_w1, cmp_k_w2, cmp_v_w1, cmp_v_w2, rel_bias, w_out, ple_proj, ple_gate):
    B, S, D = x.shape
    assert S % max(WIN, 256) == 0 and S // SEL_LEN >= 1
    x2 = x.reshape(B * S, D)
    for layer in range(w_in.shape[0]):
        x2 = _layer(x2, p[layer].reshape(B * S, -1), norm_w[layer], w_in[layer], ml_conv_w[layer],
                    ml_i_bias[layer], ml_f_bias[layer], ml_head_norm_w[layer], nsa_q_norm_w[layer],
                    nsa_k_norm_w[layer], cmp_pe_k[layer], cmp_pe_v[layer], cmp_k_w1[layer], cmp_k_w2[layer],
                    cmp_v_w1[layer], cmp_v_w2[layer], rel_bias, w_out[layer], ple_proj[layer], ple_gate[layer],
                    B, S)
    return x2.reshape(B, S, D)
```

```python
import functools
import math

import numpy as np
import jax
import jax.numpy as jnp
from jax import lax
from jax.experimental import pallas as pl
from jax.experimental.pallas import tpu as pltpu

F32 = jnp.float32
BF16 = jnp.bfloat16
HIGHEST = lax.Precision.HIGHEST

ML_HEADS = 8
ML_DQK = 128
ML_DV = 256
ML_CHUNK = 64
ML_CONV = 4
NSA_HEADS = 16
NSA_HD = 128
NSA_GROUPS = 4
NSA_HPG = 4
N_BRANCH = 3
CMP_STRIDE = 16
CMP_LEN = 32
SEL_LEN = 64
SEL_SHIFT = 6
SEL_TOPK = 16
WIN = 512
REL_BUCKETS = 32
REL_MAX_DIST = 128
EPS = 1e-6
NEG = -1e30
FORCE_SCORE = 1e4
LOG2E = math.log2(math.e)

LANES = 128
VMEM_LIMIT = 56 * 1024 * 1024


def _bucket_thresholds():
    n = np.arange(0, 4 * REL_MAX_DIST, dtype=np.int64)
    max_exact = REL_BUCKETS // 2
    nf = np.maximum(n, 1).astype(np.float32)
    large = max_exact + (np.log(nf / np.float32(max_exact)) / np.float32(math.log(REL_MAX_DIST / max_exact))
                         * np.float32(REL_BUCKETS - max_exact)).astype(np.int32)
    large = np.minimum(large, REL_BUCKETS - 1)
    bucket = np.where(n < max_exact, n, large)
    assert np.all(np.diff(bucket) >= 0)
    thr = [int(np.argmax(bucket >= b)) for b in range(REL_BUCKETS)]
    assert thr[REL_BUCKETS - 1] <= REL_MAX_DIST
    return thr


BUCKET_THR = _bucket_thresholds()


def _dot(a, b, precision=None):
    return jnp.dot(a, b, preferred_element_type=F32, precision=precision)


def _dot_nt(a, b, precision=None):
    return lax.dot_general(a, b, (((1,), (1,)), ((), ())), preferred_element_type=F32, precision=precision)


def _sigmoid(x):
    return 1.0 / (1.0 + jnp.exp(-x))


def _silu(x):
    return x * _sigmoid(x)


def _sigmoid_t(x):
    return 0.5 * jnp.tanh(0.5 * x) + 0.5


def _iota(shape, dim):
    return lax.broadcasted_iota(jnp.int32, shape, dim)


def _rel_bias_pattern(dist, tbl_ref, head):
    val = jnp.full(dist.shape, tbl_ref[0, head], F32)
    for b in range(1, REL_BUCKETS):
        val = jnp.where(dist >= BUCKET_THR[b], tbl_ref[b, head], val)
    return (val - tbl_ref[REL_BUCKETS - 1, head]) * LOG2E


def _rmsnorm_kernel(x_ref, nw_ref, wg_ref, h_ref, g_ref, wgb_ref):
    @pl.when(pl.program_id(0) == 0)
    def _():
        wgb_ref[...] = wg_ref[...].astype(BF16)

    x = x_ref[...]
    ms = jnp.mean(x * x, axis=-1, keepdims=True)
    h = (x * lax.rsqrt(ms + EPS) * nw_ref[...]).astype(BF16)
    h_ref[...] = h
    g_ref[...] = _dot_nt(h, wgb_ref[...])


def _rmsnorm(x2, norm_w, wg_t, rb=256):
    T, D = x2.shape
    NG = wg_t.shape[0]
    return pl.pallas_call(
        _rmsnorm_kernel,
        grid=(T // rb,),
        in_specs=[pl.BlockSpec((rb, D), lambda i: (i, 0)), pl.BlockSpec((1, D), lambda i: (0, 0)),
                  pl.BlockSpec((NG, D), lambda i: (0, 0))],
        out_specs=[pl.BlockSpec((rb, D), lambda i: (i, 0)), pl.BlockSpec((rb, NG), lambda i: (i, 0))],
        out_shape=[jax.ShapeDtypeStruct((T, D), BF16), jax.ShapeDtypeStruct((T, NG), F32)],
        scratch_shapes=[pltpu.VMEM((NG, D), BF16)],
        compiler_params=pltpu.CompilerParams(dimension_semantics=("arbitrary",), vmem_limit_bytes=VMEM_LIMIT),
        name="rmsnorm",
    )(x2, norm_w, wg_t)


def _inproj_kernel(h_ref, wt_ref, u_ref, wb_ref):
    @pl.when(pl.program_id(1) == 0)
    def _():
        wb_ref[...] = wt_ref[...].astype(BF16)

    u_ref[...] = _dot_nt(h_ref[...], wb_ref[...])


def _piece_blocks(pieces, tn):
    starts = []
    for start, length in pieces:
        assert length % tn == 0 and start % 8 == 0
        starts += [start + k * tn for k in range(length // tn)]
    bounds = [(b, s) for b, s in enumerate(starts) if b == 0 or s != starts[b - 1] + tn]

    def w_rows(j, i):
        row8 = (bounds[0][1] + j * tn) // 8
        for b, s in bounds[1:]:
            row8 = jnp.where(j >= b, (s + (j - b) * tn) // 8, row8)
        return row8 * 8, 0

    return len(starts), w_rows


def _inproj(h, w_t, pieces, tm=512, tn=1024):
    T, D = h.shape
    nblk, w_rows = _piece_blocks(pieces, tn)
    return pl.pallas_call(
        _inproj_kernel,
        grid=(nblk, T // tm),
        in_specs=[
            pl.BlockSpec((tm, D), lambda j, i: (i, 0)),
            pl.BlockSpec((pl.Element(tn), pl.Element(D)), w_rows),
        ],
        out_specs=pl.BlockSpec((tm, tn), lambda j, i: (i, j)),
        out_shape=jax.ShapeDtypeStruct((T, nblk * tn), F32),
        scratch_shapes=[pltpu.VMEM((tn, D), BF16)],
        compiler_params=pltpu.CompilerParams(dimension_semantics=("arbitrary", "arbitrary"),
                                             vmem_limit_bytes=VMEM_LIMIT),
        name="inproj",
    )(h, w_t)


def _mlstm_kernel(q_ref, k_ref, v_ref, o_ref, z_ref, g_ref, cw_ref, ib_ref, fb_ref, hw_ref, y_ref,
                  xbuf, c_ref, m_ref, *, lb):
    L = ML_CHUNK
    qkw = ML_HEADS * ML_DQK

    @pl.when(pl.program_id(1) == 0)
    def _():
        xbuf[0:8, :] = jnp.zeros((8, 2 * qkw), F32)
        c_ref[...] = jnp.zeros(c_ref.shape, F32)
        m_ref[...] = jnp.zeros(m_ref.shape, F32)

    row = _iota((L, L), 0)
    col = _iota((L, L), 1)
    tril = col <= row
    rowg = _iota((L, LANES), 0)
    eye_h = (_iota((8, LANES), 0) == _iota((8, LANES), 1)).astype(F32)
    ones_col = (_iota((L, LANES), 1) == 0).astype(BF16)
    heads = range(ML_HEADS)

    def chunk(c, carry):
        rows = pl.ds(pl.multiple_of(c * L, L), L)
        qk_cols = lambda h: slice(h * ML_DQK, (h + 1) * ML_DQK)
        v_cols = lambda h: slice(h * ML_DV, (h + 1) * ML_DV)
        col_of = lambda x, h: x[:, h:h + 1]

        G = g_ref[rows, :]
        li = G[:, 0:LANES] + ib_ref[...]
        fp = G[:, LANES:2 * LANES] + fb_ref[...]
        lf = jnp.minimum(fp, 0.0) - jnp.log(1.0 + jnp.exp(-jnp.abs(fp)))
        b = lf
        for sh in (1, 2, 4, 8, 16, 32):
            b = b + jnp.where(rowg >= sh, pltpu.roll(b, sh, 0), 0.0)
        g = li - b
        gmax = g
        for sh in (1, 2, 4, 8, 16, 32):
            gmax = jnp.maximum(gmax, jnp.where(rowg >= sh, pltpu.roll(gmax, sh, 0), NEG))
        m_old = m_ref[...]
        top = jnp.maximum(m_old, gmax)
        m_t = b + top
        inter = jnp.exp(m_old - top)
        floor = jnp.exp(-m_t)
        bL = b[L - 1:L, :]
        w = bL - b + li
        m_new = jnp.maximum(bL + m_old, jnp.max(w, axis=0, keepdims=True))
        wk = jnp.exp(w - m_new)
        decay = jnp.exp(bL + m_old - m_new)
        m_ref[...] = m_new
        g_rows = _dot_nt(eye_h, g, precision=HIGHEST)

        xbuf[8:8 + L, 0:qkw] = q_ref[rows, :]
        xbuf[8:8 + L, qkw:2 * qkw] = k_ref[rows, :]
        acc = cw_ref[ML_CONV - 1:ML_CONV, :] * xbuf[8:8 + L, :]
        for j in range(1, ML_CONV):
            acc = acc + cw_ref[ML_CONV - 1 - j:ML_CONV - j, :] * xbuf[8 - j:8 - j + L, :]
        act = acc * _sigmoid_t(acc)
        xbuf[0:8, :] = xbuf[L:L + 8, :]

        q = [act[:, qk_cols(h)].astype(BF16) for h in heads]
        kf = [act[:, qkw + h * ML_DQK:qkw + (h + 1) * ML_DQK] * (ML_DQK ** -0.5) for h in heads]
        vaug = [jnp.concatenate([v_ref[rows, v_cols(h)].astype(BF16), ones_col], axis=1) for h in heads]
        cst = [c_ref[h] for h in heads]
        lhs = []
        for h in heads:
            dw = jnp.exp(jnp.where(tril, g_rows[h:h + 1, :] - col_of(top, h), NEG))
            sc = (_dot_nt(q[h], kf[h].astype(BF16)) * dw).astype(BF16)
            q_in = (q[h].astype(F32) * col_of(inter, h)).astype(BF16)
            lhs.append(jnp.concatenate([q_in, sc], axis=1))
        num_aug = [_dot(lhs[h], jnp.concatenate([cst[h].astype(BF16), vaug[h]], axis=0)) for h in heads]
        for h in heads:
            kw = (kf[h] * col_of(wk, h)).astype(BF16)
            upd = lax.dot_general(kw, vaug[h], (((0,), (0,)), ((), ())), preferred_element_type=F32)
            c_ref[h] = decay[:, h:h + 1] * cst[h] + upd
        for h in heads:
            num = num_aug[h][:, 0:ML_DV]
            den = num_aug[h][:, ML_DV:ML_DV + 1]
            r = 1.0 / jnp.maximum(jnp.abs(den), col_of(floor, h))
            f = r * lax.rsqrt(r * r * jnp.mean(num * num, axis=-1, keepdims=True) + EPS)
            o = o_ref[rows, v_cols(h)]
            z = z_ref[rows, v_cols(h)]
            gate = _sigmoid_t(o) * z * _sigmoid_t(z)
            y_ref[rows, v_cols(h)] = (num * f * hw_ref[h:h + 1, :] * gate).astype(BF16)
        return carry

    lax.fori_loop(0, lb // L, chunk, 0)


def _mlstm(u, gates, gate_blk, conv_w, ib, fb, head_w, B, S, lb=256):
    T = B * S
    nsb = S // lb
    qkw = ML_HEADS * ML_DQK
    vw = ML_HEADS * ML_DV
    rowmap = lambda col: (lambda b, s: (b * nsb + s, col))
    const = lambda b, s: (0, 0)
    return pl.pallas_call(
        functools.partial(_mlstm_kernel, lb=lb),
        grid=(B, nsb),
        in_specs=[
            pl.BlockSpec((lb, qkw), rowmap(0)),
            pl.BlockSpec((lb, qkw), rowmap(1)),
            pl.BlockSpec((lb, vw), rowmap(1)),
            pl.BlockSpec((lb, vw), rowmap(2)),
            pl.BlockSpec((lb, vw), rowmap(3)),
            pl.BlockSpec((lb, 2 * LANES), rowmap(gate_blk)),
            pl.BlockSpec((ML_CONV, 2 * qkw), const),
            pl.BlockSpec((1, LANES), const),
            pl.BlockSpec((1, LANES), const),
            pl.BlockSpec((ML_HEADS, ML_DV), const),
        ],
        out_specs=pl.BlockSpec((lb, vw), rowmap(0)),
        out_shape=jax.ShapeDtypeStruct((T, vw), BF16),
        scratch_shapes=[
            pltpu.VMEM((ML_CHUNK + 8, 2 * qkw), F32),
            pltpu.VMEM((ML_HEADS, ML_DQK, ML_DV + LANES), F32),
            pltpu.VMEM((1, LANES), F32),
        ],
        compiler_params=pltpu.CompilerParams(dimension_semantics=("arbitrary", "arbitrary"),
                                             vmem_limit_bytes=VMEM_LIMIT),
        name="mlstm",
    )(u, u, u, u, u, gates, conv_w, ib, fb, head_w)


def _rms_heads(x, w, scale=1.0):
    outs = []
    for h in range(x.shape[1] // NSA_HD):
        xh = x[:, h * NSA_HD:(h + 1) * NSA_HD]
        ms = jnp.mean(xh * xh, axis=-1, keepdims=True)
        outs.append(xh * lax.rsqrt(ms + EPS) * w * scale)
    return jnp.concatenate(outs, axis=1)


VT_ROWS = NSA_HD + 16


def _nsa_prep_kernel(q_ref, ks_ref, vs_ref, kw_ref, vw_ref, qw_ref, kn_ref,
                     qn_ref, ksa_ref, vst_ref, kwn_ref, vwt_ref, *, rb, nsb):
    qn_ref[...] = _rms_heads(q_ref[...], qw_ref[...], NSA_HD ** -0.5 * LOG2E).astype(BF16)
    ksn = _rms_heads(ks_ref[...], kn_ref[1:2, :]).astype(BF16)
    kwn_ref[...] = _rms_heads(kw_ref[...], kn_ref[2:3, :]).astype(BF16)
    t = (pl.program_id(0) % nsb) * rb + _iota((rb, NSA_HD), 0)
    onehot = (_iota((rb, NSA_HD), 1) == jnp.right_shift(t, SEL_SHIFT)).astype(BF16)
    tail = (_iota((VT_ROWS - NSA_HD, rb), 0) == 0).astype(BF16)
    for g in range(NSA_GROUPS):
        cols = slice(g * NSA_HD, (g + 1) * NSA_HD)
        ksa_ref[:, 2 * g * NSA_HD:(2 * g + 1) * NSA_HD] = ksn[:, cols]
        ksa_ref[:, (2 * g + 1) * NSA_HD:(2 * g + 2) * NSA_HD] = onehot
        vst_ref[g] = jnp.concatenate([vs_ref[:, cols].T.astype(BF16), tail], axis=0)
        vwt_ref[g] = jnp.concatenate([vw_ref[:, cols].T.astype(BF16), tail], axis=0)


def _nsa_prep(u, q_norm_w, k_norm_w, col_q, col_kv, B, S, rb=256):
    T = u.shape[0]
    nsb = S // rb
    qw = NSA_HEADS * NSA_HD
    kvw = NSA_GROUPS * NSA_HD
    cq = col_q // qw
    ck = col_kv // kvw
    assert S // SEL_LEN <= NSA_HD
    kv_spec = lambda idx: pl.BlockSpec((rb, kvw), lambda i: (i, ck + idx))
    out_k = pl.BlockSpec((rb, kvw), lambda i: (i, 0))
    out_ka = pl.BlockSpec((rb, 2 * kvw), lambda i: (i, 0))
    out_vt = pl.BlockSpec((None, NSA_GROUPS, VT_ROWS, rb), lambda i: (i // nsb, 0, 0, i % nsb))
    vt_shape = jax.ShapeDtypeStruct((B, NSA_GROUPS, VT_ROWS, S), BF16)
    return pl.pallas_call(
        functools.partial(_nsa_prep_kernel, rb=rb, nsb=nsb),
        grid=(T // rb,),
        in_specs=[
            pl.BlockSpec((rb, qw), lambda i: (i, cq)),
            kv_spec(2), kv_spec(3), kv_spec(4), kv_spec(5),
            pl.BlockSpec((1, NSA_HD), lambda i: (0, 0)),
            pl.BlockSpec((N_BRANCH, NSA_HD), lambda i: (0, 0)),
        ],
        out_specs=[pl.BlockSpec((rb, qw), lambda i: (i, 0)), out_ka, out_vt, out_k, out_vt],
        out_shape=[jax.ShapeDtypeStruct((T, qw), BF16), jax.ShapeDtypeStruct((T, 2 * kvw), BF16), vt_shape,
                   jax.ShapeDtypeStruct((T, kvw), BF16), vt_shape],
        compiler_params=pltpu.CompilerParams(dimension_semantics=("arbitrary",), vmem_limit_bytes=VMEM_LIMIT),
        name="nsa_prep",
    )(u, u, u, u, u, q_norm_w, k_norm_w)


def _compress_kernel(kc_ref, vc_ref, pek_ref, pev_ref, kw1_ref, kw2_ref, vw1_ref, vw2t_ref, kn_ref,
                     kcmp_ref, vcmp_ref, *, ns):
    def hidden(x_ref, pe_ref, w1_ref):
        hid = CMP_LEN // 2
        a = jnp.zeros((ns, w1_ref.shape[1]), F32)
        bsum = jnp.zeros((ns, w1_ref.shape[1]), F32)
        for l in range(hid):
            xl = x_ref[pl.ds(l, ns, stride=CMP_STRIDE), :]
            a = a + _dot((xl + pe_ref[l:l + 1, :]).astype(BF16), w1_ref[l * NSA_HD:(l + 1) * NSA_HD, :])
            bsum = bsum + _dot((xl + pe_ref[hid + l:hid + l + 1, :]).astype(BF16),
                               w1_ref[(hid + l) * NSA_HD:(hid + l + 1) * NSA_HD, :])
        pre = a + pltpu.roll(bsum, ns - 1, 0)
        return _silu(pre).astype(BF16)

    kc = _dot(hidden(kc_ref, pek_ref, kw1_ref), kw2_ref[...])
    ms = jnp.mean(kc * kc, axis=-1, keepdims=True)
    kcmp_ref[...] = (kc * lax.rsqrt(ms + EPS) * kn_ref[0:1, :]).astype(BF16)
    vcmp_ref[...] = _dot_nt(vw2t_ref[...], hidden(vc_ref, pev_ref, vw1_ref)).astype(BF16)


def _compress(u, pe_k, pe_v, kw1, kw2, vw1, vw2, k_norm_w, col_kv, B, S):
    ns = S // CMP_STRIDE
    ck = col_kv // NSA_HD
    hidden = kw1.shape[1]
    const = lambda b, g: (0, 0)
    return pl.pallas_call(
        functools.partial(_compress_kernel, ns=ns),
        grid=(B, NSA_GROUPS),
        in_specs=[
            pl.BlockSpec((S, NSA_HD), lambda b, g: (b, ck + g)),
            pl.BlockSpec((S, NSA_HD), lambda b, g: (b, ck + NSA_GROUPS + g)),
            pl.BlockSpec((CMP_LEN, NSA_HD), const),
            pl.BlockSpec((CMP_LEN, NSA_HD), const),
            pl.BlockSpec((CMP_LEN * NSA_HD, hidden), const),
            pl.BlockSpec((hidden, NSA_HD), const),
            pl.BlockSpec((CMP_LEN * NSA_HD, hidden), const),
            pl.BlockSpec((NSA_HD, hidden), const),
            pl.BlockSpec((N_BRANCH, NSA_HD), const),
        ],
        out_specs=[pl.BlockSpec((None, None, ns, NSA_HD), lambda b, g: (b, g, 0, 0)),
                   pl.BlockSpec((None, None, NSA_HD, ns), lambda b, g: (b, g, 0, 0))],
        out_shape=[jax.ShapeDtypeStruct((B, NSA_GROUPS, ns, NSA_HD), BF16),
                   jax.ShapeDtypeStruct((B, NSA_GROUPS, NSA_HD, ns), BF16)],
        compiler_params=pltpu.CompilerParams(dimension_semantics=("arbitrary", "arbitrary"),
                                             vmem_limit_bytes=VMEM_LIMIT),
        name="compress",
    )(u, u, pe_k, pe_v, kw1, kw2, vw1, vw2.T, k_norm_w)


TQC = 256
CMP_TILES = 2
NEAR = 32
NEAR_BACK = 16


def _stack_heads(q4):
    return jnp.concatenate([q4[:, h * NSA_HD:(h + 1) * NSA_HD] for h in range(NSA_HPG)], axis=0)


def _cmp_sel_kernel(tbl_ref, q_ref, kc_ref, vct_ref, oc_ref, sel_ref, pat_ref, *, ns, nsel):
    g = pl.program_id(1)
    i = pl.program_id(2)
    t0 = i * (CMP_TILES * TQC)
    c0 = i * (CMP_TILES * TQC // CMP_STRIDE)
    W = NSA_HPG * TQC
    assert (TQC - CMP_LEN) // CMP_STRIDE < NEAR - NEAR_BACK and NEAR_BACK * CMP_STRIDE >= REL_MAX_DIST + CMP_LEN

    @pl.when(i == 0)
    def _():
        r = _iota((TQC, LANES), 0)
        lane = _iota((TQC, LANES), 1)
        d = r - CMP_STRIDE * (jnp.bitwise_and(lane, NEAR - 1) - NEAR_BACK) - (CMP_LEN - 1)
        for h in range(NSA_HPG):
            val = jnp.where(d < 0, NEG, _rel_bias_pattern(d, tbl_ref, g * NSA_HPG + h))
            hi = val.astype(BF16).astype(F32)
            lo = jnp.where(d < 0, 0.0, val - hi)
            ext = jnp.where(lane < NEAR, hi, jnp.where(lane < 2 * NEAR, lo, jnp.where(lane == 2 * NEAR, NEG, 0.0)))
            pat_ref[h * TQC:(h + 1) * TQC, :] = ext.astype(BF16)

    tiles_q = range(CMP_TILES)
    qrows = lambda tq: slice(tq * TQC, (tq + 1) * TQC)
    lane = _iota((ns, LANES), 1)
    s = []
    for tq in tiles_q:
        q4 = q_ref[qrows(tq), :]
        q_aug = jnp.concatenate([jnp.concatenate([q4[:, h * NSA_HD:(h + 1) * NSA_HD],
                                                  pat_ref[h * TQC:(h + 1) * TQC, :]], axis=1)
                                 for h in range(NSA_HPG)], axis=0)
        rel = _iota((ns, LANES), 0) - (c0 + tq * (TQC // CMP_STRIDE) - NEAR_BACK)
        ext = ((lane < 2 * NEAR) & (rel == jnp.bitwise_and(lane, NEAR - 1))) | ((lane == 2 * NEAR) & (rel >= NEAR))
        k_aug = jnp.concatenate([kc_ref[...], jnp.where(ext, 1.0, 0.0).astype(BF16)], axis=1)
        s.append(_dot_nt(k_aug, q_aug))

    e, inv = [], []
    for tq in tiles_q:
        m = jnp.max(s[tq], axis=0, keepdims=True)
        e.append(jnp.exp2(s[tq] - m))
        l = jnp.sum(e[tq], axis=0, keepdims=True)
        t_abs = t0 + tq * TQC + jnp.bitwise_and(_iota((1, W), 1), TQC - 1)
        inv.append(jnp.where(t_abs >= CMP_LEN - 1, 1.0 / l, 0.0))

    ratio = SEL_LEN // CMP_STRIDE
    blk_n = _iota((nsel, ns), 0)
    tok_c = _iota((nsel, ns), 1)
    ov = ((tok_c >= ratio * blk_n - (CMP_LEN // CMP_STRIDE - 1)) & (tok_c <= ratio * blk_n + ratio - 1)).astype(F32)
    blk = _iota((nsel, TQC), 0)
    imp, cur = [], []
    for tq in tiles_q:
        o_t = _dot(vct_ref[...], e[tq].astype(BF16)) * inv[tq]
        for h in range(NSA_HPG):
            oc_ref[qrows(tq), h * NSA_HD:(h + 1) * NSA_HD] = o_t[:, h * TQC:(h + 1) * TQC].T
        p = e[tq] * inv[tq]
        ps = p[:, 0:TQC]
        for h in range(1, NSA_HPG):
            ps = ps + p[:, h * TQC:(h + 1) * TQC]
        v = _dot(ov, ps, precision=HIGHEST)
        c = jnp.right_shift(t0 + tq * TQC + _iota((nsel, TQC), 1), SEL_SHIFT)
        v = jnp.where((blk == 0) | (blk == c) | (blk == c - 1), FORCE_SCORE, v)
        imp.append(jnp.where(blk > c, NEG, v))
        cur.append(c)
    k_top = min(SEL_TOPK, nsel)

    def emit(tq, chosen):
        chosen = jnp.concatenate([chosen.astype(BF16), jnp.zeros((LANES - nsel, TQC), BF16)], axis=0)
        eye_q = (_iota((TQC, TQC), 0) == _iota((TQC, TQC), 1)).astype(BF16)
        sel_ref[qrows(tq), :] = ((_dot_nt(eye_q, chosen) - 1.0) * (-NEG)).astype(BF16)

    few = (i + 1) * CMP_TILES * (TQC // SEL_LEN) <= k_top

    @pl.when(few)
    def _():
        for tq in tiles_q:
            emit(tq, jnp.where(blk <= cur[tq], 1.0, 0.0))

    @pl.when(jnp.logical_not(few))
    def _():
        sub = _iota((8, TQC), 0)
        for tq in tiles_q:
            tiles = [imp[tq][8 * k:8 * k + 8, :] for k in range(nsel // 8)]
            counts = [jnp.zeros((8, TQC), jnp.int32) for _ in tiles]
            for jb in range(nsel):
                vj = imp[tq][jb:jb + 1, :]
                for k, tile in enumerate(tiles):
                    if 8 * k > jb:
                        ahead = vj >= tile
                    elif 8 * k + 7 < jb:
                        ahead = vj > tile
                    else:
                        ahead = (vj > tile) | ((vj == tile) & (sub + 8 * k > jb))
                    counts[k] = counts[k] + jnp.where(ahead, 1, 0)
            rank = jnp.concatenate(counts, axis=0)
            emit(tq, jnp.where(rank < k_top, 1.0, 0.0))


def _cmp_sel(rel_bias, qn, kcmp, vcmp_t, B, S):
    ns = S // CMP_STRIDE
    nsel = S // SEL_LEN
    rows = CMP_TILES * TQC
    nt = S // rows
    gw = NSA_HPG * NSA_HD
    assert nsel % 8 == 0 and nsel <= LANES
    return pl.pallas_call(
        functools.partial(_cmp_sel_kernel, ns=ns, nsel=nsel),
        grid=(B, NSA_GROUPS, nt),
        in_specs=[
            pl.BlockSpec(memory_space=pltpu.SMEM),
            pl.BlockSpec((rows, gw), lambda b, g, i: (b * nt + i, g)),
            pl.BlockSpec((None, None, ns, NSA_HD), lambda b, g, i: (b, g, 0, 0)),
            pl.BlockSpec((None, None, NSA_HD, ns), lambda b, g, i: (b, g, 0, 0)),
        ],
        out_specs=[
            pl.BlockSpec((rows, gw), lambda b, g, i: (b * nt + i, g)),
            pl.BlockSpec((None, None, rows, LANES), lambda b, g, i: (b, g, i, 0)),
        ],
        out_shape=[jax.ShapeDtypeStruct((B * S, NSA_HEADS * NSA_HD), F32),
                   jax.ShapeDtypeStruct((B, NSA_GROUPS, S, LANES), BF16)],
        scratch_shapes=[pltpu.VMEM((NSA_HPG * TQC, LANES), BF16)],
        compiler_params=pltpu.CompilerParams(dimension_semantics=("arbitrary", "arbitrary", "arbitrary"),
                                             vmem_limit_bytes=VMEM_LIMIT),
        name="cmp_sel",
    )(rel_bias, qn, kcmp, vcmp_t)


TA = 512
SUB = 128


def _attend_kernel(tbl_ref, q_ref, ksa_ref, vst_ref, kw_ref, vwt_ref, sel_ref, oc_ref, g_ref, z_ref, y_ref,
                   pd_ref, pp_ref, sa_ref, sb_ref, ms_ref, accs_ref, mw_ref, accw_ref, gt_ref, *, gate_col):
    g = pl.program_id(1)
    i = pl.program_id(2)
    nsub = TA // SUB

    krow = _iota((SUB, SUB), 0)
    qcol = _iota((SUB, SUB), 1)

    @pl.when(i == 0)
    def _():
        for h in range(NSA_HPG):
            head = g * NSA_HPG + h
            lanes = slice(h * SUB, (h + 1) * SUB)
            pd_ref[:, lanes] = jnp.where(krow <= qcol, _rel_bias_pattern(qcol - krow, tbl_ref, head), NEG)
            pp_ref[:, lanes] = _rel_bias_pattern(qcol - krow + SUB, tbl_ref, head)

    SLAB = NSA_HPG * SUB
    q4 = q_ref[...]
    sel = sel_ref[...]
    q_win = jnp.concatenate([q4[qb * SUB:(qb + 1) * SUB, h * NSA_HD:(h + 1) * NSA_HD]
                             for qb in range(nsub) for h in range(NSA_HPG)], axis=0)
    q_sel = jnp.concatenate([jnp.concatenate([q4[qb * SUB:(qb + 1) * SUB, h * NSA_HD:(h + 1) * NSA_HD],
                                              sel[qb * SUB:(qb + 1) * SUB, :]], axis=1)
                             for qb in range(nsub) for h in range(NSA_HPG)], axis=0)
    strict = jnp.concatenate([jnp.where(krow > qcol, 0.0, NEG)] * NSA_HPG, axis=1)

    def chunk(c):
        return pl.ds(pl.multiple_of(c * TA, TA), TA)

    def keys(c, lo, hi):
        return pl.ds(pl.multiple_of(c * TA + lo * SUB, SUB), (hi - lo) * SUB)

    def rows(kb):
        return slice(kb * SUB, (kb + 1) * SUB)

    def slab(qb):
        return slice(qb * SLAB, (qb + 1) * SLAB)

    def absorb(s_ref, vt_ref, c, m_ref, acc_ref, first=False, lo=0, hi=nsub, qb=None):
        lanes = slice(None) if qb is None else slab(qb)
        krows = slice(lo * SUB, hi * SUB)
        mx = jnp.max(s_ref[krows, lanes], axis=0, keepdims=True)
        if first:
            m_new = mx
        else:
            m_old = m_ref[:, lanes]
            m_new = jnp.maximum(m_old, mx)
        p = jnp.exp2(s_ref[krows, lanes] - m_new).astype(BF16)
        pv = _dot(vt_ref[:, keys(c, lo, hi)], p)
        if first:
            acc_ref[:, lanes] = pv
        else:
            acc_ref[:, lanes] = jnp.exp2(m_old - m_new) * acc_ref[:, lanes] + pv
        m_ref[:, lanes] = m_new

    def qk_sel(s_ref, c):
        s_ref[...] = _dot_nt(ksa_ref[chunk(c), :], q_sel)

    def diag(s_ref, k_ref, q_all, vt_ref, m_ref, acc_ref):
        for qb in range(nsub):
            s_ref[0:(qb + 1) * SUB, slab(qb)] = _dot_nt(k_ref[keys(i, 0, qb + 1), :], q_all[slab(qb), :])
            s_ref[rows(qb), slab(qb)] += pd_ref[...]
            if qb >= 1:
                s_ref[rows(qb - 1), slab(qb)] += pp_ref[...]
        for qb in range(nsub):
            absorb(s_ref, vt_ref, i, m_ref, acc_ref, first=True, lo=0, hi=qb + 1, qb=qb)

    def win_prev(s_ref):
        for qb in range(nsub):
            s_ref[qb * SUB:TA, slab(qb)] = _dot_nt(kw_ref[keys(i - 1, qb, nsub), :], q_win[slab(qb), :])
            s_ref[rows(qb), slab(qb)] += strict
        s_ref[rows(nsub - 1), slab(0)] += pp_ref[...]
        for qb in range(nsub):
            absorb(s_ref, vwt_ref, i - 1, mw_ref, accw_ref, lo=qb, hi=nsub, qb=qb)

    @pl.when(i == 0)
    def _():
        diag(sa_ref, ksa_ref, q_sel, vst_ref, ms_ref, accs_ref)
        diag(sb_ref, kw_ref, q_win, vwt_ref, mw_ref, accw_ref)

    @pl.when(i >= 1)
    def _():
        qk_sel(sb_ref, i - 1)
        sb_ref[rows(nsub - 1), slab(0)] += pp_ref[...]
        diag(sa_ref, ksa_ref, q_sel, vst_ref, ms_ref, accs_ref)
        absorb(sb_ref, vst_ref, i - 1, ms_ref, accs_ref)
        diag(sa_ref, kw_ref, q_win, vwt_ref, mw_ref, accw_ref)
        win_prev(sb_ref)

    nfar = jnp.maximum(i - 1, 0)
    odd = nfar % 2

    @pl.when(odd == 1)
    def _():
        qk_sel(sa_ref, 0)
        absorb(sa_ref, vst_ref, 0, ms_ref, accs_ref)

    npair = nfar // 2

    @pl.when(npair > 0)
    def _():
        qk_sel(sa_ref, odd)

    def pair(t, carry):
        c = odd + 2 * t
        qk_sel(sb_ref, c + 1)
        absorb(sa_ref, vst_ref, c, ms_ref, accs_ref)
        qk_sel(sa_ref, jnp.minimum(c + 2, nfar - 1))
        absorb(sb_ref, vst_ref, c + 1, ms_ref, accs_ref)
        return carry

    lax.fori_loop(0, npair, pair, 0)

    gs = _sigmoid_t(g_ref[:, 0:LANES])
    gt_ref[...] = gs.T
    inv_s = 1.0 / accs_ref[NSA_HD:NSA_HD + 1, :]
    inv_w = 1.0 / accw_ref[NSA_HD:NSA_HD + 1, :]
    lane = _iota((TA, LANES), 1)
    for h in range(NSA_HPG):
        base = gate_col + (g * NSA_HPG + h) * N_BRANCH
        cols = slice(h * NSA_HD, (h + 1) * NSA_HD)
        gate_c = jnp.sum(jnp.where(lane == base, gs, 0.0), axis=-1, keepdims=True)
        gs_row = gt_ref[pl.ds(base + 1, 1), :]
        gw_row = gt_ref[pl.ds(base + 2, 1), :]
        o_sw = []
        for qb in range(nsub):
            lanes = slice(qb * SLAB + h * SUB, qb * SLAB + (h + 1) * SUB)
            gate_s = gs_row[:, rows(qb)] * inv_s[:, lanes]
            gate_w = gw_row[:, rows(qb)] * inv_w[:, lanes]
            o_sw.append((gate_s * accs_ref[0:NSA_HD, lanes] + gate_w * accw_ref[0:NSA_HD, lanes]).T)
        o = gate_c * oc_ref[:, cols] + jnp.concatenate(o_sw, axis=0)
        z = z_ref[:, cols]
        y_ref[:, cols] = (o * z * _sigmoid_t(z)).astype(BF16)


def _attend(rel_bias, qn, ksa, vst, kwn, vwt, sel, o_c, gates, gate_blk, u, col_z, gate_col, B, S):
    nt = S // TA
    gw = NSA_HPG * NSA_HD
    W = NSA_HPG * TA
    cz = col_z // gw
    assert WIN == TA
    tile = lambda b, g, i: (b * nt + i, g)
    vt_spec = pl.BlockSpec((None, None, VT_ROWS, S), lambda b, g, i: (b, g, 0, 0))
    return pl.pallas_call(
        functools.partial(_attend_kernel, gate_col=gate_col),
        grid=(B, NSA_GROUPS, nt),
        in_specs=[
            pl.BlockSpec(memory_space=pltpu.SMEM),
            pl.BlockSpec((TA, gw), tile),
            pl.BlockSpec((S, 2 * NSA_HD), lambda b, g, i: (b, g)), vt_spec,
            pl.BlockSpec((S, NSA_HD), lambda b, g, i: (b, g)), vt_spec,
            pl.BlockSpec((None, None, TA, LANES), lambda b, g, i: (b, g, i, 0)),
            pl.BlockSpec((TA, gw), tile),
            pl.BlockSpec((TA, 2 * LANES), lambda b, g, i: (b * nt + i, gate_blk)),
            pl.BlockSpec((TA, gw), lambda b, g, i: (b * nt + i, cz + g)),
        ],
        out_specs=pl.BlockSpec((TA, gw), tile),
        out_shape=jax.ShapeDtypeStruct((B * S, NSA_HEADS * NSA_HD), BF16),
        scratch_shapes=[pltpu.VMEM((SUB, NSA_HPG * SUB), F32), pltpu.VMEM((SUB, NSA_HPG * SUB), F32),
                        pltpu.VMEM((TA, W), F32), pltpu.VMEM((TA, W), F32),
                        pltpu.VMEM((1, W), F32), pltpu.VMEM((VT_ROWS, W), F32),
                        pltpu.VMEM((1, W), F32), pltpu.VMEM((VT_ROWS, W), F32),
                        pltpu.VMEM((LANES, TA), F32)],
        compiler_params=pltpu.CompilerParams(dimension_semantics=("arbitrary", "arbitrary", "arbitrary"),
                                             vmem_limit_bytes=VMEM_LIMIT),
        name="attend",
    )(rel_bias, qn, ksa, vst, kwn, vwt, sel, o_c, gates, u)


def _outproj_kernel(yml_ref, yns_ref, w_ref, x_ref, o_ref, ob_ref, wb_ref):
    @pl.when(pl.program_id(1) == 0)
    def _():
        wb_ref[...] = w_ref[...].astype(BF16)

    half = yml_ref.shape[1]
    o = x_ref[...] + _dot(yml_ref[...], wb_ref[0:half, :]) + _dot(yns_ref[...], wb_ref[half:2 * half, :])
    o_ref[...] = o
    ob_ref[...] = o.astype(BF16)


def _outproj(y_ml, y_ns, w_out, x2, tm=256, tn=1024):
    T, D = x2.shape
    half = y_ml.shape[1]
    tile = pl.BlockSpec((tm, tn), lambda j, i: (i, j))
    return pl.pallas_call(
        _outproj_kernel,
        grid=(D // tn, T // tm),
        in_specs=[
            pl.BlockSpec((tm, half), lambda j, i: (i, 0)),
            pl.BlockSpec((tm, half), lambda j, i: (i, 0)),
            pl.BlockSpec((2 * half, tn), lambda j, i: (0, j)),
            tile,
        ],
        out_specs=[tile, tile],
        out_shape=[jax.ShapeDtypeStruct((T, D), F32), jax.ShapeDtypeStruct((T, D), BF16)],
        scratch_shapes=[pltpu.VMEM((2 * half, tn), BF16)],
        compiler_params=pltpu.CompilerParams(dimension_semantics=("arbitrary", "arbitrary"),
                                             vmem_limit_bytes=VMEM_LIMIT),
        name="outproj",
    )(y_ml, y_ns, w_out, x2)


def _ple_kernel(xb_ref, x_ref, p_ref, wg_ref, wp_ref, o_ref, wgb_ref, wpb_ref):
    @pl.when(pl.program_id(1) == 0)
    def _():
        wgb_ref[...] = wg_ref[...].astype(BF16)
        wpb_ref[...] = wp_ref[...].astype(BF16)

    gate = _sigmoid(_dot(xb_ref[...], wgb_ref[...]))
    emb = _dot(p_ref[...].astype(BF16), wpb_ref[...])
    o_ref[...] = x_ref[...] + gate * emb


def _ple(x1b, x1, p2, wg, wp, tm=256, tn=1024):
    T, D = x1.shape
    P = p2.shape[1]
    tile = pl.BlockSpec((tm, tn), lambda j, i: (i, j))
    return pl.pallas_call(
        _ple_kernel,
        grid=(D // tn, T // tm),
        in_specs=[
            pl.BlockSpec((tm, D), lambda j, i: (i, 0)),
            tile,
            pl.BlockSpec((tm, P), lambda j, i: (i, 0)),
            pl.BlockSpec((D, tn), lambda j, i: (0, j)),
            pl.BlockSpec((P, tn), lambda j, i: (0, j)),
        ],
        out_specs=tile,
        out_shape=jax.ShapeDtypeStruct((T, D), F32),
        scratch_shapes=[pltpu.VMEM((D, tn), BF16), pltpu.VMEM((P, tn), BF16)],
        compiler_params=pltpu.CompilerParams(dimension_semantics=("arbitrary", "arbitrary"),
                                             vmem_limit_bytes=VMEM_LIMIT),
        name="ple",
    )(x1b, x1, p2, wg, wp)


def _layer(x2, p2, norm_w, w_in, conv_w, i_bias, f_bias, head_norm_w, q_norm_w, k_norm_w,
           pe_k, pe_v, kw1, kw2, vw1, vw2, rel_bias, w_out, ple_proj, ple_gate, B, S):
    D = x2.shape[1]
    qkw = ML_HEADS * ML_DQK
    vw = ML_HEADS * ML_DV
    nq = NSA_HEADS * NSA_HD
    nkv = NSA_GROUPS * NSA_HD
    o_i = 2 * qkw + 3 * vw
    o_f = o_i + ML_HEADS
    o_nq = o_f + ML_HEADS
    o_g = o_nq + nq + 6 * nkv
    o_z = o_g + NSA_HEADS * N_BRANCH
    w_t = w_in.T
    ngate = NSA_HEADS * N_BRANCH
    zeros = lambda n: jnp.zeros((n, D), w_in.dtype)
    wg_t = jnp.concatenate([w_t[o_i:o_f], w_t[o_g:o_z], zeros(LANES - ML_HEADS - ngate),
                            w_t[o_f:o_nq], zeros(LANES - ML_HEADS)], axis=0)
    col_nq = o_i
    col_kv = col_nq + nq
    col_z = col_kv + 6 * nkv

    h, gates = _rmsnorm(x2, norm_w.reshape(1, D), wg_t)
    u = _inproj(h, w_t, [(0, o_i), (o_nq, o_g - o_nq), (o_z, nq)])

    pad_h = lambda v: jnp.concatenate([v, jnp.zeros((LANES - ML_HEADS,), v.dtype)]).reshape(1, LANES)
    y_ml = _mlstm(u, gates, 0, conv_w, pad_h(i_bias), pad_h(f_bias), head_norm_w, B, S)

    qn, ksa, vst, kwn, vwt = _nsa_prep(u, q_norm_w.reshape(1, NSA_HD), k_norm_w, col_nq, col_kv, B, S)
    kcmp, vcmp_t = _compress(u, pe_k, pe_v, kw1.astype(BF16), kw2.astype(BF16), vw1.astype(BF16),
                             vw2.astype(BF16), k_norm_w, col_kv, B, S)
    o_c, sel = _cmp_sel(rel_bias, qn, kcmp, vcmp_t, B, S)
    y_ns = _attend(rel_bias, qn, ksa, vst, kwn, vwt, sel, o_c, gates, 0, u, col_z, ML_HEADS, B, S)

    x1, x1b = _outproj(y_ml, y_ns, w_out, x2)
    return _ple(x1b, x1, p2, ple_gate, ple_proj)


def kernel(x, p, norm_w, w_in, ml_conv_w, ml_i_bias, ml_f_bias, ml_head_norm_w, nsa_q_norm_w, nsa_k_norm_w,
           cmp_pe_k, cmp_pe_v, cmp_k_w1, cmp_k_w2, cmp_v_w1, cmp_v_w2, rel_bias, w_out, ple_proj, ple_gate):
    B, S, D = x.shape
    assert S % max(WIN, 256) == 0 and S // SEL_LEN >= 1
    x2 = x.reshape(B * S, D)
    for layer in range(w_in.shape[0]):
        x2 = _layer(x2, p[layer].reshape(B * S, -1), norm_w[layer], w_in[layer], ml_conv_w[layer],
                    ml_i_bias[layer], ml_f_bias[layer], ml_head_norm_w[layer], nsa_q_norm_w[layer],
                    nsa_k_norm_w[layer], cmp_pe_k[layer], cmp_pe_v[layer], cmp_k_w1[layer], cmp_k_w2[layer],
                    cmp_v_w1[layer], cmp_v_w2[layer], rel_bias, w_out[layer], ple_proj[layer], ple_gate[layer],
                    B, S)
    return x2.reshape(B, S, D)
```

```python
import functools
import math

import numpy as np
import jax
import jax.numpy as jnp
from jax import lax
from jax.experimental import pallas as pl
from jax.experimental.pallas import tpu as pltpu

F32 = jnp.float32
BF16 = jnp.bfloat16
HIGHEST = lax.Precision.HIGHEST

ML_HEADS = 8
ML_DQK = 128
ML_DV = 256
ML_CHUNK = 64
ML_CONV = 4
NSA_HEADS = 16
NSA_HD = 128
NSA_GROUPS = 4
NSA_HPG = 4
N_BRANCH = 3
CMP_STRIDE = 16
CMP_LEN = 32
SEL_LEN = 64
SEL_SHIFT = 6
SEL_TOPK = 16
WIN = 512
REL_BUCKETS = 32
REL_MAX_DIST = 128
EPS = 1e-6
NEG = -1e30
FORCE_SCORE = 1e4
LOG2E = math.log2(math.e)

LANES = 128
VMEM_LIMIT = 56 * 1024 * 1024


def _bucket_thresholds():
    n = np.arange(0, 4 * REL_MAX_DIST, dtype=np.int64)
    max_exact = REL_BUCKETS // 2
    nf = np.maximum(n, 1).astype(np.float32)
    large = max_exact + (np.log(nf / np.float32(max_exact)) / np.float32(math.log(REL_MAX_DIST / max_exact))
                         * np.float32(REL_BUCKETS - max_exact)).astype(np.int32)
    large = np.minimum(large, REL_BUCKETS - 1)
    bucket = np.where(n < max_exact, n, large)
    assert np.all(np.diff(bucket) >= 0)
    thr = [int(np.argmax(bucket >= b)) for b in range(REL_BUCKETS)]
    assert thr[REL_BUCKETS - 1] <= REL_MAX_DIST
    return thr


BUCKET_THR = _bucket_thresholds()


def _dot(a, b, precision=None):
    return jnp.dot(a, b, preferred_element_type=F32, precision=precision)


def _dot_nt(a, b, precision=None):
    return lax.dot_general(a, b, (((1,), (1,)), ((), ())), preferred_element_type=F32, precision=precision)


def _sigmoid(x):
    return 1.0 / (1.0 + jnp.exp(-x))


def _silu(x):
    return x * _sigmoid(x)


def _sigmoid_t(x):
    return 0.5 * jnp.tanh(0.5 * x) + 0.5


def _iota(shape, dim):
    return lax.broadcasted_iota(jnp.int32, shape, dim)


def _rel_bias_pattern(dist, tbl_ref, head):
    val = jnp.full(dist.shape, tbl_ref[0, head], F32)
    for b in range(1, REL_BUCKETS):
        val = jnp.where(dist >= BUCKET_THR[b], tbl_ref[b, head], val)
    return (val - tbl_ref[REL_BUCKETS - 1, head]) * LOG2E


def _rmsnorm_kernel(x_ref, nw_ref, wg_ref, h_ref, g_ref, wgb_ref):
    @pl.when(pl.program_id(0) == 0)
    def _():
        wgb_ref[...] = wg_ref[...].astype(BF16)

    x = x_ref[...]
    ms = jnp.mean(x * x, axis=-1, keepdims=True)
    h = (x * lax.rsqrt(ms + EPS) * nw_ref[...]).astype(BF16)
    h_ref[...] = h
    g_ref[...] = _dot_nt(h, wgb_ref[...])


def _rmsnorm(x2, norm_w, wg_t, rb=256):
    T, D = x2.shape
    NG = wg_t.shape[0]
    return pl.pallas_call(
        _rmsnorm_kernel,
        grid=(T // rb,),
        in_specs=[pl.BlockSpec((rb, D), lambda i: (i, 0)), pl.BlockSpec((1, D), lambda i: (0, 0)),
                  pl.BlockSpec((NG, D), lambda i: (0, 0))],
        out_specs=[pl.BlockSpec((rb, D), lambda i: (i, 0)), pl.BlockSpec((rb, NG), lambda i: (i, 0))],
        out_shape=[jax.ShapeDtypeStruct((T, D), BF16), jax.ShapeDtypeStruct((T, NG), F32)],
        scratch_shapes=[pltpu.VMEM((NG, D), BF16)],
        compiler_params=pltpu.CompilerParams(dimension_semantics=("arbitrary",), vmem_limit_bytes=VMEM_LIMIT),
        name="rmsnorm",
    )(x2, norm_w, wg_t)


def _inproj_kernel(h_ref, wt_ref, u_ref, wb_ref):
    @pl.when(pl.program_id(1) == 0)
    def _():
        wb_ref[...] = wt_ref[...].astype(BF16)

    u_ref[...] = _dot_nt(h_ref[...], wb_ref[...])


def _piece_blocks(pieces, tn):
    starts = []
    for start, length in pieces:
        assert length % tn == 0 and start % 8 == 0
        starts += [start + k * tn for k in range(length // tn)]
    bounds = [(b, s) for b, s in enumerate(starts) if b == 0 or s != starts[b - 1] + tn]

    def w_rows(j, i):
        row8 = (bounds[0][1] + j * tn) // 8
        for b, s in bounds[1:]:
            row8 = jnp.where(j >= b, (s + (j - b) * tn) // 8, row8)
        return row8 * 8, 0

    return len(starts), w_rows


def _inproj(h, w_t, pieces, tm=512, tn=1024):
    T, D = h.shape
    nblk, w_rows = _piece_blocks(pieces, tn)
    return pl.pallas_call(
        _inproj_kernel,
        grid=(nblk, T // tm),
        in_specs=[
            pl.BlockSpec((tm, D), lambda j, i: (i, 0)),
            pl.BlockSpec((pl.Element(tn), pl.Element(D)), w_rows),
        ],
        out_specs=pl.BlockSpec((tm, tn), lambda j, i: (i, j)),
        out_shape=jax.ShapeDtypeStruct((T, nblk * tn), F32),
        scratch_shapes=[pltpu.VMEM((tn, D), BF16)],
        compiler_params=pltpu.CompilerParams(dimension_semantics=("arbitrary", "arbitrary"),
                                             vmem_limit_bytes=VMEM_LIMIT),
        name="inproj",
    )(h, w_t)


def _mlstm_kernel(q_ref, k_ref, v_ref, o_ref, z_ref, g_ref, cw_ref, ib_ref, fb_ref, hw_ref, y_ref,
                  xbuf, c_ref, m_ref, *, lb):
    L = ML_CHUNK
    qkw = ML_HEADS * ML_DQK

    @pl.when(pl.program_id(1) == 0)
    def _():
        xbuf[0:8, :] = jnp.zeros((8, 2 * qkw), F32)
        c_ref[...] = jnp.zeros(c_ref.shape, F32)
        m_ref[...] = jnp.zeros(m_ref.shape, F32)

    row = _iota((L, L), 0)
    col = _iota((L, L), 1)
    tril = col <= row
    rowg = _iota((L, LANES), 0)
    eye_h = (_iota((8, LANES), 0) == _iota((8, LANES), 1)).astype(F32)
    ones_col = (_iota((L, LANES), 1) == 0).astype(BF16)
    heads = range(ML_HEADS)

    def chunk(c, carry):
        rows = pl.ds(pl.multiple_of(c * L, L), L)
        qk_cols = lambda h: slice(h * ML_DQK, (h + 1) * ML_DQK)
        v_cols = lambda h: slice(h * ML_DV, (h + 1) * ML_DV)
        col_of = lambda x, h: x[:, h:h + 1]

        G = g_ref[rows, :]
        li = G[:, 0:LANES] + ib_ref[...]
        fp = G[:, LANES:2 * LANES] + fb_ref[...]
        lf = jnp.minimum(fp, 0.0) - jnp.log(1.0 + jnp.exp(-jnp.abs(fp)))
        b = lf
        for sh in (1, 2, 4, 8, 16, 32):
            b = b + jnp.where(rowg >= sh, pltpu.roll(b, sh, 0), 0.0)
        g = li - b
        gmax = g
        for sh in (1, 2, 4, 8, 16, 32):
            gmax = jnp.maximum(gmax, jnp.where(rowg >= sh, pltpu.roll(gmax, sh, 0), NEG))
        m_old = m_ref[...]
        top = jnp.maximum(m_old, gmax)
        m_t = b + top
        inter = jnp.exp(m_old - top)
        floor = jnp.exp(-m_t)
        bL = b[L - 1:L, :]
        w = bL - b + li
        m_new = jnp.maximum(bL + m_old, jnp.max(w, axis=0, keepdims=True))
        wk = jnp.exp(w - m_new)
        decay = jnp.exp(bL + m_old - m_new)
        m_ref[...] = m_new
        g_rows = _dot_nt(eye_h, g, precision=HIGHEST)

        xbuf[8:8 + L, 0:qkw] = q_ref[rows, :]
        xbuf[8:8 + L, qkw:2 * qkw] = k_ref[rows, :]
        acc = cw_ref[ML_CONV - 1:ML_CONV, :] * xbuf[8:8 + L, :]
        for j in range(1, ML_CONV):
            acc = acc + cw_ref[ML_CONV - 1 - j:ML_CONV - j, :] * xbuf[8 - j:8 - j + L, :]
        act = acc * _sigmoid_t(acc)
        xbuf[0:8, :] = xbuf[L:L + 8, :]

        q = [act[:, qk_cols(h)].astype(BF16) for h in heads]
        kf = [act[:, qkw + h * ML_DQK:qkw + (h + 1) * ML_DQK] * (ML_DQK ** -0.5) for h in heads]
        vaug = [jnp.concatenate([v_ref[rows, v_cols(h)].astype(BF16), ones_col], axis=1) for h in heads]
        cst = [c_ref[h] for h in heads]
        lhs = []
        for h in heads:
            dw = jnp.exp(jnp.where(tril, g_rows[h:h + 1, :] - col_of(top, h), NEG))
            sc = (_dot_nt(q[h], kf[h].astype(BF16)) * dw).astype(BF16)
            q_in = (q[h].astype(F32) * col_of(inter, h)).astype(BF16)
            lhs.append(jnp.concatenate([q_in, sc], axis=1))
        num_aug = [_dot(lhs[h], jnp.concatenate([cst[h].astype(BF16), vaug[h]], axis=0)) for h in heads]
        for h in heads:
            kw = (kf[h] * col_of(wk, h)).astype(BF16)
            upd = lax.dot_general(kw, vaug[h], (((0,), (0,)), ((), ())), preferred_element_type=F32)
            c_ref[h] = decay[:, h:h + 1] * cst[h] + upd
        for h in heads:
            num = num_aug[h][:, 0:ML_DV]
            den = num_aug[h][:, ML_DV:ML_DV + 1]
            r = 1.0 / jnp.maximum(jnp.abs(den), col_of(floor, h))
            f = r * lax.rsqrt(r * r * jnp.mean(num * num, axis=-1, keepdims=True) + EPS)
            o = o_ref[rows, v_cols(h)]
            z = z_ref[rows, v_cols(h)]
            gate = _sigmoid_t(o) * z * _sigmoid_t(z)
            y_ref[rows, v_cols(h)] = (num * f * hw_ref[h:h + 1, :] * gate).astype(BF16)
        return carry

    lax.fori_loop(0, lb // L, chunk, 0)


def _mlstm(u, gates, gate_blk, conv_w, ib, fb, head_w, B, S, lb=512):
    T = B * S
    nsb = S // lb
    qkw = ML_HEADS * ML_DQK
    vw = ML_HEADS * ML_DV
    rowmap = lambda col: (lambda b, s: (b * nsb + s, col))
    const = lambda b, s: (0, 0)
    return pl.pallas_call(
        functools.partial(_mlstm_kernel, lb=lb),
        grid=(B, nsb),
        in_specs=[
            pl.BlockSpec((lb, qkw), rowmap(0)),
            pl.BlockSpec((lb, qkw), rowmap(1)),
            pl.BlockSpec((lb, vw), rowmap(1)),
            pl.BlockSpec((lb, vw), rowmap(2)),
            pl.BlockSpec((lb, vw), rowmap(3)),
            pl.BlockSpec((lb, 2 * LANES), rowmap(gate_blk)),
            pl.BlockSpec((ML_CONV, 2 * qkw), const),
            pl.BlockSpec((1, LANES), const),
            pl.BlockSpec((1, LANES), const),
            pl.BlockSpec((ML_HEADS, ML_DV), const),
        ],
        out_specs=pl.BlockSpec((lb, vw), rowmap(0)),
        out_shape=jax.ShapeDtypeStruct((T, vw), BF16),
        scratch_shapes=[
            pltpu.VMEM((ML_CHUNK + 8, 2 * qkw), F32),
            pltpu.VMEM((ML_HEADS, ML_DQK, ML_DV + LANES), F32),
            pltpu.VMEM((1, LANES), F32),
        ],
        compiler_params=pltpu.CompilerParams(dimension_semantics=("arbitrary", "arbitrary"),
                                             vmem_limit_bytes=VMEM_LIMIT),
        name="mlstm",
    )(u, u, u, u, u, gates, conv_w, ib, fb, head_w)


def _rms_heads(x, w, scale=1.0):
    outs = []
    for h in range(x.shape[1] // NSA_HD):
        xh = x[:, h * NSA_HD:(h + 1) * NSA_HD]
        ms = jnp.mean(xh * xh, axis=-1, keepdims=True)
        outs.append(xh * lax.rsqrt(ms + EPS) * w * scale)
    return jnp.concatenate(outs, axis=1)


VT_ROWS = NSA_HD + 16


def _nsa_prep_kernel(q_ref, ks_ref, vs_ref, kw_ref, vw_ref, qw_ref, kn_ref,
                     qn_ref, ksa_ref, vst_ref, kwn_ref, vwt_ref, *, rb, nsb):
    qn_ref[...] = _rms_heads(q_ref[...], qw_ref[...], NSA_HD ** -0.5 * LOG2E).astype(BF16)
    ksn = _rms_heads(ks_ref[...], kn_ref[1:2, :]).astype(BF16)
    kwn_ref[...] = _rms_heads(kw_ref[...], kn_ref[2:3, :]).astype(BF16)
    t = (pl.program_id(0) % nsb) * rb + _iota((rb, NSA_HD), 0)
    onehot = (_iota((rb, NSA_HD), 1) == jnp.right_shift(t, SEL_SHIFT)).astype(BF16)
    tail = (_iota((VT_ROWS - NSA_HD, rb), 0) == 0).astype(BF16)
    for g in range(NSA_GROUPS):
        cols = slice(g * NSA_HD, (g + 1) * NSA_HD)
        ksa_ref[:, 2 * g * NSA_HD:(2 * g + 1) * NSA_HD] = ksn[:, cols]
        ksa_ref[:, (2 * g + 1) * NSA_HD:(2 * g + 2) * NSA_HD] = onehot
        vst_ref[g] = jnp.concatenate([vs_ref[:, cols].T.astype(BF16), tail], axis=0)
        vwt_ref[g] = jnp.concatenate([vw_ref[:, cols].T.astype(BF16), tail], axis=0)


def _nsa_prep(u, q_norm_w, k_norm_w, col_q, col_kv, B, S, rb=512):
    T = u.shape[0]
    nsb = S // rb
    qw = NSA_HEADS * NSA_HD
    kvw = NSA_GROUPS * NSA_HD
    cq = col_q // qw
    ck = col_kv // kvw
    assert S // SEL_LEN <= NSA_HD
    kv_spec = lambda idx: pl.BlockSpec((rb, kvw), lambda i: (i, ck + idx))
    out_k = pl.BlockSpec((rb, kvw), lambda i: (i, 0))
    out_ka = pl.BlockSpec((rb, 2 * kvw), lambda i: (i, 0))
    out_vt = pl.BlockSpec((None, NSA_GROUPS, VT_ROWS, rb), lambda i: (i // nsb, 0, 0, i % nsb))
    vt_shape = jax.ShapeDtypeStruct((B, NSA_GROUPS, VT_ROWS, S), BF16)
    return pl.pallas_call(
        functools.partial(_nsa_prep_kernel, rb=rb, nsb=nsb),
        grid=(T // rb,),
        in_specs=[
            pl.BlockSpec((rb, qw), lambda i: (i, cq)),
            kv_spec(2), kv_spec(3), kv_spec(4), kv_spec(5),
            pl.BlockSpec((1, NSA_HD), lambda i: (0, 0)),
            pl.BlockSpec((N_BRANCH, NSA_HD), lambda i: (0, 0)),
        ],
        out_specs=[pl.BlockSpec((rb, qw), lambda i: (i, 0)), out_ka, out_vt, out_k, out_vt],
        out_shape=[jax.ShapeDtypeStruct((T, qw), BF16), jax.ShapeDtypeStruct((T, 2 * kvw), BF16), vt_shape,
                   jax.ShapeDtypeStruct((T, kvw), BF16), vt_shape],
        compiler_params=pltpu.CompilerParams(dimension_semantics=("arbitrary",), vmem_limit_bytes=VMEM_LIMIT),
        name="nsa_prep",
    )(u, u, u, u, u, q_norm_w, k_norm_w)


def _compress_kernel(kc_ref, vc_ref, pek_ref, pev_ref, kw1_ref, kw2_ref, vw1_ref, vw2t_ref, kn_ref,
                     kcmp_ref, vcmp_ref, *, ns):
    def hidden(x_ref, pe_ref, w1_ref):
        hid = CMP_LEN // 2
        a = jnp.zeros((ns, w1_ref.shape[1]), F32)
        bsum = jnp.zeros((ns, w1_ref.shape[1]), F32)
        for l in range(0, hid, 2):
            xl = [x_ref[pl.ds(l + t, ns, stride=CMP_STRIDE), :] for t in range(2)]
            first = jnp.concatenate([(xl[t] + pe_ref[l + t:l + t + 1, :]).astype(BF16) for t in range(2)], axis=1)
            second = jnp.concatenate([(xl[t] + pe_ref[hid + l + t:hid + l + t + 1, :]).astype(BF16)
                                      for t in range(2)], axis=1)
            a = a + _dot(first, w1_ref[l * NSA_HD:(l + 2) * NSA_HD, :])
            bsum = bsum + _dot(second, w1_ref[(hid + l) * NSA_HD:(hid + l + 2) * NSA_HD, :])
        pre = a + pltpu.roll(bsum, ns - 1, 0)
        return _silu(pre).astype(BF16)

    kc = _dot(hidden(kc_ref, pek_ref, kw1_ref), kw2_ref[...])
    ms = jnp.mean(kc * kc, axis=-1, keepdims=True)
    kcmp_ref[...] = (kc * lax.rsqrt(ms + EPS) * kn_ref[0:1, :]).astype(BF16)
    vcmp_ref[...] = _dot_nt(vw2t_ref[...], hidden(vc_ref, pev_ref, vw1_ref)).astype(BF16)


def _compress(u, pe_k, pe_v, kw1, kw2, vw1, vw2, k_norm_w, col_kv, B, S):
    ns = S // CMP_STRIDE
    ck = col_kv // NSA_HD
    hidden = kw1.shape[1]
    const = lambda b, g: (0, 0)
    return pl.pallas_call(
        functools.partial(_compress_kernel, ns=ns),
        grid=(B, NSA_GROUPS),
        in_specs=[
            pl.BlockSpec((S, NSA_HD), lambda b, g: (b, ck + g)),
            pl.BlockSpec((S, NSA_HD), lambda b, g: (b, ck + NSA_GROUPS + g)),
            pl.BlockSpec((CMP_LEN, NSA_HD), const),
            pl.BlockSpec((CMP_LEN, NSA_HD), const),
            pl.BlockSpec((CMP_LEN * NSA_HD, hidden), const),
            pl.BlockSpec((hidden, NSA_HD), const),
            pl.BlockSpec((CMP_LEN * NSA_HD, hidden), const),
            pl.BlockSpec((NSA_HD, hidden), const),
            pl.BlockSpec((N_BRANCH, NSA_HD), const),
        ],
        out_specs=[pl.BlockSpec((None, None, ns, NSA_HD), lambda b, g: (b, g, 0, 0)),
                   pl.BlockSpec((None, None, NSA_HD, ns), lambda b, g: (b, g, 0, 0))],
        out_shape=[jax.ShapeDtypeStruct((B, NSA_GROUPS, ns, NSA_HD), BF16),
                   jax.ShapeDtypeStruct((B, NSA_GROUPS, NSA_HD, ns), BF16)],
        compiler_params=pltpu.CompilerParams(dimension_semantics=("arbitrary", "arbitrary"),
                                             vmem_limit_bytes=VMEM_LIMIT),
        name="compress",
    )(u, u, pe_k, pe_v, kw1, kw2, vw1, vw2.T, k_norm_w)


TQC = 256
CMP_TILES = 4
NEAR = 32
NEAR_BACK = 16


def _stack_heads(q4):
    return jnp.concatenate([q4[:, h * NSA_HD:(h + 1) * NSA_HD] for h in range(NSA_HPG)], axis=0)


def _cmp_sel_kernel(tbl_ref, q_ref, kc_ref, vct_ref, oc_ref, sel_ref, pat_ref, *, ns, nsel):
    g = pl.program_id(1)
    i = pl.program_id(2)
    t0 = i * (CMP_TILES * TQC)
    c0 = i * (CMP_TILES * TQC // CMP_STRIDE)
    W = NSA_HPG * TQC
    assert (TQC - CMP_LEN) // CMP_STRIDE < NEAR - NEAR_BACK and NEAR_BACK * CMP_STRIDE >= REL_MAX_DIST + CMP_LEN

    @pl.when(i == 0)
    def _():
        r = _iota((TQC, LANES), 0)
        lane = _iota((TQC, LANES), 1)
        d = r - CMP_STRIDE * (jnp.bitwise_and(lane, NEAR - 1) - NEAR_BACK) - (CMP_LEN - 1)
        for h in range(NSA_HPG):
            val = jnp.where(d < 0, NEG, _rel_bias_pattern(d, tbl_ref, g * NSA_HPG + h))
            hi = val.astype(BF16).astype(F32)
            lo = jnp.where(d < 0, 0.0, val - hi)
            ext = jnp.where(lane < NEAR, hi, jnp.where(lane < 2 * NEAR, lo, jnp.where(lane == 2 * NEAR, NEG, 0.0)))
            pat_ref[h * TQC:(h + 1) * TQC, :] = ext.astype(BF16)

    tiles_q = range(CMP_TILES)
    qrows = lambda tq: slice(tq * TQC, (tq + 1) * TQC)
    lane = _iota((ns, LANES), 1)
    s = []
    for tq in tiles_q:
        q4 = q_ref[qrows(tq), :]
        q_aug = jnp.concatenate([jnp.concatenate([q4[:, h * NSA_HD:(h + 1) * NSA_HD],
                                                  pat_ref[h * TQC:(h + 1) * TQC, :]], axis=1)
                                 for h in range(NSA_HPG)], axis=0)
        rel = _iota((ns, LANES), 0) - (c0 + tq * (TQC // CMP_STRIDE) - NEAR_BACK)
        ext = ((lane < 2 * NEAR) & (rel == jnp.bitwise_and(lane, NEAR - 1))) | ((lane == 2 * NEAR) & (rel >= NEAR))
        k_aug = jnp.concatenate([kc_ref[...], jnp.where(ext, 1.0, 0.0).astype(BF16)], axis=1)
        s.append(_dot_nt(k_aug, q_aug))

    e, inv = [], []
    for tq in tiles_q:
        m = jnp.max(s[tq], axis=0, keepdims=True)
        e.append(jnp.exp2(s[tq] - m))
        l = jnp.sum(e[tq], axis=0, keepdims=True)
        t_abs = t0 + tq * TQC + jnp.bitwise_and(_iota((1, W), 1), TQC - 1)
        inv.append(jnp.where(t_abs >= CMP_LEN - 1, 1.0 / l, 0.0))

    ratio = SEL_LEN // CMP_STRIDE
    blk_n = _iota((nsel, ns), 0)
    tok_c = _iota((nsel, ns), 1)
    ov = ((tok_c >= ratio * blk_n - (CMP_LEN // CMP_STRIDE - 1)) & (tok_c <= ratio * blk_n + ratio - 1)).astype(F32)
    blk = _iota((nsel, TQC), 0)
    imp, cur = [], []
    for tq in tiles_q:
        o_t = _dot(vct_ref[...], e[tq].astype(BF16)) * inv[tq]
        for h in range(NSA_HPG):
            oc_ref[qrows(tq), h * NSA_HD:(h + 1) * NSA_HD] = o_t[:, h * TQC:(h + 1) * TQC].T
        p = e[tq] * inv[tq]
        ps = p[:, 0:TQC]
        for h in range(1, NSA_HPG):
            ps = ps + p[:, h * TQC:(h + 1) * TQC]
        v = _dot(ov, ps, precision=HIGHEST)
        c = jnp.right_shift(t0 + tq * TQC + _iota((nsel, TQC), 1), SEL_SHIFT)
        v = jnp.where((blk == 0) | (blk == c) | (blk == c - 1), FORCE_SCORE, v)
        imp.append(jnp.where(blk > c, NEG, v))
        cur.append(c)
    k_top = min(SEL_TOPK, nsel)

    def emit(tq, chosen):
        chosen = jnp.concatenate([chosen.astype(BF16), jnp.zeros((LANES - nsel, TQC), BF16)], axis=0)
        eye_q = (_iota((TQC, TQC), 0) == _iota((TQC, TQC), 1)).astype(BF16)
        sel_ref[qrows(tq), :] = ((_dot_nt(eye_q, chosen) - 1.0) * (-NEG)).astype(BF16)

    few = (i + 1) * CMP_TILES * (TQC // SEL_LEN) <= k_top

    @pl.when(few)
    def _():
        for tq in tiles_q:
            emit(tq, jnp.where(blk <= cur[tq], 1.0, 0.0))

    @pl.when(jnp.logical_not(few))
    def _():
        sub = _iota((8, TQC), 0)
        for tq in tiles_q:
            tiles = [imp[tq][8 * k:8 * k + 8, :] for k in range(nsel // 8)]
            counts = [jnp.zeros((8, TQC), jnp.int32) for _ in tiles]
            for jb in range(nsel):
                vj = imp[tq][jb:jb + 1, :]
                for k, tile in enumerate(tiles):
                    if 8 * k > jb:
                        ahead = vj >= tile
                    elif 8 * k + 7 < jb:
                        ahead = vj > tile
                    else:
                        ahead = (vj > tile) | ((vj == tile) & (sub + 8 * k > jb))
                    counts[k] = counts[k] + jnp.where(ahead, 1, 0)
            rank = jnp.concatenate(counts, axis=0)
            emit(tq, jnp.where(rank < k_top, 1.0, 0.0))


def _cmp_sel(rel_bias, qn, kcmp, vcmp_t, B, S):
    ns = S // CMP_STRIDE
    nsel = S // SEL_LEN
    rows = CMP_TILES * TQC
    nt = S // rows
    gw = NSA_HPG * NSA_HD
    assert nsel % 8 == 0 and nsel <= LANES
    return pl.pallas_call(
        functools.partial(_cmp_sel_kernel, ns=ns, nsel=nsel),
        grid=(B, NSA_GROUPS, nt),
        in_specs=[
            pl.BlockSpec(memory_space=pltpu.SMEM),
            pl.BlockSpec((rows, gw), lambda b, g, i: (b * nt + i, g)),
            pl.BlockSpec((None, None, ns, NSA_HD), lambda b, g, i: (b, g, 0, 0)),
            pl.BlockSpec((None, None, NSA_HD, ns), lambda b, g, i: (b, g, 0, 0)),
        ],
        out_specs=[
            pl.BlockSpec((rows, gw), lambda b, g, i: (b * nt + i, g)),
            pl.BlockSpec((None, None, rows, LANES), lambda b, g, i: (b, g, i, 0)),
        ],
        out_shape=[jax.ShapeDtypeStruct((B * S, NSA_HEADS * NSA_HD), F32),
                   jax.ShapeDtypeStruct((B, NSA_GROUPS, S, LANES), BF16)],
        scratch_shapes=[pltpu.VMEM((NSA_HPG * TQC, LANES), BF16)],
        compiler_params=pltpu.CompilerParams(dimension_semantics=("arbitrary", "arbitrary", "arbitrary"),
                                             vmem_limit_bytes=VMEM_LIMIT),
        name="cmp_sel",
    )(rel_bias, qn, kcmp, vcmp_t)


TA = 512
SUB = 128


def _attend_kernel(tbl_ref, q_ref, ksa_ref, vst_ref, kw_ref, vwt_ref, sel_ref, oc_ref, g_ref, z_ref, y_ref,
                   pd_ref, pp_ref, sa_ref, sb_ref, ms_ref, accs_ref, mw_ref, accw_ref, gt_ref, *, gate_col):
    g = pl.program_id(1)
    i = pl.program_id(2)
    nsub = TA // SUB

    krow = _iota((SUB, SUB), 0)
    qcol = _iota((SUB, SUB), 1)

    @pl.when(i == 0)
    def _():
        for h in range(NSA_HPG):
            head = g * NSA_HPG + h
            lanes = slice(h * SUB, (h + 1) * SUB)
            pd_ref[:, lanes] = jnp.where(krow <= qcol, _rel_bias_pattern(qcol - krow, tbl_ref, head), NEG)
            pp_ref[:, lanes] = _rel_bias_pattern(qcol - krow + SUB, tbl_ref, head)

    SLAB = NSA_HPG * SUB
    q4 = q_ref[...]
    sel = sel_ref[...]
    q_win = jnp.concatenate([q4[qb * SUB:(qb + 1) * SUB, h * NSA_HD:(h + 1) * NSA_HD]
                             for qb in range(nsub) for h in range(NSA_HPG)], axis=0)
    q_sel = jnp.concatenate([jnp.concatenate([q4[qb * SUB:(qb + 1) * SUB, h * NSA_HD:(h + 1) * NSA_HD],
                                              sel[qb * SUB:(qb + 1) * SUB, :]], axis=1)
                             for qb in range(nsub) for h in range(NSA_HPG)], axis=0)
    strict = jnp.concatenate([jnp.where(krow > qcol, 0.0, NEG)] * NSA_HPG, axis=1)

    def chunk(c):
        return pl.ds(pl.multiple_of(c * TA, TA), TA)

    def keys(c, lo, hi):
        return pl.ds(pl.multiple_of(c * TA + lo * SUB, SUB), (hi - lo) * SUB)

    def rows(kb):
        return slice(kb * SUB, (kb + 1) * SUB)

    def slab(qb):
        return slice(qb * SLAB, (qb + 1) * SLAB)

    def absorb(s_ref, vt_ref, c, m_ref, acc_ref, first=False, lo=0, hi=nsub, qb=None):
        lanes = slice(None) if qb is None else slab(qb)
        krows = slice(lo * SUB, hi * SUB)
        mx = jnp.max(s_ref[krows, lanes], axis=0, keepdims=True)
        if first:
            m_new = mx
        else:
            m_old = m_ref[:, lanes]
            m_new = jnp.maximum(m_old, mx)
        p = jnp.exp2(s_ref[krows, lanes] - m_new).astype(BF16)
        pv = _dot(vt_ref[:, keys(c, lo, hi)], p)
        if first:
            acc_ref[:, lanes] = pv
        else:
            acc_ref[:, lanes] = jnp.exp2(m_old - m_new) * acc_ref[:, lanes] + pv
        m_ref[:, lanes] = m_new

    def qk_sel(s_ref, c):
        s_ref[...] = _dot_nt(ksa_ref[chunk(c), :], q_sel)

    def diag(s_ref, k_ref, q_all, vt_ref, m_ref, acc_ref):
        for qb in range(nsub):
            s_ref[0:(qb + 1) * SUB, slab(qb)] = _dot_nt(k_ref[keys(i, 0, qb + 1), :], q_all[slab(qb), :])
            s_ref[rows(qb), slab(qb)] += pd_ref[...]
            if qb >= 1:
                s_ref[rows(qb - 1), slab(qb)] += pp_ref[...]
        for qb in range(nsub):
            absorb(s_ref, vt_ref, i, m_ref, acc_ref, first=True, lo=0, hi=qb + 1, qb=qb)

    def win_prev(s_ref):
        for qb in range(nsub):
            s_ref[qb * SUB:TA, slab(qb)] = _dot_nt(kw_ref[keys(i - 1, qb, nsub), :], q_win[slab(qb), :])
            s_ref[rows(qb), slab(qb)] += strict
        s_ref[rows(nsub - 1), slab(0)] += pp_ref[...]
        for qb in range(nsub):
            absorb(s_ref, vwt_ref, i - 1, mw_ref, accw_ref, lo=qb, hi=nsub, qb=qb)

    @pl.when(i == 0)
    def _():
        diag(sa_ref, ksa_ref, q_sel, vst_ref, ms_ref, accs_ref)
        diag(sb_ref, kw_ref, q_win, vwt_ref, mw_ref, accw_ref)

    @pl.when(i >= 1)
    def _():
        qk_sel(sb_ref, i - 1)
        sb_ref[rows(nsub - 1), slab(0)] += pp_ref[...]
        diag(sa_ref, ksa_ref, q_sel, vst_ref, ms_ref, accs_ref)
        absorb(sb_ref, vst_ref, i - 1, ms_ref, accs_ref)
        diag(sa_ref, kw_ref, q_win, vwt_ref, mw_ref, accw_ref)
        win_prev(sb_ref)

    nfar = jnp.maximum(i - 1, 0)
    odd = nfar % 2

    @pl.when(odd == 1)
    def _():
        qk_sel(sa_ref, 0)
        absorb(sa_ref, vst_ref, 0, ms_ref, accs_ref)

    npair = nfar // 2

    @pl.when(npair > 0)
    def _():
        qk_sel(sa_ref, odd)

    def pair(t, carry):
        c = odd + 2 * t
        qk_sel(sb_ref, c + 1)
        absorb(sa_ref, vst_ref, c, ms_ref, accs_ref)
        qk_sel(sa_ref, jnp.minimum(c + 2, nfar - 1))
        absorb(sb_ref, vst_ref, c + 1, ms_ref, accs_ref)
        return carry

    lax.fori_loop(0, npair, pair, 0)

    gs = _sigmoid_t(g_ref[:, 0:LANES])
    gt_ref[...] = gs.T
    inv_s = 1.0 / accs_ref[NSA_HD:NSA_HD + 1, :]
    inv_w = 1.0 / accw_ref[NSA_HD:NSA_HD + 1, :]
    lane = _iota((TA, LANES), 1)
    for h in range(NSA_HPG):
        base = gate_col + (g * NSA_HPG + h) * N_BRANCH
        cols = slice(h * NSA_HD, (h + 1) * NSA_HD)
        gate_c = jnp.sum(jnp.where(lane == base, gs, 0.0), axis=-1, keepdims=True)
        gs_row = gt_ref[pl.ds(base + 1, 1), :]
        gw_row = gt_ref[pl.ds(base + 2, 1), :]
        o_sw = []
        for qb in range(nsub):
            lanes = slice(qb * SLAB + h * SUB, qb * SLAB + (h + 1) * SUB)
            gate_s = gs_row[:, rows(qb)] * inv_s[:, lanes]
            gate_w = gw_row[:, rows(qb)] * inv_w[:, lanes]
            o_sw.append((gate_s * accs_ref[0:NSA_HD, lanes] + gate_w * accw_ref[0:NSA_HD, lanes]).T)
        o = gate_c * oc_ref[:, cols] + jnp.concatenate(o_sw, axis=0)
        z = z_ref[:, cols]
        y_ref[:, cols] = (o * z * _sigmoid_t(z)).astype(BF16)


def _attend(rel_bias, qn, ksa, vst, kwn, vwt, sel, o_c, gates, gate_blk, u, col_z, gate_col, B, S):
    nt = S // TA
    gw = NSA_HPG * NSA_HD
    W = NSA_HPG * TA
    cz = col_z // gw
    assert WIN == TA
    tile = lambda b, g, i: (b * nt + i, g)
    vt_spec = pl.BlockSpec((None, None, VT_ROWS, S), lambda b, g, i: (b, g, 0, 0))
    return pl.pallas_call(
        functools.partial(_attend_kernel, gate_col=gate_col),
        grid=(B, NSA_GROUPS, nt),
        in_specs=[
            pl.BlockSpec(memory_space=pltpu.SMEM),
            pl.BlockSpec((TA, gw), tile),
            pl.BlockSpec((S, 2 * NSA_HD), lambda b, g, i: (b, g)), vt_spec,
            pl.BlockSpec((S, NSA_HD), lambda b, g, i: (b, g)), vt_spec,
            pl.BlockSpec((None, None, TA, LANES), lambda b, g, i: (b, g, i, 0)),
            pl.BlockSpec((TA, gw), tile),
            pl.BlockSpec((TA, 2 * LANES), lambda b, g, i: (b * nt + i, gate_blk)),
            pl.BlockSpec((TA, gw), lambda b, g, i: (b * nt + i, cz + g)),
        ],
        out_specs=pl.BlockSpec((TA, gw), tile),
        out_shape=jax.ShapeDtypeStruct((B * S, NSA_HEADS * NSA_HD), BF16),
        scratch_shapes=[pltpu.VMEM((SUB, NSA_HPG * SUB), F32), pltpu.VMEM((SUB, NSA_HPG * SUB), F32),
                        pltpu.VMEM((TA, W), F32), pltpu.VMEM((TA, W), F32),
                        pltpu.VMEM((1, W), F32), pltpu.VMEM((VT_ROWS, W), F32),
                        pltpu.VMEM((1, W), F32), pltpu.VMEM((VT_ROWS, W), F32),
                        pltpu.VMEM((LANES, TA), F32)],
        compiler_params=pltpu.CompilerParams(dimension_semantics=("arbitrary", "arbitrary", "arbitrary"),
                                             vmem_limit_bytes=VMEM_LIMIT),
        name="attend",
    )(rel_bias, qn, ksa, vst, kwn, vwt, sel, o_c, gates, u)


def _outproj_kernel(yml_ref, yns_ref, w_ref, x_ref, o_ref, ob_ref, wb_ref):
    @pl.when(pl.program_id(1) == 0)
    def _():
        wb_ref[...] = w_ref[...].astype(BF16)

    half = yml_ref.shape[1]
    o = x_ref[...] + _dot(yml_ref[...], wb_ref[0:half, :]) + _dot(yns_ref[...], wb_ref[half:2 * half, :])
    o_ref[...] = o
    ob_ref[...] = o.astype(BF16)


def _outproj(y_ml, y_ns, w_out, x2, tm=256, tn=1024):
    T, D = x2.shape
    half = y_ml.shape[1]
    tile = pl.BlockSpec((tm, tn), lambda j, i: (i, j))
    return pl.pallas_call(
        _outproj_kernel,
        grid=(D // tn, T // tm),
        in_specs=[
            pl.BlockSpec((tm, half), lambda j, i: (i, 0)),
            pl.BlockSpec((tm, half), lambda j, i: (i, 0)),
            pl.BlockSpec((2 * half, tn), lambda j, i: (0, j)),
            tile,
        ],
        out_specs=[tile, tile],
        out_shape=[jax.ShapeDtypeStruct((T, D), F32), jax.ShapeDtypeStruct((T, D), BF16)],
        scratch_shapes=[pltpu.VMEM((2 * half, tn), BF16)],
        compiler_params=pltpu.CompilerParams(dimension_semantics=("arbitrary", "arbitrary"),
                                             vmem_limit_bytes=VMEM_LIMIT),
        name="outproj",
    )(y_ml, y_ns, w_out, x2)


def _ple_kernel(xb_ref, x_ref, p_ref, wg_ref, wp_ref, o_ref, wgb_ref, wpb_ref):
    @pl.when(pl.program_id(1) == 0)
    def _():
        wgb_ref[...] = wg_ref[...].astype(BF16)
        wpb_ref[...] = wp_ref[...].astype(BF16)

    gate = _sigmoid(_dot(xb_ref[...], wgb_ref[...]))
    emb = _dot(p_ref[...].astype(BF16), wpb_ref[...])
    o_ref[...] = x_ref[...] + gate * emb


def _ple(x1b, x1, p2, wg, wp, tm=256, tn=1024):
    T, D = x1.shape
    P = p2.shape[1]
    tile = pl.BlockSpec((tm, tn), lambda j, i: (i, j))
    return pl.pallas_call(
        _ple_kernel,
        grid=(D // tn, T // tm),
        in_specs=[
            pl.BlockSpec((tm, D), lambda j, i: (i, 0)),
            tile,
            pl.BlockSpec((tm, P), lambda j, i: (i, 0)),
            pl.BlockSpec((D, tn), lambda j, i: (0, j)),
            pl.BlockSpec((P, tn), lambda j, i: (0, j)),
        ],
        out_specs=tile,
        out_shape=jax.ShapeDtypeStruct((T, D), F32),
        scratch_shapes=[pltpu.VMEM((D, tn), BF16), pltpu.VMEM((P, tn), BF16)],
        compiler_params=pltpu.CompilerParams(dimension_semantics=("arbitrary", "arbitrary"),
                                             vmem_limit_bytes=VMEM_LIMIT),
        name="ple",
    )(x1b, x1, p2, wg, wp)


def _layer(x2, p2, norm_w, w_in, conv_w, i_bias, f_bias, head_norm_w, q_norm_w, k_norm_w,
           pe_k, pe_v, kw1, kw2, vw1, vw2, rel_bias, w_out, ple_proj, ple_gate, B, S):
    D = x2.shape[1]
    qkw = ML_HEADS * ML_DQK
    vw = ML_HEADS * ML_DV
    nq = NSA_HEADS * NSA_HD
    nkv = NSA_GROUPS * NSA_HD
    o_i = 2 * qkw + 3 * vw
    o_f = o_i + ML_HEADS
    o_nq = o_f + ML_HEADS
    o_g = o_nq + nq + 6 * nkv
    o_z = o_g + NSA_HEADS * N_BRANCH
    w_t = w_in.T
    ngate = NSA_HEADS * N_BRANCH
    zeros = lambda n: jnp.zeros((n, D), w_in.dtype)
    wg_t = jnp.concatenate([w_t[o_i:o_f], w_t[o_g:o_z], zeros(LANES - ML_HEADS - ngate),
                            w_t[o_f:o_nq], zeros(LANES - ML_HEADS)], axis=0)
    col_nq = o_i
    col_kv = col_nq + nq
    col_z = col_kv + 6 * nkv

    h, gates = _rmsnorm(x2, norm_w.reshape(1, D), wg_t)
    u = _inproj(h, w_t, [(0, o_i), (o_nq, o_g - o_nq), (o_z, nq)])

    pad_h = lambda v: jnp.concatenate([v, jnp.zeros((LANES - ML_HEADS,), v.dtype)]).reshape(1, LANES)
    y_ml = _mlstm(u, gates, 0, conv_w, pad_h(i_bias), pad_h(f_bias), head_norm_w, B, S)

    qn, ksa, vst, kwn, vwt = _nsa_prep(u, q_norm_w.reshape(1, NSA_HD), k_norm_w, col_nq, col_kv, B, S)
    kcmp, vcmp_t = _compress(u, pe_k, pe_v, kw1.astype(BF16), kw2.astype(BF16), vw1.astype(BF16),
                             vw2.astype(BF16), k_norm_w, col_kv, B, S)
    o_c, sel = _cmp_sel(rel_bias, qn, kcmp, vcmp_t, B, S)
    y_ns = _attend(rel_bias, qn, ksa, vst, kwn, vwt, sel, o_c, gates, 0, u, col_z, ML_HEADS, B, S)

    x1, x1b = _outproj(y_ml, y_ns, w_out, x2)
    return _ple(x1b, x1, p2, ple_gate, ple_proj)


def kernel(x, p, norm_w, w_in, ml_conv_w, ml_i_bias, ml_f_bias, ml_head_norm_w, nsa_q_norm_w, nsa_k_norm_w,
           cmp_pe_k, cmp_pe_v, cmp_k_w1, cmp_k_w2, cmp_v_w1, cmp_v_w2, rel_bias, w_out, ple_proj, ple_gate):
    B, S, D = x.shape
    assert S % max(WIN, 256) == 0 and S // SEL_LEN >= 1
    x2 = x.reshape(B * S, D)
    for layer in range(w_in.shape[0]):
        x2 = _layer(x2, p[layer].reshape(B * S, -1), norm_w[layer], w_in[layer], ml_conv_w[layer],
                    ml_i_bias[layer], ml_f_bias[layer], ml_head_norm_w[layer], nsa_q_norm_w[layer],
                    nsa_k_norm_w[layer], cmp_pe_k[layer], cmp_pe_v[layer], cmp_k_w1[layer], cmp_k_w2[layer],
                    cmp_v_w1[layer], cmp_v_w2[layer], rel_bias, w_out[layer], ple_proj[layer], ple_gate[layer],
                    B, S)
    return x2.reshape(B, S, D)
```

```python
import functools
import math

import numpy as np
import jax
import jax.numpy as jnp
from jax import lax
from jax.experimental import pallas as pl
from jax.experimental.pallas import tpu as pltpu

F32 = jnp.float32
BF16 = jnp.bfloat16
HIGHEST = lax.Precision.HIGHEST

ML_HEADS = 8
ML_DQK = 128
ML_DV = 256
ML_CHUNK = 64
ML_CONV = 4
NSA_HEADS = 16
NSA_HD = 128
NSA_GROUPS = 4
NSA_HPG = 4
N_BRANCH = 3
CMP_STRIDE = 16
CMP_LEN = 32
SEL_LEN = 64
SEL_SHIFT = 6
SEL_TOPK = 16
WIN = 512
REL_BUCKETS = 32
REL_MAX_DIST = 128
EPS = 1e-6
NEG = -1e30
FORCE_SCORE = 1e4
LOG2E = math.log2(math.e)

LANES = 128
VMEM_LIMIT = 56 * 1024 * 1024


def _bucket_thresholds():
    n = np.arange(0, 4 * REL_MAX_DIST, dtype=np.int64)
    max_exact = REL_BUCKETS // 2
    nf = np.maximum(n, 1).astype(np.float32)
    large = max_exact + (np.log(nf / np.float32(max_exact)) / np.float32(math.log(REL_MAX_DIST / max_exact))
                         * np.float32(REL_BUCKETS - max_exact)).astype(np.int32)
    large = np.minimum(large, REL_BUCKETS - 1)
    bucket = np.where(n < max_exact, n, large)
    assert np.all(np.diff(bucket) >= 0)
    thr = [int(np.argmax(bucket >= b)) for b in range(REL_BUCKETS)]
    assert thr[REL_BUCKETS - 1] <= REL_MAX_DIST
    return thr


BUCKET_THR = _bucket_thresholds()


def _dot(a, b, precision=None):
    return jnp.dot(a, b, preferred_element_type=F32, precision=precision)


def _dot_nt(a, b, precision=None):
    return lax.dot_general(a, b, (((1,), (1,)), ((), ())), preferred_element_type=F32, precision=precision)


def _sigmoid(x):
    return 1.0 / (1.0 + jnp.exp(-x))


def _silu(x):
    return x * _sigmoid(x)


def _sigmoid_t(x):
    return 0.5 * jnp.tanh(0.5 * x) + 0.5


def _iota(shape, dim):
    return lax.broadcasted_iota(jnp.int32, shape, dim)


def _rel_bias_pattern(dist, tbl_ref, head):
    val = jnp.full(dist.shape, tbl_ref[0, head], F32)
    for b in range(1, REL_BUCKETS):
        val = jnp.where(dist >= BUCKET_THR[b], tbl_ref[b, head], val)
    return (val - tbl_ref[REL_BUCKETS - 1, head]) * LOG2E


def _rmsnorm_kernel(x_ref, nw_ref, wg_ref, h_ref, g_ref, wgb_ref):
    @pl.when(pl.program_id(0) == 0)
    def _():
        wgb_ref[...] = wg_ref[...].astype(BF16)

    x = x_ref[...]
    ms = jnp.mean(x * x, axis=-1, keepdims=True)
    h = (x * lax.rsqrt(ms + EPS) * nw_ref[...]).astype(BF16)
    h_ref[...] = h
    g_ref[...] = _dot_nt(h, wgb_ref[...])


def _rmsnorm(x2, norm_w, wg_t, rb=256):
    T, D = x2.shape
    NG = wg_t.shape[0]
    return pl.pallas_call(
        _rmsnorm_kernel,
        grid=(T // rb,),
        in_specs=[pl.BlockSpec((rb, D), lambda i: (i, 0)), pl.BlockSpec((1, D), lambda i: (0, 0)),
                  pl.BlockSpec((NG, D), lambda i: (0, 0))],
        out_specs=[pl.BlockSpec((rb, D), lambda i: (i, 0)), pl.BlockSpec((rb, NG), lambda i: (i, 0))],
        out_shape=[jax.ShapeDtypeStruct((T, D), BF16), jax.ShapeDtypeStruct((T, NG), F32)],
        scratch_shapes=[pltpu.VMEM((NG, D), BF16)],
        compiler_params=pltpu.CompilerParams(dimension_semantics=("arbitrary",), vmem_limit_bytes=VMEM_LIMIT),
        name="rmsnorm",
    )(x2, norm_w, wg_t)


def _inproj_kernel(h_ref, wt_ref, u_ref, wb_ref):
    @pl.when(pl.program_id(1) == 0)
    def _():
        wb_ref[...] = wt_ref[...].astype(BF16)

    u_ref[...] = _dot_nt(h_ref[...], wb_ref[...])


def _piece_blocks(pieces, tn):
    starts = []
    for start, length in pieces:
        assert length % tn == 0 and start % 8 == 0
        starts += [start + k * tn for k in range(length // tn)]
    bounds = [(b, s) for b, s in enumerate(starts) if b == 0 or s != starts[b - 1] + tn]

    def w_rows(j, i):
        row8 = (bounds[0][1] + j * tn) // 8
        for b, s in bounds[1:]:
            row8 = jnp.where(j >= b, (s + (j - b) * tn) // 8, row8)
        return row8 * 8, 0

    return len(starts), w_rows


def _inproj(h, w_t, pieces, tm=512, tn=1024):
    T, D = h.shape
    nblk, w_rows = _piece_blocks(pieces, tn)
    return pl.pallas_call(
        _inproj_kernel,
        grid=(nblk, T // tm),
        in_specs=[
            pl.BlockSpec((tm, D), lambda j, i: (i, 0)),
            pl.BlockSpec((pl.Element(tn), pl.Element(D)), w_rows),
        ],
        out_specs=pl.BlockSpec((tm, tn), lambda j, i: (i, j)),
        out_shape=jax.ShapeDtypeStruct((T, nblk * tn), F32),
        scratch_shapes=[pltpu.VMEM((tn, D), BF16)],
        compiler_params=pltpu.CompilerParams(dimension_semantics=("arbitrary", "arbitrary"),
                                             vmem_limit_bytes=VMEM_LIMIT),
        name="inproj",
    )(h, w_t)


def _mlstm_kernel(q_ref, k_ref, v_ref, o_ref, z_ref, g_ref, cw_ref, ib_ref, fb_ref, hw_ref, y_ref,
                  xbuf, c_ref, m_ref, *, lb):
    L = ML_CHUNK
    qkw = ML_HEADS * ML_DQK

    @pl.when(pl.program_id(1) == 0)
    def _():
        xbuf[0:8, :] = jnp.zeros((8, 2 * qkw), F32)
        c_ref[...] = jnp.zeros(c_ref.shape, F32)
        m_ref[...] = jnp.zeros(m_ref.shape, F32)

    row = _iota((L, L), 0)
    col = _iota((L, L), 1)
    tril = col <= row
    rowg = _iota((L, LANES), 0)
    eye_h = (_iota((8, LANES), 0) == _iota((8, LANES), 1)).astype(F32)
    ones_col = (_iota((L, LANES), 1) == 0).astype(BF16)
    heads = range(ML_HEADS)

    def chunk(c, carry):
        rows = pl.ds(pl.multiple_of(c * L, L), L)
        qk_cols = lambda h: slice(h * ML_DQK, (h + 1) * ML_DQK)
        v_cols = lambda h: slice(h * ML_DV, (h + 1) * ML_DV)
        col_of = lambda x, h: x[:, h:h + 1]

        G = g_ref[rows, :]
        li = G[:, 0:LANES] + ib_ref[...]
        fp = G[:, LANES:2 * LANES] + fb_ref[...]
        lf = jnp.minimum(fp, 0.0) - jnp.log(1.0 + jnp.exp(-jnp.abs(fp)))
        b = lf
        for sh in (1, 2, 4, 8, 16, 32):
            b = b + jnp.where(rowg >= sh, pltpu.roll(b, sh, 0), 0.0)
        g = li - b
        gmax = g
        for sh in (1, 2, 4, 8, 16, 32):
            gmax = jnp.maximum(gmax, jnp.where(rowg >= sh, pltpu.roll(gmax, sh, 0), NEG))
        m_old = m_ref[...]
        top = jnp.maximum(m_old, gmax)
        m_t = b + top
        inter = jnp.exp(m_old - top)
        floor = jnp.exp(-m_t)
        bL = b[L - 1:L, :]
        w = bL - b + li
        m_new = jnp.maximum(bL + m_old, jnp.max(w, axis=0, keepdims=True))
        wk = jnp.exp(w - m_new)
        decay = jnp.exp(bL + m_old - m_new)
        m_ref[...] = m_new
        g_rows = _dot_nt(eye_h, g, precision=HIGHEST)

        xbuf[8:8 + L, 0:qkw] = q_ref[rows, :]
        xbuf[8:8 + L, qkw:2 * qkw] = k_ref[rows, :]
        acc = cw_ref[ML_CONV - 1:ML_CONV, :] * xbuf[8:8 + L, :]
        for j in range(1, ML_CONV):
            acc = acc + cw_ref[ML_CONV - 1 - j:ML_CONV - j, :] * xbuf[8 - j:8 - j + L, :]
        act = acc * _sigmoid_t(acc)
        xbuf[0:8, :] = xbuf[L:L + 8, :]

        q = [act[:, qk_cols(h)].astype(BF16) for h in heads]
        kf = [act[:, qkw + h * ML_DQK:qkw + (h + 1) * ML_DQK] * (ML_DQK ** -0.5) for h in heads]
        vaug = [jnp.concatenate([v_ref[rows, v_cols(h)].astype(BF16), ones_col], axis=1) for h in heads]
        cst = [c_ref[h] for h in heads]
        lhs = []
        for h in heads:
            dw = jnp.exp(jnp.where(tril, g_rows[h:h + 1, :] - col_of(top, h), NEG))
            sc = (_dot_nt(q[h], kf[h].astype(BF16)) * dw).astype(BF16)
            q_in = (q[h].astype(F32) * col_of(inter, h)).astype(BF16)
            lhs.append(jnp.concatenate([q_in, sc], axis=1))
        num_aug = [_dot(lhs[h], jnp.concatenate([cst[h].astype(BF16), vaug[h]], axis=0)) for h in heads]
        for h in heads:
            kw = (kf[h] * col_of(wk, h)).astype(BF16)
            upd = lax.dot_general(kw, vaug[h], (((0,), (0,)), ((), ())), preferred_element_type=F32)
            c_ref[h] = decay[:, h:h + 1] * cst[h] + upd
        for h in heads:
            num = num_aug[h][:, 0:ML_DV]
            den = num_aug[h][:, ML_DV:ML_DV + 1]
            r = 1.0 / jnp.maximum(jnp.abs(den), col_of(floor, h))
            f = r * lax.rsqrt(r * r * jnp.mean(num * num, axis=-1, keepdims=True) + EPS)
            o = o_ref[rows, v_cols(h)]
            z = z_ref[rows, v_cols(h)]
            gate = _sigmoid_t(o) * z * _sigmoid_t(z)
            y_ref[rows, v_cols(h)] = (num * f * hw_ref[h:h + 1, :] * gate).astype(BF16)
        return carry

    lax.fori_loop(0, lb // L, chunk, 0)


def _mlstm(u, gates, gate_blk, conv_w, ib, fb, head_w, B, S, lb=512):
    T = B * S
    nsb = S // lb
    qkw = ML_HEADS * ML_DQK
    vw = ML_HEADS * ML_DV
    rowmap = lambda col: (lambda b, s: (b * nsb + s, col))
    const = lambda b, s: (0, 0)
    return pl.pallas_call(
        functools.partial(_mlstm_kernel, lb=lb),
        grid=(B, nsb),
        in_specs=[
            pl.BlockSpec((lb, qkw), rowmap(0)),
            pl.BlockSpec((lb, qkw), rowmap(1)),
            pl.BlockSpec((lb, vw), rowmap(1)),
            pl.BlockSpec((lb, vw), rowmap(2)),
            pl.BlockSpec((lb, vw), rowmap(3)),
            pl.BlockSpec((lb, 2 * LANES), rowmap(gate_blk)),
            pl.BlockSpec((ML_CONV, 2 * qkw), const),
            pl.BlockSpec((1, LANES), const),
            pl.BlockSpec((1, LANES), const),
            pl.BlockSpec((ML_HEADS, ML_DV), const),
        ],
        out_specs=pl.BlockSpec((lb, vw), rowmap(0)),
        out_shape=jax.ShapeDtypeStruct((T, vw), BF16),
        scratch_shapes=[
            pltpu.VMEM((ML_CHUNK + 8, 2 * qkw), F32),
            pltpu.VMEM((ML_HEADS, ML_DQK, ML_DV + LANES), F32),
            pltpu.VMEM((1, LANES), F32),
        ],
        compiler_params=pltpu.CompilerParams(dimension_semantics=("arbitrary", "arbitrary"),
                                             vmem_limit_bytes=VMEM_LIMIT),
        name="mlstm",
    )(u, u, u, u, u, gates, conv_w, ib, fb, head_w)


def _rms_heads(x, w, scale=1.0):
    outs = []
    for h in range(x.shape[1] // NSA_HD):
        xh = x[:, h * NSA_HD:(h + 1) * NSA_HD]
        ms = jnp.mean(xh * xh, axis=-1, keepdims=True)
        outs.append(xh * lax.rsqrt(ms + EPS) * w * scale)
    return jnp.concatenate(outs, axis=1)


VT_ROWS = NSA_HD + 16


def _nsa_prep_kernel(q_ref, ks_ref, vs_ref, kw_ref, vw_ref, qw_ref, kn_ref,
                     qn_ref, ksa_ref, vst_ref, kwn_ref, vwt_ref, *, rb, nsb):
    qn_ref[...] = _rms_heads(q_ref[...], qw_ref[...], NSA_HD ** -0.5 * LOG2E).astype(BF16)
    ksn = _rms_heads(ks_ref[...], kn_ref[1:2, :]).astype(BF16)
    kwn_ref[...] = _rms_heads(kw_ref[...], kn_ref[2:3, :]).astype(BF16)
    t = (pl.program_id(0) % nsb) * rb + _iota((rb, NSA_HD), 0)
    onehot = (_iota((rb, NSA_HD), 1) == jnp.right_shift(t, SEL_SHIFT)).astype(BF16)
    tail = (_iota((VT_ROWS - NSA_HD, rb), 0) == 0).astype(BF16)
    for g in range(NSA_GROUPS):
        cols = slice(g * NSA_HD, (g + 1) * NSA_HD)
        ksa_ref[:, 2 * g * NSA_HD:(2 * g + 1) * NSA_HD] = ksn[:, cols]
        ksa_ref[:, (2 * g + 1) * NSA_HD:(2 * g + 2) * NSA_HD] = onehot
        vst_ref[g] = jnp.concatenate([vs_ref[:, cols].T.astype(BF16), tail], axis=0)
        vwt_ref[g] = jnp.concatenate([vw_ref[:, cols].T.astype(BF16), tail], axis=0)


def _nsa_prep(u, q_norm_w, k_norm_w, col_q, col_kv, B, S, rb=512):
    T = u.shape[0]
    nsb = S // rb
    qw = NSA_HEADS * NSA_HD
    kvw = NSA_GROUPS * NSA_HD
    cq = col_q // qw
    ck = col_kv // kvw
    assert S // SEL_LEN <= NSA_HD
    kv_spec = lambda idx: pl.BlockSpec((rb, kvw), lambda i: (i, ck + idx))
    out_k = pl.BlockSpec((rb, kvw), lambda i: (i, 0))
    out_ka = pl.BlockSpec((rb, 2 * kvw), lambda i: (i, 0))
    out_vt = pl.BlockSpec((None, NSA_GROUPS, VT_ROWS, rb), lambda i: (i // nsb, 0, 0, i % nsb))
    vt_shape = jax.ShapeDtypeStruct((B, NSA_GROUPS, VT_ROWS, S), BF16)
    return pl.pallas_call(
        functools.partial(_nsa_prep_kernel, rb=rb, nsb=nsb),
        grid=(T // rb,),
        in_specs=[
            pl.BlockSpec((rb, qw), lambda i: (i, cq)),
            kv_spec(2), kv_spec(3), kv_spec(4), kv_spec(5),
            pl.BlockSpec((1, NSA_HD), lambda i: (0, 0)),
            pl.BlockSpec((N_BRANCH, NSA_HD), lambda i: (0, 0)),
        ],
        out_specs=[pl.BlockSpec((rb, qw), lambda i: (i, 0)), out_ka, out_vt, out_k, out_vt],
        out_shape=[jax.ShapeDtypeStruct((T, qw), BF16), jax.ShapeDtypeStruct((T, 2 * kvw), BF16), vt_shape,
                   jax.ShapeDtypeStruct((T, kvw), BF16), vt_shape],
        compiler_params=pltpu.CompilerParams(dimension_semantics=("arbitrary",), vmem_limit_bytes=VMEM_LIMIT),
        name="nsa_prep",
    )(u, u, u, u, u, q_norm_w, k_norm_w)


def _compress_kernel(kc_ref, vc_ref, pek_ref, pev_ref, kw1_ref, kw2_ref, vw1_ref, vw2t_ref, kn_ref,
                     kcmp_ref, vcmp_ref, *, ns):
    def hidden(x_ref, pe_ref, w1_ref):
        hid = CMP_LEN // 2
        a = jnp.zeros((ns, w1_ref.shape[1]), F32)
        bsum = jnp.zeros((ns, w1_ref.shape[1]), F32)
        for l in range(0, hid, 2):
            xl = [x_ref[pl.ds(l + t, ns, stride=CMP_STRIDE), :] for t in range(2)]
            first = jnp.concatenate([(xl[t] + pe_ref[l + t:l + t + 1, :]).astype(BF16) for t in range(2)], axis=1)
            second = jnp.concatenate([(xl[t] + pe_ref[hid + l + t:hid + l + t + 1, :]).astype(BF16)
                                      for t in range(2)], axis=1)
            a = a + _dot(first, w1_ref[l * NSA_HD:(l + 2) * NSA_HD, :])
            bsum = bsum + _dot(second, w1_ref[(hid + l) * NSA_HD:(hid + l + 2) * NSA_HD, :])
        pre = a + pltpu.roll(bsum, ns - 1, 0)
        return _silu(pre).astype(BF16)

    kc = _dot(hidden(kc_ref, pek_ref, kw1_ref), kw2_ref[...])
    ms = jnp.mean(kc * kc, axis=-1, keepdims=True)
    kcmp_ref[...] = (kc * lax.rsqrt(ms + EPS) * kn_ref[0:1, :]).astype(BF16)
    vcmp_ref[...] = _dot_nt(vw2t_ref[...], hidden(vc_ref, pev_ref, vw1_ref)).astype(BF16)


def _compress(u, pe_k, pe_v, kw1, kw2, vw1, vw2, k_norm_w, col_kv, B, S):
    ns = S // CMP_STRIDE
    ck = col_kv // NSA_HD
    hidden = kw1.shape[1]
    const = lambda b, g: (0, 0)
    return pl.pallas_call(
        functools.partial(_compress_kernel, ns=ns),
        grid=(B, NSA_GROUPS),
        in_specs=[
            pl.BlockSpec((S, NSA_HD), lambda b, g: (b, ck + g)),
            pl.BlockSpec((S, NSA_HD), lambda b, g: (b, ck + NSA_GROUPS + g)),
            pl.BlockSpec((CMP_LEN, NSA_HD), const),
            pl.BlockSpec((CMP_LEN, NSA_HD), const),
            pl.BlockSpec((CMP_LEN * NSA_HD, hidden), const),
            pl.BlockSpec((hidden, NSA_HD), const),
            pl.BlockSpec((CMP_LEN * NSA_HD, hidden), const),
            pl.BlockSpec((NSA_HD, hidden), const),
            pl.BlockSpec((N_BRANCH, NSA_HD), const),
        ],
        out_specs=[pl.BlockSpec((None, None, ns, NSA_HD), lambda b, g: (b, g, 0, 0)),
                   pl.BlockSpec((None, None, NSA_HD, ns), lambda b, g: (b, g, 0, 0))],
        out_shape=[jax.ShapeDtypeStruct((B, NSA_GROUPS, ns, NSA_HD), BF16),
                   jax.ShapeDtypeStruct((B, NSA_GROUPS, NSA_HD, ns), BF16)],
        compiler_params=pltpu.CompilerParams(dimension_semantics=("arbitrary", "arbitrary"),
                                             vmem_limit_bytes=VMEM_LIMIT),
        name="compress",
    )(u, u, pe_k, pe_v, kw1, kw2, vw1, vw2.T, k_norm_w)


TQC = 256
CMP_TILES = 4
NEAR = 32
NEAR_BACK = 16


def _stack_heads(q4):
    return jnp.concatenate([q4[:, h * NSA_HD:(h + 1) * NSA_HD] for h in range(NSA_HPG)], axis=0)


def _cmp_sel_kernel(tbl_ref, q_ref, kc_ref, vct_ref, oc_ref, sel_ref, pat_ref, *, ns, nsel):
    g = pl.program_id(1)
    i = pl.program_id(2)
    t0 = i * (CMP_TILES * TQC)
    c0 = i * (CMP_TILES * TQC // CMP_STRIDE)
    W = NSA_HPG * TQC
    assert (TQC - CMP_LEN) // CMP_STRIDE < NEAR - NEAR_BACK and NEAR_BACK * CMP_STRIDE >= REL_MAX_DIST + CMP_LEN

    @pl.when(i == 0)
    def _():
        r = _iota((TQC, LANES), 0)
        lane = _iota((TQC, LANES), 1)
        d = r - CMP_STRIDE * (jnp.bitwise_and(lane, NEAR - 1) - NEAR_BACK) - (CMP_LEN - 1)
        for h in range(NSA_HPG):
            val = jnp.where(d < 0, NEG, _rel_bias_pattern(d, tbl_ref, g * NSA_HPG + h))
            hi = val.astype(BF16).astype(F32)
            lo = jnp.where(d < 0, 0.0, val - hi)
            ext = jnp.where(lane < NEAR, hi, jnp.where(lane < 2 * NEAR, lo, jnp.where(lane == 2 * NEAR, NEG, 0.0)))
            pat_ref[h * TQC:(h + 1) * TQC, :] = ext.astype(BF16)

    tiles_q = range(CMP_TILES)
    qrows = lambda tq: slice(tq * TQC, (tq + 1) * TQC)
    lane = _iota((ns, LANES), 1)
    s = []
    for tq in tiles_q:
        q4 = q_ref[qrows(tq), :]
        q_aug = jnp.concatenate([jnp.concatenate([q4[:, h * NSA_HD:(h + 1) * NSA_HD],
                                                  pat_ref[h * TQC:(h + 1) * TQC, :]], axis=1)
                                 for h in range(NSA_HPG)], axis=0)
        rel = _iota((ns, LANES), 0) - (c0 + tq * (TQC // CMP_STRIDE) - NEAR_BACK)
        ext = ((lane < 2 * NEAR) & (rel == jnp.bitwise_and(lane, NEAR - 1))) | ((lane == 2 * NEAR) & (rel >= NEAR))
        k_aug = jnp.concatenate([kc_ref[...], jnp.where(ext, 1.0, 0.0).astype(BF16)], axis=1)
        s.append(_dot_nt(k_aug, q_aug))

    e, inv = [], []
    for tq in tiles_q:
        m = jnp.max(s[tq], axis=0, keepdims=True)
        e.append(jnp.exp2(s[tq] - m))
        l = jnp.sum(e[tq], axis=0, keepdims=True)
        t_abs = t0 + tq * TQC + jnp.bitwise_and(_iota((1, W), 1), TQC - 1)
        inv.append(jnp.where(t_abs >= CMP_LEN - 1, 1.0 / l, 0.0))

    ratio = SEL_LEN // CMP_STRIDE
    blk_n = _iota((nsel, ns), 0)
    tok_c = _iota((nsel, ns), 1)
    ov = ((tok_c >= ratio * blk_n - (CMP_LEN // CMP_STRIDE - 1)) & (tok_c <= ratio * blk_n + ratio - 1)).astype(F32)
    blk = _iota((nsel, TQC), 0)
    imp, cur = [], []
    for tq in tiles_q:
        o_t = _dot(vct_ref[...], e[tq].astype(BF16)) * inv[tq]
        for h in range(NSA_HPG):
            oc_ref[qrows(tq), h * NSA_HD:(h + 1) * NSA_HD] = o_t[:, h * TQC:(h + 1) * TQC].T
        p = e[tq] * inv[tq]
        ps = p[:, 0:TQC]
        for h in range(1, NSA_HPG):
            ps = ps + p[:, h * TQC:(h + 1) * TQC]
        v = _dot(ov, ps, precision=HIGHEST)
        c = jnp.right_shift(t0 + tq * TQC + _iota((nsel, TQC), 1), SEL_SHIFT)
        v = jnp.where((blk == 0) | (blk == c) | (blk == c - 1), FORCE_SCORE, v)
        imp.append(jnp.where(blk > c, NEG, v))
        cur.append(c)
    k_top = min(SEL_TOPK, nsel)

    def emit(tq, chosen):
        chosen = jnp.concatenate([chosen.astype(BF16), jnp.zeros((LANES - nsel, TQC), BF16)], axis=0)
        eye_q = (_iota((TQC, TQC), 0) == _iota((TQC, TQC), 1)).astype(BF16)
        sel_ref[qrows(tq), :] = ((_dot_nt(eye_q, chosen) - 1.0) * (-NEG)).astype(BF16)

    few = (i + 1) * CMP_TILES * (TQC // SEL_LEN) <= k_top

    @pl.when(few)
    def _():
        for tq in tiles_q:
            emit(tq, jnp.where(blk <= cur[tq], 1.0, 0.0))

    @pl.when(jnp.logical_not(few))
    def _():
        sub = _iota((8, TQC), 0)
        for tq in tiles_q:
            tiles = [imp[tq][8 * k:8 * k + 8, :] for k in range(nsel // 8)]
            counts = [jnp.zeros((8, TQC), jnp.int32) for _ in tiles]
            for jb in range(nsel):
                vj = imp[tq][jb:jb + 1, :]
                for k, tile in enumerate(tiles):
                    if 8 * k > jb:
                        ahead = vj >= tile
                    elif 8 * k + 7 < jb:
                        ahead = vj > tile
                    else:
                        ahead = (vj > tile) | ((vj == tile) & (sub + 8 * k > jb))
                    counts[k] = counts[k] + jnp.where(ahead, 1, 0)
            rank = jnp.concatenate(counts, axis=0)
            emit(tq, jnp.where(rank < k_top, 1.0, 0.0))


def _cmp_sel(rel_bias, qn, kcmp, vcmp_t, B, S):
    ns = S // CMP_STRIDE
    nsel = S // SEL_LEN
    rows = CMP_TILES * TQC
    nt = S // rows
    gw = NSA_HPG * NSA_HD
    assert nsel % 8 == 0 and nsel <= LANES
    return pl.pallas_call(
        functools.partial(_cmp_sel_kernel, ns=ns, nsel=nsel),
        grid=(B, NSA_GROUPS, nt),
        in_specs=[
            pl.BlockSpec(memory_space=pltpu.SMEM),
            pl.BlockSpec((rows, gw), lambda b, g, i: (b * nt + i, g)),
            pl.BlockSpec((None, None, ns, NSA_HD), lambda b, g, i: (b, g, 0, 0)),
            pl.BlockSpec((None, None, NSA_HD, ns), lambda b, g, i: (b, g, 0, 0)),
        ],
        out_specs=[
            pl.BlockSpec((rows, gw), lambda b, g, i: (b * nt + i, g)),
            pl.BlockSpec((None, None, rows, LANES), lambda b, g, i: (b, g, i, 0)),
        ],
        out_shape=[jax.ShapeDtypeStruct((B * S, NSA_HEADS * NSA_HD), F32),
                   jax.ShapeDtypeStruct((B, NSA_GROUPS, S, LANES), BF16)],
        scratch_shapes=[pltpu.VMEM((NSA_HPG * TQC, LANES), BF16)],
        compiler_params=pltpu.CompilerParams(dimension_semantics=("arbitrary", "arbitrary", "arbitrary"),
                                             vmem_limit_bytes=VMEM_LIMIT),
        name="cmp_sel",
    )(rel_bias, qn, kcmp, vcmp_t)


TA = 512
SUB = 128


def _attend_kernel(tbl_ref, q_ref, ksa_ref, vst_ref, kw_ref, vwt_ref, sel_ref, oc_ref, g_ref, z_ref, y_ref,
                   pd_ref, pp_ref, sa_ref, sb_ref, ms_ref, accs_ref, mw_ref, accw_ref, gt_ref, *, gate_col):
    g = pl.program_id(1)
    i = pl.program_id(2)
    nsub = TA // SUB

    krow = _iota((SUB, SUB), 0)
    qcol = _iota((SUB, SUB), 1)

    @pl.when(i == 0)
    def _():
        for h in range(NSA_HPG):
            head = g * NSA_HPG + h
            lanes = slice(h * SUB, (h + 1) * SUB)
            pd_ref[:, lanes] = jnp.where(krow <= qcol, _rel_bias_pattern(qcol - krow, tbl_ref, head), NEG)
            pp_ref[:, lanes] = _rel_bias_pattern(qcol - krow + SUB, tbl_ref, head)

    SLAB = NSA_HPG * SUB
    q4 = q_ref[...]
    sel = sel_ref[...]
    q_win = jnp.concatenate([q4[qb * SUB:(qb + 1) * SUB, h * NSA_HD:(h + 1) * NSA_HD]
                             for qb in range(nsub) for h in range(NSA_HPG)], axis=0)
    q_sel = jnp.concatenate([jnp.concatenate([q4[qb * SUB:(qb + 1) * SUB, h * NSA_HD:(h + 1) * NSA_HD],
                                              sel[qb * SUB:(qb + 1) * SUB, :]], axis=1)
                             for qb in range(nsub) for h in range(NSA_HPG)], axis=0)
    strict = jnp.concatenate([jnp.where(krow > qcol, 0.0, NEG)] * NSA_HPG, axis=1)

    def chunk(c):
        return pl.ds(pl.multiple_of(c * TA, TA), TA)

    def keys(c, lo, hi):
        return pl.ds(pl.multiple_of(c * TA + lo * SUB, SUB), (hi - lo) * SUB)

    def rows(kb):
        return slice(kb * SUB, (kb + 1) * SUB)

    def slab(qb):
        return slice(qb * SLAB, (qb + 1) * SLAB)

    def absorb(s_ref, vt_ref, c, m_ref, acc_ref, first=False, lo=0, hi=nsub, qb=None, have_max=False):
        lanes = slice(None) if qb is None else slab(qb)
        krows = slice(lo * SUB, hi * SUB)
        mx = s_ref[TA:TA + 1, lanes] if have_max else jnp.max(s_ref[krows, lanes], axis=0, keepdims=True)
        if first:
            m_new = mx
        else:
            m_old = m_ref[:, lanes]
            m_new = jnp.maximum(m_old, mx)
        p = jnp.exp2(s_ref[krows, lanes] - m_new).astype(BF16)
        pv = _dot(vt_ref[:, keys(c, lo, hi)], p)
        if first:
            acc_ref[:, lanes] = pv
        else:
            acc_ref[:, lanes] = jnp.exp2(m_old - m_new) * acc_ref[:, lanes] + pv
        m_ref[:, lanes] = m_new

    def qk_sel(s_ref, c, with_max=False):
        s = _dot_nt(ksa_ref[chunk(c), :], q_sel)
        s_ref[0:TA, :] = s
        if with_max:
            s_ref[TA:TA + 1, :] = jnp.max(s, axis=0, keepdims=True)

    def diag(s_ref, k_ref, q_all, vt_ref, m_ref, acc_ref):
        for qb in range(nsub):
            s_ref[0:(qb + 1) * SUB, slab(qb)] = _dot_nt(k_ref[keys(i, 0, qb + 1), :], q_all[slab(qb), :])
            s_ref[rows(qb), slab(qb)] += pd_ref[...]
            if qb >= 1:
                s_ref[rows(qb - 1), slab(qb)] += pp_ref[...]
        for qb in range(nsub):
            absorb(s_ref, vt_ref, i, m_ref, acc_ref, first=True, lo=0, hi=qb + 1, qb=qb)

    def win_prev(s_ref):
        for qb in range(nsub):
            s_ref[qb * SUB:TA, slab(qb)] = _dot_nt(kw_ref[keys(i - 1, qb, nsub), :], q_win[slab(qb), :])
            s_ref[rows(qb), slab(qb)] += strict
        s_ref[rows(nsub - 1), slab(0)] += pp_ref[...]
        for qb in range(nsub):
            absorb(s_ref, vwt_ref, i - 1, mw_ref, accw_ref, lo=qb, hi=nsub, qb=qb)

    @pl.when(i == 0)
    def _():
        diag(sa_ref, ksa_ref, q_sel, vst_ref, ms_ref, accs_ref)
        diag(sb_ref, kw_ref, q_win, vwt_ref, mw_ref, accw_ref)

    @pl.when(i >= 1)
    def _():
        qk_sel(sb_ref, i - 1)
        sb_ref[rows(nsub - 1), slab(0)] += pp_ref[...]
        diag(sa_ref, ksa_ref, q_sel, vst_ref, ms_ref, accs_ref)
        absorb(sb_ref, vst_ref, i - 1, ms_ref, accs_ref)
        diag(sa_ref, kw_ref, q_win, vwt_ref, mw_ref, accw_ref)
        win_prev(sb_ref)

    nfar = jnp.maximum(i - 1, 0)
    odd = nfar % 2

    @pl.when(odd == 1)
    def _():
        qk_sel(sa_ref, 0, with_max=True)
        absorb(sa_ref, vst_ref, 0, ms_ref, accs_ref, have_max=True)

    npair = nfar // 2

    @pl.when(npair > 0)
    def _():
        qk_sel(sa_ref, odd, with_max=True)

    def pair(t, carry):
        c = odd + 2 * t
        qk_sel(sb_ref, c + 1, with_max=True)
        absorb(sa_ref, vst_ref, c, ms_ref, accs_ref, have_max=True)
        qk_sel(sa_ref, jnp.minimum(c + 2, nfar - 1), with_max=True)
        absorb(sb_ref, vst_ref, c + 1, ms_ref, accs_ref, have_max=True)
        return carry

    lax.fori_loop(0, npair, pair, 0)

    gs = _sigmoid_t(g_ref[:, 0:LANES])
    gt_ref[...] = gs.T
    inv_s = 1.0 / accs_ref[NSA_HD:NSA_HD + 1, :]
    inv_w = 1.0 / accw_ref[NSA_HD:NSA_HD + 1, :]
    lane = _iota((TA, LANES), 1)
    for h in range(NSA_HPG):
        base = gate_col + (g * NSA_HPG + h) * N_BRANCH
        cols = slice(h * NSA_HD, (h + 1) * NSA_HD)
        gate_c = jnp.sum(jnp.where(lane == base, gs, 0.0), axis=-1, keepdims=True)
        gs_row = gt_ref[pl.ds(base + 1, 1), :]
        gw_row = gt_ref[pl.ds(base + 2, 1), :]
        o_sw = []
        for qb in range(nsub):
            lanes = slice(qb * SLAB + h * SUB, qb * SLAB + (h + 1) * SUB)
            gate_s = gs_row[:, rows(qb)] * inv_s[:, lanes]
            gate_w = gw_row[:, rows(qb)] * inv_w[:, lanes]
            o_sw.append((gate_s * accs_ref[0:NSA_HD, lanes] + gate_w * accw_ref[0:NSA_HD, lanes]).T)
        o = gate_c * oc_ref[:, cols] + jnp.concatenate(o_sw, axis=0)
        z = z_ref[:, cols]
        y_ref[:, cols] = (o * z * _sigmoid_t(z)).astype(BF16)


def _attend(rel_bias, qn, ksa, vst, kwn, vwt, sel, o_c, gates, gate_blk, u, col_z, gate_col, B, S):
    nt = S // TA
    gw = NSA_HPG * NSA_HD
    W = NSA_HPG * TA
    cz = col_z // gw
    assert WIN == TA
    tile = lambda b, g, i: (b * nt + i, g)
    vt_spec = pl.BlockSpec((None, None, VT_ROWS, S), lambda b, g, i: (b, g, 0, 0))
    return pl.pallas_call(
        functools.partial(_attend_kernel, gate_col=gate_col),
        grid=(B, NSA_GROUPS, nt),
        in_specs=[
            pl.BlockSpec(memory_space=pltpu.SMEM),
            pl.BlockSpec((TA, gw), tile),
            pl.BlockSpec((S, 2 * NSA_HD), lambda b, g, i: (b, g)), vt_spec,
            pl.BlockSpec((S, NSA_HD), lambda b, g, i: (b, g)), vt_spec,
            pl.BlockSpec((None, None, TA, LANES), lambda b, g, i: (b, g, i, 0)),
            pl.BlockSpec((TA, gw), tile),
            pl.BlockSpec((TA, 2 * LANES), lambda b, g, i: (b * nt + i, gate_blk)),
            pl.BlockSpec((TA, gw), lambda b, g, i: (b * nt + i, cz + g)),
        ],
        out_specs=pl.BlockSpec((TA, gw), tile),
        out_shape=jax.ShapeDtypeStruct((B * S, NSA_HEADS * NSA_HD), BF16),
        scratch_shapes=[pltpu.VMEM((SUB, NSA_HPG * SUB), F32), pltpu.VMEM((SUB, NSA_HPG * SUB), F32),
                        pltpu.VMEM((TA + 8, W), F32), pltpu.VMEM((TA + 8, W), F32),
                        pltpu.VMEM((1, W), F32), pltpu.VMEM((VT_ROWS, W), F32),
                        pltpu.VMEM((1, W), F32), pltpu.VMEM((VT_ROWS, W), F32),
                        pltpu.VMEM((LANES, TA), F32)],
        compiler_params=pltpu.CompilerParams(dimension_semantics=("arbitrary", "arbitrary", "arbitrary"),
                                             vmem_limit_bytes=VMEM_LIMIT),
        name="attend",
    )(rel_bias, qn, ksa, vst, kwn, vwt, sel, o_c, gates, u)


def _outproj_kernel(yml_ref, yns_ref, w_ref, x_ref, o_ref, ob_ref, wb_ref):
    @pl.when(pl.program_id(1) == 0)
    def _():
        wb_ref[...] = w_ref[...].astype(BF16)

    half = yml_ref.shape[1]
    o = x_ref[...] + _dot(yml_ref[...], wb_ref[0:half, :]) + _dot(yns_ref[...], wb_ref[half:2 * half, :])
    o_ref[...] = o
    ob_ref[...] = o.astype(BF16)


def _outproj(y_ml, y_ns, w_out, x2, tm=256, tn=1024):
    T, D = x2.shape
    half = y_ml.shape[1]
    tile = pl.BlockSpec((tm, tn), lambda j, i: (i, j))
    return pl.pallas_call(
        _outproj_kernel,
        grid=(D // tn, T // tm),
        in_specs=[
            pl.BlockSpec((tm, half), lambda j, i: (i, 0)),
            pl.BlockSpec((tm, half), lambda j, i: (i, 0)),
            pl.BlockSpec((2 * half, tn), lambda j, i: (0, j)),
            tile,
        ],
        out_specs=[tile, tile],
        out_shape=[jax.ShapeDtypeStruct((T, D), F32), jax.ShapeDtypeStruct((T, D), BF16)],
        scratch_shapes=[pltpu.VMEM((2 * half, tn), BF16)],
        compiler_params=pltpu.CompilerParams(dimension_semantics=("arbitrary", "arbitrary"),
                                             vmem_limit_bytes=VMEM_LIMIT),
        name="outproj",
    )(y_ml, y_ns, w_out, x2)


def _ple_kernel(xb_ref, x_ref, p_ref, wg_ref, wp_ref, o_ref, wgb_ref, wpb_ref):
    @pl.when(pl.program_id(1) == 0)
    def _():
        wgb_ref[...] = wg_ref[...].astype(BF16)
        wpb_ref[...] = wp_ref[...].astype(BF16)

    gate = _sigmoid(_dot(xb_ref[...], wgb_ref[...]))
    emb = _dot(p_ref[...].astype(BF16), wpb_ref[...])
    o_ref[...] = x_ref[...] + gate * emb


def _ple(x1b, x1, p2, wg, wp, tm=256, tn=1024):
    T, D = x1.shape
    P = p2.shape[1]
    tile = pl.BlockSpec((tm, tn), lambda j, i: (i, j))
    return pl.pallas_call(
        _ple_kernel,
        grid=(D // tn, T // tm),
        in_specs=[
            pl.BlockSpec((tm, D), lambda j, i: (i, 0)),
            tile,
            pl.BlockSpec((tm, P), lambda j, i: (i, 0)),
            pl.BlockSpec((D, tn), lambda j, i: (0, j)),
            pl.BlockSpec((P, tn), lambda j, i: (0, j)),
        ],
        out_specs=tile,
        out_shape=jax.ShapeDtypeStruct((T, D), F32),
        scratch_shapes=[pltpu.VMEM((D, tn), BF16), pltpu.VMEM((P, tn), BF16)],
        compiler_params=pltpu.CompilerParams(dimension_semantics=("arbitrary", "arbitrary"),
                                             vmem_limit_bytes=VMEM_LIMIT),
        name="ple",
    )(x1b, x1, p2, wg, wp)


def _layer(x2, p2, norm_w, w_in, conv_w, i_bias, f_bias, head_norm_w, q_norm_w, k_norm_w,
           pe_k, pe_v, kw1, kw2, vw1, vw2, rel_bias, w_out, ple_proj, ple_gate, B, S):
    D = x2.shape[1]
    qkw = ML_HEADS * ML_DQK
    vw = ML_HEADS * ML_DV
    nq = NSA_HEADS * NSA_HD
    nkv = NSA_GROUPS * NSA_HD
    o_i = 2 * qkw + 3 * vw
    o_f = o_i + ML_HEADS
    o_nq = o_f + ML_HEADS
    o_g = o_nq + nq + 6 * nkv
    o_z = o_g + NSA_HEADS * N_BRANCH
    w_t = w_in.T
    ngate = NSA_HEADS * N_BRANCH
    zeros = lambda n: jnp.zeros((n, D), w_in.dtype)
    wg_t = jnp.concatenate([w_t[o_i:o_f], w_t[o_g:o_z], zeros(LANES - ML_HEADS - ngate),
                            w_t[o_f:o_nq], zeros(LANES - ML_HEADS)], axis=0)
    col_nq = o_i
    col_kv = col_nq + nq
    col_z = col_kv + 6 * nkv

    h, gates = _rmsnorm(x2, norm_w.reshape(1, D), wg_t)
    u = _inproj(h, w_t, [(0, o_i), (o_nq, o_g - o_nq), (o_z, nq)])

    pad_h = lambda v: jnp.concatenate([v, jnp.zeros((LANES - ML_HEADS,), v.dtype)]).reshape(1, LANES)
    y_ml = _mlstm(u, gates, 0, conv_w, pad_h(i_bias), pad_h(f_bias), head_norm_w, B, S)

    qn, ksa, vst, kwn, vwt = _nsa_prep(u, q_norm_w.reshape(1, NSA_HD), k_norm_w, col_nq, col_kv, B, S)
    kcmp, vcmp_t = _compress(u, pe_k, pe_v, kw1.astype(BF16), kw2.astype(BF16), vw1.astype(BF16),
                             vw2.astype(BF16), k_norm_w, col_kv, B, S)
    o_c, sel = _cmp_sel(rel_bias, qn, kcmp, vcmp_t, B, S)
    y_ns = _attend(rel_bias, qn, ksa, vst, kwn, vwt, sel, o_c, gates, 0, u, col_z, ML_HEADS, B, S)

    x1, x1b = _outproj(y_ml, y_ns, w_out, x2)
    return _ple(x1b, x1, p2, ple_gate, ple_proj)


def kernel(x, p, norm_w, w_in, ml_conv_w, ml_i_bias, ml_f_bias, ml_head_norm_w, nsa_q_norm_w, nsa_k_norm_w,
           cmp_pe_k, cmp_pe_v, cmp_k_w1, cmp_k_w2, cmp_v_w1, cmp_v_w2, rel_bias, w_out, ple_proj, ple_gate):
    B, S, D = x.shape
    assert S % max(WIN, 256) == 0 and S // SEL_LEN >= 1
    x2 = x.reshape(B * S, D)
    for layer in range(w_in.shape[0]):
        x2 = _layer(x2, p[layer].reshape(B * S, -1), norm_w[layer], w_in[layer], ml_conv_w[layer],
                    ml_i_bias[layer], ml_f_bias[layer], ml_head_norm_w[layer], nsa_q_norm_w[layer],
                    nsa_k_norm_w[layer], cmp_pe_k[layer], cmp_pe_v[layer], cmp_k_w1[layer], cmp_k_w2[layer],
                    cmp_v_w1[layer], cmp_v_w2[layer], rel_bias, w_out[layer], ple_proj[layer], ple_gate[layer],
                    B, S)
    return x2.reshape(B, S, D)
```

```python
import functools
import math

import numpy as np
import jax
import jax.numpy as jnp
from jax import lax
from jax.experimental import pallas as pl
from jax.experimental.pallas import tpu as pltpu

F32 = jnp.float32
BF16 = jnp.bfloat16
HIGHEST = lax.Precision.HIGHEST

ML_HEADS = 8
ML_DQK = 128
ML_DV = 256
ML_CHUNK = 64
ML_CONV = 4
NSA_HEADS = 16
NSA_HD = 128
NSA_GROUPS = 4
NSA_HPG = 4
N_BRANCH = 3
CMP_STRIDE = 16
CMP_LEN = 32
SEL_LEN = 64
SEL_SHIFT = 6
SEL_TOPK = 16
WIN = 512
REL_BUCKETS = 32
REL_MAX_DIST = 128
EPS = 1e-6
NEG = -1e30
FORCE_SCORE = 1e4
LOG2E = math.log2(math.e)

LANES = 128
VMEM_LIMIT = 56 * 1024 * 1024

ROWS_RMSNORM = 256
TILE_INPROJ = (512, 1024)
TILE_OUTPROJ = (256, 1024)
ROWS_MLSTM = 512
ROWS_NSA_PREP = 512


def _bucket_thresholds():
    n = np.arange(0, 4 * REL_MAX_DIST, dtype=np.int64)
    max_exact = REL_BUCKETS // 2
    nf = np.maximum(n, 1).astype(np.float32)
    large = max_exact + (np.log(nf / np.float32(max_exact)) / np.float32(math.log(REL_MAX_DIST / max_exact))
                         * np.float32(REL_BUCKETS - max_exact)).astype(np.int32)
    large = np.minimum(large, REL_BUCKETS - 1)
    bucket = np.where(n < max_exact, n, large)
    assert np.all(np.diff(bucket) >= 0)
    thr = [int(np.argmax(bucket >= b)) for b in range(REL_BUCKETS)]
    assert thr[REL_BUCKETS - 1] <= REL_MAX_DIST
    return thr


BUCKET_THR = _bucket_thresholds()


def _dot(a, b, precision=None):
    return jnp.dot(a, b, preferred_element_type=F32, precision=precision)


def _dot_nt(a, b, precision=None):
    return lax.dot_general(a, b, (((1,), (1,)), ((), ())), preferred_element_type=F32, precision=precision)


def _sigmoid_t(x):
    return 0.5 * jnp.tanh(0.5 * x) + 0.5


def _iota(shape, dim):
    return lax.broadcasted_iota(jnp.int32, shape, dim)


def _rel_bias_pattern(dist, tbl_ref, head):
    val = jnp.full(dist.shape, tbl_ref[0, head], F32)
    for b in range(1, REL_BUCKETS):
        val = jnp.where(dist >= BUCKET_THR[b], tbl_ref[b, head], val)
    return (val - tbl_ref[REL_BUCKETS - 1, head]) * LOG2E


def _rmsnorm_kernel(x_ref, nw_ref, wg_ref, h_ref, g_ref, wgb_ref):
    @pl.when(pl.program_id(0) == 0)
    def _():
        wgb_ref[...] = wg_ref[...].astype(BF16)

    x = x_ref[...]
    ms = jnp.mean(x * x, axis=-1, keepdims=True)
    h = (x * lax.rsqrt(ms + EPS) * nw_ref[...]).astype(BF16)
    h_ref[...] = h
    g_ref[...] = _dot_nt(h, wgb_ref[...])


def _rmsnorm(x2, norm_w, wg_t, rb=ROWS_RMSNORM):
    T, D = x2.shape
    NG = wg_t.shape[0]
    return pl.pallas_call(
        _rmsnorm_kernel,
        grid=(T // rb,),
        in_specs=[pl.BlockSpec((rb, D), lambda i: (i, 0)), pl.BlockSpec((1, D), lambda i: (0, 0)),
                  pl.BlockSpec((NG, D), lambda i: (0, 0))],
        out_specs=[pl.BlockSpec((rb, D), lambda i: (i, 0)), pl.BlockSpec((rb, NG), lambda i: (i, 0))],
        out_shape=[jax.ShapeDtypeStruct((T, D), BF16), jax.ShapeDtypeStruct((T, NG), F32)],
        scratch_shapes=[pltpu.VMEM((NG, D), BF16)],
        compiler_params=pltpu.CompilerParams(dimension_semantics=("arbitrary",), vmem_limit_bytes=VMEM_LIMIT),
        name="rmsnorm",
    )(x2, norm_w, wg_t)


def _inproj_kernel(h_ref, wt_ref, u_ref, wb_ref):
    @pl.when(pl.program_id(1) == 0)
    def _():
        wb_ref[...] = wt_ref[...].astype(BF16)

    u_ref[...] = _dot_nt(h_ref[...], wb_ref[...])


def _piece_blocks(pieces, tn):
    starts = []
    for start, length in pieces:
        assert length % tn == 0 and start % 8 == 0
        starts += [start + k * tn for k in range(length // tn)]
    bounds = [(b, s) for b, s in enumerate(starts) if b == 0 or s != starts[b - 1] + tn]

    def w_rows(j, i):
        row8 = (bounds[0][1] + j * tn) // 8
        for b, s in bounds[1:]:
            row8 = jnp.where(j >= b, (s + (j - b) * tn) // 8, row8)
        return row8 * 8, 0

    return len(starts), w_rows


def _inproj(h, w_t, pieces, tm=TILE_INPROJ[0], tn=TILE_INPROJ[1]):
    T, D = h.shape
    nblk, w_rows = _piece_blocks(pieces, tn)
    return pl.pallas_call(
        _inproj_kernel,
        grid=(nblk, T // tm),
        in_specs=[
            pl.BlockSpec((tm, D), lambda j, i: (i, 0)),
            pl.BlockSpec((pl.Element(tn), pl.Element(D)), w_rows),
        ],
        out_specs=pl.BlockSpec((tm, tn), lambda j, i: (i, j)),
        out_shape=jax.ShapeDtypeStruct((T, nblk * tn), F32),
        scratch_shapes=[pltpu.VMEM((tn, D), BF16)],
        compiler_params=pltpu.CompilerParams(dimension_semantics=("arbitrary", "arbitrary"),
                                             vmem_limit_bytes=VMEM_LIMIT),
        name="inproj",
    )(h, w_t)


def _mlstm_kernel(q_ref, k_ref, v_ref, o_ref, z_ref, g_ref, cw_ref, ib_ref, fb_ref, hw_ref, y_ref,
                  xbuf, c_ref, m_ref, *, lb):
    L = ML_CHUNK
    qkw = ML_HEADS * ML_DQK

    @pl.when(pl.program_id(1) == 0)
    def _():
        xbuf[0:8, :] = jnp.zeros((8, 2 * qkw), F32)
        c_ref[...] = jnp.zeros(c_ref.shape, F32)
        m_ref[...] = jnp.zeros(m_ref.shape, F32)

    row = _iota((L, L), 0)
    col = _iota((L, L), 1)
    tril = col <= row
    rowg = _iota((L, LANES), 0)
    eye_h = (_iota((8, LANES), 0) == _iota((8, LANES), 1)).astype(F32)
    ones_col = (_iota((L, LANES), 1) == 0).astype(BF16)
    heads = range(ML_HEADS)

    def chunk(c, carry):
        rows = pl.ds(pl.multiple_of(c * L, L), L)
        qk_cols = lambda h: slice(h * ML_DQK, (h + 1) * ML_DQK)
        v_cols = lambda h: slice(h * ML_DV, (h + 1) * ML_DV)
        col_of = lambda x, h: x[:, h:h + 1]

        G = g_ref[rows, :]
        li = G[:, 0:LANES] + ib_ref[...]
        fp = G[:, LANES:2 * LANES] + fb_ref[...]
        lf = jnp.minimum(fp, 0.0) - jnp.log(1.0 + jnp.exp(-jnp.abs(fp)))
        b = lf
        for sh in (1, 2, 4, 8, 16, 32):
            b = b + jnp.where(rowg >= sh, pltpu.roll(b, sh, 0), 0.0)
        g = li - b
        gmax = g
        for sh in (1, 2, 4, 8, 16, 32):
            gmax = jnp.maximum(gmax, jnp.where(rowg >= sh, pltpu.roll(gmax, sh, 0), NEG))
        m_old = m_ref[...]
        top = jnp.maximum(m_old, gmax)
        m_t = b + top
        inter = jnp.exp(m_old - top)
        floor = jnp.exp(-m_t)
        bL = b[L - 1:L, :]
        w = bL - b + li
        m_new = jnp.maximum(bL + m_old, jnp.max(w, axis=0, keepdims=True))
        wk = jnp.exp(w - m_new)
        decay = jnp.exp(bL + m_old - m_new)
        m_ref[...] = m_new
        g_rows = _dot_nt(eye_h, g, precision=HIGHEST)

        xbuf[8:8 + L, 0:qkw] = q_ref[rows, :]
        xbuf[8:8 + L, qkw:2 * qkw] = k_ref[rows, :]
        acc = cw_ref[ML_CONV - 1:ML_CONV, :] * xbuf[8:8 + L, :]
        for j in range(1, ML_CONV):
            acc = acc + cw_ref[ML_CONV - 1 - j:ML_CONV - j, :] * xbuf[8 - j:8 - j + L, :]
        act = acc * _sigmoid_t(acc)
        xbuf[0:8, :] = xbuf[L:L + 8, :]

        q = [act[:, qk_cols(h)].astype(BF16) for h in heads]
        kf = [act[:, qkw + h * ML_DQK:qkw + (h + 1) * ML_DQK] * (ML_DQK ** -0.5) for h in heads]
        vaug = [jnp.concatenate([v_ref[rows, v_cols(h)].astype(BF16), ones_col], axis=1) for h in heads]
        cst = [c_ref[h] for h in heads]
        lhs = []
        for h in heads:
            dw = jnp.exp(jnp.where(tril, g_rows[h:h + 1, :] - col_of(top, h), NEG))
            sc = (_dot_nt(q[h], kf[h].astype(BF16)) * dw).astype(BF16)
            q_in = (q[h].astype(F32) * col_of(inter, h)).astype(BF16)
            lhs.append(jnp.concatenate([q_in, sc], axis=1))
        num_aug = [_dot(lhs[h], jnp.concatenate([cst[h].astype(BF16), vaug[h]], axis=0)) for h in heads]
        for h in heads:
            kw = (kf[h] * col_of(wk, h)).astype(BF16)
            upd = lax.dot_general(kw, vaug[h], (((0,), (0,)), ((), ())), preferred_element_type=F32)
            c_ref[h] = decay[:, h:h + 1] * cst[h] + upd
        for h in heads:
            num = num_aug[h][:, 0:ML_DV]
            den = num_aug[h][:, ML_DV:ML_DV + 1]
            r = 1.0 / jnp.maximum(jnp.abs(den), col_of(floor, h))
            f = r * lax.rsqrt(r * r * jnp.mean(num * num, axis=-1, keepdims=True) + EPS)
            o = o_ref[rows, v_cols(h)]
            z = z_ref[rows, v_cols(h)]
            gate = _sigmoid_t(o) * z * _sigmoid_t(z)
            y_ref[rows, v_cols(h)] = (num * f * hw_ref[h:h + 1, :] * gate).astype(BF16)
        return carry

    lax.fori_loop(0, lb // L, chunk, 0)


def _mlstm(u, gates, gate_blk, conv_w, ib, fb, head_w, B, S, lb=ROWS_MLSTM):
    T = B * S
    nsb = S // lb
    qkw = ML_HEADS * ML_DQK
    vw = ML_HEADS * ML_DV
    rowmap = lambda col: (lambda b, s: (b * nsb + s, col))
    const = lambda b, s: (0, 0)
    return pl.pallas_call(
        functools.partial(_mlstm_kernel, lb=lb),
        grid=(B, nsb),
        in_specs=[
            pl.BlockSpec((lb, qkw), rowmap(0)),
            pl.BlockSpec((lb, qkw), rowmap(1)),
            pl.BlockSpec((lb, vw), rowmap(1)),
            pl.BlockSpec((lb, vw), rowmap(2)),
            pl.BlockSpec((lb, vw), rowmap(3)),
            pl.BlockSpec((lb, 2 * LANES), rowmap(gate_blk)),
            pl.BlockSpec((ML_CONV, 2 * qkw), const),
            pl.BlockSpec((1, LANES), const),
            pl.BlockSpec((1, LANES), const),
            pl.BlockSpec((ML_HEADS, ML_DV), const),
        ],
        out_specs=pl.BlockSpec((lb, vw), rowmap(0)),
        out_shape=jax.ShapeDtypeStruct((T, vw), BF16),
        scratch_shapes=[
            pltpu.VMEM((ML_CHUNK + 8, 2 * qkw), F32),
            pltpu.VMEM((ML_HEADS, ML_DQK, ML_DV + LANES), F32),
            pltpu.VMEM((1, LANES), F32),
        ],
        compiler_params=pltpu.CompilerParams(dimension_semantics=("arbitrary", "arbitrary"),
                                             vmem_limit_bytes=VMEM_LIMIT),
        name="mlstm",
    )(u, u, u, u, u, gates, conv_w, ib, fb, head_w)


def _rms_heads(x, w, scale=1.0):
    outs = []
    for h in range(x.shape[1] // NSA_HD):
        xh = x[:, h * NSA_HD:(h + 1) * NSA_HD]
        ms = jnp.mean(xh * xh, axis=-1, keepdims=True)
        outs.append(xh * lax.rsqrt(ms + EPS) * w * scale)
    return jnp.concatenate(outs, axis=1)


VT_ROWS = NSA_HD + 16


def _nsa_prep_kernel(q_ref, ks_ref, vs_ref, kw_ref, vw_ref, qw_ref, kn_ref,
                     qn_ref, ksa_ref, vst_ref, kwn_ref, vwt_ref, *, rb, nsb):
    qn_ref[...] = _rms_heads(q_ref[...], qw_ref[...], NSA_HD ** -0.5 * LOG2E).astype(BF16)
    ksn = _rms_heads(ks_ref[...], kn_ref[1:2, :]).astype(BF16)
    kwn_ref[...] = _rms_heads(kw_ref[...], kn_ref[2:3, :]).astype(BF16)
    t = (pl.program_id(0) % nsb) * rb + _iota((rb, NSA_HD), 0)
    onehot = (_iota((rb, NSA_HD), 1) == jnp.right_shift(t, SEL_SHIFT)).astype(BF16)
    tail = (_iota((VT_ROWS - NSA_HD, rb), 0) == 0).astype(BF16)
    for g in range(NSA_GROUPS):
        cols = slice(g * NSA_HD, (g + 1) * NSA_HD)
        ksa_ref[:, 2 * g * NSA_HD:(2 * g + 1) * NSA_HD] = ksn[:, cols]
        ksa_ref[:, (2 * g + 1) * NSA_HD:(2 * g + 2) * NSA_HD] = onehot
        vst_ref[g] = jnp.concatenate([vs_ref[:, cols].T.astype(BF16), tail], axis=0)
        vwt_ref[g] = jnp.concatenate([vw_ref[:, cols].T.astype(BF16), tail], axis=0)


def _nsa_prep(u, q_norm_w, k_norm_w, col_q, col_kv, B, S, rb=ROWS_NSA_PREP):
    T = u.shape[0]
    nsb = S // rb
    qw = NSA_HEADS * NSA_HD
    kvw = NSA_GROUPS * NSA_HD
    cq = col_q // qw
    ck = col_kv // kvw
    assert S // SEL_LEN <= NSA_HD
    kv_spec = lambda idx: pl.BlockSpec((rb, kvw), lambda i: (i, ck + idx))
    out_k = pl.BlockSpec((rb, kvw), lambda i: (i, 0))
    out_ka = pl.BlockSpec((rb, 2 * kvw), lambda i: (i, 0))
    out_vt = pl.BlockSpec((None, NSA_GROUPS, VT_ROWS, rb), lambda i: (i // nsb, 0, 0, i % nsb))
    vt_shape = jax.ShapeDtypeStruct((B, NSA_GROUPS, VT_ROWS, S), BF16)
    return pl.pallas_call(
        functools.partial(_nsa_prep_kernel, rb=rb, nsb=nsb),
        grid=(T // rb,),
        in_specs=[
            pl.BlockSpec((rb, qw), lambda i: (i, cq)),
            kv_spec(2), kv_spec(3), kv_spec(4), kv_spec(5),
            pl.BlockSpec((1, NSA_HD), lambda i: (0, 0)),
            pl.BlockSpec((N_BRANCH, NSA_HD), lambda i: (0, 0)),
        ],
        out_specs=[pl.BlockSpec((rb, qw), lambda i: (i, 0)), out_ka, out_vt, out_k, out_vt],
        out_shape=[jax.ShapeDtypeStruct((T, qw), BF16), jax.ShapeDtypeStruct((T, 2 * kvw), BF16), vt_shape,
                   jax.ShapeDtypeStruct((T, kvw), BF16), vt_shape],
        compiler_params=pltpu.CompilerParams(dimension_semantics=("arbitrary",), vmem_limit_bytes=VMEM_LIMIT),
        name="nsa_prep",
    )(u, u, u, u, u, q_norm_w, k_norm_w)


def _compress_kernel(kc_ref, vc_ref, pek_ref, pev_ref, kw1_ref, kw2_ref, vw1_ref, vw2t_ref, kn_ref,
                     kcmp_ref, vcmp_ref, *, ns):
    def hidden(x_ref, pe_ref, w1_ref):
        hid = CMP_LEN // 2
        a = jnp.zeros((ns, w1_ref.shape[1]), F32)
        bsum = jnp.zeros((ns, w1_ref.shape[1]), F32)
        for l in range(0, hid, 2):
            xl = [x_ref[pl.ds(l + t, ns, stride=CMP_STRIDE), :] for t in range(2)]
            first = jnp.concatenate([(xl[t] + pe_ref[l + t:l + t + 1, :]).astype(BF16) for t in range(2)], axis=1)
            second = jnp.concatenate([(xl[t] + pe_ref[hid + l + t:hid + l + t + 1, :]).astype(BF16)
                                      for t in range(2)], axis=1)
            a = a + _dot(first, w1_ref[l * NSA_HD:(l + 2) * NSA_HD, :])
            bsum = bsum + _dot(second, w1_ref[(hid + l) * NSA_HD:(hid + l + 2) * NSA_HD, :])
        pre = a + pltpu.roll(bsum, ns - 1, 0)
        return (pre * _sigmoid_t(pre)).astype(BF16)

    kc = _dot(hidden(kc_ref, pek_ref, kw1_ref), kw2_ref[...])
    ms = jnp.mean(kc * kc, axis=-1, keepdims=True)
    kcmp_ref[...] = (kc * lax.rsqrt(ms + EPS) * kn_ref[0:1, :]).astype(BF16)
    vcmp_ref[...] = _dot_nt(vw2t_ref[...], hidden(vc_ref, pev_ref, vw1_ref)).astype(BF16)


def _compress(u, pe_k, pe_v, kw1, kw2, vw1, vw2, k_norm_w, col_kv, B, S):
    ns = S // CMP_STRIDE
    ck = col_kv // NSA_HD
    hidden = kw1.shape[1]
    const = lambda b, g: (0, 0)
    return pl.pallas_call(
        functools.partial(_compress_kernel, ns=ns),
        grid=(B, NSA_GROUPS),
        in_specs=[
            pl.BlockSpec((S, NSA_HD), lambda b, g: (b, ck + g)),
            pl.BlockSpec((S, NSA_HD), lambda b, g: (b, ck + NSA_GROUPS + g)),
            pl.BlockSpec((CMP_LEN, NSA_HD), const),
            pl.BlockSpec((CMP_LEN, NSA_HD), const),
            pl.BlockSpec((CMP_LEN * NSA_HD, hidden), const),
            pl.BlockSpec((hidden, NSA_HD), const),
            pl.BlockSpec((CMP_LEN * NSA_HD, hidden), const),
            pl.BlockSpec((NSA_HD, hidden), const),
            pl.BlockSpec((N_BRANCH, NSA_HD), const),
        ],
        out_specs=[pl.BlockSpec((None, None, ns, NSA_HD), lambda b, g: (b, g, 0, 0)),
                   pl.BlockSpec((None, None, NSA_HD, ns), lambda b, g: (b, g, 0, 0))],
        out_shape=[jax.ShapeDtypeStruct((B, NSA_GROUPS, ns, NSA_HD), BF16),
                   jax.ShapeDtypeStruct((B, NSA_GROUPS, NSA_HD, ns), BF16)],
        compiler_params=pltpu.CompilerParams(dimension_semantics=("arbitrary", "arbitrary"),
                                             vmem_limit_bytes=VMEM_LIMIT),
        name="compress",
    )(u, u, pe_k, pe_v, kw1, kw2, vw1, vw2.T, k_norm_w)


TQC = 256
CMP_TILES = 4
NEAR = 32
NEAR_BACK = 16


def _cmp_sel_kernel(tbl_ref, q_ref, kc_ref, vct_ref, oc_ref, sel_ref, pat_ref, *, ns, nsel):
    g = pl.program_id(1)
    i = pl.program_id(2)
    t0 = i * (CMP_TILES * TQC)
    c0 = i * (CMP_TILES * TQC // CMP_STRIDE)
    W = NSA_HPG * TQC
    assert (TQC - CMP_LEN) // CMP_STRIDE < NEAR - NEAR_BACK and NEAR_BACK * CMP_STRIDE >= REL_MAX_DIST + CMP_LEN

    @pl.when(i == 0)
    def _():
        r = _iota((TQC, LANES), 0)
        lane = _iota((TQC, LANES), 1)
        d = r - CMP_STRIDE * (jnp.bitwise_and(lane, NEAR - 1) - NEAR_BACK) - (CMP_LEN - 1)
        for h in range(NSA_HPG):
            val = jnp.where(d < 0, NEG, _rel_bias_pattern(d, tbl_ref, g * NSA_HPG + h))
            hi = val.astype(BF16).astype(F32)
            lo = jnp.where(d < 0, 0.0, val - hi)
            ext = jnp.where(lane < NEAR, hi, jnp.where(lane < 2 * NEAR, lo, jnp.where(lane == 2 * NEAR, NEG, 0.0)))
            pat_ref[h * TQC:(h + 1) * TQC, :] = ext.astype(BF16)

    tiles_q = range(CMP_TILES)
    qrows = lambda tq: slice(tq * TQC, (tq + 1) * TQC)
    lane = _iota((ns, LANES), 1)
    s = []
    for tq in tiles_q:
        q4 = q_ref[qrows(tq), :]
        q_aug = jnp.concatenate([jnp.concatenate([q4[:, h * NSA_HD:(h + 1) * NSA_HD],
                                                  pat_ref[h * TQC:(h + 1) * TQC, :]], axis=1)
                                 for h in range(NSA_HPG)], axis=0)
        rel = _iota((ns, LANES), 0) - (c0 + tq * (TQC // CMP_STRIDE) - NEAR_BACK)
        ext = ((lane < 2 * NEAR) & (rel == jnp.bitwise_and(lane, NEAR - 1))) | ((lane == 2 * NEAR) & (rel >= NEAR))
        k_aug = jnp.concatenate([kc_ref[...], jnp.where(ext, 1.0, 0.0).astype(BF16)], axis=1)
        s.append(_dot_nt(k_aug, q_aug))

    e, inv = [], []
    for tq in tiles_q:
        m = jnp.max(s[tq], axis=0, keepdims=True)
        e.append(jnp.exp2(s[tq] - m))
        l = jnp.sum(e[tq], axis=0, keepdims=True)
        t_abs = t0 + tq * TQC + jnp.bitwise_and(_iota((1, W), 1), TQC - 1)
        inv.append(jnp.where(t_abs >= CMP_LEN - 1, 1.0 / l, 0.0))

    ratio = SEL_LEN // CMP_STRIDE
    blk_n = _iota((nsel, ns), 0)
    tok_c = _iota((nsel, ns), 1)
    ov = ((tok_c >= ratio * blk_n - (CMP_LEN // CMP_STRIDE - 1)) & (tok_c <= ratio * blk_n + ratio - 1)).astype(F32)
    blk = _iota((nsel, TQC), 0)
    imp, cur = [], []
    for tq in tiles_q:
        o_t = _dot(vct_ref[...], e[tq].astype(BF16)) * inv[tq]
        for h in range(NSA_HPG):
            oc_ref[qrows(tq), h * NSA_HD:(h + 1) * NSA_HD] = o_t[:, h * TQC:(h + 1) * TQC].T
        p = e[tq] * inv[tq]
        ps = p[:, 0:TQC]
        for h in range(1, NSA_HPG):
            ps = ps + p[:, h * TQC:(h + 1) * TQC]
        v = _dot(ov, ps, precision=HIGHEST)
        c = jnp.right_shift(t0 + tq * TQC + _iota((nsel, TQC), 1), SEL_SHIFT)
        v = jnp.where((blk == 0) | (blk == c) | (blk == c - 1), FORCE_SCORE, v)
        imp.append(jnp.where(blk > c, NEG, v))
        cur.append(c)
    k_top = min(SEL_TOPK, nsel)

    def emit(tq, chosen):
        chosen = jnp.concatenate([chosen.astype(BF16), jnp.zeros((LANES - nsel, TQC), BF16)], axis=0)
        eye_q = (_iota((TQC, TQC), 0) == _iota((TQC, TQC), 1)).astype(BF16)
        sel_ref[qrows(tq), :] = ((_dot_nt(eye_q, chosen) - 1.0) * (-NEG)).astype(BF16)

    few = (i + 1) * CMP_TILES * (TQC // SEL_LEN) <= k_top

    @pl.when(few)
    def _():
        for tq in tiles_q:
            emit(tq, jnp.where(blk <= cur[tq], 1.0, 0.0))

    @pl.when(jnp.logical_not(few))
    def _():
        sub = _iota((8, TQC), 0)
        for tq in tiles_q:
            tiles = [imp[tq][8 * k:8 * k + 8, :] for k in range(nsel // 8)]
            counts = [jnp.zeros((8, TQC), jnp.int32) for _ in tiles]
            for jb in range(nsel):
                vj = imp[tq][jb:jb + 1, :]
                for k, tile in enumerate(tiles):
                    if 8 * k > jb:
                        ahead = vj >= tile
                    elif 8 * k + 7 < jb:
                        ahead = vj > tile
                    else:
                        ahead = (vj > tile) | ((vj == tile) & (sub + 8 * k > jb))
                    counts[k] = counts[k] + jnp.where(ahead, 1, 0)
            rank = jnp.concatenate(counts, axis=0)
            emit(tq, jnp.where(rank < k_top, 1.0, 0.0))


def _cmp_sel(rel_bias, qn, kcmp, vcmp_t, B, S):
    ns = S // CMP_STRIDE
    nsel = S // SEL_LEN
    rows = CMP_TILES * TQC
    nt = S // rows
    gw = NSA_HPG * NSA_HD
    assert nsel % 8 == 0 and nsel <= LANES
    return pl.pallas_call(
        functools.partial(_cmp_sel_kernel, ns=ns, nsel=nsel),
        grid=(B, NSA_GROUPS, nt),
        in_specs=[
            pl.BlockSpec(memory_space=pltpu.SMEM),
            pl.BlockSpec((rows, gw), lambda b, g, i: (b * nt + i, g)),
            pl.BlockSpec((None, None, ns, NSA_HD), lambda b, g, i: (b, g, 0, 0)),
            pl.BlockSpec((None, None, NSA_HD, ns), lambda b, g, i: (b, g, 0, 0)),
        ],
        out_specs=[
            pl.BlockSpec((rows, gw), lambda b, g, i: (b * nt + i, g)),
            pl.BlockSpec((None, None, rows, LANES), lambda b, g, i: (b, g, i, 0)),
        ],
        out_shape=[jax.ShapeDtypeStruct((B * S, NSA_HEADS * NSA_HD), F32),
                   jax.ShapeDtypeStruct((B, NSA_GROUPS, S, LANES), BF16)],
        scratch_shapes=[pltpu.VMEM((NSA_HPG * TQC, LANES), BF16)],
        compiler_params=pltpu.CompilerParams(dimension_semantics=("arbitrary", "arbitrary", "arbitrary"),
                                             vmem_limit_bytes=VMEM_LIMIT),
        name="cmp_sel",
    )(rel_bias, qn, kcmp, vcmp_t)


TA = 512
SUB = 128


def _attend_kernel(tbl_ref, q_ref, ksa_ref, vst_ref, kw_ref, vwt_ref, sel_ref, oc_ref, g_ref, z_ref, y_ref,
                   pd_ref, pp_ref, sa_ref, sb_ref, ms_ref, accs_ref, mw_ref, accw_ref, gt_ref, *, gate_col):
    g = pl.program_id(1)
    i = pl.program_id(2)
    nsub = TA // SUB

    krow = _iota((SUB, SUB), 0)
    qcol = _iota((SUB, SUB), 1)

    @pl.when(i == 0)
    def _():
        for h in range(NSA_HPG):
            head = g * NSA_HPG + h
            lanes = slice(h * SUB, (h + 1) * SUB)
            pd_ref[:, lanes] = jnp.where(krow <= qcol, _rel_bias_pattern(qcol - krow, tbl_ref, head), NEG)
            pp_ref[:, lanes] = _rel_bias_pattern(qcol - krow + SUB, tbl_ref, head)

    SLAB = NSA_HPG * SUB
    q4 = q_ref[...]
    sel = sel_ref[...]
    q_win = jnp.concatenate([q4[qb * SUB:(qb + 1) * SUB, h * NSA_HD:(h + 1) * NSA_HD]
                             for qb in range(nsub) for h in range(NSA_HPG)], axis=0)
    q_sel = jnp.concatenate([jnp.concatenate([q4[qb * SUB:(qb + 1) * SUB, h * NSA_HD:(h + 1) * NSA_HD],
                                              sel[qb * SUB:(qb + 1) * SUB, :]], axis=1)
                             for qb in range(nsub) for h in range(NSA_HPG)], axis=0)
    strict = jnp.concatenate([jnp.where(krow > qcol, 0.0, NEG)] * NSA_HPG, axis=1)

    def chunk(c):
        return pl.ds(pl.multiple_of(c * TA, TA), TA)

    def keys(c, lo, hi):
        return pl.ds(pl.multiple_of(c * TA + lo * SUB, SUB), (hi - lo) * SUB)

    def rows(kb):
        return slice(kb * SUB, (kb + 1) * SUB)

    def slab(qb):
        return slice(qb * SLAB, (qb + 1) * SLAB)

    def absorb(s_ref, vt_ref, c, m_ref, acc_ref, first=False, lo=0, hi=nsub, qb=None, have_max=False):
        lanes = slice(None) if qb is None else slab(qb)
        krows = slice(lo * SUB, hi * SUB)
        mx = s_ref[TA:TA + 1, lanes] if have_max else jnp.max(s_ref[krows, lanes], axis=0, keepdims=True)
        if first:
            m_new = mx
        else:
            m_old = m_ref[:, lanes]
            m_new = jnp.maximum(m_old, mx)
        p = jnp.exp2(s_ref[krows, lanes] - m_new).astype(BF16)
        pv = _dot(vt_ref[:, keys(c, lo, hi)], p)
        if first:
            acc_ref[:, lanes] = pv
        else:
            acc_ref[:, lanes] = jnp.exp2(m_old - m_new) * acc_ref[:, lanes] + pv
        m_ref[:, lanes] = m_new

    def qk_sel(s_ref, c, with_max=False):
        s = _dot_nt(ksa_ref[chunk(c), :], q_sel)
        s_ref[0:TA, :] = s
        if with_max:
            s_ref[TA:TA + 1, :] = jnp.max(s, axis=0, keepdims=True)

    def diag(s_ref, k_ref, q_all, vt_ref, m_ref, acc_ref):
        for qb in range(nsub):
            s_ref[0:(qb + 1) * SUB, slab(qb)] = _dot_nt(k_ref[keys(i, 0, qb + 1), :], q_all[slab(qb), :])
            s_ref[rows(qb), slab(qb)] += pd_ref[...]
            if qb >= 1:
                s_ref[rows(qb - 1), slab(qb)] += pp_ref[...]
        for qb in range(nsub):
            absorb(s_ref, vt_ref, i, m_ref, acc_ref, first=True, lo=0, hi=qb + 1, qb=qb)

    def win_prev(s_ref):
        for qb in range(nsub):
            s_ref[qb * SUB:TA, slab(qb)] = _dot_nt(kw_ref[keys(i - 1, qb, nsub), :], q_win[slab(qb), :])
            s_ref[rows(qb), slab(qb)] += strict
        s_ref[rows(nsub - 1), slab(0)] += pp_ref[...]
        for qb in range(nsub):
            absorb(s_ref, vwt_ref, i - 1, mw_ref, accw_ref, lo=qb, hi=nsub, qb=qb)

    @pl.when(i == 0)
    def _():
        diag(sa_ref, ksa_ref, q_sel, vst_ref, ms_ref, accs_ref)
        diag(sb_ref, kw_ref, q_win, vwt_ref, mw_ref, accw_ref)

    @pl.when(i >= 1)
    def _():
        qk_sel(sb_ref, i - 1)
        sb_ref[rows(nsub - 1), slab(0)] += pp_ref[...]
        diag(sa_ref, ksa_ref, q_sel, vst_ref, ms_ref, accs_ref)
        absorb(sb_ref, vst_ref, i - 1, ms_ref, accs_ref)
        diag(sa_ref, kw_ref, q_win, vwt_ref, mw_ref, accw_ref)
        win_prev(sb_ref)

    nfar = jnp.maximum(i - 1, 0)
    odd = nfar % 2

    @pl.when(odd == 1)
    def _():
        qk_sel(sa_ref, 0, with_max=True)
        absorb(sa_ref, vst_ref, 0, ms_ref, accs_ref, have_max=True)

    npair = nfar // 2

    @pl.when(npair > 0)
    def _():
        qk_sel(sa_ref, odd, with_max=True)

    def pair(t, carry):
        c = odd + 2 * t
        qk_sel(sb_ref, c + 1, with_max=True)
        absorb(sa_ref, vst_ref, c, ms_ref, accs_ref, have_max=True)
        qk_sel(sa_ref, jnp.minimum(c + 2, nfar - 1), with_max=True)
        absorb(sb_ref, vst_ref, c + 1, ms_ref, accs_ref, have_max=True)
        return carry

    lax.fori_loop(0, npair, pair, 0)

    gs = _sigmoid_t(g_ref[:, 0:LANES])
    gt_ref[...] = gs.T
    inv_s = 1.0 / accs_ref[NSA_HD:NSA_HD + 1, :]
    inv_w = 1.0 / accw_ref[NSA_HD:NSA_HD + 1, :]
    lane = _iota((TA, LANES), 1)
    for h in range(NSA_HPG):
        base = gate_col + (g * NSA_HPG + h) * N_BRANCH
        cols = slice(h * NSA_HD, (h + 1) * NSA_HD)
        gate_c = jnp.sum(jnp.where(lane == base, gs, 0.0), axis=-1, keepdims=True)
        gs_row = gt_ref[pl.ds(base + 1, 1), :]
        gw_row = gt_ref[pl.ds(base + 2, 1), :]
        o_sw = []
        for qb in range(nsub):
            lanes = slice(qb * SLAB + h * SUB, qb * SLAB + (h + 1) * SUB)
            gate_s = gs_row[:, rows(qb)] * inv_s[:, lanes]
            gate_w = gw_row[:, rows(qb)] * inv_w[:, lanes]
            o_sw.append((gate_s * accs_ref[0:NSA_HD, lanes] + gate_w * accw_ref[0:NSA_HD, lanes]).T)
        o = gate_c * oc_ref[:, cols] + jnp.concatenate(o_sw, axis=0)
        z = z_ref[:, cols]
        y_ref[:, cols] = (o * z * _sigmoid_t(z)).astype(BF16)


def _attend(rel_bias, qn, ksa, vst, kwn, vwt, sel, o_c, gates, gate_blk, u, col_z, gate_col, B, S):
    nt = S // TA
    gw = NSA_HPG * NSA_HD
    W = NSA_HPG * TA
    cz = col_z // gw
    assert WIN == TA
    tile = lambda b, g, i: (b * nt + i, g)
    vt_spec = pl.BlockSpec((None, None, VT_ROWS, S), lambda b, g, i: (b, g, 0, 0))
    return pl.pallas_call(
        functools.partial(_attend_kernel, gate_col=gate_col),
        grid=(B, NSA_GROUPS, nt),
        in_specs=[
            pl.BlockSpec(memory_space=pltpu.SMEM),
            pl.BlockSpec((TA, gw), tile),
            pl.BlockSpec((S, 2 * NSA_HD), lambda b, g, i: (b, g)), vt_spec,
            pl.BlockSpec((S, NSA_HD), lambda b, g, i: (b, g)), vt_spec,
            pl.BlockSpec((None, None, TA, LANES), lambda b, g, i: (b, g, i, 0)),
            pl.BlockSpec((TA, gw), tile),
            pl.BlockSpec((TA, 2 * LANES), lambda b, g, i: (b * nt + i, gate_blk)),
            pl.BlockSpec((TA, gw), lambda b, g, i: (b * nt + i, cz + g)),
        ],
        out_specs=pl.BlockSpec((TA, gw), tile),
        out_shape=jax.ShapeDtypeStruct((B * S, NSA_HEADS * NSA_HD), BF16),
        scratch_shapes=[pltpu.VMEM((SUB, NSA_HPG * SUB), F32), pltpu.VMEM((SUB, NSA_HPG * SUB), F32),
                        pltpu.VMEM((TA + 8, W), F32), pltpu.VMEM((TA + 8, W), F32),
                        pltpu.VMEM((1, W), F32), pltpu.VMEM((VT_ROWS, W), F32),
                        pltpu.VMEM((1, W), F32), pltpu.VMEM((VT_ROWS, W), F32),
                        pltpu.VMEM((LANES, TA), F32)],
        compiler_params=pltpu.CompilerParams(dimension_semantics=("arbitrary", "arbitrary", "arbitrary"),
                                             vmem_limit_bytes=VMEM_LIMIT),
        name="attend",
    )(rel_bias, qn, ksa, vst, kwn, vwt, sel, o_c, gates, u)


def _outproj_kernel(yml_ref, yns_ref, w_ref, x_ref, o_ref, ob_ref, wb_ref):
    @pl.when(pl.program_id(1) == 0)
    def _():
        wb_ref[...] = w_ref[...].astype(BF16)

    half = yml_ref.shape[1]
    o = x_ref[...] + _dot(yml_ref[...], wb_ref[0:half, :]) + _dot(yns_ref[...], wb_ref[half:2 * half, :])
    o_ref[...] = o
    ob_ref[...] = o.astype(BF16)


def _outproj(y_ml, y_ns, w_out, x2, tm=TILE_OUTPROJ[0], tn=TILE_OUTPROJ[1]):
    T, D = x2.shape
    half = y_ml.shape[1]
    tile = pl.BlockSpec((tm, tn), lambda j, i: (i, j))
    return pl.pallas_call(
        _outproj_kernel,
        grid=(D // tn, T // tm),
        in_specs=[
            pl.BlockSpec((tm, half), lambda j, i: (i, 0)),
            pl.BlockSpec((tm, half), lambda j, i: (i, 0)),
            pl.BlockSpec((2 * half, tn), lambda j, i: (0, j)),
            tile,
        ],
        out_specs=[tile, tile],
        out_shape=[jax.ShapeDtypeStruct((T, D), F32), jax.ShapeDtypeStruct((T, D), BF16)],
        scratch_shapes=[pltpu.VMEM((2 * half, tn), BF16)],
        compiler_params=pltpu.CompilerParams(dimension_semantics=("arbitrary", "arbitrary"),
                                             vmem_limit_bytes=VMEM_LIMIT),
        name="outproj",
    )(y_ml, y_ns, w_out, x2)


def _ple_kernel(xb_ref, x_ref, p_ref, wg_ref, wp_ref, o_ref, wgb_ref, wpb_ref):
    @pl.when(pl.program_id(1) == 0)
    def _():
        wgb_ref[...] = wg_ref[...].astype(BF16)
        wpb_ref[...] = wp_ref[...].astype(BF16)

    gate = _sigmoid_t(_dot(xb_ref[...], wgb_ref[...]))
    emb = _dot(p_ref[...].astype(BF16), wpb_ref[...])
    o_ref[...] = x_ref[...] + gate * emb


def _ple(x1b, x1, p2, wg, wp, tm=TILE_OUTPROJ[0], tn=TILE_OUTPROJ[1]):
    T, D = x1.shape
    P = p2.shape[1]
    tile = pl.BlockSpec((tm, tn), lambda j, i: (i, j))
    return pl.pallas_call(
        _ple_kernel,
        grid=(D // tn, T // tm),
        in_specs=[
            pl.BlockSpec((tm, D), lambda j, i: (i, 0)),
            tile,
            pl.BlockSpec((tm, P), lambda j, i: (i, 0)),
            pl.BlockSpec((D, tn), lambda j, i: (0, j)),
            pl.BlockSpec((P, tn), lambda j, i: (0, j)),
        ],
        out_specs=tile,
        out_shape=jax.ShapeDtypeStruct((T, D), F32),
        scratch_shapes=[pltpu.VMEM((D, tn), BF16), pltpu.VMEM((P, tn), BF16)],
        compiler_params=pltpu.CompilerParams(dimension_semantics=("arbitrary", "arbitrary"),
                                             vmem_limit_bytes=VMEM_LIMIT),
        name="ple",
    )(x1b, x1, p2, wg, wp)


def _layer(x2, p2, norm_w, w_in, conv_w, i_bias, f_bias, head_norm_w, q_norm_w, k_norm_w,
           pe_k, pe_v, kw1, kw2, vw1, vw2, rel_bias, w_out, ple_proj, ple_gate, B, S):
    D = x2.shape[1]
    qkw = ML_HEADS * ML_DQK
    vw = ML_HEADS * ML_DV
    nq = NSA_HEADS * NSA_HD
    nkv = NSA_GROUPS * NSA_HD
    o_i = 2 * qkw + 3 * vw
    o_f = o_i + ML_HEADS
    o_nq = o_f + ML_HEADS
    o_g = o_nq + nq + 6 * nkv
    o_z = o_g + NSA_HEADS * N_BRANCH
    w_t = w_in.T
    ngate = NSA_HEADS * N_BRANCH
    zeros = lambda n: jnp.zeros((n, D), w_in.dtype)
    wg_t = jnp.concatenate([w_t[o_i:o_f], w_t[o_g:o_z], zeros(LANES - ML_HEADS - ngate),
                            w_t[o_f:o_nq], zeros(LANES - ML_HEADS)], axis=0)
    col_nq = o_i
    col_kv = col_nq + nq
    col_z = col_kv + 6 * nkv

    h, gates = _rmsnorm(x2, norm_w.reshape(1, D), wg_t)
    u = _inproj(h, w_t, [(0, o_i), (o_nq, o_g - o_nq), (o_z, nq)])

    pad_h = lambda v: jnp.concatenate([v, jnp.zeros((LANES - ML_HEADS,), v.dtype)]).reshape(1, LANES)
    y_ml = _mlstm(u, gates, 0, conv_w, pad_h(i_bias), pad_h(f_bias), head_norm_w, B, S)

    qn, ksa, vst, kwn, vwt = _nsa_prep(u, q_norm_w.reshape(1, NSA_HD), k_norm_w, col_nq, col_kv, B, S)
    kcmp, vcmp_t = _compress(u, pe_k, pe_v, kw1.astype(BF16), kw2.astype(BF16), vw1.astype(BF16),
                             vw2.astype(BF16), k_norm_w, col_kv, B, S)
    o_c, sel = _cmp_sel(rel_bias, qn, kcmp, vcmp_t, B, S)
    y_ns = _attend(rel_bias, qn, ksa, vst, kwn, vwt, sel, o_c, gates, 0, u, col_z, ML_HEADS, B, S)

    x1, x1b = _outproj(y_ml, y_ns, w_out, x2)
    return _ple(x1b, x1, p2, ple_gate, ple_proj)


def kernel(x, p, norm_w, w_in, ml_conv_w, ml_i_bias, ml_f_bias, ml_head_norm_w, nsa_q_norm_w, nsa_k_norm_w,
           cmp_pe_k, cmp_pe_v, cmp_k_w1, cmp_k_w2, cmp_v_w1, cmp_v_w2, rel_bias, w_out, ple_proj, ple_gate):
    B, S, D = x.shape
    assert S % max(WIN, 256) == 0 and S // SEL_LEN >= 1
    x2 = x.reshape(B * S, D)
    for layer in range(w_in.shape[0]):
        x2 = _layer(x2, p[layer].reshape(B * S, -1), norm_w[layer], w_in[layer], ml_conv_w[layer],
                    ml_i_bias[layer], ml_f_bias[layer], ml_head_norm_w[layer], nsa_q_norm_w[layer],
                    nsa_k_norm_w[layer], cmp_pe_k[layer], cmp_pe_v[layer], cmp_k_w1[layer], cmp_k_w2[layer],
                    cmp_v_w1[layer], cmp_v_w2[layer], rel_bias, w_out[layer], ple_proj[layer], ple_gate[layer],
                    B, S)
    return x2.reshape(B, S, D)
```

```python
import functools
import math

import numpy as np
import jax
import jax.numpy as jnp
from jax import lax
from jax.experimental import pallas as pl
from jax.experimental.pallas import tpu as pltpu

F32 = jnp.float32
BF16 = jnp.bfloat16
HIGHEST = lax.Precision.HIGHEST

ML_HEADS = 8
ML_DQK = 128
ML_DV = 256
ML_CHUNK = 64
ML_CONV = 4
NSA_HEADS = 16
NSA_HD = 128
NSA_GROUPS = 4
NSA_HPG = 4
N_BRANCH = 3
CMP_STRIDE = 16
CMP_LEN = 32
SEL_LEN = 64
SEL_SHIFT = 6
SEL_TOPK = 16
WIN = 512
REL_BUCKETS = 32
REL_MAX_DIST = 128
EPS = 1e-6
NEG = -1e30
FORCE_SCORE = 1e4
LOG2E = math.log2(math.e)

LANES = 128
VMEM_LIMIT = 56 * 1024 * 1024

ROWS_RMSNORM = 256
TILE_INPROJ = (1024, 1024)
TILE_OUTPROJ = (512, 1024)
ROWS_MLSTM = 512
ROWS_NSA_PREP = 512


def _bucket_thresholds():
    n = np.arange(0, 4 * REL_MAX_DIST, dtype=np.int64)
    max_exact = REL_BUCKETS // 2
    nf = np.maximum(n, 1).astype(np.float32)
    large = max_exact + (np.log(nf / np.float32(max_exact)) / np.float32(math.log(REL_MAX_DIST / max_exact))
                         * np.float32(REL_BUCKETS - max_exact)).astype(np.int32)
    large = np.minimum(large, REL_BUCKETS - 1)
    bucket = np.where(n < max_exact, n, large)
    assert np.all(np.diff(bucket) >= 0)
    thr = [int(np.argmax(bucket >= b)) for b in range(REL_BUCKETS)]
    assert thr[REL_BUCKETS - 1] <= REL_MAX_DIST
    return thr


BUCKET_THR = _bucket_thresholds()


def _dot(a, b, precision=None):
    return jnp.dot(a, b, preferred_element_type=F32, precision=precision)


def _dot_nt(a, b, precision=None):
    return lax.dot_general(a, b, (((1,), (1,)), ((), ())), preferred_element_type=F32, precision=precision)


def _sigmoid_t(x):
    return 0.5 * jnp.tanh(0.5 * x) + 0.5


def _iota(shape, dim):
    return lax.broadcasted_iota(jnp.int32, shape, dim)


def _rel_bias_pattern(dist, tbl_ref, head):
    val = jnp.full(dist.shape, tbl_ref[0, head], F32)
    for b in range(1, REL_BUCKETS):
        val = jnp.where(dist >= BUCKET_THR[b], tbl_ref[b, head], val)
    return (val - tbl_ref[REL_BUCKETS - 1, head]) * LOG2E


def _rmsnorm_kernel(x_ref, nw_ref, wg_ref, h_ref, g_ref, wgb_ref):
    @pl.when(pl.program_id(0) == 0)
    def _():
        wgb_ref[...] = wg_ref[...].astype(BF16)

    x = x_ref[...]
    ms = jnp.mean(x * x, axis=-1, keepdims=True)
    h = (x * lax.rsqrt(ms + EPS) * nw_ref[...]).astype(BF16)
    h_ref[...] = h
    g_ref[...] = _dot_nt(h, wgb_ref[...])


def _rmsnorm(x2, norm_w, wg_t, rb=ROWS_RMSNORM):
    T, D = x2.shape
    NG = wg_t.shape[0]
    return pl.pallas_call(
        _rmsnorm_kernel,
        grid=(T // rb,),
        in_specs=[pl.BlockSpec((rb, D), lambda i: (i, 0)), pl.BlockSpec((1, D), lambda i: (0, 0)),
                  pl.BlockSpec((NG, D), lambda i: (0, 0))],
        out_specs=[pl.BlockSpec((rb, D), lambda i: (i, 0)), pl.BlockSpec((rb, NG), lambda i: (i, 0))],
        out_shape=[jax.ShapeDtypeStruct((T, D), BF16), jax.ShapeDtypeStruct((T, NG), F32)],
        scratch_shapes=[pltpu.VMEM((NG, D), BF16)],
        compiler_params=pltpu.CompilerParams(dimension_semantics=("arbitrary",), vmem_limit_bytes=VMEM_LIMIT),
        name="rmsnorm",
    )(x2, norm_w, wg_t)


def _stage_weight_block(copy_of, j, nblk, wf_ref, wb_ref):
    @pl.when(j == 0)
    def _():
        copy_of(j).start()

    copy_of(j).wait()
    wb_ref[...] = wf_ref[...].astype(BF16)

    @pl.when(j + 1 < nblk)
    def _():
        copy_of(j + 1).start()


def _inproj_kernel(h_ref, wt_hbm, u_ref, wf_ref, wb_ref, sem, *, nblk, row_of):
    j = pl.program_id(0)

    def copy_of(jj):
        return pltpu.make_async_copy(wt_hbm.at[pl.ds(row_of(jj), wf_ref.shape[0]), :], wf_ref, sem)

    @pl.when(pl.program_id(1) == 0)
    def _():
        _stage_weight_block(copy_of, j, nblk, wf_ref, wb_ref)

    u_ref[...] = _dot_nt(h_ref[...], wb_ref[...])


def _piece_blocks(pieces, tn):
    starts = []
    for start, length in pieces:
        assert length % tn == 0 and start % 8 == 0
        starts += [start + k * tn for k in range(length // tn)]
    bounds = [(b, s) for b, s in enumerate(starts) if b == 0 or s != starts[b - 1] + tn]

    def w_rows(j, i):
        row8 = (bounds[0][1] + j * tn) // 8
        for b, s in bounds[1:]:
            row8 = jnp.where(j >= b, (s + (j - b) * tn) // 8, row8)
        return row8 * 8, 0

    return len(starts), w_rows


def _inproj(h, w_t, pieces, tm=TILE_INPROJ[0], tn=TILE_INPROJ[1]):
    T, D = h.shape
    nblk, w_rows = _piece_blocks(pieces, tn)
    return pl.pallas_call(
        functools.partial(_inproj_kernel, nblk=nblk, row_of=lambda j: w_rows(j, 0)[0]),
        grid=(nblk, T // tm),
        in_specs=[
            pl.BlockSpec((tm, D), lambda j, i: (i, 0)),
            pl.BlockSpec(memory_space=pl.ANY),
        ],
        out_specs=pl.BlockSpec((tm, tn), lambda j, i: (i, j)),
        out_shape=jax.ShapeDtypeStruct((T, nblk * tn), F32),
        scratch_shapes=[pltpu.VMEM((tn, D), F32), pltpu.VMEM((tn, D), BF16), pltpu.SemaphoreType.DMA(())],
        compiler_params=pltpu.CompilerParams(dimension_semantics=("arbitrary", "arbitrary"),
                                             vmem_limit_bytes=VMEM_LIMIT),
        name="inproj",
    )(h, w_t)


def _mlstm_kernel(q_ref, k_ref, v_ref, o_ref, z_ref, g_ref, cw_ref, ib_ref, fb_ref, hw_ref, y_ref,
                  xbuf, c_ref, m_ref, *, lb):
    L = ML_CHUNK
    qkw = ML_HEADS * ML_DQK

    @pl.when(pl.program_id(1) == 0)
    def _():
        xbuf[0:8, :] = jnp.zeros((8, 2 * qkw), F32)
        c_ref[...] = jnp.zeros(c_ref.shape, F32)
        m_ref[...] = jnp.zeros(m_ref.shape, F32)

    row = _iota((L, L), 0)
    col = _iota((L, L), 1)
    tril = col <= row
    rowg = _iota((L, LANES), 0)
    eye_h = (_iota((8, LANES), 0) == _iota((8, LANES), 1)).astype(F32)
    ones_col = (_iota((L, LANES), 1) == 0).astype(BF16)
    heads = range(ML_HEADS)

    def chunk(c, carry):
        rows = pl.ds(pl.multiple_of(c * L, L), L)
        qk_cols = lambda h: slice(h * ML_DQK, (h + 1) * ML_DQK)
        v_cols = lambda h: slice(h * ML_DV, (h + 1) * ML_DV)
        col_of = lambda x, h: x[:, h:h + 1]

        G = g_ref[rows, :]
        li = G[:, 0:LANES] + ib_ref[...]
        fp = G[:, LANES:2 * LANES] + fb_ref[...]
        lf = jnp.minimum(fp, 0.0) - jnp.log(1.0 + jnp.exp(-jnp.abs(fp)))
        b = lf
        for sh in (1, 2, 4, 8, 16, 32):
            b = b + jnp.where(rowg >= sh, pltpu.roll(b, sh, 0), 0.0)
        g = li - b
        gmax = g
        for sh in (1, 2, 4, 8, 16, 32):
            gmax = jnp.maximum(gmax, jnp.where(rowg >= sh, pltpu.roll(gmax, sh, 0), NEG))
        m_old = m_ref[...]
        top = jnp.maximum(m_old, gmax)
        m_t = b + top
        inter = jnp.exp(m_old - top)
        floor = jnp.exp(-m_t)
        bL = b[L - 1:L, :]
        w = bL - b + li
        m_new = jnp.maximum(bL + m_old, jnp.max(w, axis=0, keepdims=True))
        wk = jnp.exp(w - m_new)
        decay = jnp.exp(bL + m_old - m_new)
        m_ref[...] = m_new
        g_rows = _dot_nt(eye_h, g, precision=HIGHEST)

        xbuf[8:8 + L, 0:qkw] = q_ref[rows, :]
        xbuf[8:8 + L, qkw:2 * qkw] = k_ref[rows, :]
        acc = cw_ref[ML_CONV - 1:ML_CONV, :] * xbuf[8:8 + L, :]
        for j in range(1, ML_CONV):
            acc = acc + cw_ref[ML_CONV - 1 - j:ML_CONV - j, :] * xbuf[8 - j:8 - j + L, :]
        act = acc * _sigmoid_t(acc)
        xbuf[0:8, :] = xbuf[L:L + 8, :]

        q = [act[:, qk_cols(h)].astype(BF16) for h in heads]
        kf = [act[:, qkw + h * ML_DQK:qkw + (h + 1) * ML_DQK] * (ML_DQK ** -0.5) for h in heads]
        vaug = [jnp.concatenate([v_ref[rows, v_cols(h)].astype(BF16), ones_col], axis=1) for h in heads]
        cst = [c_ref[h] for h in heads]
        lhs = []
        for h in heads:
            dw = jnp.exp(jnp.where(tril, g_rows[h:h + 1, :] - col_of(top, h), NEG))
            sc = (_dot_nt(q[h], kf[h].astype(BF16)) * dw).astype(BF16)
            q_in = (q[h].astype(F32) * col_of(inter, h)).astype(BF16)
            lhs.append(jnp.concatenate([q_in, sc], axis=1))
        num_aug = [_dot(lhs[h], jnp.concatenate([cst[h].astype(BF16), vaug[h]], axis=0)) for h in heads]
        for h in heads:
            kw = (kf[h] * col_of(wk, h)).astype(BF16)
            upd = lax.dot_general(kw, vaug[h], (((0,), (0,)), ((), ())), preferred_element_type=F32)
            c_ref[h] = decay[:, h:h + 1] * cst[h] + upd
        for h in heads:
            num = num_aug[h][:, 0:ML_DV]
            den = num_aug[h][:, ML_DV:ML_DV + 1]
            r = 1.0 / jnp.maximum(jnp.abs(den), col_of(floor, h))
            f = r * lax.rsqrt(r * r * jnp.mean(num * num, axis=-1, keepdims=True) + EPS)
            o = o_ref[rows, v_cols(h)]
            z = z_ref[rows, v_cols(h)]
            gate = _sigmoid_t(o) * z * _sigmoid_t(z)
            y_ref[rows, v_cols(h)] = (num * f * hw_ref[h:h + 1, :] * gate).astype(BF16)
        return carry

    lax.fori_loop(0, lb // L, chunk, 0)


def _mlstm(u, gates, gate_blk, conv_w, ib, fb, head_w, B, S, lb=ROWS_MLSTM):
    T = B * S
    nsb = S // lb
    qkw = ML_HEADS * ML_DQK
    vw = ML_HEADS * ML_DV
    rowmap = lambda col: (lambda b, s: (b * nsb + s, col))
    const = lambda b, s: (0, 0)
    return pl.pallas_call(
        functools.partial(_mlstm_kernel, lb=lb),
        grid=(B, nsb),
        in_specs=[
            pl.BlockSpec((lb, qkw), rowmap(0)),
            pl.BlockSpec((lb, qkw), rowmap(1)),
            pl.BlockSpec((lb, vw), rowmap(1)),
            pl.BlockSpec((lb, vw), rowmap(2)),
            pl.BlockSpec((lb, vw), rowmap(3)),
            pl.BlockSpec((lb, 2 * LANES), rowmap(gate_blk)),
            pl.BlockSpec((ML_CONV, 2 * qkw), const),
            pl.BlockSpec((1, LANES), const),
            pl.BlockSpec((1, LANES), const),
            pl.BlockSpec((ML_HEADS, ML_DV), const),
        ],
        out_specs=pl.BlockSpec((lb, vw), rowmap(0)),
        out_shape=jax.ShapeDtypeStruct((T, vw), BF16),
        scratch_shapes=[
            pltpu.VMEM((ML_CHUNK + 8, 2 * qkw), F32),
            pltpu.VMEM((ML_HEADS, ML_DQK, ML_DV + LANES), F32),
            pltpu.VMEM((1, LANES), F32),
        ],
        compiler_params=pltpu.CompilerParams(dimension_semantics=("arbitrary", "arbitrary"),
                                             vmem_limit_bytes=VMEM_LIMIT),
        name="mlstm",
    )(u, u, u, u, u, gates, conv_w, ib, fb, head_w)


def _rms_heads(x, w, scale=1.0):
    outs = []
    for h in range(x.shape[1] // NSA_HD):
        xh = x[:, h * NSA_HD:(h + 1) * NSA_HD]
        ms = jnp.mean(xh * xh, axis=-1, keepdims=True)
        outs.append(xh * lax.rsqrt(ms + EPS) * w * scale)
    return jnp.concatenate(outs, axis=1)


VT_ROWS = NSA_HD + 16


def _nsa_prep_kernel(q_ref, ks_ref, vs_ref, kw_ref, vw_ref, qw_ref, kn_ref,
                     qn_ref, ksa_ref, vst_ref, kwn_ref, vwt_ref, *, rb, nsb):
    qn_ref[...] = _rms_heads(q_ref[...], qw_ref[...], NSA_HD ** -0.5 * LOG2E).astype(BF16)
    ksn = _rms_heads(ks_ref[...], kn_ref[1:2, :]).astype(BF16)
    kwn_ref[...] = _rms_heads(kw_ref[...], kn_ref[2:3, :]).astype(BF16)
    t = (pl.program_id(0) % nsb) * rb + _iota((rb, NSA_HD), 0)
    onehot = (_iota((rb, NSA_HD), 1) == jnp.right_shift(t, SEL_SHIFT)).astype(BF16)
    tail = (_iota((VT_ROWS - NSA_HD, rb), 0) == 0).astype(BF16)
    for g in range(NSA_GROUPS):
        cols = slice(g * NSA_HD, (g + 1) * NSA_HD)
        ksa_ref[:, 2 * g * NSA_HD:(2 * g + 1) * NSA_HD] = ksn[:, cols]
        ksa_ref[:, (2 * g + 1) * NSA_HD:(2 * g + 2) * NSA_HD] = onehot
        vst_ref[g] = jnp.concatenate([vs_ref[:, cols].T.astype(BF16), tail], axis=0)
        vwt_ref[g] = jnp.concatenate([vw_ref[:, cols].T.astype(BF16), tail], axis=0)


def _nsa_prep(u, q_norm_w, k_norm_w, col_q, col_kv, B, S, rb=ROWS_NSA_PREP):
    T = u.shape[0]
    nsb = S // rb
    qw = NSA_HEADS * NSA_HD
    kvw = NSA_GROUPS * NSA_HD
    cq = col_q // qw
    ck = col_kv // kvw
    assert S // SEL_LEN <= NSA_HD
    kv_spec = lambda idx: pl.BlockSpec((rb, kvw), lambda i: (i, ck + idx))
    out_k = pl.BlockSpec((rb, kvw), lambda i: (i, 0))
    out_ka = pl.BlockSpec((rb, 2 * kvw), lambda i: (i, 0))
    out_vt = pl.BlockSpec((None, NSA_GROUPS, VT_ROWS, rb), lambda i: (i // nsb, 0, 0, i % nsb))
    vt_shape = jax.ShapeDtypeStruct((B, NSA_GROUPS, VT_ROWS, S), BF16)
    return pl.pallas_call(
        functools.partial(_nsa_prep_kernel, rb=rb, nsb=nsb),
        grid=(T // rb,),
        in_specs=[
            pl.BlockSpec((rb, qw), lambda i: (i, cq)),
            kv_spec(2), kv_spec(3), kv_spec(4), kv_spec(5),
            pl.BlockSpec((1, NSA_HD), lambda i: (0, 0)),
            pl.BlockSpec((N_BRANCH, NSA_HD), lambda i: (0, 0)),
        ],
        out_specs=[pl.BlockSpec((rb, qw), lambda i: (i, 0)), out_ka, out_vt, out_k, out_vt],
        out_shape=[jax.ShapeDtypeStruct((T, qw), BF16), jax.ShapeDtypeStruct((T, 2 * kvw), BF16), vt_shape,
                   jax.ShapeDtypeStruct((T, kvw), BF16), vt_shape],
        compiler_params=pltpu.CompilerParams(dimension_semantics=("arbitrary",), vmem_limit_bytes=VMEM_LIMIT),
        name="nsa_prep",
    )(u, u, u, u, u, q_norm_w, k_norm_w)


def _compress_kernel(kc_ref, vc_ref, pek_ref, pev_ref, kw1_ref, kw2_ref, vw1_ref, vw2t_ref, kn_ref,
                     kcmp_ref, vcmp_ref, *, ns):
    def hidden(x_ref, pe_ref, w1_ref):
        hid = CMP_LEN // 2
        a = jnp.zeros((ns, w1_ref.shape[1]), F32)
        bsum = jnp.zeros((ns, w1_ref.shape[1]), F32)
        for l in range(0, hid, 2):
            xl = [x_ref[pl.ds(l + t, ns, stride=CMP_STRIDE), :] for t in range(2)]
            first = jnp.concatenate([(xl[t] + pe_ref[l + t:l + t + 1, :]).astype(BF16) for t in range(2)], axis=1)
            second = jnp.concatenate([(xl[t] + pe_ref[hid + l + t:hid + l + t + 1, :]).astype(BF16)
                                      for t in range(2)], axis=1)
            a = a + _dot(first, w1_ref[l * NSA_HD:(l + 2) * NSA_HD, :])
            bsum = bsum + _dot(second, w1_ref[(hid + l) * NSA_HD:(hid + l + 2) * NSA_HD, :])
        pre = a + pltpu.roll(bsum, ns - 1, 0)
        return (pre * _sigmoid_t(pre)).astype(BF16)

    kc = _dot(hidden(kc_ref, pek_ref, kw1_ref), kw2_ref[...])
    ms = jnp.mean(kc * kc, axis=-1, keepdims=True)
    kcmp_ref[...] = (kc * lax.rsqrt(ms + EPS) * kn_ref[0:1, :]).astype(BF16)
    vcmp_ref[...] = _dot_nt(vw2t_ref[...], hidden(vc_ref, pev_ref, vw1_ref)).astype(BF16)


def _compress(u, pe_k, pe_v, kw1, kw2, vw1, vw2, k_norm_w, col_kv, B, S):
    ns = S // CMP_STRIDE
    ck = col_kv // NSA_HD
    hidden = kw1.shape[1]
    const = lambda b, g: (0, 0)
    return pl.pallas_call(
        functools.partial(_compress_kernel, ns=ns),
        grid=(B, NSA_GROUPS),
        in_specs=[
            pl.BlockSpec((S, NSA_HD), lambda b, g: (b, ck + g)),
            pl.BlockSpec((S, NSA_HD), lambda b, g: (b, ck + NSA_GROUPS + g)),
            pl.BlockSpec((CMP_LEN, NSA_HD), const),
            pl.BlockSpec((CMP_LEN, NSA_HD), const),
            pl.BlockSpec((CMP_LEN * NSA_HD, hidden), const),
            pl.BlockSpec((hidden, NSA_HD), const),
            pl.BlockSpec((CMP_LEN * NSA_HD, hidden), const),
            pl.BlockSpec((NSA_HD, hidden), const),
            pl.BlockSpec((N_BRANCH, NSA_HD), const),
        ],
        out_specs=[pl.BlockSpec((None, None, ns, NSA_HD), lambda b, g: (b, g, 0, 0)),
                   pl.BlockSpec((None, None, NSA_HD, ns), lambda b, g: (b, g, 0, 0))],
        out_shape=[jax.ShapeDtypeStruct((B, NSA_GROUPS, ns, NSA_HD), BF16),
                   jax.ShapeDtypeStruct((B, NSA_GROUPS, NSA_HD, ns), BF16)],
        compiler_params=pltpu.CompilerParams(dimension_semantics=("arbitrary", "arbitrary"),
                                             vmem_limit_bytes=VMEM_LIMIT),
        name="compress",
    )(u, u, pe_k, pe_v, kw1, kw2, vw1, vw2.T, k_norm_w)


TQC = 256
CMP_TILES = 4
NEAR = 32
NEAR_BACK = 16


def _cmp_sel_kernel(tbl_ref, q_ref, kc_ref, vct_ref, oc_ref, sel_ref, pat_ref, *, ns, nsel):
    g = pl.program_id(1)
    i = pl.program_id(2)
    t0 = i * (CMP_TILES * TQC)
    c0 = i * (CMP_TILES * TQC // CMP_STRIDE)
    W = NSA_HPG * TQC
    assert (TQC - CMP_LEN) // CMP_STRIDE < NEAR - NEAR_BACK and NEAR_BACK * CMP_STRIDE >= REL_MAX_DIST + CMP_LEN

    @pl.when(i == 0)
    def _():
        r = _iota((TQC, LANES), 0)
        lane = _iota((TQC, LANES), 1)
        d = r - CMP_STRIDE * (jnp.bitwise_and(lane, NEAR - 1) - NEAR_BACK) - (CMP_LEN - 1)
        for h in range(NSA_HPG):
            val = jnp.where(d < 0, NEG, _rel_bias_pattern(d, tbl_ref, g * NSA_HPG + h))
            hi = val.astype(BF16).astype(F32)
            lo = jnp.where(d < 0, 0.0, val - hi)
            ext = jnp.where(lane < NEAR, hi, jnp.where(lane < 2 * NEAR, lo, jnp.where(lane == 2 * NEAR, NEG, 0.0)))
            pat_ref[h * TQC:(h + 1) * TQC, :] = ext.astype(BF16)

    tiles_q = range(CMP_TILES)
    qrows = lambda tq: slice(tq * TQC, (tq + 1) * TQC)
    lane = _iota((ns, LANES), 1)
    s = []
    for tq in tiles_q:
        q4 = q_ref[qrows(tq), :]
        q_aug = jnp.concatenate([jnp.concatenate([q4[:, h * NSA_HD:(h + 1) * NSA_HD],
                                                  pat_ref[h * TQC:(h + 1) * TQC, :]], axis=1)
                                 for h in range(NSA_HPG)], axis=0)
        rel = _iota((ns, LANES), 0) - (c0 + tq * (TQC // CMP_STRIDE) - NEAR_BACK)
        ext = ((lane < 2 * NEAR) & (rel == jnp.bitwise_and(lane, NEAR - 1))) | ((lane == 2 * NEAR) & (rel >= NEAR))
        k_aug = jnp.concatenate([kc_ref[...], jnp.where(ext, 1.0, 0.0).astype(BF16)], axis=1)
        s.append(_dot_nt(k_aug, q_aug))

    e, inv = [], []
    for tq in tiles_q:
        m = jnp.max(s[tq], axis=0, keepdims=True)
        e.append(jnp.exp2(s[tq] - m))
        l = jnp.sum(e[tq], axis=0, keepdims=True)
        t_abs = t0 + tq * TQC + jnp.bitwise_and(_iota((1, W), 1), TQC - 1)
        inv.append(jnp.where(t_abs >= CMP_LEN - 1, 1.0 / l, 0.0))

    ratio = SEL_LEN // CMP_STRIDE
    blk_n = _iota((nsel, ns), 0)
    tok_c = _iota((nsel, ns), 1)
    ov = ((tok_c >= ratio * blk_n - (CMP_LEN // CMP_STRIDE - 1)) & (tok_c <= ratio * blk_n + ratio - 1)).astype(F32)
    blk = _iota((nsel, TQC), 0)
    imp, cur = [], []
    for tq in tiles_q:
        o_t = _dot(vct_ref[...], e[tq].astype(BF16)) * inv[tq]
        for h in range(NSA_HPG):
            oc_ref[qrows(tq), h * NSA_HD:(h + 1) * NSA_HD] = o_t[:, h * TQC:(h + 1) * TQC].T
        p = e[tq] * inv[tq]
        ps = p[:, 0:TQC]
        for h in range(1, NSA_HPG):
            ps = ps + p[:, h * TQC:(h + 1) * TQC]
        v = _dot(ov, ps, precision=HIGHEST)
        c = jnp.right_shift(t0 + tq * TQC + _iota((nsel, TQC), 1), SEL_SHIFT)
        v = jnp.where((blk == 0) | (blk == c) | (blk == c - 1), FORCE_SCORE, v)
        imp.append(jnp.where(blk > c, NEG, v))
        cur.append(c)
    k_top = min(SEL_TOPK, nsel)

    def emit(tq, chosen):
        chosen = jnp.concatenate([chosen.astype(BF16), jnp.zeros((LANES - nsel, TQC), BF16)], axis=0)
        eye_q = (_iota((TQC, TQC), 0) == _iota((TQC, TQC), 1)).astype(BF16)
        sel_ref[qrows(tq), :] = ((_dot_nt(eye_q, chosen) - 1.0) * (-NEG)).astype(BF16)

    few = (i + 1) * CMP_TILES * (TQC // SEL_LEN) <= k_top

    @pl.when(few)
    def _():
        for tq in tiles_q:
            emit(tq, jnp.where(blk <= cur[tq], 1.0, 0.0))

    @pl.when(jnp.logical_not(few))
    def _():
        sub = _iota((8, TQC), 0)
        for tq in tiles_q:
            tiles = [imp[tq][8 * k:8 * k + 8, :] for k in range(nsel // 8)]
            counts = [jnp.zeros((8, TQC), jnp.int32) for _ in tiles]
            for jb in range(nsel):
                vj = imp[tq][jb:jb + 1, :]
                for k, tile in enumerate(tiles):
                    if 8 * k > jb:
                        ahead = vj >= tile
                    elif 8 * k + 7 < jb:
                        ahead = vj > tile
                    else:
                        ahead = (vj > tile) | ((vj == tile) & (sub + 8 * k > jb))
                    counts[k] = counts[k] + jnp.where(ahead, 1, 0)
            rank = jnp.concatenate(counts, axis=0)
            emit(tq, jnp.where(rank < k_top, 1.0, 0.0))


def _cmp_sel(rel_bias, qn, kcmp, vcmp_t, B, S):
    ns = S // CMP_STRIDE
    nsel = S // SEL_LEN
    rows = CMP_TILES * TQC
    nt = S // rows
    gw = NSA_HPG * NSA_HD
    assert nsel % 8 == 0 and nsel <= LANES
    return pl.pallas_call(
        functools.partial(_cmp_sel_kernel, ns=ns, nsel=nsel),
        grid=(B, NSA_GROUPS, nt),
        in_specs=[
            pl.BlockSpec(memory_space=pltpu.SMEM),
            pl.BlockSpec((rows, gw), lambda b, g, i: (b * nt + i, g)),
            pl.BlockSpec((None, None, ns, NSA_HD), lambda b, g, i: (b, g, 0, 0)),
            pl.BlockSpec((None, None, NSA_HD, ns), lambda b, g, i: (b, g, 0, 0)),
        ],
        out_specs=[
            pl.BlockSpec((rows, gw), lambda b, g, i: (b * nt + i, g)),
            pl.BlockSpec((None, None, rows, LANES), lambda b, g, i: (b, g, i, 0)),
        ],
        out_shape=[jax.ShapeDtypeStruct((B * S, NSA_HEADS * NSA_HD), F32),
                   jax.ShapeDtypeStruct((B, NSA_GROUPS, S, LANES), BF16)],
        scratch_shapes=[pltpu.VMEM((NSA_HPG * TQC, LANES), BF16)],
        compiler_params=pltpu.CompilerParams(dimension_semantics=("arbitrary", "arbitrary", "arbitrary"),
                                             vmem_limit_bytes=VMEM_LIMIT),
        name="cmp_sel",
    )(rel_bias, qn, kcmp, vcmp_t)


TA = 512
SUB = 128


def _attend_kernel(tbl_ref, q_ref, ksa_ref, vst_ref, kw_ref, vwt_ref, sel_ref, oc_ref, g_ref, z_ref, y_ref,
                   pd_ref, pp_ref, sa_ref, sb_ref, ms_ref, accs_ref, mw_ref, accw_ref, gt_ref, *, gate_col):
    g = pl.program_id(1)
    i = pl.program_id(2)
    nsub = TA // SUB

    krow = _iota((SUB, SUB), 0)
    qcol = _iota((SUB, SUB), 1)

    @pl.when(i == 0)
    def _():
        for h in range(NSA_HPG):
            head = g * NSA_HPG + h
            lanes = slice(h * SUB, (h + 1) * SUB)
            pd_ref[:, lanes] = jnp.where(krow <= qcol, _rel_bias_pattern(qcol - krow, tbl_ref, head), NEG)
            pp_ref[:, lanes] = _rel_bias_pattern(qcol - krow + SUB, tbl_ref, head)

    SLAB = NSA_HPG * SUB
    q4 = q_ref[...]
    sel = sel_ref[...]
    q_win = jnp.concatenate([q4[qb * SUB:(qb + 1) * SUB, h * NSA_HD:(h + 1) * NSA_HD]
                             for qb in range(nsub) for h in range(NSA_HPG)], axis=0)
    q_sel = jnp.concatenate([jnp.concatenate([q4[qb * SUB:(qb + 1) * SUB, h * NSA_HD:(h + 1) * NSA_HD],
                                              sel[qb * SUB:(qb + 1) * SUB, :]], axis=1)
                             for qb in range(nsub) for h in range(NSA_HPG)], axis=0)
    strict = jnp.concatenate([jnp.where(krow > qcol, 0.0, NEG)] * NSA_HPG, axis=1)

    def chunk(c):
        return pl.ds(pl.multiple_of(c * TA, TA), TA)

    def keys(c, lo, hi):
        return pl.ds(pl.multiple_of(c * TA + lo * SUB, SUB), (hi - lo) * SUB)

    def rows(kb):
        return slice(kb * SUB, (kb + 1) * SUB)

    def slab(qb):
        return slice(qb * SLAB, (qb + 1) * SLAB)

    def absorb(s_ref, vt_ref, c, m_ref, acc_ref, first=False, lo=0, hi=nsub, qb=None, have_max=False):
        lanes = slice(None) if qb is None else slab(qb)
        krows = slice(lo * SUB, hi * SUB)
        mx = s_ref[TA:TA + 1, lanes] if have_max else jnp.max(s_ref[krows, lanes], axis=0, keepdims=True)
        if first:
            m_new = mx
        else:
            m_old = m_ref[:, lanes]
            m_new = jnp.maximum(m_old, mx)
        p = jnp.exp2(s_ref[krows, lanes] - m_new).astype(BF16)
        pv = _dot(vt_ref[:, keys(c, lo, hi)], p)
        if first:
            acc_ref[:, lanes] = pv
        else:
            acc_ref[:, lanes] = jnp.exp2(m_old - m_new) * acc_ref[:, lanes] + pv
        m_ref[:, lanes] = m_new

    def qk_sel(s_ref, c, with_max=False):
        s = _dot_nt(ksa_ref[chunk(c), :], q_sel)
        s_ref[0:TA, :] = s
        if with_max:
            s_ref[TA:TA + 1, :] = jnp.max(s, axis=0, keepdims=True)

    def diag(s_ref, k_ref, q_all, vt_ref, m_ref, acc_ref):
        for qb in range(nsub):
            s_ref[0:(qb + 1) * SUB, slab(qb)] = _dot_nt(k_ref[keys(i, 0, qb + 1), :], q_all[slab(qb), :])
            s_ref[rows(qb), slab(qb)] += pd_ref[...]
            if qb >= 1:
                s_ref[rows(qb - 1), slab(qb)] += pp_ref[...]
        for qb in range(nsub):
            absorb(s_ref, vt_ref, i, m_ref, acc_ref, first=True, lo=0, hi=qb + 1, qb=qb)

    def win_prev(s_ref):
        for qb in range(nsub):
            s_ref[qb * SUB:TA, slab(qb)] = _dot_nt(kw_ref[keys(i - 1, qb, nsub), :], q_win[slab(qb), :])
            s_ref[rows(qb), slab(qb)] += strict
        s_ref[rows(nsub - 1), slab(0)] += pp_ref[...]
        for qb in range(nsub):
            absorb(s_ref, vwt_ref, i - 1, mw_ref, accw_ref, lo=qb, hi=nsub, qb=qb)

    @pl.when(i == 0)
    def _():
        diag(sa_ref, ksa_ref, q_sel, vst_ref, ms_ref, accs_ref)
        diag(sb_ref, kw_ref, q_win, vwt_ref, mw_ref, accw_ref)

    @pl.when(i >= 1)
    def _():
        qk_sel(sb_ref, i - 1)
        sb_ref[rows(nsub - 1), slab(0)] += pp_ref[...]
        diag(sa_ref, ksa_ref, q_sel, vst_ref, ms_ref, accs_ref)
        absorb(sb_ref, vst_ref, i - 1, ms_ref, accs_ref)
        diag(sa_ref, kw_ref, q_win, vwt_ref, mw_ref, accw_ref)
        win_prev(sb_ref)

    nfar = jnp.maximum(i - 1, 0)
    odd = nfar % 2

    @pl.when(odd == 1)
    def _():
        qk_sel(sa_ref, 0, with_max=True)
        absorb(sa_ref, vst_ref, 0, ms_ref, accs_ref, have_max=True)

    npair = nfar // 2

    @pl.when(npair > 0)
    def _():
        qk_sel(sa_ref, odd, with_max=True)

    def pair(t, carry):
        c = odd + 2 * t
        qk_sel(sb_ref, c + 1, with_max=True)
        absorb(sa_ref, vst_ref, c, ms_ref, accs_ref, have_max=True)
        qk_sel(sa_ref, jnp.minimum(c + 2, nfar - 1), with_max=True)
        absorb(sb_ref, vst_ref, c + 1, ms_ref, accs_ref, have_max=True)
        return carry

    lax.fori_loop(0, npair, pair, 0)

    gs = _sigmoid_t(g_ref[:, 0:LANES])
    gt_ref[...] = gs.T
    inv_s = 1.0 / accs_ref[NSA_HD:NSA_HD + 1, :]
    inv_w = 1.0 / accw_ref[NSA_HD:NSA_HD + 1, :]
    lane = _iota((TA, LANES), 1)
    for h in range(NSA_HPG):
        base = gate_col + (g * NSA_HPG + h) * N_BRANCH
        cols = slice(h * NSA_HD, (h + 1) * NSA_HD)
        gate_c = jnp.sum(jnp.where(lane == base, gs, 0.0), axis=-1, keepdims=True)
        gs_row = gt_ref[pl.ds(base + 1, 1), :]
        gw_row = gt_ref[pl.ds(base + 2, 1), :]
        o_sw = []
        for qb in range(nsub):
            lanes = slice(qb * SLAB + h * SUB, qb * SLAB + (h + 1) * SUB)
            gate_s = gs_row[:, rows(qb)] * inv_s[:, lanes]
            gate_w = gw_row[:, rows(qb)] * inv_w[:, lanes]
            o_sw.append((gate_s * accs_ref[0:NSA_HD, lanes] + gate_w * accw_ref[0:NSA_HD, lanes]).T)
        o = gate_c * oc_ref[:, cols] + jnp.concatenate(o_sw, axis=0)
        z = z_ref[:, cols]
        y_ref[:, cols] = (o * z * _sigmoid_t(z)).astype(BF16)


def _attend(rel_bias, qn, ksa, vst, kwn, vwt, sel, o_c, gates, gate_blk, u, col_z, gate_col, B, S):
    nt = S // TA
    gw = NSA_HPG * NSA_HD
    W = NSA_HPG * TA
    cz = col_z // gw
    assert WIN == TA
    tile = lambda b, g, i: (b * nt + i, g)
    vt_spec = pl.BlockSpec((None, None, VT_ROWS, S), lambda b, g, i: (b, g, 0, 0))
    return pl.pallas_call(
        functools.partial(_attend_kernel, gate_col=gate_col),
        grid=(B, NSA_GROUPS, nt),
        in_specs=[
            pl.BlockSpec(memory_space=pltpu.SMEM),
            pl.BlockSpec((TA, gw), tile),
            pl.BlockSpec((S, 2 * NSA_HD), lambda b, g, i: (b, g)), vt_spec,
            pl.BlockSpec((S, NSA_HD), lambda b, g, i: (b, g)), vt_spec,
            pl.BlockSpec((None, None, TA, LANES), lambda b, g, i: (b, g, i, 0)),
            pl.BlockSpec((TA, gw), tile),
            pl.BlockSpec((TA, 2 * LANES), lambda b, g, i: (b * nt + i, gate_blk)),
            pl.BlockSpec((TA, gw), lambda b, g, i: (b * nt + i, cz + g)),
        ],
        out_specs=pl.BlockSpec((TA, gw), tile),
        out_shape=jax.ShapeDtypeStruct((B * S, NSA_HEADS * NSA_HD), BF16),
        scratch_shapes=[pltpu.VMEM((SUB, NSA_HPG * SUB), F32), pltpu.VMEM((SUB, NSA_HPG * SUB), F32),
                        pltpu.VMEM((TA + 8, W), F32), pltpu.VMEM((TA + 8, W), F32),
                        pltpu.VMEM((1, W), F32), pltpu.VMEM((VT_ROWS, W), F32),
                        pltpu.VMEM((1, W), F32), pltpu.VMEM((VT_ROWS, W), F32),
                        pltpu.VMEM((LANES, TA), F32)],
        compiler_params=pltpu.CompilerParams(dimension_semantics=("arbitrary", "arbitrary", "arbitrary"),
                                             vmem_limit_bytes=VMEM_LIMIT),
        name="attend",
    )(rel_bias, qn, ksa, vst, kwn, vwt, sel, o_c, gates, u)


def _outproj_kernel(yml_ref, yns_ref, w_hbm, x_ref, o_ref, ob_ref, wf_ref, wb_ref, sem, *, nblk):
    j = pl.program_id(0)
    tn = wf_ref.shape[1]

    def copy_of(jj):
        return pltpu.make_async_copy(w_hbm.at[:, pl.ds(pl.multiple_of(jj * tn, tn), tn)], wf_ref, sem)

    @pl.when(pl.program_id(1) == 0)
    def _():
        _stage_weight_block(copy_of, j, nblk, wf_ref, wb_ref)

    half = yml_ref.shape[1]
    o = x_ref[...] + _dot(yml_ref[...], wb_ref[0:half, :]) + _dot(yns_ref[...], wb_ref[half:2 * half, :])
    o_ref[...] = o
    ob_ref[...] = o.astype(BF16)


def _outproj(y_ml, y_ns, w_out, x2, tm=TILE_OUTPROJ[0], tn=TILE_OUTPROJ[1]):
    T, D = x2.shape
    half = y_ml.shape[1]
    tile = pl.BlockSpec((tm, tn), lambda j, i: (i, j))
    return pl.pallas_call(
        functools.partial(_outproj_kernel, nblk=D // tn),
        grid=(D // tn, T // tm),
        in_specs=[
            pl.BlockSpec((tm, half), lambda j, i: (i, 0)),
            pl.BlockSpec((tm, half), lambda j, i: (i, 0)),
            pl.BlockSpec(memory_space=pl.ANY),
            tile,
        ],
        out_specs=[tile, tile],
        out_shape=[jax.ShapeDtypeStruct((T, D), F32), jax.ShapeDtypeStruct((T, D), BF16)],
        scratch_shapes=[pltpu.VMEM((2 * half, tn), F32), pltpu.VMEM((2 * half, tn), BF16),
                        pltpu.SemaphoreType.DMA(())],
        compiler_params=pltpu.CompilerParams(dimension_semantics=("arbitrary", "arbitrary"),
                                             vmem_limit_bytes=VMEM_LIMIT),
        name="outproj",
    )(y_ml, y_ns, w_out, x2)


def _ple_kernel(xb_ref, x_ref, p_ref, wg_hbm, wp_ref, o_ref, wgf_ref, wgb_ref, wpb_ref, sem, *, nblk):
    j = pl.program_id(0)
    tn = wgf_ref.shape[1]

    def copy_of(jj):
        return pltpu.make_async_copy(wg_hbm.at[:, pl.ds(pl.multiple_of(jj * tn, tn), tn)], wgf_ref, sem)

    @pl.when(pl.program_id(1) == 0)
    def _():
        _stage_weight_block(copy_of, j, nblk, wgf_ref, wgb_ref)
        wpb_ref[...] = wp_ref[...].astype(BF16)

    gate = _sigmoid_t(_dot(xb_ref[...], wgb_ref[...]))
    emb = _dot(p_ref[...].astype(BF16), wpb_ref[...])
    o_ref[...] = x_ref[...] + gate * emb


def _ple(x1b, x1, p2, wg, wp, tm=TILE_OUTPROJ[0], tn=TILE_OUTPROJ[1]):
    T, D = x1.shape
    P = p2.shape[1]
    tile = pl.BlockSpec((tm, tn), lambda j, i: (i, j))
    return pl.pallas_call(
        functools.partial(_ple_kernel, nblk=D // tn),
        grid=(D // tn, T // tm),
        in_specs=[
            pl.BlockSpec((tm, D), lambda j, i: (i, 0)),
            tile,
            pl.BlockSpec((tm, P), lambda j, i: (i, 0)),
            pl.BlockSpec(memory_space=pl.ANY),
            pl.BlockSpec((P, tn), lambda j, i: (0, j)),
        ],
        out_specs=tile,
        out_shape=jax.ShapeDtypeStruct((T, D), F32),
        scratch_shapes=[pltpu.VMEM((D, tn), F32), pltpu.VMEM((D, tn), BF16), pltpu.VMEM((P, tn), BF16),
                        pltpu.SemaphoreType.DMA(())],
        compiler_params=pltpu.CompilerParams(dimension_semantics=("arbitrary", "arbitrary"),
                                             vmem_limit_bytes=VMEM_LIMIT),
        name="ple",
    )(x1b, x1, p2, wg, wp)


def _layer(x2, p2, norm_w, w_in, conv_w, i_bias, f_bias, head_norm_w, q_norm_w, k_norm_w,
           pe_k, pe_v, kw1, kw2, vw1, vw2, rel_bias, w_out, ple_proj, ple_gate, B, S):
    D = x2.shape[1]
    qkw = ML_HEADS * ML_DQK
    vw = ML_HEADS * ML_DV
    nq = NSA_HEADS * NSA_HD
    nkv = NSA_GROUPS * NSA_HD
    o_i = 2 * qkw + 3 * vw
    o_f = o_i + ML_HEADS
    o_nq = o_f + ML_HEADS
    o_g = o_nq + nq + 6 * nkv
    o_z = o_g + NSA_HEADS * N_BRANCH
    w_t = w_in.T
    ngate = NSA_HEADS * N_BRANCH
    zeros = lambda n: jnp.zeros((n, D), w_in.dtype)
    wg_t = jnp.concatenate([w_t[o_i:o_f], w_t[o_g:o_z], zeros(LANES - ML_HEADS - ngate),
                            w_t[o_f:o_nq], zeros(LANES - ML_HEADS)], axis=0)
    col_nq = o_i
    col_kv = col_nq + nq
    col_z = col_kv + 6 * nkv

    h, gates = _rmsnorm(x2, norm_w.reshape(1, D), wg_t)
    u = _inproj(h, w_t, [(0, o_i), (o_nq, o_g - o_nq), (o_z, nq)])

    pad_h = lambda v: jnp.concatenate([v, jnp.zeros((LANES - ML_HEADS,), v.dtype)]).reshape(1, LANES)
    y_ml = _mlstm(u, gates, 0, conv_w, pad_h(i_bias), pad_h(f_bias), head_norm_w, B, S)

    qn, ksa, vst, kwn, vwt = _nsa_prep(u, q_norm_w.reshape(1, NSA_HD), k_norm_w, col_nq, col_kv, B, S)
    kcmp, vcmp_t = _compress(u, pe_k, pe_v, kw1.astype(BF16), kw2.astype(BF16), vw1.astype(BF16),
                             vw2.astype(BF16), k_norm_w, col_kv, B, S)
    o_c, sel = _cmp_sel(rel_bias, qn, kcmp, vcmp_t, B, S)
    y_ns = _attend(rel_bias, qn, ksa, vst, kwn, vwt, sel, o_c, gates, 0, u, col_z, ML_HEADS, B, S)

    x1, x1b = _outproj(y_ml, y_ns, w_out, x2)
    return _ple(x1b, x1, p2, ple_gate, ple_proj)


def kernel(x, p, norm_w, w_in, ml_conv_w, ml_i_bias, ml_f_bias, ml_head_norm_w, nsa_q_norm_w, nsa_k_norm_w,
           cmp_pe_k, cmp_pe_v, cmp_k_w1, cmp_k_w2, cmp_v_w1, cmp_v_w2, rel_bias, w_out, ple_proj, ple_gate):
    B, S, D = x.shape
    assert S % max(WIN, 256) == 0 and S // SEL_LEN >= 1
    x2 = x.reshape(B * S, D)
    for layer in range(w_in.shape[0]):
        x2 = _layer(x2, p[layer].reshape(B * S, -1), norm_w[layer], w_in[layer], ml_conv_w[layer],
                    ml_i_bias[layer], ml_f_bias[layer], ml_head_norm_w[layer], nsa_q_norm_w[layer],
                    nsa_k_norm_w[layer], cmp_pe_k[layer], cmp_pe_v[layer], cmp_k_w1[layer], cmp_k_w2[layer],
                    cmp_v_w1[layer], cmp_v_w2[layer], rel_bias, w_out[layer], ple_proj[layer], ple_gate[layer],
                    B, S)
    return x2.reshape(B, S, D)
```

```python
import functools
import math

import numpy as np
import jax
import jax.numpy as jnp
from jax import lax
from jax.experimental import pallas as pl
from jax.experimental.pallas import tpu as pltpu

F32 = jnp.float32
BF16 = jnp.bfloat16
HIGHEST = lax.Precision.HIGHEST

ML_HEADS = 8
ML_DQK = 128
ML_DV = 256
ML_CHUNK = 64
ML_CONV = 4
NSA_HEADS = 16
NSA_HD = 128
NSA_GROUPS = 4
NSA_HPG = 4
N_BRANCH = 3
CMP_STRIDE = 16
CMP_LEN = 32
SEL_LEN = 64
SEL_SHIFT = 6
SEL_TOPK = 16
WIN = 512
REL_BUCKETS = 32
REL_MAX_DIST = 128
EPS = 1e-6
NEG = -1e30
FORCE_SCORE = 1e4
LOG2E = math.log2(math.e)

LANES = 128
VMEM_LIMIT = 56 * 1024 * 1024

ROWS_RMSNORM = 256
TILE_INPROJ = (1024, 1024)
TILE_OUTPROJ = (512, 1024)
ROWS_MLSTM = 512
ROWS_NSA_PREP = 512


def _bucket_thresholds():
    n = np.arange(0, 4 * REL_MAX_DIST, dtype=np.int64)
    max_exact = REL_BUCKETS // 2
    nf = np.maximum(n, 1).astype(np.float32)
    large = max_exact + (np.log(nf / np.float32(max_exact)) / np.float32(math.log(REL_MAX_DIST / max_exact))
                         * np.float32(REL_BUCKETS - max_exact)).astype(np.int32)
    large = np.minimum(large, REL_BUCKETS - 1)
    bucket = np.where(n < max_exact, n, large)
    assert np.all(np.diff(bucket) >= 0)
    thr = [int(np.argmax(bucket >= b)) for b in range(REL_BUCKETS)]
    assert thr[REL_BUCKETS - 1] <= REL_MAX_DIST
    return thr


BUCKET_THR = _bucket_thresholds()


def _dot(a, b, precision=None):
    return jnp.dot(a, b, preferred_element_type=F32, precision=precision)


def _dot_nt(a, b, precision=None):
    return lax.dot_general(a, b, (((1,), (1,)), ((), ())), preferred_element_type=F32, precision=precision)


def _sigmoid_t(x):
    return 0.5 * jnp.tanh(0.5 * x) + 0.5


def _iota(shape, dim):
    return lax.broadcasted_iota(jnp.int32, shape, dim)


def _rel_bias_pattern(dist, tbl_ref, head):
    val = jnp.full(dist.shape, tbl_ref[0, head], F32)
    for b in range(1, REL_BUCKETS):
        val = jnp.where(dist >= BUCKET_THR[b], tbl_ref[b, head], val)
    return (val - tbl_ref[REL_BUCKETS - 1, head]) * LOG2E


def _rmsnorm_kernel(x_ref, nw_ref, wg_ref, h_ref, g_ref, wgb_ref):
    @pl.when(pl.program_id(0) == 0)
    def _():
        wgb_ref[...] = wg_ref[...].astype(BF16)

    x = x_ref[...]
    ms = jnp.mean(x * x, axis=-1, keepdims=True)
    h = (x * lax.rsqrt(ms + EPS) * nw_ref[...]).astype(BF16)
    h_ref[...] = h
    g_ref[...] = _dot_nt(h, wgb_ref[...])


def _rmsnorm(x2, norm_w, wg_t, rb=ROWS_RMSNORM):
    T, D = x2.shape
    NG = wg_t.shape[0]
    return pl.pallas_call(
        _rmsnorm_kernel,
        grid=(T // rb,),
        in_specs=[pl.BlockSpec((rb, D), lambda i: (i, 0)), pl.BlockSpec((1, D), lambda i: (0, 0)),
                  pl.BlockSpec((NG, D), lambda i: (0, 0))],
        out_specs=[pl.BlockSpec((rb, D), lambda i: (i, 0)), pl.BlockSpec((rb, NG), lambda i: (i, 0))],
        out_shape=[jax.ShapeDtypeStruct((T, D), BF16), jax.ShapeDtypeStruct((T, NG), F32)],
        scratch_shapes=[pltpu.VMEM((NG, D), BF16)],
        compiler_params=pltpu.CompilerParams(dimension_semantics=("arbitrary",), vmem_limit_bytes=VMEM_LIMIT),
        name="rmsnorm",
    )(x2, norm_w, wg_t)


def _stage_weight_block(copy_of, j, nblk, wf_ref, wb_ref):
    @pl.when(j == 0)
    def _():
        copy_of(j).start()

    copy_of(j).wait()
    wb_ref[...] = wf_ref[...].astype(BF16)

    @pl.when(j + 1 < nblk)
    def _():
        copy_of(j + 1).start()


def _inproj_kernel(h_hbm, wt_hbm, u_hbm, hbuf, obuf, wf_ref, wb_ref, hsem, osem, wsem, *, nblk, ni, row_of):
    j = pl.program_id(0)
    tm = hbuf.shape[1]
    tn = wf_ref.shape[0]

    def w_copy(jj):
        return pltpu.make_async_copy(wt_hbm.at[pl.ds(row_of(jj), tn), :], wf_ref, wsem)

    def h_copy(i, slot):
        return pltpu.make_async_copy(h_hbm.at[pl.ds(pl.multiple_of(i * tm, tm), tm), :], hbuf.at[slot],
                                     hsem.at[slot])

    def u_copy(i, slot):
        return pltpu.make_async_copy(obuf.at[slot],
                                     u_hbm.at[pl.ds(pl.multiple_of(i * tm, tm), tm),
                                              pl.ds(pl.multiple_of(j * tn, tn), tn)], osem.at[slot])

    @pl.when(j == 0)
    def _():
        h_copy(0, 0).start()

    _stage_weight_block(w_copy, j, nblk, wf_ref, wb_ref)

    def row_tile(i, carry):
        slot = i % 2
        h_copy(i, slot).wait()

        @pl.when(i + 1 < ni)
        def _():
            h_copy(i + 1, 1 - slot).start()

        @pl.when((i + 1 == ni) & (j + 1 < nblk))
        def _():
            h_copy(0, 1 - slot).start()

        @pl.when(i >= 2)
        def _():
            u_copy(i - 2, slot).wait()

        obuf[slot] = _dot_nt(hbuf[slot], wb_ref[...])
        u_copy(i, slot).start()
        return carry

    lax.fori_loop(0, ni, row_tile, 0)
    u_copy(ni - 2, (ni - 2) % 2).wait()
    u_copy(ni - 1, (ni - 1) % 2).wait()


def _piece_blocks(pieces, tn):
    starts = []
    for start, length in pieces:
        assert length % tn == 0 and start % 8 == 0
        starts += [start + k * tn for k in range(length // tn)]
    bounds = [(b, s) for b, s in enumerate(starts) if b == 0 or s != starts[b - 1] + tn]

    def w_rows(j, i):
        row8 = (bounds[0][1] + j * tn) // 8
        for b, s in bounds[1:]:
            row8 = jnp.where(j >= b, (s + (j - b) * tn) // 8, row8)
        return row8 * 8, 0

    return len(starts), w_rows


def _inproj(h, w_t, pieces, tm=TILE_INPROJ[0], tn=TILE_INPROJ[1]):
    T, D = h.shape
    nblk, w_rows = _piece_blocks(pieces, tn)
    ni = T // tm
    assert ni % 2 == 0 and ni >= 2
    return pl.pallas_call(
        functools.partial(_inproj_kernel, nblk=nblk, ni=ni, row_of=lambda j: w_rows(j, 0)[0]),
        grid=(nblk,),
        in_specs=[pl.BlockSpec(memory_space=pl.ANY), pl.BlockSpec(memory_space=pl.ANY)],
        out_specs=pl.BlockSpec(memory_space=pl.ANY),
        out_shape=jax.ShapeDtypeStruct((T, nblk * tn), F32),
        scratch_shapes=[pltpu.VMEM((2, tm, D), BF16), pltpu.VMEM((2, tm, tn), F32),
                        pltpu.VMEM((tn, D), F32), pltpu.VMEM((tn, D), BF16),
                        pltpu.SemaphoreType.DMA((2,)), pltpu.SemaphoreType.DMA((2,)), pltpu.SemaphoreType.DMA(())],
        compiler_params=pltpu.CompilerParams(dimension_semantics=("arbitrary",), vmem_limit_bytes=VMEM_LIMIT),
        name="inproj",
    )(h, w_t)


def _mlstm_kernel(q_ref, k_ref, v_ref, o_ref, z_ref, g_ref, cw_ref, ib_ref, fb_ref, hw_ref, y_ref,
                  xbuf, c_ref, m_ref, *, lb):
    L = ML_CHUNK
    qkw = ML_HEADS * ML_DQK

    @pl.when(pl.program_id(1) == 0)
    def _():
        xbuf[0:8, :] = jnp.zeros((8, 2 * qkw), F32)
        c_ref[...] = jnp.zeros(c_ref.shape, F32)
        m_ref[...] = jnp.zeros(m_ref.shape, F32)

    row = _iota((L, L), 0)
    col = _iota((L, L), 1)
    tril = col <= row
    rowg = _iota((L, LANES), 0)
    eye_h = (_iota((8, LANES), 0) == _iota((8, LANES), 1)).astype(F32)
    ones_col = (_iota((L, LANES), 1) == 0).astype(BF16)
    heads = range(ML_HEADS)

    def chunk(c, carry):
        rows = pl.ds(pl.multiple_of(c * L, L), L)
        qk_cols = lambda h: slice(h * ML_DQK, (h + 1) * ML_DQK)
        v_cols = lambda h: slice(h * ML_DV, (h + 1) * ML_DV)
        col_of = lambda x, h: x[:, h:h + 1]

        G = g_ref[rows, :]
        li = G[:, 0:LANES] + ib_ref[...]
        fp = G[:, LANES:2 * LANES] + fb_ref[...]
        lf = jnp.minimum(fp, 0.0) - jnp.log(1.0 + jnp.exp(-jnp.abs(fp)))
        b = lf
        for sh in (1, 2, 4, 8, 16, 32):
            b = b + jnp.where(rowg >= sh, pltpu.roll(b, sh, 0), 0.0)
        g = li - b
        gmax = g
        for sh in (1, 2, 4, 8, 16, 32):
            gmax = jnp.maximum(gmax, jnp.where(rowg >= sh, pltpu.roll(gmax, sh, 0), NEG))
        m_old = m_ref[...]
        top = jnp.maximum(m_old, gmax)
        m_t = b + top
        inter = jnp.exp(m_old - top)
        floor = jnp.exp(-m_t)
        bL = b[L - 1:L, :]
        w = bL - b + li
        m_new = jnp.maximum(bL + m_old, jnp.max(w, axis=0, keepdims=True))
        wk = jnp.exp(w - m_new)
        decay = jnp.exp(bL + m_old - m_new)
        m_ref[...] = m_new
        g_rows = _dot_nt(eye_h, g, precision=HIGHEST)

        xbuf[8:8 + L, 0:qkw] = q_ref[rows, :]
        xbuf[8:8 + L, qkw:2 * qkw] = k_ref[rows, :]
        acc = cw_ref[ML_CONV - 1:ML_CONV, :] * xbuf[8:8 + L, :]
        for j in range(1, ML_CONV):
            acc = acc + cw_ref[ML_CONV - 1 - j:ML_CONV - j, :] * xbuf[8 - j:8 - j + L, :]
        act = acc * _sigmoid_t(acc)
        xbuf[0:8, :] = xbuf[L:L + 8, :]

        q = [act[:, qk_cols(h)].astype(BF16) for h in heads]
        kf = [act[:, qkw + h * ML_DQK:qkw + (h + 1) * ML_DQK] * (ML_DQK ** -0.5) for h in heads]
        vaug = [jnp.concatenate([v_ref[rows, v_cols(h)].astype(BF16), ones_col], axis=1) for h in heads]
        cst = [c_ref[h] for h in heads]
        lhs = []
        for h in heads:
            dw = jnp.exp(jnp.where(tril, g_rows[h:h + 1, :] - col_of(top, h), NEG))
            sc = (_dot_nt(q[h], kf[h].astype(BF16)) * dw).astype(BF16)
            q_in = (q[h].astype(F32) * col_of(inter, h)).astype(BF16)
            lhs.append(jnp.concatenate([q_in, sc], axis=1))
        num_aug = [_dot(lhs[h], jnp.concatenate([cst[h].astype(BF16), vaug[h]], axis=0)) for h in heads]
        for h in heads:
            kw = (kf[h] * col_of(wk, h)).astype(BF16)
            upd = lax.dot_general(kw, vaug[h], (((0,), (0,)), ((), ())), preferred_element_type=F32)
            c_ref[h] = decay[:, h:h + 1] * cst[h] + upd
        for h in heads:
            num = num_aug[h][:, 0:ML_DV]
            den = num_aug[h][:, ML_DV:ML_DV + 1]
            r = 1.0 / jnp.maximum(jnp.abs(den), col_of(floor, h))
            f = r * lax.rsqrt(r * r * jnp.mean(num * num, axis=-1, keepdims=True) + EPS)
            o = o_ref[rows, v_cols(h)]
            z = z_ref[rows, v_cols(h)]
            gate = _sigmoid_t(o) * z * _sigmoid_t(z)
            y_ref[rows, v_cols(h)] = (num * f * hw_ref[h:h + 1, :] * gate).astype(BF16)
        return carry

    lax.fori_loop(0, lb // L, chunk, 0)


def _mlstm(u, gates, gate_blk, conv_w, ib, fb, head_w, B, S, lb=ROWS_MLSTM):
    T = B * S
    nsb = S // lb
    qkw = ML_HEADS * ML_DQK
    vw = ML_HEADS * ML_DV
    rowmap = lambda col: (lambda b, s: (b * nsb + s, col))
    const = lambda b, s: (0, 0)
    return pl.pallas_call(
        functools.partial(_mlstm_kernel, lb=lb),
        grid=(B, nsb),
        in_specs=[
            pl.BlockSpec((lb, qkw), rowmap(0)),
            pl.BlockSpec((lb, qkw), rowmap(1)),
            pl.BlockSpec((lb, vw), rowmap(1)),
            pl.BlockSpec((lb, vw), rowmap(2)),
            pl.BlockSpec((lb, vw), rowmap(3)),
            pl.BlockSpec((lb, 2 * LANES), rowmap(gate_blk)),
            pl.BlockSpec((ML_CONV, 2 * qkw), const),
            pl.BlockSpec((1, LANES), const),
            pl.BlockSpec((1, LANES), const),
            pl.BlockSpec((ML_HEADS, ML_DV), const),
        ],
        out_specs=pl.BlockSpec((lb, vw), rowmap(0)),
        out_shape=jax.ShapeDtypeStruct((T, vw), BF16),
        scratch_shapes=[
            pltpu.VMEM((ML_CHUNK + 8, 2 * qkw), F32),
            pltpu.VMEM((ML_HEADS, ML_DQK, ML_DV + LANES), F32),
            pltpu.VMEM((1, LANES), F32),
        ],
        compiler_params=pltpu.CompilerParams(dimension_semantics=("arbitrary", "arbitrary"),
                                             vmem_limit_bytes=VMEM_LIMIT),
        name="mlstm",
    )(u, u, u, u, u, gates, conv_w, ib, fb, head_w)


def _rms_heads(x, w, scale=1.0):
    outs = []
    for h in range(x.shape[1] // NSA_HD):
        xh = x[:, h * NSA_HD:(h + 1) * NSA_HD]
        ms = jnp.mean(xh * xh, axis=-1, keepdims=True)
        outs.append(xh * lax.rsqrt(ms + EPS) * w * scale)
    return jnp.concatenate(outs, axis=1)


VT_ROWS = NSA_HD + 16


def _nsa_prep_kernel(q_ref, ks_ref, vs_ref, kw_ref, vw_ref, qw_ref, kn_ref,
                     qn_ref, ksa_ref, vst_ref, kwn_ref, vwt_ref, *, rb, nsb):
    qn_ref[...] = _rms_heads(q_ref[...], qw_ref[...], NSA_HD ** -0.5 * LOG2E).astype(BF16)
    ksn = _rms_heads(ks_ref[...], kn_ref[1:2, :]).astype(BF16)
    kwn_ref[...] = _rms_heads(kw_ref[...], kn_ref[2:3, :]).astype(BF16)
    t = (pl.program_id(0) % nsb) * rb + _iota((rb, NSA_HD), 0)
    onehot = (_iota((rb, NSA_HD), 1) == jnp.right_shift(t, SEL_SHIFT)).astype(BF16)
    tail = (_iota((VT_ROWS - NSA_HD, rb), 0) == 0).astype(BF16)
    for g in range(NSA_GROUPS):
        cols = slice(g * NSA_HD, (g + 1) * NSA_HD)
        ksa_ref[:, 2 * g * NSA_HD:(2 * g + 1) * NSA_HD] = ksn[:, cols]
        ksa_ref[:, (2 * g + 1) * NSA_HD:(2 * g + 2) * NSA_HD] = onehot
        vst_ref[g] = jnp.concatenate([vs_ref[:, cols].T.astype(BF16), tail], axis=0)
        vwt_ref[g] = jnp.concatenate([vw_ref[:, cols].T.astype(BF16), tail], axis=0)


def _nsa_prep(u, q_norm_w, k_norm_w, col_q, col_kv, B, S, rb=ROWS_NSA_PREP):
    T = u.shape[0]
    nsb = S // rb
    qw = NSA_HEADS * NSA_HD
    kvw = NSA_GROUPS * NSA_HD
    cq = col_q // qw
    ck = col_kv // kvw
    assert S // SEL_LEN <= NSA_HD
    kv_spec = lambda idx: pl.BlockSpec((rb, kvw), lambda i: (i, ck + idx))
    out_k = pl.BlockSpec((rb, kvw), lambda i: (i, 0))
    out_ka = pl.BlockSpec((rb, 2 * kvw), lambda i: (i, 0))
    out_vt = pl.BlockSpec((None, NSA_GROUPS, VT_ROWS, rb), lambda i: (i // nsb, 0, 0, i % nsb))
    vt_shape = jax.ShapeDtypeStruct((B, NSA_GROUPS, VT_ROWS, S), BF16)
    return pl.pallas_call(
        functools.partial(_nsa_prep_kernel, rb=rb, nsb=nsb),
        grid=(T // rb,),
        in_specs=[
            pl.BlockSpec((rb, qw), lambda i: (i, cq)),
            kv_spec(2), kv_spec(3), kv_spec(4), kv_spec(5),
            pl.BlockSpec((1, NSA_HD), lambda i: (0, 0)),
            pl.BlockSpec((N_BRANCH, NSA_HD), lambda i: (0, 0)),
        ],
        out_specs=[pl.BlockSpec((rb, qw), lambda i: (i, 0)), out_ka, out_vt, out_k, out_vt],
        out_shape=[jax.ShapeDtypeStruct((T, qw), BF16), jax.ShapeDtypeStruct((T, 2 * kvw), BF16), vt_shape,
                   jax.ShapeDtypeStruct((T, kvw), BF16), vt_shape],
        compiler_params=pltpu.CompilerParams(dimension_semantics=("arbitrary",), vmem_limit_bytes=VMEM_LIMIT),
        name="nsa_prep",
    )(u, u, u, u, u, q_norm_w, k_norm_w)


def _compress_kernel(kc_ref, vc_ref, pek_ref, pev_ref, kw1_ref, kw2_ref, vw1_ref, vw2t_ref, kn_ref,
                     kcmp_ref, vcmp_ref, *, ns):
    def hidden(x_ref, pe_ref, w1_ref):
        hid = CMP_LEN // 2
        a = jnp.zeros((ns, w1_ref.shape[1]), F32)
        bsum = jnp.zeros((ns, w1_ref.shape[1]), F32)
        for l in range(0, hid, 2):
            xl = [x_ref[pl.ds(l + t, ns, stride=CMP_STRIDE), :] for t in range(2)]
            first = jnp.concatenate([(xl[t] + pe_ref[l + t:l + t + 1, :]).astype(BF16) for t in range(2)], axis=1)
            second = jnp.concatenate([(xl[t] + pe_ref[hid + l + t:hid + l + t + 1, :]).astype(BF16)
                                      for t in range(2)], axis=1)
            a = a + _dot(first, w1_ref[l * NSA_HD:(l + 2) * NSA_HD, :])
            bsum = bsum + _dot(second, w1_ref[(hid + l) * NSA_HD:(hid + l + 2) * NSA_HD, :])
        pre = a + pltpu.roll(bsum, ns - 1, 0)
        return (pre * _sigmoid_t(pre)).astype(BF16)

    kc = _dot(hidden(kc_ref, pek_ref, kw1_ref), kw2_ref[...])
    ms = jnp.mean(kc * kc, axis=-1, keepdims=True)
    kcmp_ref[...] = (kc * lax.rsqrt(ms + EPS) * kn_ref[0:1, :]).astype(BF16)
    vcmp_ref[...] = _dot_nt(vw2t_ref[...], hidden(vc_ref, pev_ref, vw1_ref)).astype(BF16)


def _compress(u, pe_k, pe_v, kw1, kw2, vw1, vw2, k_norm_w, col_kv, B, S):
    ns = S // CMP_STRIDE
    ck = col_kv // NSA_HD
    hidden = kw1.shape[1]
    const = lambda b, g: (0, 0)
    return pl.pallas_call(
        functools.partial(_compress_kernel, ns=ns),
        grid=(B, NSA_GROUPS),
        in_specs=[
            pl.BlockSpec((S, NSA_HD), lambda b, g: (b, ck + g)),
            pl.BlockSpec((S, NSA_HD), lambda b, g: (b, ck + NSA_GROUPS + g)),
            pl.BlockSpec((CMP_LEN, NSA_HD), const),
            pl.BlockSpec((CMP_LEN, NSA_HD), const),
            pl.BlockSpec((CMP_LEN * NSA_HD, hidden), const),
            pl.BlockSpec((hidden, NSA_HD), const),
            pl.BlockSpec((CMP_LEN * NSA_HD, hidden), const),
            pl.BlockSpec((NSA_HD, hidden), const),
            pl.BlockSpec((N_BRANCH, NSA_HD), const),
        ],
        out_specs=[pl.BlockSpec((None, None, ns, NSA_HD), lambda b, g: (b, g, 0, 0)),
                   pl.BlockSpec((None, None, NSA_HD, ns), lambda b, g: (b, g, 0, 0))],
        out_shape=[jax.ShapeDtypeStruct((B, NSA_GROUPS, ns, NSA_HD), BF16),
                   jax.ShapeDtypeStruct((B, NSA_GROUPS, NSA_HD, ns), BF16)],
        compiler_params=pltpu.CompilerParams(dimension_semantics=("arbitrary", "arbitrary"),
                                             vmem_limit_bytes=VMEM_LIMIT),
        name="compress",
    )(u, u, pe_k, pe_v, kw1, kw2, vw1, vw2.T, k_norm_w)


TQC = 256
CMP_TILES = 4
NEAR = 32
NEAR_BACK = 16


def _cmp_sel_kernel(tbl_ref, q_ref, kc_ref, vct_ref, oc_ref, sel_ref, pat_ref, *, ns, nsel):
    g = pl.program_id(1)
    i = pl.program_id(2)
    t0 = i * (CMP_TILES * TQC)
    c0 = i * (CMP_TILES * TQC // CMP_STRIDE)
    W = NSA_HPG * TQC
    assert (TQC - CMP_LEN) // CMP_STRIDE < NEAR - NEAR_BACK and NEAR_BACK * CMP_STRIDE >= REL_MAX_DIST + CMP_LEN

    @pl.when(i == 0)
    def _():
        r = _iota((TQC, LANES), 0)
        lane = _iota((TQC, LANES), 1)
        d = r - CMP_STRIDE * (jnp.bitwise_and(lane, NEAR - 1) - NEAR_BACK) - (CMP_LEN - 1)
        for h in range(NSA_HPG):
            val = jnp.where(d < 0, NEG, _rel_bias_pattern(d, tbl_ref, g * NSA_HPG + h))
            hi = val.astype(BF16).astype(F32)
            lo = jnp.where(d < 0, 0.0, val - hi)
            ext = jnp.where(lane < NEAR, hi, jnp.where(lane < 2 * NEAR, lo, jnp.where(lane == 2 * NEAR, NEG, 0.0)))
            pat_ref[h * TQC:(h + 1) * TQC, :] = ext.astype(BF16)

    tiles_q = range(CMP_TILES)
    qrows = lambda tq: slice(tq * TQC, (tq + 1) * TQC)
    lane = _iota((ns, LANES), 1)
    s = []
    for tq in tiles_q:
        q4 = q_ref[qrows(tq), :]
        q_aug = jnp.concatenate([jnp.concatenate([q4[:, h * NSA_HD:(h + 1) * NSA_HD],
                                                  pat_ref[h * TQC:(h + 1) * TQC, :]], axis=1)
                                 for h in range(NSA_HPG)], axis=0)
        rel = _iota((ns, LANES), 0) - (c0 + tq * (TQC // CMP_STRIDE) - NEAR_BACK)
        ext = ((lane < 2 * NEAR) & (rel == jnp.bitwise_and(lane, NEAR - 1))) | ((lane == 2 * NEAR) & (rel >= NEAR))
        k_aug = jnp.concatenate([kc_ref[...], jnp.where(ext, 1.0, 0.0).astype(BF16)], axis=1)
        s.append(_dot_nt(k_aug, q_aug))

    e, inv = [], []
    for tq in tiles_q:
        m = jnp.max(s[tq], axis=0, keepdims=True)
        e.append(jnp.exp2(s[tq] - m))
        l = jnp.sum(e[tq], axis=0, keepdims=True)
        t_abs = t0 + tq * TQC + jnp.bitwise_and(_iota((1, W), 1), TQC - 1)
        inv.append(jnp.where(t_abs >= CMP_LEN - 1, 1.0 / l, 0.0))

    ratio = SEL_LEN // CMP_STRIDE
    blk_n = _iota((nsel, ns), 0)
    tok_c = _iota((nsel, ns), 1)
    ov = ((tok_c >= ratio * blk_n - (CMP_LEN // CMP_STRIDE - 1)) & (tok_c <= ratio * blk_n + ratio - 1)).astype(F32)
    blk = _iota((nsel, TQC), 0)
    imp, cur = [], []
    for tq in tiles_q:
        o_t = _dot(vct_ref[...], e[tq].astype(BF16)) * inv[tq]
        for h in range(NSA_HPG):
            oc_ref[qrows(tq), h * NSA_HD:(h + 1) * NSA_HD] = o_t[:, h * TQC:(h + 1) * TQC].T
        p = e[tq] * inv[tq]
        ps = p[:, 0:TQC]
        for h in range(1, NSA_HPG):
            ps = ps + p[:, h * TQC:(h + 1) * TQC]
        v = _dot(ov, ps, precision=HIGHEST)
        c = jnp.right_shift(t0 + tq * TQC + _iota((nsel, TQC), 1), SEL_SHIFT)
        v = jnp.where((blk == 0) | (blk == c) | (blk == c - 1), FORCE_SCORE, v)
        imp.append(jnp.where(blk > c, NEG, v))
        cur.append(c)
    k_top = min(SEL_TOPK, nsel)

    def emit(tq, chosen):
        chosen = jnp.concatenate([chosen.astype(BF16), jnp.zeros((LANES - nsel, TQC), BF16)], axis=0)
        eye_q = (_iota((TQC, TQC), 0) == _iota((TQC, TQC), 1)).astype(BF16)
        sel_ref[qrows(tq), :] = ((_dot_nt(eye_q, chosen) - 1.0) * (-NEG)).astype(BF16)

    few = (i + 1) * CMP_TILES * (TQC // SEL_LEN) <= k_top

    @pl.when(few)
    def _():
        for tq in tiles_q:
            emit(tq, jnp.where(blk <= cur[tq], 1.0, 0.0))

    @pl.when(jnp.logical_not(few))
    def _():
        sub = _iota((8, TQC), 0)
        for tq in tiles_q:
            tiles = [imp[tq][8 * k:8 * k + 8, :] for k in range(nsel // 8)]
            counts = [jnp.zeros((8, TQC), jnp.int32) for _ in tiles]
            for jb in range(nsel):
                vj = imp[tq][jb:jb + 1, :]
                for k, tile in enumerate(tiles):
                    if 8 * k > jb:
                        ahead = vj >= tile
                    elif 8 * k + 7 < jb:
                        ahead = vj > tile
                    else:
                        ahead = (vj > tile) | ((vj == tile) & (sub + 8 * k > jb))
                    counts[k] = counts[k] + jnp.where(ahead, 1, 0)
            rank = jnp.concatenate(counts, axis=0)
            emit(tq, jnp.where(rank < k_top, 1.0, 0.0))


def _cmp_sel(rel_bias, qn, kcmp, vcmp_t, B, S):
    ns = S // CMP_STRIDE
    nsel = S // SEL_LEN
    rows = CMP_TILES * TQC
    nt = S // rows
    gw = NSA_HPG * NSA_HD
    assert nsel % 8 == 0 and nsel <= LANES
    return pl.pallas_call(
        functools.partial(_cmp_sel_kernel, ns=ns, nsel=nsel),
        grid=(B, NSA_GROUPS, nt),
        in_specs=[
            pl.BlockSpec(memory_space=pltpu.SMEM),
            pl.BlockSpec((rows, gw), lambda b, g, i: (b * nt + i, g)),
            pl.BlockSpec((None, None, ns, NSA_HD), lambda b, g, i: (b, g, 0, 0)),
            pl.BlockSpec((None, None, NSA_HD, ns), lambda b, g, i: (b, g, 0, 0)),
        ],
        out_specs=[
            pl.BlockSpec((rows, gw), lambda b, g, i: (b * nt + i, g)),
            pl.BlockSpec((None, None, rows, LANES), lambda b, g, i: (b, g, i, 0)),
        ],
        out_shape=[jax.ShapeDtypeStruct((B * S, NSA_HEADS * NSA_HD), F32),
                   jax.ShapeDtypeStruct((B, NSA_GROUPS, S, LANES), BF16)],
        scratch_shapes=[pltpu.VMEM((NSA_HPG * TQC, LANES), BF16)],
        compiler_params=pltpu.CompilerParams(dimension_semantics=("arbitrary", "arbitrary", "arbitrary"),
                                             vmem_limit_bytes=VMEM_LIMIT),
        name="cmp_sel",
    )(rel_bias, qn, kcmp, vcmp_t)


TA = 512
SUB = 128


def _attend_kernel(tbl_ref, q_ref, ksa_ref, vst_ref, kw_ref, vwt_ref, sel_ref, oc_ref, g_ref, z_ref, y_ref,
                   pd_ref, pp_ref, sa_ref, sb_ref, ms_ref, accs_ref, mw_ref, accw_ref, gt_ref, *, gate_col):
    g = pl.program_id(1)
    i = pl.program_id(2)
    nsub = TA // SUB

    krow = _iota((SUB, SUB), 0)
    qcol = _iota((SUB, SUB), 1)

    @pl.when(i == 0)
    def _():
        for h in range(NSA_HPG):
            head = g * NSA_HPG + h
            lanes = slice(h * SUB, (h + 1) * SUB)
            pd_ref[:, lanes] = jnp.where(krow <= qcol, _rel_bias_pattern(qcol - krow, tbl_ref, head), NEG)
            pp_ref[:, lanes] = _rel_bias_pattern(qcol - krow + SUB, tbl_ref, head)

    SLAB = NSA_HPG * SUB
    q4 = q_ref[...]
    sel = sel_ref[...]
    q_win = jnp.concatenate([q4[qb * SUB:(qb + 1) * SUB, h * NSA_HD:(h + 1) * NSA_HD]
                             for qb in range(nsub) for h in range(NSA_HPG)], axis=0)
    q_sel = jnp.concatenate([jnp.concatenate([q4[qb * SUB:(qb + 1) * SUB, h * NSA_HD:(h + 1) * NSA_HD],
                                              sel[qb * SUB:(qb + 1) * SUB, :]], axis=1)
                             for qb in range(nsub) for h in range(NSA_HPG)], axis=0)
    strict = jnp.concatenate([jnp.where(krow > qcol, 0.0, NEG)] * NSA_HPG, axis=1)

    def chunk(c):
        return pl.ds(pl.multiple_of(c * TA, TA), TA)

    def keys(c, lo, hi):
        return pl.ds(pl.multiple_of(c * TA + lo * SUB, SUB), (hi - lo) * SUB)

    def rows(kb):
        return slice(kb * SUB, (kb + 1) * SUB)

    def slab(qb):
        return slice(qb * SLAB, (qb + 1) * SLAB)

    def absorb(s_ref, vt_ref, c, m_ref, acc_ref, first=False, lo=0, hi=nsub, qb=None, have_max=False):
        lanes = slice(None) if qb is None else slab(qb)
        krows = slice(lo * SUB, hi * SUB)
        mx = s_ref[TA:TA + 1, lanes] if have_max else jnp.max(s_ref[krows, lanes], axis=0, keepdims=True)
        if first:
            m_new = mx
        else:
            m_old = m_ref[:, lanes]
            m_new = jnp.maximum(m_old, mx)
        p = jnp.exp2(s_ref[krows, lanes] - m_new).astype(BF16)
        pv = _dot(vt_ref[:, keys(c, lo, hi)], p)
        if first:
            acc_ref[:, lanes] = pv
        else:
            acc_ref[:, lanes] = jnp.exp2(m_old - m_new) * acc_ref[:, lanes] + pv
        m_ref[:, lanes] = m_new

    def qk_sel(s_ref, c, with_max=False):
        s = _dot_nt(ksa_ref[chunk(c), :], q_sel)
        s_ref[0:TA, :] = s
        if with_max:
            s_ref[TA:TA + 1, :] = jnp.max(s, axis=0, keepdims=True)

    def diag(s_ref, k_ref, q_all, vt_ref, m_ref, acc_ref):
        for qb in range(nsub):
            s_ref[0:(qb + 1) * SUB, slab(qb)] = _dot_nt(k_ref[keys(i, 0, qb + 1), :], q_all[slab(qb), :])
            s_ref[rows(qb), slab(qb)] += pd_ref[...]
            if qb >= 1:
                s_ref[rows(qb - 1), slab(qb)] += pp_ref[...]
        for qb in range(nsub):
            absorb(s_ref, vt_ref, i, m_ref, acc_ref, first=True, lo=0, hi=qb + 1, qb=qb)

    def win_prev(s_ref):
        for qb in range(nsub):
            s_ref[qb * SUB:TA, slab(qb)] = _dot_nt(kw_ref[keys(i - 1, qb, nsub), :], q_win[slab(qb), :])
            s_ref[rows(qb), slab(qb)] += strict
        s_ref[rows(nsub - 1), slab(0)] += pp_ref[...]
        for qb in range(nsub):
            absorb(s_ref, vwt_ref, i - 1, mw_ref, accw_ref, lo=qb, hi=nsub, qb=qb)

    @pl.when(i == 0)
    def _():
        diag(sa_ref, ksa_ref, q_sel, vst_ref, ms_ref, accs_ref)
        diag(sb_ref, kw_ref, q_win, vwt_ref, mw_ref, accw_ref)

    @pl.when(i >= 1)
    def _():
        qk_sel(sb_ref, i - 1)
        sb_ref[rows(nsub - 1), slab(0)] += pp_ref[...]
        diag(sa_ref, ksa_ref, q_sel, vst_ref, ms_ref, accs_ref)
        absorb(sb_ref, vst_ref, i - 1, ms_ref, accs_ref)
        diag(sa_ref, kw_ref, q_win, vwt_ref, mw_ref, accw_ref)
        win_prev(sb_ref)

    nfar = jnp.maximum(i - 1, 0)
    odd = nfar % 2

    @pl.when(odd == 1)
    def _():
        qk_sel(sa_ref, 0, with_max=True)
        absorb(sa_ref, vst_ref, 0, ms_ref, accs_ref, have_max=True)

    npair = nfar // 2

    @pl.when(npair > 0)
    def _():
        qk_sel(sa_ref, odd, with_max=True)

    def pair(t, carry):
        c = odd + 2 * t
        qk_sel(sb_ref, c + 1, with_max=True)
        absorb(sa_ref, vst_ref, c, ms_ref, accs_ref, have_max=True)
        qk_sel(sa_ref, jnp.minimum(c + 2, nfar - 1), with_max=True)
        absorb(sb_ref, vst_ref, c + 1, ms_ref, accs_ref, have_max=True)
        return carry

    lax.fori_loop(0, npair, pair, 0)

    gs = _sigmoid_t(g_ref[:, 0:LANES])
    gt_ref[...] = gs.T
    inv_s = 1.0 / accs_ref[NSA_HD:NSA_HD + 1, :]
    inv_w = 1.0 / accw_ref[NSA_HD:NSA_HD + 1, :]
    lane = _iota((TA, LANES), 1)
    for h in range(NSA_HPG):
        base = gate_col + (g * NSA_HPG + h) * N_BRANCH
        cols = slice(h * NSA_HD, (h + 1) * NSA_HD)
        gate_c = jnp.sum(jnp.where(lane == base, gs, 0.0), axis=-1, keepdims=True)
        gs_row = gt_ref[pl.ds(base + 1, 1), :]
        gw_row = gt_ref[pl.ds(base + 2, 1), :]
        o_sw = []
        for qb in range(nsub):
            lanes = slice(qb * SLAB + h * SUB, qb * SLAB + (h + 1) * SUB)
            gate_s = gs_row[:, rows(qb)] * inv_s[:, lanes]
            gate_w = gw_row[:, rows(qb)] * inv_w[:, lanes]
            o_sw.append((gate_s * accs_ref[0:NSA_HD, lanes] + gate_w * accw_ref[0:NSA_HD, lanes]).T)
        o = gate_c * oc_ref[:, cols] + jnp.concatenate(o_sw, axis=0)
        z = z_ref[:, cols]
        y_ref[:, cols] = (o * z * _sigmoid_t(z)).astype(BF16)


def _attend(rel_bias, qn, ksa, vst, kwn, vwt, sel, o_c, gates, gate_blk, u, col_z, gate_col, B, S):
    nt = S // TA
    gw = NSA_HPG * NSA_HD
    W = NSA_HPG * TA
    cz = col_z // gw
    assert WIN == TA
    tile = lambda b, g, i: (b * nt + i, g)
    vt_spec = pl.BlockSpec((None, None, VT_ROWS, S), lambda b, g, i: (b, g, 0, 0))
    return pl.pallas_call(
        functools.partial(_attend_kernel, gate_col=gate_col),
        grid=(B, NSA_GROUPS, nt),
        in_specs=[
            pl.BlockSpec(memory_space=pltpu.SMEM),
            pl.BlockSpec((TA, gw), tile),
            pl.BlockSpec((S, 2 * NSA_HD), lambda b, g, i: (b, g)), vt_spec,
            pl.BlockSpec((S, NSA_HD), lambda b, g, i: (b, g)), vt_spec,
            pl.BlockSpec((None, None, TA, LANES), lambda b, g, i: (b, g, i, 0)),
            pl.BlockSpec((TA, gw), tile),
            pl.BlockSpec((TA, 2 * LANES), lambda b, g, i: (b * nt + i, gate_blk)),
            pl.BlockSpec((TA, gw), lambda b, g, i: (b * nt + i, cz + g)),
        ],
        out_specs=pl.BlockSpec((TA, gw), tile),
        out_shape=jax.ShapeDtypeStruct((B * S, NSA_HEADS * NSA_HD), BF16),
        scratch_shapes=[pltpu.VMEM((SUB, NSA_HPG * SUB), F32), pltpu.VMEM((SUB, NSA_HPG * SUB), F32),
                        pltpu.VMEM((TA + 8, W), F32), pltpu.VMEM((TA + 8, W), F32),
                        pltpu.VMEM((1, W), F32), pltpu.VMEM((VT_ROWS, W), F32),
                        pltpu.VMEM((1, W), F32), pltpu.VMEM((VT_ROWS, W), F32),
                        pltpu.VMEM((LANES, TA), F32)],
        compiler_params=pltpu.CompilerParams(dimension_semantics=("arbitrary", "arbitrary", "arbitrary"),
                                             vmem_limit_bytes=VMEM_LIMIT),
        name="attend",
    )(rel_bias, qn, ksa, vst, kwn, vwt, sel, o_c, gates, u)


def _outproj_kernel(yml_ref, yns_ref, w_hbm, x_ref, o_ref, ob_ref, wf_ref, wb_ref, sem, *, nblk):
    j = pl.program_id(0)
    tn = wf_ref.shape[1]

    def copy_of(jj):
        return pltpu.make_async_copy(w_hbm.at[:, pl.ds(pl.multiple_of(jj * tn, tn), tn)], wf_ref, sem)

    @pl.when(pl.program_id(1) == 0)
    def _():
        _stage_weight_block(copy_of, j, nblk, wf_ref, wb_ref)

    half = yml_ref.shape[1]
    o = x_ref[...] + _dot(yml_ref[...], wb_ref[0:half, :]) + _dot(yns_ref[...], wb_ref[half:2 * half, :])
    o_ref[...] = o
    ob_ref[...] = o.astype(BF16)


def _outproj(y_ml, y_ns, w_out, x2, tm=TILE_OUTPROJ[0], tn=TILE_OUTPROJ[1]):
    T, D = x2.shape
    half = y_ml.shape[1]
    tile = pl.BlockSpec((tm, tn), lambda j, i: (i, j))
    return pl.pallas_call(
        functools.partial(_outproj_kernel, nblk=D // tn),
        grid=(D // tn, T // tm),
        in_specs=[
            pl.BlockSpec((tm, half), lambda j, i: (i, 0)),
            pl.BlockSpec((tm, half), lambda j, i: (i, 0)),
            pl.BlockSpec(memory_space=pl.ANY),
            tile,
        ],
        out_specs=[tile, tile],
        out_shape=[jax.ShapeDtypeStruct((T, D), F32), jax.ShapeDtypeStruct((T, D), BF16)],
        scratch_shapes=[pltpu.VMEM((2 * half, tn), F32), pltpu.VMEM((2 * half, tn), BF16),
                        pltpu.SemaphoreType.DMA(())],
        compiler_params=pltpu.CompilerParams(dimension_semantics=("arbitrary", "arbitrary"),
                                             vmem_limit_bytes=VMEM_LIMIT),
        name="outproj",
    )(y_ml, y_ns, w_out, x2)


def _ple_kernel(xb_ref, x_ref, p_ref, wg_hbm, wp_ref, o_ref, wgf_ref, wgb_ref, wpb_ref, sem, *, nblk):
    j = pl.program_id(0)
    tn = wgf_ref.shape[1]

    def copy_of(jj):
        return pltpu.make_async_copy(wg_hbm.at[:, pl.ds(pl.multiple_of(jj * tn, tn), tn)], wgf_ref, sem)

    @pl.when(pl.program_id(1) == 0)
    def _():
        _stage_weight_block(copy_of, j, nblk, wgf_ref, wgb_ref)
        wpb_ref[...] = wp_ref[...].astype(BF16)

    gate = _sigmoid_t(_dot(xb_ref[...], wgb_ref[...]))
    emb = _dot(p_ref[...].astype(BF16), wpb_ref[...])
    o_ref[...] = x_ref[...] + gate * emb


def _ple(x1b, x1, p2, wg, wp, tm=TILE_OUTPROJ[0], tn=TILE_OUTPROJ[1]):
    T, D = x1.shape
    P = p2.shape[1]
    tile = pl.BlockSpec((tm, tn), lambda j, i: (i, j))
    return pl.pallas_call(
        functools.partial(_ple_kernel, nblk=D // tn),
        grid=(D // tn, T // tm),
        in_specs=[
            pl.BlockSpec((tm, D), lambda j, i: (i, 0)),
            tile,
            pl.BlockSpec((tm, P), lambda j, i: (i, 0)),
            pl.BlockSpec(memory_space=pl.ANY),
            pl.BlockSpec((P, tn), lambda j, i: (0, j)),
        ],
        out_specs=tile,
        out_shape=jax.ShapeDtypeStruct((T, D), F32),
        scratch_shapes=[pltpu.VMEM((D, tn), F32), pltpu.VMEM((D, tn), BF16), pltpu.VMEM((P, tn), BF16),
                        pltpu.SemaphoreType.DMA(())],
        compiler_params=pltpu.CompilerParams(dimension_semantics=("arbitrary", "arbitrary"),
                                             vmem_limit_bytes=VMEM_LIMIT),
        name="ple",
    )(x1b, x1, p2, wg, wp)


def _layer(x2, p2, norm_w, w_in, conv_w, i_bias, f_bias, head_norm_w, q_norm_w, k_norm_w,
           pe_k, pe_v, kw1, kw2, vw1, vw2, rel_bias, w_out, ple_proj, ple_gate, B, S):
    D = x2.shape[1]
    qkw = ML_HEADS * ML_DQK
    vw = ML_HEADS * ML_DV
    nq = NSA_HEADS * NSA_HD
    nkv = NSA_GROUPS * NSA_HD
    o_i = 2 * qkw + 3 * vw
    o_f = o_i + ML_HEADS
    o_nq = o_f + ML_HEADS
    o_g = o_nq + nq + 6 * nkv
    o_z = o_g + NSA_HEADS * N_BRANCH
    w_t = w_in.T
    ngate = NSA_HEADS * N_BRANCH
    zeros = lambda n: jnp.zeros((n, D), w_in.dtype)
    wg_t = jnp.concatenate([w_t[o_i:o_f], w_t[o_g:o_z], zeros(LANES - ML_HEADS - ngate),
                            w_t[o_f:o_nq], zeros(LANES - ML_HEADS)], axis=0)
    col_nq = o_i
    col_kv = col_nq + nq
    col_z = col_kv + 6 * nkv

    h, gates = _rmsnorm(x2, norm_w.reshape(1, D), wg_t)
    u = _inproj(h, w_t, [(0, o_i), (o_nq, o_g - o_nq), (o_z, nq)])

    pad_h = lambda v: jnp.concatenate([v, jnp.zeros((LANES - ML_HEADS,), v.dtype)]).reshape(1, LANES)
    y_ml = _mlstm(u, gates, 0, conv_w, pad_h(i_bias), pad_h(f_bias), head_norm_w, B, S)

    qn, ksa, vst, kwn, vwt = _nsa_prep(u, q_norm_w.reshape(1, NSA_HD), k_norm_w, col_nq, col_kv, B, S)
    kcmp, vcmp_t = _compress(u, pe_k, pe_v, kw1.astype(BF16), kw2.astype(BF16), vw1.astype(BF16),
                             vw2.astype(BF16), k_norm_w, col_kv, B, S)
    o_c, sel = _cmp_sel(rel_bias, qn, kcmp, vcmp_t, B, S)
    y_ns = _attend(rel_bias, qn, ksa, vst, kwn, vwt, sel, o_c, gates, 0, u, col_z, ML_HEADS, B, S)

    x1, x1b = _outproj(y_ml, y_ns, w_out, x2)
    return _ple(x1b, x1, p2, ple_gate, ple_proj)


def kernel(x, p, norm_w, w_in, ml_conv_w, ml_i_bias, ml_f_bias, ml_head_norm_w, nsa_q_norm_w, nsa_k_norm_w,
           cmp_pe_k, cmp_pe_v, cmp_k_w1, cmp_k_w2, cmp_v_w1, cmp_v_w2, rel_bias, w_out, ple_proj, ple_gate):
    B, S, D = x.shape
    assert S % max(WIN, 256) == 0 and S // SEL_LEN >= 1
    x2 = x.reshape(B * S, D)
    for layer in range(w_in.shape[0]):
        x2 = _layer(x2, p[layer].reshape(B * S, -1), norm_w[layer], w_in[layer], ml_conv_w[layer],
                    ml_i_bias[layer], ml_f_bias[layer], ml_head_norm_w[layer], nsa_q_norm_w[layer],
                    nsa_k_norm_w[layer], cmp_pe_k[layer], cmp_pe_v[layer], cmp_k_w1[layer], cmp_k_w2[layer],
                    cmp_v_w1[layer], cmp_v_w2[layer], rel_bias, w_out[layer], ple_proj[layer], ple_gate[layer],
                    B, S)
    return x2.reshape(B, S, D)
```

```python
import functools
import math

import numpy as np
import jax
import jax.numpy as jnp
from jax import lax
from jax.experimental import pallas as pl
from jax.experimental.pallas import tpu as pltpu

F32 = jnp.float32
BF16 = jnp.bfloat16
HIGHEST = lax.Precision.HIGHEST

ML_HEADS = 8
ML_DQK = 128
ML_DV = 256
ML_CHUNK = 64
ML_CONV = 4
NSA_HEADS = 16
NSA_HD = 128
NSA_GROUPS = 4
NSA_HPG = 4
N_BRANCH = 3
CMP_STRIDE = 16
CMP_LEN = 32
SEL_LEN = 64
SEL_SHIFT = 6
SEL_TOPK = 16
WIN = 512
REL_BUCKETS = 32
REL_MAX_DIST = 128
EPS = 1e-6
NEG = -1e30
FORCE_SCORE = 1e4
LOG2E = math.log2(math.e)

LANES = 128
VMEM_LIMIT = 56 * 1024 * 1024

ROWS_RMSNORM = 512
TILE_INPROJ = (1024, 1024)
TILE_OUTPROJ = (512, 1024)
ROWS_MLSTM = 512
ROWS_NSA_PREP = 512


def _bucket_thresholds():
    n = np.arange(0, 4 * REL_MAX_DIST, dtype=np.int64)
    max_exact = REL_BUCKETS // 2
    nf = np.maximum(n, 1).astype(np.float32)
    large = max_exact + (np.log(nf / np.float32(max_exact)) / np.float32(math.log(REL_MAX_DIST / max_exact))
                         * np.float32(REL_BUCKETS - max_exact)).astype(np.int32)
    large = np.minimum(large, REL_BUCKETS - 1)
    bucket = np.where(n < max_exact, n, large)
    assert np.all(np.diff(bucket) >= 0)
    thr = [int(np.argmax(bucket >= b)) for b in range(REL_BUCKETS)]
    assert thr[REL_BUCKETS - 1] <= REL_MAX_DIST
    return thr


BUCKET_THR = _bucket_thresholds()


def _dot(a, b, precision=None):
    return jnp.dot(a, b, preferred_element_type=F32, precision=precision)


def _dot_nt(a, b, precision=None):
    return lax.dot_general(a, b, (((1,), (1,)), ((), ())), preferred_element_type=F32, precision=precision)


def _sigmoid_t(x):
    return 0.5 * jnp.tanh(0.5 * x) + 0.5


def _silu_t(x):
    h = 0.5 * x
    return h * jnp.tanh(h) + h


def _iota(shape, dim):
    return lax.broadcasted_iota(jnp.int32, shape, dim)


def _rel_bias_pattern(dist, tbl_ref, head):
    val = jnp.full(dist.shape, tbl_ref[0, head], F32)
    for b in range(1, REL_BUCKETS):
        val = jnp.where(dist >= BUCKET_THR[b], tbl_ref[b, head], val)
    return (val - tbl_ref[REL_BUCKETS - 1, head]) * LOG2E


def _rmsnorm_kernel(x_ref, nw_ref, wg_ref, h_ref, g_ref, wgb_ref):
    @pl.when(pl.program_id(0) == 0)
    def _():
        wgb_ref[...] = wg_ref[...].astype(BF16)

    x = x_ref[...]
    ms = jnp.mean(x * x, axis=-1, keepdims=True)
    h = (x * lax.rsqrt(ms + EPS) * nw_ref[...]).astype(BF16)
    h_ref[...] = h
    g_ref[...] = _dot_nt(h, wgb_ref[...])


def _rmsnorm(x2, norm_w, wg_t, rb=ROWS_RMSNORM):
    T, D = x2.shape
    NG = wg_t.shape[0]
    return pl.pallas_call(
        _rmsnorm_kernel,
        grid=(T // rb,),
        in_specs=[pl.BlockSpec((rb, D), lambda i: (i, 0)), pl.BlockSpec((1, D), lambda i: (0, 0)),
                  pl.BlockSpec((NG, D), lambda i: (0, 0))],
        out_specs=[pl.BlockSpec((rb, D), lambda i: (i, 0)), pl.BlockSpec((rb, NG), lambda i: (i, 0))],
        out_shape=[jax.ShapeDtypeStruct((T, D), BF16), jax.ShapeDtypeStruct((T, NG), F32)],
        scratch_shapes=[pltpu.VMEM((NG, D), BF16)],
        compiler_params=pltpu.CompilerParams(dimension_semantics=("arbitrary",), vmem_limit_bytes=VMEM_LIMIT),
        name="rmsnorm",
    )(x2, norm_w, wg_t)


def _stage_weight_block(copy_of, j, nblk, wf_ref, wb_ref):
    @pl.when(j == 0)
    def _():
        copy_of(j).start()

    copy_of(j).wait()
    wb_ref[...] = wf_ref[...].astype(BF16)

    @pl.when(j + 1 < nblk)
    def _():
        copy_of(j + 1).start()


def _inproj_kernel(h_ref, wt_hbm, u_ref, wf_ref, wb_ref, sem, *, nblk, row_of):
    j = pl.program_id(0)

    def copy_of(jj):
        return pltpu.make_async_copy(wt_hbm.at[pl.ds(row_of(jj), wf_ref.shape[0]), :], wf_ref, sem)

    @pl.when(pl.program_id(1) == 0)
    def _():
        _stage_weight_block(copy_of, j, nblk, wf_ref, wb_ref)

    u_ref[...] = _dot_nt(h_ref[...], wb_ref[...])


def _piece_blocks(pieces, tn):
    starts = []
    for start, length in pieces:
        assert length % tn == 0 and start % 8 == 0
        starts += [start + k * tn for k in range(length // tn)]
    bounds = [(b, s) for b, s in enumerate(starts) if b == 0 or s != starts[b - 1] + tn]

    def w_rows(j, i):
        row8 = (bounds[0][1] + j * tn) // 8
        for b, s in bounds[1:]:
            row8 = jnp.where(j >= b, (s + (j - b) * tn) // 8, row8)
        return row8 * 8, 0

    return len(starts), w_rows


def _inproj(h, w_t, pieces, tm=TILE_INPROJ[0], tn=TILE_INPROJ[1]):
    T, D = h.shape
    nblk, w_rows = _piece_blocks(pieces, tn)
    return pl.pallas_call(
        functools.partial(_inproj_kernel, nblk=nblk, row_of=lambda j: w_rows(j, 0)[0]),
        grid=(nblk, T // tm),
        in_specs=[
            pl.BlockSpec((tm, D), lambda j, i: (i, 0)),
            pl.BlockSpec(memory_space=pl.ANY),
        ],
        out_specs=pl.BlockSpec((tm, tn), lambda j, i: (i, j)),
        out_shape=jax.ShapeDtypeStruct((T, nblk * tn), F32),
        scratch_shapes=[pltpu.VMEM((tn, D), F32), pltpu.VMEM((tn, D), BF16), pltpu.SemaphoreType.DMA(())],
        compiler_params=pltpu.CompilerParams(dimension_semantics=("arbitrary", "arbitrary"),
                                             vmem_limit_bytes=VMEM_LIMIT),
        name="inproj",
    )(h, w_t)


def _mlstm_kernel(q_ref, k_ref, v_ref, o_ref, z_ref, g_ref, cw_ref, ib_ref, fb_ref, hw_ref, y_ref,
                  xbuf, c_ref, m_ref, *, lb):
    L = ML_CHUNK
    qkw = ML_HEADS * ML_DQK

    @pl.when(pl.program_id(1) == 0)
    def _():
        xbuf[0:8, :] = jnp.zeros((8, 2 * qkw), F32)
        c_ref[...] = jnp.zeros(c_ref.shape, F32)
        m_ref[...] = jnp.zeros(m_ref.shape, F32)

    row = _iota((L, L), 0)
    col = _iota((L, L), 1)
    tril = col <= row
    rowg = _iota((L, LANES), 0)
    eye_h = (_iota((8, LANES), 0) == _iota((8, LANES), 1)).astype(F32)
    ones_col = (_iota((L, LANES), 1) == 0).astype(BF16)
    heads = range(ML_HEADS)

    def chunk(c, carry):
        rows = pl.ds(pl.multiple_of(c * L, L), L)
        qk_cols = lambda h: slice(h * ML_DQK, (h + 1) * ML_DQK)
        v_cols = lambda h: slice(h * ML_DV, (h + 1) * ML_DV)
        col_of = lambda x, h: x[:, h:h + 1]

        G = g_ref[rows, :]
        li = G[:, 0:LANES] + ib_ref[...]
        fp = G[:, LANES:2 * LANES] + fb_ref[...]
        lf = jnp.minimum(fp, 0.0) - jnp.log(1.0 + jnp.exp(-jnp.abs(fp)))
        b = lf
        for sh in (1, 2, 4, 8, 16, 32):
            b = b + jnp.where(rowg >= sh, pltpu.roll(b, sh, 0), 0.0)
        g = li - b
        gmax = g
        for sh in (1, 2, 4, 8, 16, 32):
            gmax = jnp.maximum(gmax, jnp.where(rowg >= sh, pltpu.roll(gmax, sh, 0), NEG))
        m_old = m_ref[...]
        top = jnp.maximum(m_old, gmax)
        m_t = b + top
        inter = jnp.exp(m_old - top)
        floor = jnp.exp(-m_t)
        bL = b[L - 1:L, :]
        w = bL - b + li
        m_new = jnp.maximum(bL + m_old, jnp.max(w, axis=0, keepdims=True))
        wk = jnp.exp(w - m_new)
        decay = jnp.exp(bL + m_old - m_new)
        m_ref[...] = m_new
        g_rows = _dot_nt(eye_h, g, precision=HIGHEST)

        xbuf[8:8 + L, 0:qkw] = q_ref[rows, :]
        xbuf[8:8 + L, qkw:2 * qkw] = k_ref[rows, :]
        acc = cw_ref[ML_CONV - 1:ML_CONV, :] * xbuf[8:8 + L, :]
        for j in range(1, ML_CONV):
            acc = acc + cw_ref[ML_CONV - 1 - j:ML_CONV - j, :] * xbuf[8 - j:8 - j + L, :]
        act = _silu_t(acc)
        xbuf[0:8, :] = xbuf[L:L + 8, :]

        q = [act[:, qk_cols(h)] for h in heads]
        kf = [act[:, qkw + h * ML_DQK:qkw + (h + 1) * ML_DQK] * (ML_DQK ** -0.5) for h in heads]
        vaug = [jnp.concatenate([v_ref[rows, v_cols(h)].astype(BF16), ones_col], axis=1) for h in heads]
        cst = [c_ref[h] for h in heads]
        lhs = []
        for h in heads:
            dw = jnp.exp(jnp.where(tril, g_rows[h:h + 1, :] - col_of(top, h), NEG))
            sc = (_dot_nt(q[h].astype(BF16), kf[h].astype(BF16)) * dw).astype(BF16)
            q_in = (q[h] * col_of(inter, h)).astype(BF16)
            lhs.append(jnp.concatenate([q_in, sc], axis=1))
        num_aug = [_dot(lhs[h], jnp.concatenate([cst[h].astype(BF16), vaug[h]], axis=0)) for h in heads]
        for h in heads:
            kw = (kf[h] * col_of(wk, h)).astype(BF16)
            upd = lax.dot_general(kw, vaug[h], (((0,), (0,)), ((), ())), preferred_element_type=F32)
            c_ref[h] = decay[:, h:h + 1] * cst[h] + upd
        for h in heads:
            num = num_aug[h][:, 0:ML_DV]
            den = num_aug[h][:, ML_DV:ML_DV + 1]
            r = 1.0 / jnp.maximum(jnp.abs(den), col_of(floor, h))
            f = r * lax.rsqrt(r * r * jnp.mean(num * num, axis=-1, keepdims=True) + EPS)
            o = o_ref[rows, v_cols(h)]
            z = z_ref[rows, v_cols(h)]
            gate = _sigmoid_t(o) * _silu_t(z)
            y_ref[rows, v_cols(h)] = (num * f * hw_ref[h:h + 1, :] * gate).astype(BF16)
        return carry

    lax.fori_loop(0, lb // L, chunk, 0)


def _mlstm(u, gates, gate_blk, conv_w, ib, fb, head_w, B, S, lb=ROWS_MLSTM):
    T = B * S
    nsb = S // lb
    qkw = ML_HEADS * ML_DQK
    vw = ML_HEADS * ML_DV
    rowmap = lambda col: (lambda b, s: (b * nsb + s, col))
    const = lambda b, s: (0, 0)
    return pl.pallas_call(
        functools.partial(_mlstm_kernel, lb=lb),
        grid=(B, nsb),
        in_specs=[
            pl.BlockSpec((lb, qkw), rowmap(0)),
            pl.BlockSpec((lb, qkw), rowmap(1)),
            pl.BlockSpec((lb, vw), rowmap(1)),
            pl.BlockSpec((lb, vw), rowmap(2)),
            pl.BlockSpec((lb, vw), rowmap(3)),
            pl.BlockSpec((lb, 2 * LANES), rowmap(gate_blk)),
            pl.BlockSpec((ML_CONV, 2 * qkw), const),
            pl.BlockSpec((1, LANES), const),
            pl.BlockSpec((1, LANES), const),
            pl.BlockSpec((ML_HEADS, ML_DV), const),
        ],
        out_specs=pl.BlockSpec((lb, vw), rowmap(0)),
        out_shape=jax.ShapeDtypeStruct((T, vw), BF16),
        scratch_shapes=[
            pltpu.VMEM((ML_CHUNK + 8, 2 * qkw), F32),
            pltpu.VMEM((ML_HEADS, ML_DQK, ML_DV + LANES), F32),
            pltpu.VMEM((1, LANES), F32),
        ],
        compiler_params=pltpu.CompilerParams(dimension_semantics=("arbitrary", "arbitrary"),
                                             vmem_limit_bytes=VMEM_LIMIT),
        name="mlstm",
    )(u, u, u, u, u, gates, conv_w, ib, fb, head_w)


def _rms_heads(x, w, scale=1.0):
    outs = []
    for h in range(x.shape[1] // NSA_HD):
        xh = x[:, h * NSA_HD:(h + 1) * NSA_HD]
        ms = jnp.mean(xh * xh, axis=-1, keepdims=True)
        outs.append(xh * lax.rsqrt(ms + EPS) * w * scale)
    return jnp.concatenate(outs, axis=1)


VT_ROWS = NSA_HD + 16


def _nsa_prep_kernel(q_ref, ks_ref, vs_ref, kw_ref, vw_ref, qw_ref, kn_ref,
                     qn_ref, ksa_ref, vst_ref, kwn_ref, vwt_ref, *, rb, nsb):
    qn_ref[...] = _rms_heads(q_ref[...], qw_ref[...], NSA_HD ** -0.5 * LOG2E).astype(BF16)
    ksn = _rms_heads(ks_ref[...], kn_ref[1:2, :]).astype(BF16)
    kwn_ref[...] = _rms_heads(kw_ref[...], kn_ref[2:3, :]).astype(BF16)
    t = (pl.program_id(0) % nsb) * rb + _iota((rb, NSA_HD), 0)
    onehot = (_iota((rb, NSA_HD), 1) == jnp.right_shift(t, SEL_SHIFT)).astype(BF16)
    tail = (_iota((VT_ROWS - NSA_HD, rb), 0) == 0).astype(BF16)
    for g in range(NSA_GROUPS):
        cols = slice(g * NSA_HD, (g + 1) * NSA_HD)
        ksa_ref[:, 2 * g * NSA_HD:(2 * g + 1) * NSA_HD] = ksn[:, cols]
        ksa_ref[:, (2 * g + 1) * NSA_HD:(2 * g + 2) * NSA_HD] = onehot
        vst_ref[g] = jnp.concatenate([vs_ref[:, cols].T.astype(BF16), tail], axis=0)
        vwt_ref[g] = jnp.concatenate([vw_ref[:, cols].T.astype(BF16), tail], axis=0)


def _nsa_prep(u, q_norm_w, k_norm_w, col_q, col_kv, B, S, rb=ROWS_NSA_PREP):
    T = u.shape[0]
    nsb = S // rb
    qw = NSA_HEADS * NSA_HD
    kvw = NSA_GROUPS * NSA_HD
    cq = col_q // qw
    ck = col_kv // kvw
    assert S // SEL_LEN <= NSA_HD
    kv_spec = lambda idx: pl.BlockSpec((rb, kvw), lambda i: (i, ck + idx))
    out_k = pl.BlockSpec((rb, kvw), lambda i: (i, 0))
    out_ka = pl.BlockSpec((rb, 2 * kvw), lambda i: (i, 0))
    out_vt = pl.BlockSpec((None, NSA_GROUPS, VT_ROWS, rb), lambda i: (i // nsb, 0, 0, i % nsb))
    vt_shape = jax.ShapeDtypeStruct((B, NSA_GROUPS, VT_ROWS, S), BF16)
    return pl.pallas_call(
        functools.partial(_nsa_prep_kernel, rb=rb, nsb=nsb),
        grid=(T // rb,),
        in_specs=[
            pl.BlockSpec((rb, qw), lambda i: (i, cq)),
            kv_spec(2), kv_spec(3), kv_spec(4), kv_spec(5),
            pl.BlockSpec((1, NSA_HD), lambda i: (0, 0)),
            pl.BlockSpec((N_BRANCH, NSA_HD), lambda i: (0, 0)),
        ],
        out_specs=[pl.BlockSpec((rb, qw), lambda i: (i, 0)), out_ka, out_vt, out_k, out_vt],
        out_shape=[jax.ShapeDtypeStruct((T, qw), BF16), jax.ShapeDtypeStruct((T, 2 * kvw), BF16), vt_shape,
                   jax.ShapeDtypeStruct((T, kvw), BF16), vt_shape],
        compiler_params=pltpu.CompilerParams(dimension_semantics=("arbitrary",), vmem_limit_bytes=VMEM_LIMIT),
        name="nsa_prep",
    )(u, u, u, u, u, q_norm_w, k_norm_w)


def _compress_kernel(kc_ref, vc_ref, pek_ref, pev_ref, kw1_ref, kw2_ref, vw1_ref, vw2t_ref, kn_ref,
                     kcmp_ref, vcmp_ref, *, ns):
    def hidden(x_ref, pe_ref, w1_ref):
        hid = CMP_LEN // 2
        a = jnp.zeros((ns, w1_ref.shape[1]), F32)
        bsum = jnp.zeros((ns, w1_ref.shape[1]), F32)
        for l in range(0, hid, 2):
            xl = [x_ref[pl.ds(l + t, ns, stride=CMP_STRIDE), :] for t in range(2)]
            first = jnp.concatenate([(xl[t] + pe_ref[l + t:l + t + 1, :]).astype(BF16) for t in range(2)], axis=1)
            second = jnp.concatenate([(xl[t] + pe_ref[hid + l + t:hid + l + t + 1, :]).astype(BF16)
                                      for t in range(2)], axis=1)
            a = a + _dot(first, w1_ref[l * NSA_HD:(l + 2) * NSA_HD, :])
            bsum = bsum + _dot(second, w1_ref[(hid + l) * NSA_HD:(hid + l + 2) * NSA_HD, :])
        pre = a + pltpu.roll(bsum, ns - 1, 0)
        return _silu_t(pre).astype(BF16)

    kc = _dot(hidden(kc_ref, pek_ref, kw1_ref), kw2_ref[...])
    ms = jnp.mean(kc * kc, axis=-1, keepdims=True)
    kcmp_ref[...] = (kc * lax.rsqrt(ms + EPS) * kn_ref[0:1, :]).astype(BF16)
    vcmp_ref[...] = _dot_nt(vw2t_ref[...], hidden(vc_ref, pev_ref, vw1_ref)).astype(BF16)


def _compress(u, pe_k, pe_v, kw1, kw2, vw1, vw2, k_norm_w, col_kv, B, S):
    ns = S // CMP_STRIDE
    ck = col_kv // NSA_HD
    hidden = kw1.shape[1]
    const = lambda b, g: (0, 0)
    return pl.pallas_call(
        functools.partial(_compress_kernel, ns=ns),
        grid=(B, NSA_GROUPS),
        in_specs=[
            pl.BlockSpec((S, NSA_HD), lambda b, g: (b, ck + g)),
            pl.BlockSpec((S, NSA_HD), lambda b, g: (b, ck + NSA_GROUPS + g)),
            pl.BlockSpec((CMP_LEN, NSA_HD), const),
            pl.BlockSpec((CMP_LEN, NSA_HD), const),
            pl.BlockSpec((CMP_LEN * NSA_HD, hidden), const),
            pl.BlockSpec((hidden, NSA_HD), const),
            pl.BlockSpec((CMP_LEN * NSA_HD, hidden), const),
            pl.BlockSpec((NSA_HD, hidden), const),
            pl.BlockSpec((N_BRANCH, NSA_HD), const),
        ],
        out_specs=[pl.BlockSpec((None, None, ns, NSA_HD), lambda b, g: (b, g, 0, 0)),
                   pl.BlockSpec((None, None, NSA_HD, ns), lambda b, g: (b, g, 0, 0))],
        out_shape=[jax.ShapeDtypeStruct((B, NSA_GROUPS, ns, NSA_HD), BF16),
                   jax.ShapeDtypeStruct((B, NSA_GROUPS, NSA_HD, ns), BF16)],
        compiler_params=pltpu.CompilerParams(dimension_semantics=("arbitrary", "arbitrary"),
                                             vmem_limit_bytes=VMEM_LIMIT),
        name="compress",
    )(u, u, pe_k, pe_v, kw1, kw2, vw1, vw2.T, k_norm_w)


TQC = 256
CMP_TILES = 4
NEAR = 32
NEAR_BACK = 16


def _cmp_sel_kernel(tbl_ref, q_ref, kc_ref, vct_ref, oc_ref, sel_ref, pat_ref, *, ns, nsel):
    g = pl.program_id(1)
    i = pl.program_id(2)
    t0 = i * (CMP_TILES * TQC)
    c0 = i * (CMP_TILES * TQC // CMP_STRIDE)
    W = NSA_HPG * TQC
    assert (TQC - CMP_LEN) // CMP_STRIDE < NEAR - NEAR_BACK and NEAR_BACK * CMP_STRIDE >= REL_MAX_DIST + CMP_LEN

    @pl.when(i == 0)
    def _():
        r = _iota((TQC, LANES), 0)
        lane = _iota((TQC, LANES), 1)
        d = r - CMP_STRIDE * (jnp.bitwise_and(lane, NEAR - 1) - NEAR_BACK) - (CMP_LEN - 1)
        for h in range(NSA_HPG):
            val = jnp.where(d < 0, NEG, _rel_bias_pattern(d, tbl_ref, g * NSA_HPG + h))
            hi = val.astype(BF16).astype(F32)
            lo = jnp.where(d < 0, 0.0, val - hi)
            ext = jnp.where(lane < NEAR, hi, jnp.where(lane < 2 * NEAR, lo, jnp.where(lane == 2 * NEAR, NEG, 0.0)))
            pat_ref[h * TQC:(h + 1) * TQC, :] = ext.astype(BF16)

    tiles_q = range(CMP_TILES)
    qrows = lambda tq: slice(tq * TQC, (tq + 1) * TQC)
    lane = _iota((ns, LANES), 1)
    s = []
    for tq in tiles_q:
        q4 = q_ref[qrows(tq), :]
        q_aug = jnp.concatenate([jnp.concatenate([q4[:, h * NSA_HD:(h + 1) * NSA_HD],
                                                  pat_ref[h * TQC:(h + 1) * TQC, :]], axis=1)
                                 for h in range(NSA_HPG)], axis=0)
        rel = _iota((ns, LANES), 0) - (c0 + tq * (TQC // CMP_STRIDE) - NEAR_BACK)
        ext = ((lane < 2 * NEAR) & (rel == jnp.bitwise_and(lane, NEAR - 1))) | ((lane == 2 * NEAR) & (rel >= NEAR))
        k_aug = jnp.concatenate([kc_ref[...], jnp.where(ext, 1.0, 0.0).astype(BF16)], axis=1)
        s.append(_dot_nt(k_aug, q_aug))

    e, inv = [], []
    for tq in tiles_q:
        m = jnp.max(s[tq], axis=0, keepdims=True)
        e.append(jnp.exp2(s[tq] - m))
        l = jnp.sum(e[tq], axis=0, keepdims=True)
        t_abs = t0 + tq * TQC + jnp.bitwise_and(_iota((1, W), 1), TQC - 1)
        inv.append(jnp.where(t_abs >= CMP_LEN - 1, 1.0 / l, 0.0))

    ratio = SEL_LEN // CMP_STRIDE
    blk_n = _iota((nsel, ns), 0)
    tok_c = _iota((nsel, ns), 1)
    ov = ((tok_c >= ratio * blk_n - (CMP_LEN // CMP_STRIDE - 1)) & (tok_c <= ratio * blk_n + ratio - 1)).astype(F32)
    blk = _iota((nsel, TQC), 0)
    imp, cur = [], []
    for tq in tiles_q:
        o_t = _dot(vct_ref[...], e[tq].astype(BF16)) * inv[tq]
        for h in range(NSA_HPG):
            oc_ref[qrows(tq), h * NSA_HD:(h + 1) * NSA_HD] = o_t[:, h * TQC:(h + 1) * TQC].T
        p = e[tq] * inv[tq]
        ps = p[:, 0:TQC]
        for h in range(1, NSA_HPG):
            ps = ps + p[:, h * TQC:(h + 1) * TQC]
        v = _dot(ov, ps, precision=HIGHEST)
        c = jnp.right_shift(t0 + tq * TQC + _iota((nsel, TQC), 1), SEL_SHIFT)
        v = jnp.where((blk == 0) | (blk == c) | (blk == c - 1), FORCE_SCORE, v)
        imp.append(jnp.where(blk > c, NEG, v))
        cur.append(c)
    k_top = min(SEL_TOPK, nsel)

    def emit(tq, chosen):
        chosen = jnp.concatenate([chosen.astype(BF16), jnp.zeros((LANES - nsel, TQC), BF16)], axis=0)
        eye_q = (_iota((TQC, TQC), 0) == _iota((TQC, TQC), 1)).astype(BF16)
        sel_ref[qrows(tq), :] = ((_dot_nt(eye_q, chosen) - 1.0) * (-NEG)).astype(BF16)

    few = (i + 1) * CMP_TILES * (TQC // SEL_LEN) <= k_top

    @pl.when(few)
    def _():
        for tq in tiles_q:
            emit(tq, jnp.where(blk <= cur[tq], 1.0, 0.0))

    @pl.when(jnp.logical_not(few))
    def _():
        sub = _iota((8, TQC), 0)
        for tq in tiles_q:
            tiles = [imp[tq][8 * k:8 * k + 8, :] for k in range(nsel // 8)]
            counts = [jnp.zeros((8, TQC), jnp.int32) for _ in tiles]
            for jb in range(nsel):
                vj = imp[tq][jb:jb + 1, :]
                for k, tile in enumerate(tiles):
                    if 8 * k > jb:
                        ahead = vj >= tile
                    elif 8 * k + 7 < jb:
                        ahead = vj > tile
                    else:
                        ahead = (vj > tile) | ((vj == tile) & (sub + 8 * k > jb))
                    counts[k] = counts[k] + jnp.where(ahead, 1, 0)
            rank = jnp.concatenate(counts, axis=0)
            emit(tq, jnp.where(rank < k_top, 1.0, 0.0))


def _cmp_sel(rel_bias, qn, kcmp, vcmp_t, B, S):
    ns = S // CMP_STRIDE
    nsel = S // SEL_LEN
    rows = CMP_TILES * TQC
    nt = S // rows
    gw = NSA_HPG * NSA_HD
    assert nsel % 8 == 0 and nsel <= LANES
    return pl.pallas_call(
        functools.partial(_cmp_sel_kernel, ns=ns, nsel=nsel),
        grid=(B, NSA_GROUPS, nt),
        in_specs=[
            pl.BlockSpec(memory_space=pltpu.SMEM),
            pl.BlockSpec((rows, gw), lambda b, g, i: (b * nt + i, g)),
            pl.BlockSpec((None, None, ns, NSA_HD), lambda b, g, i: (b, g, 0, 0)),
            pl.BlockSpec((None, None, NSA_HD, ns), lambda b, g, i: (b, g, 0, 0)),
        ],
        out_specs=[
            pl.BlockSpec((rows, gw), lambda b, g, i: (b * nt + i, g)),
            pl.BlockSpec((None, None, rows, LANES), lambda b, g, i: (b, g, i, 0)),
        ],
        out_shape=[jax.ShapeDtypeStruct((B * S, NSA_HEADS * NSA_HD), F32),
                   jax.ShapeDtypeStruct((B, NSA_GROUPS, S, LANES), BF16)],
        scratch_shapes=[pltpu.VMEM((NSA_HPG * TQC, LANES), BF16)],
        compiler_params=pltpu.CompilerParams(dimension_semantics=("arbitrary", "arbitrary", "arbitrary"),
                                             vmem_limit_bytes=VMEM_LIMIT),
        name="cmp_sel",
    )(rel_bias, qn, kcmp, vcmp_t)


TA = 512
SUB = 128


def _attend_kernel(tbl_ref, q_ref, ksa_ref, vst_ref, kw_ref, vwt_ref, sel_ref, oc_ref, g_ref, z_ref, y_ref,
                   pd_ref, pp_ref, sa_ref, sb_ref, ms_ref, accs_ref, mw_ref, accw_ref, gt_ref, *, gate_col):
    g = pl.program_id(1)
    i = pl.program_id(2)
    nsub = TA // SUB

    krow = _iota((SUB, SUB), 0)
    qcol = _iota((SUB, SUB), 1)

    @pl.when(i == 0)
    def _():
        for h in range(NSA_HPG):
            head = g * NSA_HPG + h
            lanes = slice(h * SUB, (h + 1) * SUB)
            pd_ref[:, lanes] = jnp.where(krow <= qcol, _rel_bias_pattern(qcol - krow, tbl_ref, head), NEG)
            pp_ref[:, lanes] = _rel_bias_pattern(qcol - krow + SUB, tbl_ref, head)

    SLAB = NSA_HPG * SUB
    q4 = q_ref[...]
    sel = sel_ref[...]
    q_win = jnp.concatenate([q4[qb * SUB:(qb + 1) * SUB, h * NSA_HD:(h + 1) * NSA_HD]
                             for qb in range(nsub) for h in range(NSA_HPG)], axis=0)
    q_sel = jnp.concatenate([jnp.concatenate([q4[qb * SUB:(qb + 1) * SUB, h * NSA_HD:(h + 1) * NSA_HD],
                                              sel[qb * SUB:(qb + 1) * SUB, :]], axis=1)
                             for qb in range(nsub) for h in range(NSA_HPG)], axis=0)
    strict = jnp.concatenate([jnp.where(krow > qcol, 0.0, NEG)] * NSA_HPG, axis=1)

    def chunk(c):
        return pl.ds(pl.multiple_of(c * TA, TA), TA)

    def keys(c, lo, hi):
        return pl.ds(pl.multiple_of(c * TA + lo * SUB, SUB), (hi - lo) * SUB)

    def rows(kb):
        return slice(kb * SUB, (kb + 1) * SUB)

    def slab(qb):
        return slice(qb * SLAB, (qb + 1) * SLAB)

    def absorb(s_ref, vt_ref, c, m_ref, acc_ref, first=False, lo=0, hi=nsub, qb=None, have_max=False):
        lanes = slice(None) if qb is None else slab(qb)
        krows = slice(lo * SUB, hi * SUB)
        mx = s_ref[TA:TA + 1, lanes] if have_max else jnp.max(s_ref[krows, lanes], axis=0, keepdims=True)
        if first:
            m_new = mx
        else:
            m_old = m_ref[:, lanes]
            m_new = jnp.maximum(m_old, mx)
        p = jnp.exp2(s_ref[krows, lanes] - m_new).astype(BF16)
        pv = _dot(vt_ref[:, keys(c, lo, hi)], p)
        if first:
            acc_ref[:, lanes] = pv
        else:
            acc_ref[:, lanes] = jnp.exp2(m_old - m_new) * acc_ref[:, lanes] + pv
        m_ref[:, lanes] = m_new

    def qk_sel(s_ref, c, with_max=False):
        s = _dot_nt(ksa_ref[chunk(c), :], q_sel)
        s_ref[0:TA, :] = s
        if with_max:
            s_ref[TA:TA + 1, :] = jnp.max(s, axis=0, keepdims=True)

    def diag(s_ref, k_ref, q_all, vt_ref, m_ref, acc_ref):
        for qb in range(nsub):
            s_ref[0:(qb + 1) * SUB, slab(qb)] = _dot_nt(k_ref[keys(i, 0, qb + 1), :], q_all[slab(qb), :])
            s_ref[rows(qb), slab(qb)] += pd_ref[...]
            if qb >= 1:
                s_ref[rows(qb - 1), slab(qb)] += pp_ref[...]
        for qb in range(nsub):
            absorb(s_ref, vt_ref, i, m_ref, acc_ref, first=True, lo=0, hi=qb + 1, qb=qb)

    def win_prev(s_ref):
        for qb in range(nsub):
            s_ref[qb * SUB:TA, slab(qb)] = _dot_nt(kw_ref[keys(i - 1, qb, nsub), :], q_win[slab(qb), :])
            s_ref[rows(qb), slab(qb)] += strict
        s_ref[rows(nsub - 1), slab(0)] += pp_ref[...]
        for qb in range(nsub):
            absorb(s_ref, vwt_ref, i - 1, mw_ref, accw_ref, lo=qb, hi=nsub, qb=qb)

    @pl.when(i == 0)
    def _():
        diag(sa_ref, ksa_ref, q_sel, vst_ref, ms_ref, accs_ref)
        diag(sb_ref, kw_ref, q_win, vwt_ref, mw_ref, accw_ref)

    @pl.when(i >= 1)
    def _():
        qk_sel(sb_ref, i - 1)
        sb_ref[rows(nsub - 1), slab(0)] += pp_ref[...]
        diag(sa_ref, ksa_ref, q_sel, vst_ref, ms_ref, accs_ref)
        absorb(sb_ref, vst_ref, i - 1, ms_ref, accs_ref)
        diag(sa_ref, kw_ref, q_win, vwt_ref, mw_ref, accw_ref)
        win_prev(sb_ref)

    nfar = jnp.maximum(i - 1, 0)
    odd = nfar % 2

    @pl.when(odd == 1)
    def _():
        qk_sel(sa_ref, 0, with_max=True)
        absorb(sa_ref, vst_ref, 0, ms_ref, accs_ref, have_max=True)

    npair = nfar // 2

    @pl.when(npair > 0)
    def _():
        qk_sel(sa_ref, odd, with_max=True)

    def pair(t, carry):
        c = odd + 2 * t
        qk_sel(sb_ref, c + 1, with_max=True)
        absorb(sa_ref, vst_ref, c, ms_ref, accs_ref, have_max=True)
        qk_sel(sa_ref, jnp.minimum(c + 2, nfar - 1), with_max=True)
        absorb(sb_ref, vst_ref, c + 1, ms_ref, accs_ref, have_max=True)
        return carry

    lax.fori_loop(0, npair, pair, 0)

    gs = _sigmoid_t(g_ref[:, 0:LANES])
    gt_ref[...] = gs.T
    inv_s = 1.0 / accs_ref[NSA_HD:NSA_HD + 1, :]
    inv_w = 1.0 / accw_ref[NSA_HD:NSA_HD + 1, :]
    lane = _iota((TA, LANES), 1)
    for h in range(NSA_HPG):
        base = gate_col + (g * NSA_HPG + h) * N_BRANCH
        cols = slice(h * NSA_HD, (h + 1) * NSA_HD)
        gate_c = jnp.sum(jnp.where(lane == base, gs, 0.0), axis=-1, keepdims=True)
        gs_row = gt_ref[pl.ds(base + 1, 1), :]
        gw_row = gt_ref[pl.ds(base + 2, 1), :]
        o_sw = []
        for qb in range(nsub):
            lanes = slice(qb * SLAB + h * SUB, qb * SLAB + (h + 1) * SUB)
            gate_s = gs_row[:, rows(qb)] * inv_s[:, lanes]
            gate_w = gw_row[:, rows(qb)] * inv_w[:, lanes]
            o_sw.append((gate_s * accs_ref[0:NSA_HD, lanes] + gate_w * accw_ref[0:NSA_HD, lanes]).T)
        o = gate_c * oc_ref[:, cols] + jnp.concatenate(o_sw, axis=0)
        z = z_ref[:, cols]
        y_ref[:, cols] = (o * _silu_t(z)).astype(BF16)


def _attend(rel_bias, qn, ksa, vst, kwn, vwt, sel, o_c, gates, gate_blk, u, col_z, gate_col, B, S):
    nt = S // TA
    gw = NSA_HPG * NSA_HD
    W = NSA_HPG * TA
    cz = col_z // gw
    assert WIN == TA
    tile = lambda b, g, i: (b * nt + i, g)
    vt_spec = pl.BlockSpec((None, None, VT_ROWS, S), lambda b, g, i: (b, g, 0, 0))
    return pl.pallas_call(
        functools.partial(_attend_kernel, gate_col=gate_col),
        grid=(B, NSA_GROUPS, nt),
        in_specs=[
            pl.BlockSpec(memory_space=pltpu.SMEM),
            pl.BlockSpec((TA, gw), tile),
            pl.BlockSpec((S, 2 * NSA_HD), lambda b, g, i: (b, g)), vt_spec,
            pl.BlockSpec((S, NSA_HD), lambda b, g, i: (b, g)), vt_spec,
            pl.BlockSpec((None, None, TA, LANES), lambda b, g, i: (b, g, i, 0)),
            pl.BlockSpec((TA, gw), tile),
            pl.BlockSpec((TA, 2 * LANES), lambda b, g, i: (b * nt + i, gate_blk)),
            pl.BlockSpec((TA, gw), lambda b, g, i: (b * nt + i, cz + g)),
        ],
        out_specs=pl.BlockSpec((TA, gw), tile),
        out_shape=jax.ShapeDtypeStruct((B * S, NSA_HEADS * NSA_HD), BF16),
        scratch_shapes=[pltpu.VMEM((SUB, NSA_HPG * SUB), F32), pltpu.VMEM((SUB, NSA_HPG * SUB), F32),
                        pltpu.VMEM((TA + 8, W), F32), pltpu.VMEM((TA + 8, W), F32),
                        pltpu.VMEM((1, W), F32), pltpu.VMEM((VT_ROWS, W), F32),
                        pltpu.VMEM((1, W), F32), pltpu.VMEM((VT_ROWS, W), F32),
                        pltpu.VMEM((LANES, TA), F32)],
        compiler_params=pltpu.CompilerParams(dimension_semantics=("arbitrary", "arbitrary", "arbitrary"),
                                             vmem_limit_bytes=VMEM_LIMIT),
        name="attend",
    )(rel_bias, qn, ksa, vst, kwn, vwt, sel, o_c, gates, u)


def _outproj_kernel(yml_ref, yns_ref, w_hbm, x_ref, o_ref, ob_ref, wf_ref, wb_ref, sem, *, nblk):
    j = pl.program_id(0)
    tn = wf_ref.shape[1]

    def copy_of(jj):
        return pltpu.make_async_copy(w_hbm.at[:, pl.ds(pl.multiple_of(jj * tn, tn), tn)], wf_ref, sem)

    @pl.when(pl.program_id(1) == 0)
    def _():
        _stage_weight_block(copy_of, j, nblk, wf_ref, wb_ref)

    half = yml_ref.shape[1]
    o = x_ref[...] + _dot(yml_ref[...], wb_ref[0:half, :]) + _dot(yns_ref[...], wb_ref[half:2 * half, :])
    o_ref[...] = o
    ob_ref[...] = o.astype(BF16)


def _outproj(y_ml, y_ns, w_out, x2, tm=TILE_OUTPROJ[0], tn=TILE_OUTPROJ[1]):
    T, D = x2.shape
    half = y_ml.shape[1]
    tile = pl.BlockSpec((tm, tn), lambda j, i: (i, j))
    return pl.pallas_call(
        functools.partial(_outproj_kernel, nblk=D // tn),
        grid=(D // tn, T // tm),
        in_specs=[
            pl.BlockSpec((tm, half), lambda j, i: (i, 0)),
            pl.BlockSpec((tm, half), lambda j, i: (i, 0)),
            pl.BlockSpec(memory_space=pl.ANY),
            tile,
        ],
        out_specs=[tile, tile],
        out_shape=[jax.ShapeDtypeStruct((T, D), F32), jax.ShapeDtypeStruct((T, D), BF16)],
        scratch_shapes=[pltpu.VMEM((2 * half, tn), F32), pltpu.VMEM((2 * half, tn), BF16),
                        pltpu.SemaphoreType.DMA(())],
        compiler_params=pltpu.CompilerParams(dimension_semantics=("arbitrary", "arbitrary"),
                                             vmem_limit_bytes=VMEM_LIMIT),
        name="outproj",
    )(y_ml, y_ns, w_out, x2)


def _ple_kernel(xb_hbm, x_ref, p_ref, wg_hbm, wp_ref, o_ref, wgf_ref, wgb_ref, wpb_ref, xring, sem, xsem,
                *, nblk, ni):
    j = pl.program_id(0)
    i = pl.program_id(1)
    tn = wgf_ref.shape[1]
    tm = xring.shape[1]
    step = j * ni + i
    n_steps = nblk * ni

    def copy_of(jj):
        return pltpu.make_async_copy(wg_hbm.at[:, pl.ds(pl.multiple_of(jj * tn, tn), tn)], wgf_ref, sem)

    def x_copy(s):
        rows = pl.ds(pl.multiple_of((s % ni) * tm, tm), tm)
        return pltpu.make_async_copy(xb_hbm.at[rows, :], xring.at[s % 3], xsem.at[s % 3])

    @pl.when(step == 0)
    def _():
        x_copy(0).start()
        x_copy(1).start()

    @pl.when(step + 2 < n_steps)
    def _():
        x_copy(step + 2).start()

    @pl.when(i == 0)
    def _():
        _stage_weight_block(copy_of, j, nblk, wgf_ref, wgb_ref)
        wpb_ref[...] = wp_ref[...].astype(BF16)

    x_copy(step).wait()
    gate = _sigmoid_t(_dot(xring[step % 3], wgb_ref[...]))
    emb = _dot(p_ref[...].astype(BF16), wpb_ref[...])
    o_ref[...] = x_ref[...] + gate * emb


def _ple(x1b, x1, p2, wg, wp, tm=TILE_OUTPROJ[0], tn=TILE_OUTPROJ[1]):
    T, D = x1.shape
    P = p2.shape[1]
    tile = pl.BlockSpec((tm, tn), lambda j, i: (i, j))
    assert (D // tn) * (T // tm) >= 2
    return pl.pallas_call(
        functools.partial(_ple_kernel, nblk=D // tn, ni=T // tm),
        grid=(D // tn, T // tm),
        in_specs=[
            pl.BlockSpec(memory_space=pl.ANY),
            tile,
            pl.BlockSpec((tm, P), lambda j, i: (i, 0)),
            pl.BlockSpec(memory_space=pl.ANY),
            pl.BlockSpec((P, tn), lambda j, i: (0, j)),
        ],
        out_specs=tile,
        out_shape=jax.ShapeDtypeStruct((T, D), F32),
        scratch_shapes=[pltpu.VMEM((D, tn), F32), pltpu.VMEM((D, tn), BF16), pltpu.VMEM((P, tn), BF16),
                        pltpu.VMEM((3, tm, D), BF16), pltpu.SemaphoreType.DMA(()), pltpu.SemaphoreType.DMA((3,))],
        compiler_params=pltpu.CompilerParams(dimension_semantics=("arbitrary", "arbitrary"),
                                             vmem_limit_bytes=VMEM_LIMIT),
        name="ple",
    )(x1b, x1, p2, wg, wp)


def _layer(x2, p2, norm_w, w_in, conv_w, i_bias, f_bias, head_norm_w, q_norm_w, k_norm_w,
           pe_k, pe_v, kw1, kw2, vw1, vw2, rel_bias, w_out, ple_proj, ple_gate, B, S):
    D = x2.shape[1]
    qkw = ML_HEADS * ML_DQK
    vw = ML_HEADS * ML_DV
    nq = NSA_HEADS * NSA_HD
    nkv = NSA_GROUPS * NSA_HD
    o_i = 2 * qkw + 3 * vw
    o_f = o_i + ML_HEADS
    o_nq = o_f + ML_HEADS
    o_g = o_nq + nq + 6 * nkv
    o_z = o_g + NSA_HEADS * N_BRANCH
    w_t = w_in.T
    ngate = NSA_HEADS * N_BRANCH
    zeros = lambda n: jnp.zeros((n, D), w_in.dtype)
    wg_t = jnp.concatenate([w_t[o_i:o_f], w_t[o_g:o_z], zeros(LANES - ML_HEADS - ngate),
                            w_t[o_f:o_nq], zeros(LANES - ML_HEADS)], axis=0)
    col_nq = o_i
    col_kv = col_nq + nq
    col_z = col_kv + 6 * nkv

    h, gates = _rmsnorm(x2, norm_w.reshape(1, D), wg_t)
    u = _inproj(h, w_t, [(0, o_i), (o_nq, o_g - o_nq), (o_z, nq)])

    pad_h = lambda v: jnp.concatenate([v, jnp.zeros((LANES - ML_HEADS,), v.dtype)]).reshape(1, LANES)
    y_ml = _mlstm(u, gates, 0, conv_w, pad_h(i_bias), pad_h(f_bias), head_norm_w, B, S)

    qn, ksa, vst, kwn, vwt = _nsa_prep(u, q_norm_w.reshape(1, NSA_HD), k_norm_w, col_nq, col_kv, B, S)
    kcmp, vcmp_t = _compress(u, pe_k, pe_v, kw1.astype(BF16), kw2.astype(BF16), vw1.astype(BF16),
                             vw2.astype(BF16), k_norm_w, col_kv, B, S)
    o_c, sel = _cmp_sel(rel_bias, qn, kcmp, vcmp_t, B, S)
    y_ns = _attend(rel_bias, qn, ksa, vst, kwn, vwt, sel, o_c, gates, 0, u, col_z, ML_HEADS, B, S)

    x1, x1b = _outproj(y_ml, y_ns, w_out, x2)
    return _ple(x1b, x1, p2, ple_gate, ple_proj)


def kernel(x, p, norm_w, w_in, ml_conv_w, ml_i_bias, ml_f_bias, ml_head_norm_w, nsa_q_norm_w, nsa_k_norm_w,
           cmp_pe_k, cmp_pe_v, cmp_k_w1, cmp_k_w2, cmp_v_w1, cmp_v_w2, rel_bias, w_out, ple_proj, ple_gate):
    B, S, D = x.shape
    assert S % max(WIN, 256) == 0 and S // SEL_LEN >= 1
    x2 = x.reshape(B * S, D)
    for layer in range(w_in.shape[0]):
        x2 = _layer(x2, p[layer].reshape(B * S, -1), norm_w[layer], w_in[layer], ml_conv_w[layer],
                    ml_i_bias[layer], ml_f_bias[layer], ml_head_norm_w[layer], nsa_q_norm_w[layer],
                    nsa_k_norm_w[layer], cmp_pe_k[layer], cmp_pe_v[layer], cmp_k_w1[layer], cmp_k_w2[layer],
                    cmp_v_w1[layer], cmp_v_w2[layer], rel_bias, w_out[layer], ple_proj[layer], ple_gate[layer],
                    B, S)
    return x2.reshape(B, S, D)
```

```python
import functools
import math

import numpy as np
import jax
import jax.numpy as jnp
from jax import lax
from jax.experimental import pallas as pl
from jax.experimental.pallas import tpu as pltpu

F32 = jnp.float32
BF16 = jnp.bfloat16
HIGHEST = lax.Precision.HIGHEST

ML_HEADS = 8
ML_DQK = 128
ML_DV = 256
ML_CHUNK = 64
ML_CONV = 4
NSA_HEADS = 16
NSA_HD = 128
NSA_GROUPS = 4
NSA_HPG = 4
N_BRANCH = 3
CMP_STRIDE = 16
CMP_LEN = 32
SEL_LEN = 64
SEL_SHIFT = 6
SEL_TOPK = 16
WIN = 512
REL_BUCKETS = 32
REL_MAX_DIST = 128
EPS = 1e-6
NEG = -1e30
FORCE_SCORE = 1e4
LOG2E = math.log2(math.e)

LANES = 128
VMEM_LIMIT = 56 * 1024 * 1024

ROWS_RMSNORM = 512
TILE_INPROJ = (1024, 1024)
TILE_OUTPROJ = (512, 1024)
ROWS_MLSTM = 512
ROWS_NSA_PREP = 512


def _bucket_thresholds():
    n = np.arange(0, 4 * REL_MAX_DIST, dtype=np.int64)
    max_exact = REL_BUCKETS // 2
    nf = np.maximum(n, 1).astype(np.float32)
    large = max_exact + (np.log(nf / np.float32(max_exact)) / np.float32(math.log(REL_MAX_DIST / max_exact))
                         * np.float32(REL_BUCKETS - max_exact)).astype(np.int32)
    large = np.minimum(large, REL_BUCKETS - 1)
    bucket = np.where(n < max_exact, n, large)
    assert np.all(np.diff(bucket) >= 0)
    thr = [int(np.argmax(bucket >= b)) for b in range(REL_BUCKETS)]
    assert thr[REL_BUCKETS - 1] <= REL_MAX_DIST
    return thr


BUCKET_THR = _bucket_thresholds()


def _dot(a, b, precision=None):
    return jnp.dot(a, b, preferred_element_type=F32, precision=precision)


def _dot_nt(a, b, precision=None):
    return lax.dot_general(a, b, (((1,), (1,)), ((), ())), preferred_element_type=F32, precision=precision)


def _sigmoid_t(x):
    return 0.5 * jnp.tanh(0.5 * x) + 0.5


def _silu_t(x):
    h = 0.5 * x
    return h * jnp.tanh(h) + h


def _iota(shape, dim):
    return lax.broadcasted_iota(jnp.int32, shape, dim)


def _rel_bias_pattern(dist, tbl_ref, head):
    val = jnp.full(dist.shape, tbl_ref[0, head], F32)
    for b in range(1, REL_BUCKETS):
        val = jnp.where(dist >= BUCKET_THR[b], tbl_ref[b, head], val)
    return (val - tbl_ref[REL_BUCKETS - 1, head]) * LOG2E


def _rmsnorm_kernel(x_ref, nw_ref, wg_ref, h_ref, g_ref, wgb_ref):
    @pl.when(pl.program_id(0) == 0)
    def _():
        wgb_ref[...] = wg_ref[...].astype(BF16)

    x = x_ref[...]
    ms = jnp.mean(x * x, axis=-1, keepdims=True)
    h = (x * lax.rsqrt(ms + EPS) * nw_ref[...]).astype(BF16)
    h_ref[...] = h
    g_ref[...] = _dot_nt(h, wgb_ref[...])


def _rmsnorm(x2, norm_w, wg_t, rb=ROWS_RMSNORM):
    T, D = x2.shape
    NG = wg_t.shape[0]
    return pl.pallas_call(
        _rmsnorm_kernel,
        grid=(T // rb,),
        in_specs=[pl.BlockSpec((rb, D), lambda i: (i, 0)), pl.BlockSpec((1, D), lambda i: (0, 0)),
                  pl.BlockSpec((NG, D), lambda i: (0, 0))],
        out_specs=[pl.BlockSpec((rb, D), lambda i: (i, 0)), pl.BlockSpec((rb, NG), lambda i: (i, 0))],
        out_shape=[jax.ShapeDtypeStruct((T, D), BF16), jax.ShapeDtypeStruct((T, NG), F32)],
        scratch_shapes=[pltpu.VMEM((NG, D), BF16)],
        compiler_params=pltpu.CompilerParams(dimension_semantics=("arbitrary",), vmem_limit_bytes=VMEM_LIMIT),
        name="rmsnorm",
    )(x2, norm_w, wg_t)


def _stage_weight_block(copy_of, j, nblk, wf_ref, wb_ref):
    @pl.when(j == 0)
    def _():
        copy_of(j).start()

    copy_of(j).wait()
    wb_ref[...] = wf_ref[...].astype(BF16)

    @pl.when(j + 1 < nblk)
    def _():
        copy_of(j + 1).start()


def _inproj_kernel(h_ref, wt_hbm, u_ref, wf_ref, wb_ref, sem, *, nblk, row_of):
    j = pl.program_id(0)

    def copy_of(jj):
        return pltpu.make_async_copy(wt_hbm.at[pl.ds(row_of(jj), wf_ref.shape[0]), :], wf_ref, sem)

    @pl.when(pl.program_id(1) == 0)
    def _():
        _stage_weight_block(copy_of, j, nblk, wf_ref, wb_ref)

    u_ref[...] = _dot_nt(h_ref[...], wb_ref[...])


def _piece_blocks(pieces, tn):
    starts = []
    for start, length in pieces:
        assert length % tn == 0 and start % 8 == 0
        starts += [start + k * tn for k in range(length // tn)]
    bounds = [(b, s) for b, s in enumerate(starts) if b == 0 or s != starts[b - 1] + tn]

    def w_rows(j, i):
        row8 = (bounds[0][1] + j * tn) // 8
        for b, s in bounds[1:]:
            row8 = jnp.where(j >= b, (s + (j - b) * tn) // 8, row8)
        return row8 * 8, 0

    return len(starts), w_rows


def _inproj(h, w_t, pieces, tm=TILE_INPROJ[0], tn=TILE_INPROJ[1]):
    T, D = h.shape
    nblk, w_rows = _piece_blocks(pieces, tn)
    return pl.pallas_call(
        functools.partial(_inproj_kernel, nblk=nblk, row_of=lambda j: w_rows(j, 0)[0]),
        grid=(nblk, T // tm),
        in_specs=[
            pl.BlockSpec((tm, D), lambda j, i: (i, 0)),
            pl.BlockSpec(memory_space=pl.ANY),
        ],
        out_specs=pl.BlockSpec((tm, tn), lambda j, i: (i, j)),
        out_shape=jax.ShapeDtypeStruct((T, nblk * tn), F32),
        scratch_shapes=[pltpu.VMEM((tn, D), F32), pltpu.VMEM((tn, D), BF16), pltpu.SemaphoreType.DMA(())],
        compiler_params=pltpu.CompilerParams(dimension_semantics=("arbitrary", "arbitrary"),
                                             vmem_limit_bytes=VMEM_LIMIT),
        name="inproj",
    )(h, w_t)


def _mlstm_kernel(q_ref, k_ref, v_ref, o_ref, z_ref, g_ref, cw_ref, ib_ref, fb_ref, hw_ref, y_ref,
                  xbuf, c_ref, m_ref, *, lb):
    L = ML_CHUNK
    qkw = ML_HEADS * ML_DQK

    @pl.when(pl.program_id(1) == 0)
    def _():
        xbuf[0:8, :] = jnp.zeros((8, 2 * qkw), F32)
        c_ref[...] = jnp.zeros(c_ref.shape, F32)
        m_ref[...] = jnp.zeros(m_ref.shape, F32)

    row = _iota((L, L), 0)
    col = _iota((L, L), 1)
    tril = col <= row
    rowg = _iota((L, LANES), 0)
    eye_h = (_iota((8, LANES), 0) == _iota((8, LANES), 1)).astype(F32)
    ones_col = (_iota((L, LANES), 1) == 0).astype(BF16)
    heads = range(ML_HEADS)

    def chunk(c, carry):
        rows = pl.ds(pl.multiple_of(c * L, L), L)
        qk_cols = lambda h: slice(h * ML_DQK, (h + 1) * ML_DQK)
        v_cols = lambda h: slice(h * ML_DV, (h + 1) * ML_DV)
        col_of = lambda x, h: x[:, h:h + 1]

        G = g_ref[rows, :]
        li = G[:, 0:LANES] + ib_ref[...]
        fp = G[:, LANES:2 * LANES] + fb_ref[...]
        lf = jnp.minimum(fp, 0.0) - jnp.log(1.0 + jnp.exp(-jnp.abs(fp)))
        b = lf
        for sh in (1, 2, 4, 8, 16, 32):
            b = b + jnp.where(rowg >= sh, pltpu.roll(b, sh, 0), 0.0)
        g = li - b
        gmax = g
        for sh in (1, 2, 4, 8, 16, 32):
            gmax = jnp.maximum(gmax, jnp.where(rowg >= sh, pltpu.roll(gmax, sh, 0), NEG))
        m_old = m_ref[...]
        top = jnp.maximum(m_old, gmax)
        m_t = b + top
        inter = jnp.exp(m_old - top)
        floor = jnp.exp(-m_t)
        bL = b[L - 1:L, :]
        w = bL - b + li
        m_new = jnp.maximum(bL + m_old, jnp.max(w, axis=0, keepdims=True))
        wk = jnp.exp(w - m_new)
        decay = jnp.exp(bL + m_old - m_new)
        m_ref[...] = m_new
        g_rows = _dot_nt(eye_h, g, precision=HIGHEST)

        xbuf[8:8 + L, 0:qkw] = q_ref[rows, :]
        xbuf[8:8 + L, qkw:2 * qkw] = k_ref[rows, :]
        acc = cw_ref[ML_CONV - 1:ML_CONV, :] * xbuf[8:8 + L, :]
        for j in range(1, ML_CONV):
            acc = acc + cw_ref[ML_CONV - 1 - j:ML_CONV - j, :] * xbuf[8 - j:8 - j + L, :]
        act = _silu_t(acc)
        xbuf[0:8, :] = xbuf[L:L + 8, :]

        q = [act[:, qk_cols(h)] for h in heads]
        kf = [act[:, qkw + h * ML_DQK:qkw + (h + 1) * ML_DQK] * (ML_DQK ** -0.5) for h in heads]
        vaug = [jnp.concatenate([v_ref[rows, v_cols(h)].astype(BF16), ones_col], axis=1) for h in heads]
        cst = [c_ref[h] for h in heads]
        lhs = []
        for h in heads:
            dw = jnp.exp(jnp.where(tril, g_rows[h:h + 1, :] - col_of(top, h), NEG))
            sc = (_dot_nt(q[h].astype(BF16), kf[h].astype(BF16)) * dw).astype(BF16)
            q_in = (q[h] * col_of(inter, h)).astype(BF16)
            lhs.append(jnp.concatenate([q_in, sc], axis=1))
        num_aug = [_dot(lhs[h], jnp.concatenate([cst[h].astype(BF16), vaug[h]], axis=0)) for h in heads]
        for h in heads:
            kw = (kf[h] * col_of(wk, h)).astype(BF16)
            upd = lax.dot_general(kw, vaug[h], (((0,), (0,)), ((), ())), preferred_element_type=F32)
            c_ref[h] = decay[:, h:h + 1] * cst[h] + upd
        for h in heads:
            num = num_aug[h][:, 0:ML_DV]
            den = num_aug[h][:, ML_DV:ML_DV + 1]
            r = 1.0 / jnp.maximum(jnp.abs(den), col_of(floor, h))
            f = r * lax.rsqrt(r * r * jnp.mean(num * num, axis=-1, keepdims=True) + EPS)
            o = o_ref[rows, v_cols(h)]
            z = z_ref[rows, v_cols(h)]
            gate = _sigmoid_t(o) * _silu_t(z)
            y_ref[rows, v_cols(h)] = (num * f * hw_ref[h:h + 1, :] * gate).astype(BF16)
        return carry

    lax.fori_loop(0, lb // L, chunk, 0)


def _mlstm(u, gates, gate_blk, conv_w, ib, fb, head_w, B, S, lb=ROWS_MLSTM):
    T = B * S
    nsb = S // lb
    qkw = ML_HEADS * ML_DQK
    vw = ML_HEADS * ML_DV
    rowmap = lambda col: (lambda b, s: (b * nsb + s, col))
    const = lambda b, s: (0, 0)
    return pl.pallas_call(
        functools.partial(_mlstm_kernel, lb=lb),
        grid=(B, nsb),
        in_specs=[
            pl.BlockSpec((lb, qkw), rowmap(0)),
            pl.BlockSpec((lb, qkw), rowmap(1)),
            pl.BlockSpec((lb, vw), rowmap(1)),
            pl.BlockSpec((lb, vw), rowmap(2)),
            pl.BlockSpec((lb, vw), rowmap(3)),
            pl.BlockSpec((lb, 2 * LANES), rowmap(gate_blk)),
            pl.BlockSpec((ML_CONV, 2 * qkw), const),
            pl.BlockSpec((1, LANES), const),
            pl.BlockSpec((1, LANES), const),
            pl.BlockSpec((ML_HEADS, ML_DV), const),
        ],
        out_specs=pl.BlockSpec((lb, vw), rowmap(0)),
        out_shape=jax.ShapeDtypeStruct((T, vw), BF16),
        scratch_shapes=[
            pltpu.VMEM((ML_CHUNK + 8, 2 * qkw), F32),
            pltpu.VMEM((ML_HEADS, ML_DQK, ML_DV + LANES), F32),
            pltpu.VMEM((1, LANES), F32),
        ],
        compiler_params=pltpu.CompilerParams(dimension_semantics=("arbitrary", "arbitrary"),
                                             vmem_limit_bytes=VMEM_LIMIT),
        name="mlstm",
    )(u, u, u, u, u, gates, conv_w, ib, fb, head_w)


def _rms_heads(x, w, scale=1.0):
    outs = []
    for h in range(x.shape[1] // NSA_HD):
        xh = x[:, h * NSA_HD:(h + 1) * NSA_HD]
        ms = jnp.mean(xh * xh, axis=-1, keepdims=True)
        outs.append(xh * lax.rsqrt(ms + EPS) * w * scale)
    return jnp.concatenate(outs, axis=1)


VT_ROWS = NSA_HD + 16


def _nsa_prep_kernel(q_ref, ks_ref, vs_ref, kw_ref, vw_ref, qw_ref, kn_ref,
                     qn_ref, ksa_ref, vst_ref, kwn_ref, vwt_ref, *, rb, nsb):
    qn_ref[...] = _rms_heads(q_ref[...], qw_ref[...], NSA_HD ** -0.5 * LOG2E).astype(BF16)
    ksn = _rms_heads(ks_ref[...], kn_ref[1:2, :]).astype(BF16)
    kwn_ref[...] = _rms_heads(kw_ref[...], kn_ref[2:3, :]).astype(BF16)
    t = (pl.program_id(0) % nsb) * rb + _iota((rb, NSA_HD), 0)
    onehot = (_iota((rb, NSA_HD), 1) == jnp.right_shift(t, SEL_SHIFT)).astype(BF16)
    tail = (_iota((VT_ROWS - NSA_HD, rb), 0) == 0).astype(BF16)
    for g in range(NSA_GROUPS):
        cols = slice(g * NSA_HD, (g + 1) * NSA_HD)
        ksa_ref[:, 2 * g * NSA_HD:(2 * g + 1) * NSA_HD] = ksn[:, cols]
        ksa_ref[:, (2 * g + 1) * NSA_HD:(2 * g + 2) * NSA_HD] = onehot
        vst_ref[g] = jnp.concatenate([vs_ref[:, cols].T.astype(BF16), tail], axis=0)
        vwt_ref[g] = jnp.concatenate([vw_ref[:, cols].T.astype(BF16), tail], axis=0)


def _nsa_prep(u, q_norm_w, k_norm_w, col_q, col_kv, B, S, rb=ROWS_NSA_PREP):
    T = u.shape[0]
    nsb = S // rb
    qw = NSA_HEADS * NSA_HD
    kvw = NSA_GROUPS * NSA_HD
    cq = col_q // qw
    ck = col_kv // kvw
    assert S // SEL_LEN <= NSA_HD
    kv_spec = lambda idx: pl.BlockSpec((rb, kvw), lambda i: (i, ck + idx))
    out_k = pl.BlockSpec((rb, kvw), lambda i: (i, 0))
    out_ka = pl.BlockSpec((rb, 2 * kvw), lambda i: (i, 0))
    out_vt = pl.BlockSpec((None, NSA_GROUPS, VT_ROWS, rb), lambda i: (i // nsb, 0, 0, i % nsb))
    vt_shape = jax.ShapeDtypeStruct((B, NSA_GROUPS, VT_ROWS, S), BF16)
    return pl.pallas_call(
        functools.partial(_nsa_prep_kernel, rb=rb, nsb=nsb),
        grid=(T // rb,),
        in_specs=[
            pl.BlockSpec((rb, qw), lambda i: (i, cq)),
            kv_spec(2), kv_spec(3), kv_spec(4), kv_spec(5),
            pl.BlockSpec((1, NSA_HD), lambda i: (0, 0)),
            pl.BlockSpec((N_BRANCH, NSA_HD), lambda i: (0, 0)),
        ],
        out_specs=[pl.BlockSpec((rb, qw), lambda i: (i, 0)), out_ka, out_vt, out_k, out_vt],
        out_shape=[jax.ShapeDtypeStruct((T, qw), BF16), jax.ShapeDtypeStruct((T, 2 * kvw), BF16), vt_shape,
                   jax.ShapeDtypeStruct((T, kvw), BF16), vt_shape],
        compiler_params=pltpu.CompilerParams(dimension_semantics=("arbitrary",), vmem_limit_bytes=VMEM_LIMIT),
        name="nsa_prep",
    )(u, u, u, u, u, q_norm_w, k_norm_w)


def _compress_kernel(kc_ref, vc_ref, pek_ref, pev_ref, kw1_ref, kw2_ref, vw1_ref, vw2t_ref, kn_ref,
                     kcmp_ref, vcmp_ref, *, ns):
    def hidden(x_ref, pe_ref, w1_ref):
        hid = CMP_LEN // 2
        a = jnp.zeros((ns, w1_ref.shape[1]), F32)
        bsum = jnp.zeros((ns, w1_ref.shape[1]), F32)
        for l in range(0, hid, 2):
            xl = [x_ref[pl.ds(l + t, ns, stride=CMP_STRIDE), :] for t in range(2)]
            first = jnp.concatenate([(xl[t] + pe_ref[l + t:l + t + 1, :]).astype(BF16) for t in range(2)], axis=1)
            second = jnp.concatenate([(xl[t] + pe_ref[hid + l + t:hid + l + t + 1, :]).astype(BF16)
                                      for t in range(2)], axis=1)
            a = a + _dot(first, w1_ref[l * NSA_HD:(l + 2) * NSA_HD, :])
            bsum = bsum + _dot(second, w1_ref[(hid + l) * NSA_HD:(hid + l + 2) * NSA_HD, :])
        pre = a + pltpu.roll(bsum, ns - 1, 0)
        return _silu_t(pre).astype(BF16)

    kc = _dot(hidden(kc_ref, pek_ref, kw1_ref), kw2_ref[...])
    ms = jnp.mean(kc * kc, axis=-1, keepdims=True)
    kcmp_ref[...] = (kc * lax.rsqrt(ms + EPS) * kn_ref[0:1, :]).astype(BF16)
    vcmp_ref[...] = _dot_nt(vw2t_ref[...], hidden(vc_ref, pev_ref, vw1_ref)).astype(BF16)


def _compress(u, pe_k, pe_v, kw1, kw2, vw1, vw2, k_norm_w, col_kv, B, S):
    ns = S // CMP_STRIDE
    ck = col_kv // NSA_HD
    hidden = kw1.shape[1]
    const = lambda b, g: (0, 0)
    return pl.pallas_call(
        functools.partial(_compress_kernel, ns=ns),
        grid=(B, NSA_GROUPS),
        in_specs=[
            pl.BlockSpec((S, NSA_HD), lambda b, g: (b, ck + g)),
            pl.BlockSpec((S, NSA_HD), lambda b, g: (b, ck + NSA_GROUPS + g)),
            pl.BlockSpec((CMP_LEN, NSA_HD), const),
            pl.BlockSpec((CMP_LEN, NSA_HD), const),
            pl.BlockSpec((CMP_LEN * NSA_HD, hidden), const),
            pl.BlockSpec((hidden, NSA_HD), const),
            pl.BlockSpec((CMP_LEN * NSA_HD, hidden), const),
            pl.BlockSpec((NSA_HD, hidden), const),
            pl.BlockSpec((N_BRANCH, NSA_HD), const),
        ],
        out_specs=[pl.BlockSpec((None, None, ns, NSA_HD), lambda b, g: (b, g, 0, 0)),
                   pl.BlockSpec((None, None, NSA_HD, ns), lambda b, g: (b, g, 0, 0))],
        out_shape=[jax.ShapeDtypeStruct((B, NSA_GROUPS, ns, NSA_HD), BF16),
                   jax.ShapeDtypeStruct((B, NSA_GROUPS, NSA_HD, ns), BF16)],
        compiler_params=pltpu.CompilerParams(dimension_semantics=("arbitrary", "arbitrary"),
                                             vmem_limit_bytes=VMEM_LIMIT),
        name="compress",
    )(u, u, pe_k, pe_v, kw1, kw2, vw1, vw2.T, k_norm_w)


TQC = 256
CMP_TILES = 4
NEAR = 32
NEAR_BACK = 16


def _cmp_sel_kernel(tbl_ref, q_ref, kc_ref, vct_ref, oc_ref, sel_ref, pat_ref, *, ns, nsel):
    g = pl.program_id(1)
    i = pl.program_id(2)
    t0 = i * (CMP_TILES * TQC)
    c0 = i * (CMP_TILES * TQC // CMP_STRIDE)
    W = NSA_HPG * TQC
    assert (TQC - CMP_LEN) // CMP_STRIDE < NEAR - NEAR_BACK and NEAR_BACK * CMP_STRIDE >= REL_MAX_DIST + CMP_LEN

    @pl.when(i == 0)
    def _():
        r = _iota((TQC, LANES), 0)
        lane = _iota((TQC, LANES), 1)
        d = r - CMP_STRIDE * (jnp.bitwise_and(lane, NEAR - 1) - NEAR_BACK) - (CMP_LEN - 1)
        for h in range(NSA_HPG):
            val = jnp.where(d < 0, NEG, _rel_bias_pattern(d, tbl_ref, g * NSA_HPG + h))
            hi = val.astype(BF16).astype(F32)
            lo = jnp.where(d < 0, 0.0, val - hi)
            ext = jnp.where(lane < NEAR, hi, jnp.where(lane < 2 * NEAR, lo, jnp.where(lane == 2 * NEAR, NEG, 0.0)))
            pat_ref[h * TQC:(h + 1) * TQC, :] = ext.astype(BF16)

    tiles_q = range(CMP_TILES)
    qrows = lambda tq: slice(tq * TQC, (tq + 1) * TQC)
    lane = _iota((ns, LANES), 1)
    s = []
    for tq in tiles_q:
        q4 = q_ref[qrows(tq), :]
        q_aug = jnp.concatenate([jnp.concatenate([q4[:, h * NSA_HD:(h + 1) * NSA_HD],
                                                  pat_ref[h * TQC:(h + 1) * TQC, :]], axis=1)
                                 for h in range(NSA_HPG)], axis=0)
        rel = _iota((ns, LANES), 0) - (c0 + tq * (TQC // CMP_STRIDE) - NEAR_BACK)
        ext = ((lane < 2 * NEAR) & (rel == jnp.bitwise_and(lane, NEAR - 1))) | ((lane == 2 * NEAR) & (rel >= NEAR))
        k_aug = jnp.concatenate([kc_ref[...], jnp.where(ext, 1.0, 0.0).astype(BF16)], axis=1)
        s.append(_dot_nt(k_aug, q_aug))

    e, inv = [], []
    for tq in tiles_q:
        m = jnp.max(s[tq], axis=0, keepdims=True)
        e.append(jnp.exp2(s[tq] - m))
        l = jnp.sum(e[tq], axis=0, keepdims=True)
        t_abs = t0 + tq * TQC + jnp.bitwise_and(_iota((1, W), 1), TQC - 1)
        inv.append(jnp.where(t_abs >= CMP_LEN - 1, 1.0 / l, 0.0))

    ratio = SEL_LEN // CMP_STRIDE
    blk_n = _iota((nsel, ns), 0)
    tok_c = _iota((nsel, ns), 1)
    ov = ((tok_c >= ratio * blk_n - (CMP_LEN // CMP_STRIDE - 1)) & (tok_c <= ratio * blk_n + ratio - 1)).astype(F32)
    blk = _iota((nsel, TQC), 0)
    imp, cur = [], []
    for tq in tiles_q:
        o_t = _dot(vct_ref[...], e[tq].astype(BF16)) * inv[tq]
        for h in range(NSA_HPG):
            oc_ref[qrows(tq), h * NSA_HD:(h + 1) * NSA_HD] = o_t[:, h * TQC:(h + 1) * TQC].T
        p = e[tq] * inv[tq]
        ps = p[:, 0:TQC]
        for h in range(1, NSA_HPG):
            ps = ps + p[:, h * TQC:(h + 1) * TQC]
        v = _dot(ov, ps, precision=HIGHEST)
        c = jnp.right_shift(t0 + tq * TQC + _iota((nsel, TQC), 1), SEL_SHIFT)
        v = jnp.where((blk == 0) | (blk == c) | (blk == c - 1), FORCE_SCORE, v)
        imp.append(jnp.where(blk > c, NEG, v))
        cur.append(c)
    k_top = min(SEL_TOPK, nsel)

    def emit(tq, chosen):
        chosen = jnp.concatenate([chosen.astype(BF16), jnp.zeros((LANES - nsel, TQC), BF16)], axis=0)
        eye_q = (_iota((TQC, TQC), 0) == _iota((TQC, TQC), 1)).astype(BF16)
        sel_ref[qrows(tq), :] = ((_dot_nt(eye_q, chosen) - 1.0) * (-NEG)).astype(BF16)

    few = (i + 1) * CMP_TILES * (TQC // SEL_LEN) <= k_top

    @pl.when(few)
    def _():
        for tq in tiles_q:
            emit(tq, jnp.where(blk <= cur[tq], 1.0, 0.0))

    @pl.when(jnp.logical_not(few))
    def _():
        sub = _iota((8, TQC), 0)
        for tq in tiles_q:
            tiles = [imp[tq][8 * k:8 * k + 8, :] for k in range(nsel // 8)]
            counts = [jnp.zeros((8, TQC), jnp.int32) for _ in tiles]
            for jb in range(nsel):
                vj = imp[tq][jb:jb + 1, :]
                for k, tile in enumerate(tiles):
                    if 8 * k > jb:
                        ahead = vj >= tile
                    elif 8 * k + 7 < jb:
                        ahead = vj > tile
                    else:
                        ahead = (vj > tile) | ((vj == tile) & (sub + 8 * k > jb))
                    counts[k] = counts[k] + jnp.where(ahead, 1, 0)
            rank = jnp.concatenate(counts, axis=0)
            emit(tq, jnp.where(rank < k_top, 1.0, 0.0))


def _cmp_sel(rel_bias, qn, kcmp, vcmp_t, B, S):
    ns = S // CMP_STRIDE
    nsel = S // SEL_LEN
    rows = CMP_TILES * TQC
    nt = S // rows
    gw = NSA_HPG * NSA_HD
    assert nsel % 8 == 0 and nsel <= LANES
    return pl.pallas_call(
        functools.partial(_cmp_sel_kernel, ns=ns, nsel=nsel),
        grid=(B, NSA_GROUPS, nt),
        in_specs=[
            pl.BlockSpec(memory_space=pltpu.SMEM),
            pl.BlockSpec((rows, gw), lambda b, g, i: (b * nt + i, g)),
            pl.BlockSpec((None, None, ns, NSA_HD), lambda b, g, i: (b, g, 0, 0)),
            pl.BlockSpec((None, None, NSA_HD, ns), lambda b, g, i: (b, g, 0, 0)),
        ],
        out_specs=[
            pl.BlockSpec((rows, gw), lambda b, g, i: (b * nt + i, g)),
            pl.BlockSpec((None, None, rows, LANES), lambda b, g, i: (b, g, i, 0)),
        ],
        out_shape=[jax.ShapeDtypeStruct((B * S, NSA_HEADS * NSA_HD), F32),
                   jax.ShapeDtypeStruct((B, NSA_GROUPS, S, LANES), BF16)],
        scratch_shapes=[pltpu.VMEM((NSA_HPG * TQC, LANES), BF16)],
        compiler_params=pltpu.CompilerParams(dimension_semantics=("arbitrary", "arbitrary", "arbitrary"),
                                             vmem_limit_bytes=VMEM_LIMIT),
        name="cmp_sel",
    )(rel_bias, qn, kcmp, vcmp_t)


TA = 512
SUB = 128


def _attend_kernel(tbl_ref, q_ref, ksa_ref, vst_ref, kw_ref, vwt_ref, sel_ref, oc_ref, g_ref, z_ref, y_ref,
                   pd_ref, pp_ref, sa_ref, sb_ref, ms_ref, accs_ref, mw_ref, accw_ref, gt_ref, *, gate_col):
    g = pl.program_id(1)
    i = pl.program_id(2)
    nsub = TA // SUB

    krow = _iota((SUB, SUB), 0)
    qcol = _iota((SUB, SUB), 1)

    @pl.when(i == 0)
    def _():
        for h in range(NSA_HPG):
            head = g * NSA_HPG + h
            lanes = slice(h * SUB, (h + 1) * SUB)
            pd_ref[:, lanes] = jnp.where(krow <= qcol, _rel_bias_pattern(qcol - krow, tbl_ref, head), NEG)
            pp_ref[:, lanes] = _rel_bias_pattern(qcol - krow + SUB, tbl_ref, head)

    SLAB = NSA_HPG * SUB
    q4 = q_ref[...]
    sel = sel_ref[...]
    q_win = jnp.concatenate([q4[qb * SUB:(qb + 1) * SUB, h * NSA_HD:(h + 1) * NSA_HD]
                             for qb in range(nsub) for h in range(NSA_HPG)], axis=0)
    q_sel = jnp.concatenate([jnp.concatenate([q4[qb * SUB:(qb + 1) * SUB, h * NSA_HD:(h + 1) * NSA_HD],
                                              sel[qb * SUB:(qb + 1) * SUB, :]], axis=1)
                             for qb in range(nsub) for h in range(NSA_HPG)], axis=0)
    strict = jnp.concatenate([jnp.where(krow > qcol, 0.0, NEG)] * NSA_HPG, axis=1)

    def chunk(c):
        return pl.ds(pl.multiple_of(c * TA, TA), TA)

    def keys(c, lo, hi):
        return pl.ds(pl.multiple_of(c * TA + lo * SUB, SUB), (hi - lo) * SUB)

    def rows(kb):
        return slice(kb * SUB, (kb + 1) * SUB)

    def slab(qb):
        return slice(qb * SLAB, (qb + 1) * SLAB)

    def absorb(s_ref, vt_ref, c, m_ref, acc_ref, first=False, lo=0, hi=nsub, qb=None, have_max=False):
        lanes = slice(None) if qb is None else slab(qb)
        krows = slice(lo * SUB, hi * SUB)
        mx = s_ref[TA:TA + 1, lanes] if have_max else jnp.max(s_ref[krows, lanes], axis=0, keepdims=True)
        if first:
            m_new = mx
        else:
            m_old = m_ref[:, lanes]
            m_new = jnp.maximum(m_old, mx)
        p = jnp.exp2(s_ref[krows, lanes] - m_new).astype(BF16)
        pv = _dot(vt_ref[:, keys(c, lo, hi)], p)
        if first:
            acc_ref[:, lanes] = pv
        else:
            acc_ref[:, lanes] = jnp.exp2(m_old - m_new) * acc_ref[:, lanes] + pv
        m_ref[:, lanes] = m_new

    def qk_sel(s_ref, c, with_max=False):
        s = _dot_nt(ksa_ref[chunk(c), :], q_sel)
        s_ref[0:TA, :] = s
        if with_max:
            s_ref[TA:TA + 1, :] = jnp.max(s, axis=0, keepdims=True)

    def diag(s_ref, k_ref, q_all, vt_ref, m_ref, acc_ref):
        for qb in range(nsub):
            s_ref[0:(qb + 1) * SUB, slab(qb)] = _dot_nt(k_ref[keys(i, 0, qb + 1), :], q_all[slab(qb), :])
            s_ref[rows(qb), slab(qb)] += pd_ref[...]
            if qb >= 1:
                s_ref[rows(qb - 1), slab(qb)] += pp_ref[...]
        for qb in range(nsub):
            absorb(s_ref, vt_ref, i, m_ref, acc_ref, first=True, lo=0, hi=qb + 1, qb=qb)

    def win_prev(s_ref):
        for qb in range(nsub):
            s_ref[qb * SUB:TA, slab(qb)] = _dot_nt(kw_ref[keys(i - 1, qb, nsub), :], q_win[slab(qb), :])
            s_ref[rows(qb), slab(qb)] += strict
        s_ref[rows(nsub - 1), slab(0)] += pp_ref[...]
        for qb in range(nsub):
            absorb(s_ref, vwt_ref, i - 1, mw_ref, accw_ref, lo=qb, hi=nsub, qb=qb)

    @pl.when(i == 0)
    def _():
        diag(sa_ref, ksa_ref, q_sel, vst_ref, ms_ref, accs_ref)
        diag(sb_ref, kw_ref, q_win, vwt_ref, mw_ref, accw_ref)

    @pl.when(i >= 1)
    def _():
        qk_sel(sb_ref, i - 1)
        sb_ref[rows(nsub - 1), slab(0)] += pp_ref[...]
        diag(sa_ref, ksa_ref, q_sel, vst_ref, ms_ref, accs_ref)
        absorb(sb_ref, vst_ref, i - 1, ms_ref, accs_ref)
        diag(sa_ref, kw_ref, q_win, vwt_ref, mw_ref, accw_ref)
        win_prev(sb_ref)

    nfar = jnp.maximum(i - 1, 0)
    odd = nfar % 2

    @pl.when(odd == 1)
    def _():
        qk_sel(sa_ref, 0, with_max=True)
        absorb(sa_ref, vst_ref, 0, ms_ref, accs_ref, have_max=True)

    npair = nfar // 2

    @pl.when(npair > 0)
    def _():
        qk_sel(sa_ref, odd, with_max=True)

    def pair(t, carry):
        c = odd + 2 * t
        qk_sel(sb_ref, c + 1, with_max=True)
        absorb(sa_ref, vst_ref, c, ms_ref, accs_ref, have_max=True)
        qk_sel(sa_ref, jnp.minimum(c + 2, nfar - 1), with_max=True)
        absorb(sb_ref, vst_ref, c + 1, ms_ref, accs_ref, have_max=True)
        return carry

    lax.fori_loop(0, npair, pair, 0)

    gs = _sigmoid_t(g_ref[:, 0:LANES])
    gt_ref[...] = gs.T
    inv_s = 1.0 / accs_ref[NSA_HD:NSA_HD + 1, :]
    inv_w = 1.0 / accw_ref[NSA_HD:NSA_HD + 1, :]
    lane = _iota((TA, LANES), 1)
    for h in range(NSA_HPG):
        base = gate_col + (g * NSA_HPG + h) * N_BRANCH
        cols = slice(h * NSA_HD, (h + 1) * NSA_HD)
        gate_c = jnp.sum(jnp.where(lane == base, gs, 0.0), axis=-1, keepdims=True)
        gs_row = gt_ref[pl.ds(base + 1, 1), :]
        gw_row = gt_ref[pl.ds(base + 2, 1), :]
        o_sw = []
        for qb in range(nsub):
            lanes = slice(qb * SLAB + h * SUB, qb * SLAB + (h + 1) * SUB)
            gate_s = gs_row[:, rows(qb)] * inv_s[:, lanes]
            gate_w = gw_row[:, rows(qb)] * inv_w[:, lanes]
            o_sw.append((gate_s * accs_ref[0:NSA_HD, lanes] + gate_w * accw_ref[0:NSA_HD, lanes]).T)
        o = gate_c * oc_ref[:, cols] + jnp.concatenate(o_sw, axis=0)
        z = z_ref[:, cols]
        y_ref[:, cols] = (o * _silu_t(z)).astype(BF16)


def _attend(rel_bias, qn, ksa, vst, kwn, vwt, sel, o_c, gates, gate_blk, u, col_z, gate_col, B, S):
    nt = S // TA
    gw = NSA_HPG * NSA_HD
    W = NSA_HPG * TA
    cz = col_z // gw
    assert WIN == TA
    tile = lambda b, g, i: (b * nt + i, g)
    vt_spec = pl.BlockSpec((None, None, VT_ROWS, S), lambda b, g, i: (b, g, 0, 0))
    return pl.pallas_call(
        functools.partial(_attend_kernel, gate_col=gate_col),
        grid=(B, NSA_GROUPS, nt),
        in_specs=[
            pl.BlockSpec(memory_space=pltpu.SMEM),
            pl.BlockSpec((TA, gw), tile),
            pl.BlockSpec((S, 2 * NSA_HD), lambda b, g, i: (b, g)), vt_spec,
            pl.BlockSpec((S, NSA_HD), lambda b, g, i: (b, g)), vt_spec,
            pl.BlockSpec((None, None, TA, LANES), lambda b, g, i: (b, g, i, 0)),
            pl.BlockSpec((TA, gw), tile),
            pl.BlockSpec((TA, 2 * LANES), lambda b, g, i: (b * nt + i, gate_blk)),
            pl.BlockSpec((TA, gw), lambda b, g, i: (b * nt + i, cz + g)),
        ],
        out_specs=pl.BlockSpec((TA, gw), tile),
        out_shape=jax.ShapeDtypeStruct((B * S, NSA_HEADS * NSA_HD), BF16),
        scratch_shapes=[pltpu.VMEM((SUB, NSA_HPG * SUB), F32), pltpu.VMEM((SUB, NSA_HPG * SUB), F32),
                        pltpu.VMEM((TA + 8, W), F32), pltpu.VMEM((TA + 8, W), F32),
                        pltpu.VMEM((1, W), F32), pltpu.VMEM((VT_ROWS, W), F32),
                        pltpu.VMEM((1, W), F32), pltpu.VMEM((VT_ROWS, W), F32),
                        pltpu.VMEM((LANES, TA), F32)],
        compiler_params=pltpu.CompilerParams(dimension_semantics=("arbitrary", "arbitrary", "arbitrary"),
                                             vmem_limit_bytes=VMEM_LIMIT),
        name="attend",
    )(rel_bias, qn, ksa, vst, kwn, vwt, sel, o_c, gates, u)


class _CopyPair:
    def __init__(self, *copies):
        self.copies = copies

    def start(self):
        for c in self.copies:
            c.start()

    def wait(self):
        for c in self.copies:
            c.wait()


def _outproj_kernel(yml_hbm, yns_hbm, w_hbm, x_ref, o_ref, ob_ref, wf_ref, wb_ref, yring, sem, ysem, *, nblk, ni):
    j = pl.program_id(0)
    i = pl.program_id(1)
    tn = wf_ref.shape[1]
    tm = yring.shape[2]
    step = j * ni + i

    def copy_of(jj):
        return pltpu.make_async_copy(w_hbm.at[:, pl.ds(pl.multiple_of(jj * tn, tn), tn)], wf_ref, sem)

    def y_copy(s):
        rows = pl.ds(pl.multiple_of((s % ni) * tm, tm), tm)
        return _CopyPair(*[pltpu.make_async_copy(src.at[rows, :], yring.at[s % 3, k], ysem.at[s % 3, k])
                           for k, src in enumerate((yml_hbm, yns_hbm))])

    @pl.when(step == 0)
    def _():
        y_copy(0).start()
        y_copy(1).start()

    @pl.when(step + 2 < nblk * ni)
    def _():
        y_copy(step + 2).start()

    @pl.when(i == 0)
    def _():
        _stage_weight_block(copy_of, j, nblk, wf_ref, wb_ref)

    y_copy(step).wait()
    half = yring.shape[3]
    slot = step % 3
    o = x_ref[...] + _dot(yring[slot, 0], wb_ref[0:half, :]) + _dot(yring[slot, 1], wb_ref[half:2 * half, :])
    o_ref[...] = o
    ob_ref[...] = o.astype(BF16)


def _outproj(y_ml, y_ns, w_out, x2, tm=TILE_OUTPROJ[0], tn=TILE_OUTPROJ[1]):
    T, D = x2.shape
    half = y_ml.shape[1]
    tile = pl.BlockSpec((tm, tn), lambda j, i: (i, j))
    assert (D // tn) * (T // tm) >= 2
    return pl.pallas_call(
        functools.partial(_outproj_kernel, nblk=D // tn, ni=T // tm),
        grid=(D // tn, T // tm),
        in_specs=[
            pl.BlockSpec(memory_space=pl.ANY),
            pl.BlockSpec(memory_space=pl.ANY),
            pl.BlockSpec(memory_space=pl.ANY),
            tile,
        ],
        out_specs=[tile, tile],
        out_shape=[jax.ShapeDtypeStruct((T, D), F32), jax.ShapeDtypeStruct((T, D), BF16)],
        scratch_shapes=[pltpu.VMEM((2 * half, tn), F32), pltpu.VMEM((2 * half, tn), BF16),
                        pltpu.VMEM((3, 2, tm, half), BF16),
                        pltpu.SemaphoreType.DMA(()), pltpu.SemaphoreType.DMA((3, 2))],
        compiler_params=pltpu.CompilerParams(dimension_semantics=("arbitrary", "arbitrary"),
                                             vmem_limit_bytes=VMEM_LIMIT),
        name="outproj",
    )(y_ml, y_ns, w_out, x2)


def _ple_kernel(xb_ref, x_ref, p_ref, wg_hbm, wp_ref, o_ref, wgf_ref, wgb_ref, wpb_ref, sem, *, nblk):
    j = pl.program_id(0)
    tn = wgf_ref.shape[1]

    def copy_of(jj):
        return pltpu.make_async_copy(wg_hbm.at[:, pl.ds(pl.multiple_of(jj * tn, tn), tn)], wgf_ref, sem)

    @pl.when(pl.program_id(1) == 0)
    def _():
        _stage_weight_block(copy_of, j, nblk, wgf_ref, wgb_ref)
        wpb_ref[...] = wp_ref[...].astype(BF16)

    gate = _sigmoid_t(_dot(xb_ref[...], wgb_ref[...]))
    emb = _dot(p_ref[...].astype(BF16), wpb_ref[...])
    o_ref[...] = x_ref[...] + gate * emb


def _ple(x1b, x1, p2, wg, wp, tm=TILE_OUTPROJ[0], tn=TILE_OUTPROJ[1]):
    T, D = x1.shape
    P = p2.shape[1]
    tile = pl.BlockSpec((tm, tn), lambda j, i: (i, j))
    return pl.pallas_call(
        functools.partial(_ple_kernel, nblk=D // tn),
        grid=(D // tn, T // tm),
        in_specs=[
            pl.BlockSpec((tm, D), lambda j, i: (i, 0)),
            tile,
            pl.BlockSpec((tm, P), lambda j, i: (i, 0)),
            pl.BlockSpec(memory_space=pl.ANY),
            pl.BlockSpec((P, tn), lambda j, i: (0, j)),
        ],
        out_specs=tile,
        out_shape=jax.ShapeDtypeStruct((T, D), F32),
        scratch_shapes=[pltpu.VMEM((D, tn), F32), pltpu.VMEM((D, tn), BF16), pltpu.VMEM((P, tn), BF16),
                        pltpu.SemaphoreType.DMA(())],
        compiler_params=pltpu.CompilerParams(dimension_semantics=("arbitrary", "arbitrary"),
                                             vmem_limit_bytes=VMEM_LIMIT),
        name="ple",
    )(x1b, x1, p2, wg, wp)


def _layer(x2, p2, norm_w, w_in, conv_w, i_bias, f_bias, head_norm_w, q_norm_w, k_norm_w,
           pe_k, pe_v, kw1, kw2, vw1, vw2, rel_bias, w_out, ple_proj, ple_gate, B, S):
    D = x2.shape[1]
    qkw = ML_HEADS * ML_DQK
    vw = ML_HEADS * ML_DV
    nq = NSA_HEADS * NSA_HD
    nkv = NSA_GROUPS * NSA_HD
    o_i = 2 * qkw + 3 * vw
    o_f = o_i + ML_HEADS
    o_nq = o_f + ML_HEADS
    o_g = o_nq + nq + 6 * nkv
    o_z = o_g + NSA_HEADS * N_BRANCH
    w_t = w_in.T
    ngate = NSA_HEADS * N_BRANCH
    zeros = lambda n: jnp.zeros((n, D), w_in.dtype)
    wg_t = jnp.concatenate([w_t[o_i:o_f], w_t[o_g:o_z], zeros(LANES - ML_HEADS - ngate),
                            w_t[o_f:o_nq], zeros(LANES - ML_HEADS)], axis=0)
    col_nq = o_i
    col_kv = col_nq + nq
    col_z = col_kv + 6 * nkv

    h, gates = _rmsnorm(x2, norm_w.reshape(1, D), wg_t)
    u = _inproj(h, w_t, [(0, o_i), (o_nq, o_g - o_nq), (o_z, nq)])

    pad_h = lambda v: jnp.concatenate([v, jnp.zeros((LANES - ML_HEADS,), v.dtype)]).reshape(1, LANES)
    y_ml = _mlstm(u, gates, 0, conv_w, pad_h(i_bias), pad_h(f_bias), head_norm_w, B, S)

    qn, ksa, vst, kwn, vwt = _nsa_prep(u, q_norm_w.reshape(1, NSA_HD), k_norm_w, col_nq, col_kv, B, S)
    kcmp, vcmp_t = _compress(u, pe_k, pe_v, kw1.astype(BF16), kw2.astype(BF16), vw1.astype(BF16),
                             vw2.astype(BF16), k_norm_w, col_kv, B, S)
    o_c, sel = _cmp_sel(rel_bias, qn, kcmp, vcmp_t, B, S)
    y_ns = _attend(rel_bias, qn, ksa, vst, kwn, vwt, sel, o_c, gates, 0, u, col_z, ML_HEADS, B, S)

    x1, x1b = _outproj(y_ml, y_ns, w_out, x2)
    return _ple(x1b, x1, p2, ple_gate, ple_proj)


def kernel(x, p, norm_w, w_in, ml_conv_w, ml_i_bias, ml_f_bias, ml_head_norm_w, nsa_q_norm_w, nsa_k_norm_w,
           cmp_pe_k, cmp_pe_v, cmp_k_w1, cmp_k_w2, cmp_v_w1, cmp_v_w2, rel_bias, w_out, ple_proj, ple_gate):
    B, S, D = x.shape
    assert S % max(WIN, 256) == 0 and S // SEL_LEN >= 1
    x2 = x.reshape(B * S, D)
    for layer in range(w_in.shape[0]):
        x2 = _layer(x2, p[layer].reshape(B * S, -1), norm_w[layer], w_in[layer], ml_conv_w[layer],
                    ml_i_bias[layer], ml_f_bias[layer], ml_head_norm_w[layer], nsa_q_norm_w[layer],
                    nsa_k_norm_w[layer], cmp_pe_k[layer], cmp_pe_v[layer], cmp_k_w1[layer], cmp_k_w2[layer],
                    cmp_v_w1[layer], cmp_v_w2[layer], rel_bias, w_out[layer], ple_proj[layer], ple_gate[layer],
                    B, S)
    return x2.reshape(B, S, D)
```
